```python
import math
import jax
import jax.numpy as jnp
from jax import lax
import numpy as np

D_MODEL = 1024
BATCH = 2
SEQ = 8192
DEPTH = 1

SSM_EXPAND = 2
D_INNER = SSM_EXPAND * D_MODEL
SSM_HEAD_DIM = 64
SSM_HEADS = D_INNER // SSM_HEAD_DIM
SSM_GROUPS = 2
D_STATE = 128
CONV_K = 4
CONV_DIM = D_INNER + 2 * SSM_GROUPS * D_STATE
SSD_CHUNK = 128
ATTN_HEAD_DIM = 64
DILATED_CONFIGS = ((128, 1), (512, 4), (2048, 16))
HEADS_PER_DIL_GROUP = 8
ATTN_HEADS = HEADS_PER_DIL_GROUP * 3
ATTN_WIDTH = ATTN_HEADS * ATTN_HEAD_DIM
ATTN_OUT_WIDTH = HEADS_PER_DIL_GROUP * ATTN_HEAD_DIM
NUM_BUCKETS = 32
MAX_DISTANCE = 2048
N_EXPERT_GROUPS = 8
EXPERTS_PER_GROUP = 8
N_EXPERTS = N_EXPERT_GROUPS * EXPERTS_PER_GROUP
TOP_K_FINE = 2
D_EXPERT = 512
MOE_BLOCK = 128
NORM_EPS = 1e-6

IN_SPLITS = (D_INNER, CONV_DIM, SSM_HEADS, ATTN_WIDTH, ATTN_WIDTH, ATTN_WIDTH, D_MODEL, D_MODEL)
IN_TOTAL = D_INNER + CONV_DIM + SSM_HEADS + 3 * ATTN_WIDTH + 2 * D_MODEL

kernel_name = 'hybrid_ssd_dilated_attn_hmoe_layer'


def rmsnorm(x, w):
    xf = x.astype(jnp.float32)
    y = xf * lax.rsqrt(jnp.mean(xf * xf, axis=-1, keepdims=True) + NORM_EPS)
    return (y * w.astype(jnp.float32)).astype(x.dtype)


def segsum(a):
    t = a.shape[-1]
    a_rep = jnp.broadcast_to(a[..., :, None], a.shape + (t,))
    a_rep = jnp.where(np.tril(np.ones((t, t), dtype=bool), -1), a_rep, 0.0)
    cs = jnp.cumsum(a_rep, axis=-2)
    return jnp.where(np.tril(np.ones((t, t), dtype=bool)), cs, -jnp.inf)


def ssd_chunked(xdt, a_dt, bm, cm):
    b, s, h, p = xdt.shape
    g, n = bm.shape[2], bm.shape[3]
    hg = h // g
    c, l = s // SSD_CHUNK, SSD_CHUNK
    xc = xdt.reshape(b, c, l, g, hg, p)
    ac = a_dt.reshape(b, c, l, g, hg).transpose(0, 3, 4, 1, 2)
    bc = bm.reshape(b, c, l, g, n)
    cc = cm.reshape(b, c, l, g, n)
    a_cs = jnp.cumsum(ac, axis=-1)
    decay_in = jnp.exp(segsum(ac))
    cb = jnp.einsum('bclgn,bcsgn->bgcls', cc, bc)
    y_diag = jnp.einsum('bghcls,bcsghp->bclghp', cb[:, :, None] * decay_in, xc)
    decay_to_end = jnp.exp(a_cs[..., -1:] - a_cs).transpose(0, 3, 4, 1, 2)
    states = jnp.einsum('bclgn,bclghp->bcghpn', bc, xc * decay_to_end[..., None])
    states = jnp.concatenate([jnp.zeros_like(states[:, :1]), states], axis=1)
    chunk_tot = jnp.pad(a_cs[..., -1], ((0, 0), (0, 0), (0, 0), (1, 0)))
    chunk_decay = jnp.exp(segsum(chunk_tot))
    states = jnp.einsum('bghzc,bcghpn->bzghpn', chunk_decay, states)[:, :-1]
    decay_from_start = jnp.exp(a_cs).transpose(0, 3, 4, 1, 2)
    y_off = jnp.einsum('bclgn,bcghpn->bclghp', cc, states) * decay_from_start[..., None]
    return (y_diag + y_off).reshape(b, s, h, p)


def ssd_mixer(z, xbc, dt_raw, conv_w, conv_b, dt_bias, a_log, d_skip, ssm_norm_w):
    b, s, _ = xbc.shape
    f32 = jnp.float32
    xbc = lax.conv_general_dilated(xbc, conv_w[:, None, :].astype(xbc.dtype), window_strides=(1,),
                                   padding=[(CONV_K - 1, 0)], dimension_numbers=('NWC', 'WIO', 'NWC'),
                                   feature_group_count=CONV_DIM)
    xbc = jax.nn.silu(xbc + conv_b.astype(xbc.dtype)).astype(f32)
    xs, bm, cm = jnp.split(xbc, [D_INNER, D_INNER + SSM_GROUPS * D_STATE], axis=-1)
    xs = xs.reshape(b, s, SSM_HEADS, SSM_HEAD_DIM)
    bm = bm.reshape(b, s, SSM_GROUPS, D_STATE)
    cm = cm.reshape(b, s, SSM_GROUPS, D_STATE)
    dt = jax.nn.softplus(dt_raw.astype(f32) + dt_bias.astype(f32))
    a = -jnp.exp(a_log.astype(f32))
    y = ssd_chunked(xs * dt[..., None], dt * a, bm, cm) + d_skip.astype(f32)[:, None] * xs
    y = y.reshape(b, s, D_INNER) * jax.nn.silu(z.astype(f32))
    y = y.reshape(b, s, SSM_GROUPS, D_INNER // SSM_GROUPS)
    y = y * lax.rsqrt(jnp.mean(y * y, axis=-1, keepdims=True) + NORM_EPS)
    y = y.reshape(b, s, D_INNER) * ssm_norm_w.astype(f32)
    return y.astype(z.dtype)


def t5_causal_bucket(dist):
    max_exact = NUM_BUCKETS // 2
    large = max_exact + (np.log(np.maximum(dist, max_exact) / max_exact)
                         / math.log(MAX_DISTANCE / max_exact) * (NUM_BUCKETS - max_exact)).astype(np.int32)
    return np.where(dist < max_exact, dist, np.minimum(large, NUM_BUCKETS - 1)).astype(np.int32)


def dilated_band_attention(q, k, v, bias_tab, window, dilation):
    b, s, h, dh = q.shape
    f32 = jnp.float32
    blk = window // dilation
    span = blk * dilation
    s_pad = -(-s // span) * span
    nb = s_pad // span

    def to_residue_blocks(t):
        t = jnp.pad(t, ((0, 0), (0, s_pad - s), (0, 0), (0, 0)))
        t = t.reshape(b, s_pad // dilation, dilation, h, dh).transpose(0, 2, 1, 3, 4)
        return t.reshape(b, dilation, nb, blk, h, dh).astype(f32)

    def band(t):
        prev = jnp.concatenate([jnp.zeros_like(t[:, :, :1]), t[:, :, :-1]], axis=2)
        return jnp.concatenate([prev, t], axis=3)

    qb = to_residue_blocks(q)
    kb = band(to_residue_blocks(k))
    vb = band(to_residue_blocks(v))
    off = np.arange(blk)[:, None] + blk - np.arange(2 * blk)[None, :]
    in_win = (off >= 0) & (off <= blk)
    first = (np.arange(nb)[:, None, None] == 0) & (np.arange(2 * blk)[None, None, :] < blk)
    valid = in_win[None] & ~first
    bucket = t5_causal_bucket(np.clip(off, 0, None) * dilation)
    bias = jnp.transpose(bias_tab[bucket], (2, 0, 1)).astype(f32)
    scores = jnp.einsum('brnqhd,brnkhd->brnhqk', qb, kb) * (dh ** -0.5) + bias
    scores = jnp.where(valid[:, None], scores, -jnp.inf)
    m = jnp.max(scores, axis=-1, keepdims=True)
    p = jnp.exp(scores - m)
    denom = jnp.sum(p, axis=-1)
    out = jnp.einsum('brnhqk,brnkhd->brnqhd', p, vb) / jnp.swapaxes(denom, -1, -2)[..., None]
    lse = jnp.swapaxes(m[..., 0] + jnp.log(denom), -1, -2)

    def from_residue_blocks(t):
        t = t.reshape((b, dilation, s_pad // dilation) + t.shape[4:])
        t = jnp.swapaxes(t, 1, 2)
        return t.reshape((b, s_pad) + t.shape[3:])[:, :s]

    return from_residue_blocks(out), from_residue_blocks(lse)


def dilated_attention_mixer(q, k, v, q_norm_w, k_norm_w, rel_bias):
    b, s, _ = q.shape
    shp = (b, s, ATTN_HEADS, ATTN_HEAD_DIM)
    q = rmsnorm(q.reshape(shp), q_norm_w)
    k = rmsnorm(k.reshape(shp), k_norm_w)
    v = v.reshape(shp)
    outs, lses = [], []
    for gi, (window, dilation) in enumerate(DILATED_CONFIGS):
        hs = slice(gi * HEADS_PER_DIL_GROUP, (gi + 1) * HEADS_PER_DIL_GROUP)
        o, lse = dilated_band_attention(q[:, :, hs], k[:, :, hs], v[:, :, hs], rel_bias[:, hs], window, dilation)
        outs.append(o)
        lses.append(lse)
    w = jax.nn.softmax(jnp.stack(lses), axis=0)
    out = jnp.sum(w[..., None] * jnp.stack(outs), axis=0)
    return out.reshape(b, s, ATTN_OUT_WIDTH).astype(v.dtype)


def hierarchical_moe(h, w_coarse, b_coarse, w_fine, b_fine, w_gate_exp, w_up_exp, w_down_exp):
    b, s, d = h.shape
    t = b * s
    hf = h.reshape(t, d)
    coarse_logits = (hf @ w_coarse + b_coarse).astype(jnp.float32)
    group = jnp.argmax(coarse_logits, axis=-1)
    group_p = jnp.take_along_axis(jax.nn.softmax(coarse_logits, axis=-1), group[:, None], axis=-1)
    fine_logits = (hf @ w_fine + b_fine).astype(jnp.float32).reshape(t, N_EXPERT_GROUPS, EXPERTS_PER_GROUP)
    fine_sel = jnp.take_along_axis(fine_logits, group[:, None, None], axis=1)[:, 0]
    top_p, top_i = lax.top_k(jax.nn.softmax(fine_sel, axis=-1), TOP_K_FINE)
    gates = group_p * top_p / jnp.sum(top_p, axis=-1, keepdims=True)
    eid = (group[:, None] * EXPERTS_PER_GROUP + top_i).reshape(-1)
    tok = jnp.repeat(jnp.arange(t), TOP_K_FINE)
    gw = gates.reshape(-1)
    order = jnp.argsort(eid)
    e_sorted, tok_sorted, gw_sorted = eid[order], tok[order], gw[order]
    counts = jnp.bincount(eid, length=N_EXPERTS)
    start = jnp.cumsum(counts) - counts
    padded = (counts + MOE_BLOCK - 1) // MOE_BLOCK * MOE_BLOCK
    pad_end = jnp.cumsum(padded)
    pad_start = pad_end - padded
    n_assign = t * TOP_K_FINE
    dest = pad_start[e_sorted] + jnp.arange(n_assign) - start[e_sorted]
    n_blocks = -(-(n_assign + N_EXPERTS * (MOE_BLOCK - 1)) // MOE_BLOCK)
    rows = jnp.zeros((n_blocks * MOE_BLOCK, d), h.dtype).at[dest].set(hf[tok_sorted])
    block_expert = jnp.minimum(jnp.searchsorted(pad_end, jnp.arange(n_blocks) * MOE_BLOCK, side='right'),
                               N_EXPERTS - 1)

    def run_block(args):
        xb, e = args
        hid = jax.nn.silu(xb @ w_gate_exp[e]) * (xb @ w_up_exp[e])
        return hid @ w_down_exp[e]

    y_rows = lax.map(run_block, (rows.reshape(n_blocks, MOE_BLOCK, d), block_expert)).reshape(-1, d)
    contrib = y_rows[dest] * gw_sorted[:, None].astype(h.dtype)
    return jnp.zeros((t, d), h.dtype).at[tok_sorted].add(contrib).reshape(b, s, d)


def hybrid_layer(x, norm_mix_w, w_in, conv_w, conv_b, dt_bias, a_log, d_skip, ssm_norm_w, w_ssm_proj,
                 q_norm_w, k_norm_w, rel_bias, w_attn_proj, w_out, norm_ffn_w, w_coarse, b_coarse,
                 w_fine, b_fine, w_gate_exp, w_up_exp, w_down_exp):
    h = rmsnorm(x, norm_mix_w)
    proj = h @ w_in
    split_points = np.cumsum(IN_SPLITS)[:-1].tolist()
    z, xbc, dt_raw, q, k, v, g_ssm, g_attn = jnp.split(proj, split_points, axis=-1)
    y_ssm = ssd_mixer(z, xbc, dt_raw, conv_w, conv_b, dt_bias, a_log, d_skip, ssm_norm_w) @ w_ssm_proj
    y_attn = dilated_attention_mixer(q, k, v, q_norm_w, k_norm_w, rel_bias) @ w_attn_proj
    merged = jax.nn.sigmoid(g_ssm) * y_ssm + jax.nn.sigmoid(g_attn) * y_attn
    x = x + merged @ w_out
    x = x + hierarchical_moe(rmsnorm(x, norm_ffn_w), w_coarse, b_coarse, w_fine, b_fine,
                             w_gate_exp, w_up_exp, w_down_exp)
    return x


def setup_inputs(seed: int = 0) -> dict:
    key = jax.random.key(seed)
    ks = jax.random.split(key, 23)
    f32 = jnp.float32

    def nrm(k, shape, scale):
        return jax.random.normal(k, shape, f32) * scale

    def gain(k, shape):
        return 1.0 + 0.02 * jax.random.normal(k, shape, f32)

    dt_init = jnp.exp(jax.random.uniform(ks[5], (DEPTH, SSM_HEADS), f32, math.log(1e-3), math.log(1e-1)))
    return {
        'x': nrm(ks[0], (BATCH, SEQ, D_MODEL), 1.0),
        'norm_mix_w': gain(ks[1], (DEPTH, D_MODEL)),
        'w_in': nrm(ks[2], (DEPTH, D_MODEL, IN_TOTAL), D_MODEL ** -0.5),
        'conv_w': nrm(ks[3], (DEPTH, CONV_K, CONV_DIM), CONV_K ** -0.5),
        'conv_b': nrm(ks[4], (DEPTH, CONV_DIM), 0.01),
        'dt_bias': dt_init + jnp.log(-jnp.expm1(-dt_init)),
        'a_log': jnp.log(jax.random.uniform(ks[6], (DEPTH, SSM_HEADS), f32, 1.0, 16.0)),
        'd_skip': gain(ks[7], (DEPTH, SSM_HEADS)),
        'ssm_norm_w': gain(ks[8], (DEPTH, D_INNER)),
        'w_ssm_proj': nrm(ks[9], (DEPTH, D_INNER, D_MODEL), D_INNER ** -0.5),
        'q_norm_w': gain(ks[10], (DEPTH, ATTN_HEAD_DIM)),
        'k_norm_w': gain(ks[11], (DEPTH, ATTN_HEAD_DIM)),
        'rel_bias': nrm(ks[12], (NUM_BUCKETS, ATTN_HEADS), 0.5),
        'w_attn_proj': nrm(ks[13], (DEPTH, ATTN_OUT_WIDTH, D_MODEL), ATTN_OUT_WIDTH ** -0.5),
        'w_out': nrm(ks[14], (DEPTH, D_MODEL, D_MODEL), D_MODEL ** -0.5),
        'norm_ffn_w': gain(ks[15], (DEPTH, D_MODEL)),
        'w_coarse': nrm(ks[16], (DEPTH, D_MODEL, N_EXPERT_GROUPS), D_MODEL ** -0.5),
        'b_coarse': nrm(ks[17], (DEPTH, N_EXPERT_GROUPS), 0.01),
        'w_fine': nrm(ks[18], (DEPTH, D_MODEL, N_EXPERTS), D_MODEL ** -0.5),
        'b_fine': nrm(ks[19], (DEPTH, N_EXPERTS), 0.01),
        'w_gate_exp': nrm(ks[20], (DEPTH, N_EXPERTS, D_MODEL, D_EXPERT), D_MODEL ** -0.5),
        'w_up_exp': nrm(ks[21], (DEPTH, N_EXPERTS, D_MODEL, D_EXPERT), D_MODEL ** -0.5),
        'w_down_exp': nrm(ks[22], (DEPTH, N_EXPERTS, D_EXPERT, D_MODEL), D_EXPERT ** -0.5),
    }


def reference(x, norm_mix_w, w_in, conv_w, conv_b, dt_bias, a_log, d_skip, ssm_norm_w, w_ssm_proj,
              q_norm_w, k_norm_w, rel_bias, w_attn_proj, w_out, norm_ffn_w, w_coarse, b_coarse,
              w_fine, b_fine, w_gate_exp, w_up_exp, w_down_exp):
    for layer in range(DEPTH):
        x = hybrid_layer(x, norm_mix_w[layer], w_in[layer], conv_w[layer], conv_b[layer], dt_bias[layer],
                         a_log[layer], d_skip[layer], ssm_norm_w[layer], w_ssm_proj[layer],
                         q_norm_w[layer], k_norm_w[layer], rel_bias, w_attn_proj[layer], w_out[layer],
                         norm_ffn_w[layer], w_coarse[layer], b_coarse[layer], w_fine[layer], b_fine[layer],
                         w_gate_exp[layer], w_up_exp[layer], w_down_exp[layer])
    return x
```

```python
import functools
import math

import jax
import jax.numpy as jnp
import numpy as np
from jax import lax
from jax.experimental import pallas as pl
from jax.experimental.pallas import tpu as pltpu

F32 = jnp.float32
BF16 = jnp.bfloat16
HIGHEST = lax.Precision.HIGHEST

LANES = 128
NORM_EPS = 1e-6
NEG_BIG = -1e30

D_MODEL = 1024
D_INNER = 2048
SSM_HEAD_DIM = 64
SSM_HEADS = 32
SSM_GROUPS = 2
D_STATE = 128
CONV_K = 4
BC_WIDTH = 2 * SSM_GROUPS * D_STATE
SSD_CHUNK = 128
ATTN_HEAD_DIM = 64
DILATED_CONFIGS = ((128, 1), (512, 4), (2048, 16))
HEADS_PER_GROUP = 8
GROUP_WIDTH = HEADS_PER_GROUP * ATTN_HEAD_DIM
ATTN_WIDTH = 3 * GROUP_WIDTH
ATTN_BLK = 128
NUM_BUCKETS = 32
MAX_DISTANCE = 2048
N_EXPERT_GROUPS = 8
EXPERTS_PER_GROUP = 8
N_EXPERTS = 64
D_EXPERT = 512
MOE_BLOCK = 128

COL_Z = 0
COL_XS = D_INNER
COL_BC = 2 * D_INNER
COL_Q = COL_BC + BC_WIDTH
COL_K = COL_Q + ATTN_WIDTH
COL_V = COL_K + ATTN_WIDTH
COL_GS = COL_V + ATTN_WIDTH
COL_GA = COL_GS + D_MODEL
COL_DT = COL_GA + D_MODEL
PROJ_WIDTH = COL_DT + 512

VMEM_LIMIT = 56 * 1024 * 1024


def _sigmoid(x):
    return 1.0 / (1.0 + jnp.exp(-x))


def _cparams(n_axes):
    return pltpu.CompilerParams(dimension_semantics=("arbitrary",) * n_axes,
                                vmem_limit_bytes=VMEM_LIMIT)


def _in_proj_body(x_ref, nw_ref, w_ref, o_ref, h_scr):
    @pl.when(pl.program_id(1) == 0)
    def _():
        x = x_ref[...]
        ms = jnp.mean(x * x, axis=-1, keepdims=True)
        h_scr[...] = (x * lax.rsqrt(ms + NORM_EPS) * nw_ref[...]).astype(BF16)

    o_ref[...] = jnp.dot(h_scr[...], w_ref[...], preferred_element_type=F32)


def _in_proj(x2d, norm_w, w):
    t, d = x2d.shape
    n = w.shape[1]
    tm = min(1024, t)
    tn = 512
    return pl.pallas_call(
        _in_proj_body,
        grid=(t // tm, n // tn),
        in_specs=[pl.BlockSpec((tm, d), lambda i, j: (i, 0)),
                  pl.BlockSpec((1, d), lambda i, j: (0, 0)),
                  pl.BlockSpec((d, tn), lambda i, j: (0, j))],
        out_specs=pl.BlockSpec((tm, tn), lambda i, j: (i, j)),
        out_shape=jax.ShapeDtypeStruct((t, n), F32),
        scratch_shapes=[pltpu.VMEM((tm, d), BF16)],
        compiler_params=_cparams(2),
    )(x2d, norm_w, w)


def _conv_silu(x, tail, w_ref, b_ref):
    b = b_ref[...]
    w3 = w_ref[3:4, :]
    acc = b + w3 * x
    xc = jnp.concatenate([tail, x[0:8]], axis=0)
    hacc = b + w3 * xc
    for k in range(CONV_K - 1):
        s = CONV_K - 1 - k
        wk = w_ref[k:k + 1, :]
        acc = acc + wk * pltpu.roll(x, s, 0)
        hacc = hacc + wk * pltpu.roll(xc, s, 0)
    out = jnp.concatenate([hacc[8:16], acc[8:]], axis=0)
    return out * _sigmoid(out)


def _ssd_body(z_ref, xs_ref, bc_ref, dt_ref, cwx_ref, cbx_ref, cwb_ref, cbb_ref, dtb_ref, alog_ref,
              dskip_ref, nw_ref, e_ref, y_ref, state, tail_x, tail_bc):
    L = SSD_CHUNK
    half = D_INNER // SSM_GROUPS

    @pl.when(pl.program_id(1) == 0)
    def _():
        state[...] = jnp.zeros_like(state)
        tail_x[...] = jnp.zeros_like(tail_x)
        tail_bc[...] = jnp.zeros_like(tail_bc)

    xs_raw = xs_ref[0]
    bc_raw = bc_ref[0]
    xs = _conv_silu(xs_raw, tail_x[...], cwx_ref, cbx_ref)
    bc = _conv_silu(bc_raw, tail_bc[...], cwb_ref, cbb_ref)
    tail_x[...] = xs_raw[L - 8:L]
    tail_bc[...] = bc_raw[L - 8:L]

    lane = lax.broadcasted_iota(jnp.int32, (L, LANES), 1)
    row = lax.broadcasted_iota(jnp.int32, (L, L), 0)
    col = lax.broadcasted_iota(jnp.int32, (L, L), 1)
    causal = row >= col

    v = dt_ref[0] + dtb_ref[...]
    dt = jnp.maximum(v, 0.0) + jnp.log1p(jnp.exp(-jnp.abs(v)))
    head_lane = lane < SSM_HEADS
    dt = jnp.where(head_lane, dt, 0.0)
    adt = dt * (-jnp.exp(alog_ref[...]))
    a_cs = jnp.dot(causal.astype(F32), adt, precision=HIGHEST, preferred_element_type=F32)
    a_cs_t = a_cs.T

    expand = e_ref[...]
    a_full = jnp.dot(a_cs, expand, precision=HIGHEST, preferred_element_type=F32)
    dt_full = jnp.dot(dt, expand, precision=HIGHEST, preferred_element_type=F32)
    a_tot = a_full[L - 1:L, :]
    decay_from_start = jnp.exp(a_full)
    decay_to_end = jnp.exp(a_tot - a_full)
    decay_chunk = jnp.exp(a_tot)

    xdt = xs * dt_full
    xdt_b = xdt.astype(BF16)
    xw_b = (xdt * decay_to_end).astype(BF16)

    y_groups = []
    for g in range(SSM_GROUPS):
        gs = slice(g * half, (g + 1) * half)
        bg = bc[:, g * D_STATE:(g + 1) * D_STATE]
        cg = bc[:, (SSM_GROUPS + g) * D_STATE:(SSM_GROUPS + g + 1) * D_STATE]
        cb16 = cg.astype(BF16)
        cb = lax.dot_general(cb16, bg.astype(BF16), (((1,), (1,)), ((), ())),
                             preferred_element_type=F32)
        s_prev = state[:, gs]
        y_off = jnp.dot(cb16, s_prev.astype(BF16), preferred_element_type=F32) * decay_from_start[:, gs]

        pair_cols = []
        for pr in range(half // LANES):
            h0 = g * (SSM_HEADS // SSM_GROUPS) + 2 * pr
            xp = xdt_b[:, h0 * SSM_HEAD_DIM:(h0 + 2) * SSM_HEAD_DIM]
            ys = []
            for h in (h0, h0 + 1):
                seg = a_cs[:, h:h + 1] - a_cs_t[h:h + 1, :]
                m = (cb * jnp.exp(jnp.where(causal, seg, NEG_BIG))).astype(BF16)
                ys.append(jnp.dot(m, xp, preferred_element_type=F32))
            pair_cols.append(jnp.where(lane < SSM_HEAD_DIM, ys[0], ys[1]))
        y_diag = jnp.concatenate(pair_cols, axis=1)

        state[:, gs] = decay_chunk[:, gs] * s_prev + jnp.dot(
            bg.T.astype(BF16), xw_b[:, gs], preferred_element_type=F32)
        y_groups.append(y_diag + y_off)

    y = jnp.concatenate(y_groups, axis=1) + dskip_ref[...] * xs
    z = z_ref[0]
    y = y * (z * _sigmoid(z))
    normed = []
    for g in range(SSM_GROUPS):
        yg = y[:, g * half:(g + 1) * half]
        ms = jnp.mean(yg * yg, axis=-1, keepdims=True)
        normed.append(yg * lax.rsqrt(ms + NORM_EPS))
    y_ref[0] = jnp.concatenate(normed, axis=1) * nw_ref[...]


def _ssd(proj3, conv_w, conv_b, dt_bias, a_log, d_skip, ssm_norm_w):
    b, s, _ = proj3.shape
    L = SSD_CHUNK
    pad = LANES - SSM_HEADS
    cwx, cwb = conv_w[:, :D_INNER], conv_w[:, D_INNER:]
    cbx, cbb = conv_b[None, :D_INNER], conv_b[None, D_INNER:]
    dtb = jnp.pad(dt_bias, (0, pad))[None]
    alog = jnp.pad(a_log, (0, pad))[None]
    dskip = jnp.repeat(d_skip, SSM_HEAD_DIM)[None]
    expand = (np.arange(LANES)[:, None] == np.arange(D_INNER)[None, :] // SSM_HEAD_DIM).astype(np.float32)

    def const(shape):
        return pl.BlockSpec(shape, lambda i, c: (0,) * len(shape))

    return pl.pallas_call(
        _ssd_body,
        grid=(b, s // L),
        in_specs=[pl.BlockSpec((1, L, D_INNER), lambda i, c: (i, c, COL_Z // D_INNER)),
                  pl.BlockSpec((1, L, D_INNER), lambda i, c: (i, c, COL_XS // D_INNER)),
                  pl.BlockSpec((1, L, BC_WIDTH), lambda i, c: (i, c, COL_BC // BC_WIDTH)),
                  pl.BlockSpec((1, L, LANES), lambda i, c: (i, c, COL_DT // LANES)),
                  const((CONV_K, D_INNER)), const((1, D_INNER)),
                  const((CONV_K, BC_WIDTH)), const((1, BC_WIDTH)),
                  const((1, LANES)), const((1, LANES)),
                  const((1, D_INNER)), const((1, D_INNER)),
                  const((LANES, D_INNER))],
        out_specs=pl.BlockSpec((1, L, D_INNER), lambda i, c: (i, c, 0)),
        out_shape=jax.ShapeDtypeStruct((b, s, D_INNER), F32),
        scratch_shapes=[pltpu.VMEM((D_STATE, D_INNER), F32),
                        pltpu.VMEM((8, D_INNER), F32),
                        pltpu.VMEM((8, BC_WIDTH), F32)],
        compiler_params=_cparams(2),
    )(proj3, proj3, proj3, proj3, cwx, cbx, cwb, cbb, dtb, alog, dskip, ssm_norm_w[None],
      jnp.asarray(expand))


def _t5_causal_bucket(dist):
    max_exact = NUM_BUCKETS // 2
    large = max_exact + (np.log(np.maximum(dist, max_exact) / max_exact)
                         / math.log(MAX_DISTANCE / max_exact) * (NUM_BUCKETS - max_exact)).astype(np.int32)
    return np.where(dist < max_exact, dist, np.minimum(large, NUM_BUCKETS - 1)).astype(np.int32)


def _band_bias(rel_bias_group, dilation):
    blk = ATTN_BLK
    off = np.arange(blk)[:, None] + blk - np.arange(2 * blk)[None, :]
    in_win = (off >= 0) & (off <= blk)
    bucket = _t5_causal_bucket(np.clip(off, 0, None) * dilation)
    bias = jnp.transpose(rel_bias_group[bucket], (2, 0, 1)).astype(F32)
    bias = jnp.where(in_win[None], bias, NEG_BIG)
    return bias.reshape(HEADS_PER_GROUP // 2, 2 * blk, 2 * blk)


def _attn_body(q_ref, k_ref, v_ref, bias_ref, qw_ref, kw_ref, bd_ref, o_ref, lse_ref, kbuf, vbuf):
    blk = ATTN_BLK
    n = pl.program_id(2)

    @pl.when(n == 0)
    def _():
        kbuf[...] = jnp.zeros_like(kbuf)
        vbuf[...] = jnp.zeros_like(vbuf)

    bd = bd_ref[...]

    def head_norm(x, w_ref):
        xx = x * x
        hi = xx.astype(BF16)
        lo = (xx - hi.astype(F32)).astype(BF16)
        ss = (jnp.dot(hi, bd, preferred_element_type=F32)
              + jnp.dot(lo, bd, preferred_element_type=F32))
        return x * lax.rsqrt(ss * (1.0 / ATTN_HEAD_DIM) + NORM_EPS) * w_ref[...]

    q = head_norm(q_ref[0], qw_ref) * (ATTN_HEAD_DIM ** -0.5)
    k = head_norm(k_ref[0], kw_ref).astype(BF16)
    v = v_ref[0].astype(BF16)
    slot = n % 2
    kbuf[slot] = k
    vbuf[slot] = v

    lane = lax.broadcasted_iota(jnp.int32, (blk, LANES), 1)
    first_pen = jnp.where(n == 0, NEG_BIG, 0.0)
    lo = lane < ATTN_HEAD_DIM
    nt = (((1,), (1,)), ((), ()))

    lse_tile = jnp.zeros((blk, LANES), F32)
    outs = []
    for p in range(HEADS_PER_GROUP // 2):
        ls = slice(p * LANES, (p + 1) * LANES)
        qp = q[:, ls]
        q2 = jnp.concatenate([jnp.where(lo, qp, 0.0), jnp.where(lo, 0.0, qp)], axis=0).astype(BF16)
        s_prev = lax.dot_general(q2, kbuf[1 - slot, :, ls], nt, preferred_element_type=F32)
        s_cur = lax.dot_general(q2, k[:, ls], nt, preferred_element_type=F32)
        s_prev = s_prev + bias_ref[p, :, 0:blk] + first_pen
        s_cur = s_cur + bias_ref[p, :, blk:2 * blk]
        m = jnp.maximum(jnp.max(s_prev, axis=-1, keepdims=True), jnp.max(s_cur, axis=-1, keepdims=True))
        e_prev = jnp.exp(s_prev - m)
        e_cur = jnp.exp(s_cur - m)
        d = jnp.sum(e_prev, axis=-1, keepdims=True) + jnp.sum(e_cur, axis=-1, keepdims=True)
        pv = (jnp.dot(e_prev.astype(BF16), vbuf[1 - slot, :, ls], preferred_element_type=F32)
              + jnp.dot(e_cur.astype(BF16), v[:, ls], preferred_element_type=F32)) / d
        outs.append(jnp.where(lo, pv[0:blk], pv[blk:2 * blk]))
        lse = m + jnp.log(d)
        lse_tile = jnp.where(lane == 2 * p, lse[0:blk], lse_tile)
        lse_tile = jnp.where(lane == 2 * p + 1, lse[blk:2 * blk], lse_tile)
    o_ref[0] = jnp.concatenate(outs, axis=1)
    lse_ref[0] = lse_tile


def _attn_group(proj3, gi, dilation, bias, qw, kw, bd):
    b, s, _ = proj3.shape
    r = dilation
    blk = ATTN_BLK
    nb = s // (blk * r)
    pv = proj3.reshape(b, s // r, r * PROJ_WIDTH)
    per_res = PROJ_WIDTH // GROUP_WIDTH

    def col(base):
        cb = base // GROUP_WIDTH + gi
        return lambda i, res, n: (i, n, res * per_res + cb)

    out, lse = pl.pallas_call(
        _attn_body,
        grid=(b, r, nb),
        in_specs=[pl.BlockSpec((1, blk, GROUP_WIDTH), col(COL_Q)),
                  pl.BlockSpec((1, blk, GROUP_WIDTH), col(COL_K)),
                  pl.BlockSpec((1, blk, GROUP_WIDTH), col(COL_V)),
                  pl.BlockSpec((HEADS_PER_GROUP // 2, 2 * blk, 2 * blk), lambda i, res, n: (0, 0, 0)),
                  pl.BlockSpec((1, GROUP_WIDTH), lambda i, res, n: (0, 0)),
                  pl.BlockSpec((1, GROUP_WIDTH), lambda i, res, n: (0, 0)),
                  pl.BlockSpec((GROUP_WIDTH, GROUP_WIDTH), lambda i, res, n: (0, 0))],
        out_specs=[pl.BlockSpec((1, blk, GROUP_WIDTH), lambda i, res, n: (i, n, res)),
                   pl.BlockSpec((1, blk, LANES), lambda i, res, n: (i, n, res))],
        out_shape=[jax.ShapeDtypeStruct((b, s // r, r * GROUP_WIDTH), F32),
                   jax.ShapeDtypeStruct((b, s // r, r * LANES), F32)],
        scratch_shapes=[pltpu.VMEM((2, blk, GROUP_WIDTH), BF16),
                        pltpu.VMEM((2, blk, GROUP_WIDTH), BF16)],
        compiler_params=_cparams(3),
    )(pv, pv, pv, bias, qw, kw, bd)
    return out.reshape(b * s, GROUP_WIDTH), lse.reshape(b * s, LANES)


def _mix_body(y_ref, o0_ref, o1_ref, o2_ref, l0_ref, l1_ref, l2_ref, gs_ref, ga_ref, x_ref,
              wssm_ref, wattn_ref, wout_ref, e8_ref, nfw_ref, wr_ref, br_ref,
              x2_ref, h2_ref, lg_ref):
    tm = x_ref.shape[0]
    lses = [l0_ref[...], l1_ref[...], l2_ref[...]]
    outs = [o0_ref, o1_ref, o2_ref]
    mx = jnp.maximum(jnp.maximum(lses[0], lses[1]), lses[2])
    es = [jnp.exp(l - mx) for l in lses]
    inv = 1.0 / (es[0] + es[1] + es[2])
    e8 = e8_ref[...]
    att = jnp.zeros((tm, GROUP_WIDTH), F32)
    for g in range(3):
        w = es[g] * inv
        w_hi = w.astype(BF16)
        w_lo = (w - w_hi.astype(F32)).astype(BF16)
        w_full = (jnp.dot(w_hi, e8, preferred_element_type=F32)
                  + jnp.dot(w_lo, e8, preferred_element_type=F32))
        att = att + w_full * outs[g][...]
    y_attn = jnp.dot(att.astype(BF16), wattn_ref[...], preferred_element_type=F32)
    y_ssm = jnp.dot(y_ref[...].astype(BF16), wssm_ref[...], preferred_element_type=F32)
    merged = _sigmoid(gs_ref[...]) * y_ssm + _sigmoid(ga_ref[...]) * y_attn
    x2 = x_ref[...] + jnp.dot(merged.astype(BF16), wout_ref[...], preferred_element_type=F32)
    x2_ref[...] = x2
    ms = jnp.mean(x2 * x2, axis=-1, keepdims=True)
    h2 = x2 * lax.rsqrt(ms + NORM_EPS) * nfw_ref[...]
    h2_ref[...] = h2.reshape(tm, 1, D_MODEL)
    lg_ref[...] = jnp.dot(h2.astype(BF16), wr_ref[...], preferred_element_type=F32) + br_ref[...]


def _mix_out(y_ssm, attn_outs, attn_lses, proj, x2d, w_ssm, w_attn, w_out, norm_ffn_w, w_router, b_router):
    t = x2d.shape[0]
    tm = min(256, t)
    e8 = (np.arange(LANES)[:, None] == np.arange(GROUP_WIDTH)[None, :] // ATTN_HEAD_DIM)
    e8 = jnp.asarray(e8.astype(np.float32), dtype=BF16)

    def rows(width, cb=0):
        return pl.BlockSpec((tm, width), lambda i: (i, cb))

    def const(shape):
        return pl.BlockSpec(shape, lambda i: (0,) * len(shape))

    return pl.pallas_call(
        _mix_body,
        grid=(t // tm,),
        in_specs=[rows(D_INNER), rows(GROUP_WIDTH), rows(GROUP_WIDTH), rows(GROUP_WIDTH),
                  rows(LANES), rows(LANES), rows(LANES),
                  rows(D_MODEL, COL_GS // D_MODEL), rows(D_MODEL, COL_GA // D_MODEL), rows(D_MODEL),
                  const((D_INNER, D_MODEL)), const((GROUP_WIDTH, D_MODEL)), const((D_MODEL, D_MODEL)),
                  const((LANES, GROUP_WIDTH)), const((1, D_MODEL)),
                  const((D_MODEL, LANES)), const((1, LANES))],
        out_specs=[rows(D_MODEL),
                   pl.BlockSpec((tm, 1, D_MODEL), lambda i: (i, 0, 0)),
                   rows(LANES)],
        out_shape=[jax.ShapeDtypeStruct((t, D_MODEL), F32),
                   jax.ShapeDtypeStruct((t, 1, D_MODEL), F32),
                   jax.ShapeDtypeStruct((t, LANES), F32)],
        compiler_params=_cparams(1),
    )(y_ssm, *attn_outs, *attn_lses, proj, proj, x2d, w_ssm, w_attn, w_out, e8,
      norm_ffn_w[None], w_router, b_router)


def _route_body(lg_ref, dest_ref, gate_ref, cnt_ref, counts, pstart):
    tm = lg_ref.shape[0]
    ps = pl.program_id(0)
    i = pl.program_id(1)

    @pl.when((ps == 0) & (i == 0))
    def _():
        counts[...] = jnp.zeros_like(counts)
        pstart[...] = jnp.zeros_like(pstart)

    @pl.when((ps == 1) & (i == 0))
    def _():
        cnt = counts[...]
        padded = jnp.floor((cnt + (MOE_BLOCK - 1)) * (1.0 / MOE_BLOCK)) * MOE_BLOCK
        a = lax.broadcasted_iota(jnp.int32, (LANES, LANES), 0)
        bcol = lax.broadcasted_iota(jnp.int32, (LANES, LANES), 1)
        pad_end = jnp.dot(padded, (a <= bcol).astype(F32), precision=HIGHEST, preferred_element_type=F32)
        pstart[...] = pad_end - padded
        cnt_ref[...] = cnt
        counts[...] = jnp.zeros_like(counts)

    lg = lg_ref[...]
    lane = lax.broadcasted_iota(jnp.int32, (tm, LANES), 1)
    is_coarse = lane < N_EXPERT_GROUPS
    cmax = jnp.max(jnp.where(is_coarse, lg, NEG_BIG), axis=-1, keepdims=True)
    grp = jnp.min(jnp.where(is_coarse & (lg == cmax), lane, LANES), axis=-1, keepdims=True)
    group_p = 1.0 / jnp.sum(jnp.where(is_coarse, jnp.exp(lg - cmax), 0.0), axis=-1, keepdims=True)
    f_lo = N_EXPERT_GROUPS + EXPERTS_PER_GROUP * grp
    in_grp = (lane >= f_lo) & (lane < f_lo + EXPERTS_PER_GROUP)
    f1 = jnp.max(jnp.where(in_grp, lg, NEG_BIG), axis=-1, keepdims=True)
    i1 = jnp.min(jnp.where(in_grp & (lg == f1), lane, LANES), axis=-1, keepdims=True)
    rest = in_grp & (lane != i1)
    f2 = jnp.max(jnp.where(rest, lg, NEG_BIG), axis=-1, keepdims=True)
    i2 = jnp.min(jnp.where(rest & (lg == f2), lane, LANES), axis=-1, keepdims=True)
    e2 = jnp.exp(f2 - f1)
    g1 = group_p / (1.0 + e2)
    g2 = group_p * e2 / (1.0 + e2)

    oh1 = lane == i1 - N_EXPERT_GROUPS
    oh2 = lane == i2 - N_EXPERT_GROUPS
    onehot = jnp.where(oh1 | oh2, 1.0, 0.0)
    r = lax.broadcasted_iota(jnp.int32, (tm, tm), 0)
    c = lax.broadcasted_iota(jnp.int32, (tm, tm), 1)
    before = jnp.dot((r > c).astype(BF16), onehot.astype(BF16), preferred_element_type=F32)
    pos = pstart[0:1, :] + counts[0:1, :] + before
    d1 = jnp.sum(jnp.where(oh1, pos, 0.0), axis=-1, keepdims=True)
    d2 = jnp.sum(jnp.where(oh2, pos, 0.0), axis=-1, keepdims=True)
    counts[...] = counts[...] + jnp.sum(onehot, axis=0, keepdims=True)
    dest_ref[...] = jnp.where(lane == 0, d1, jnp.where(lane == 1, d2, 0.0)).astype(jnp.int32)
    gate_ref[...] = jnp.where(lane == 0, g1, jnp.where(lane == 1, g2, 0.0))


def _route(logits):
    t = logits.shape[0]
    tm = min(512, t)
    return pl.pallas_call(
        _route_body,
        grid=(2, t // tm),
        in_specs=[pl.BlockSpec((tm, LANES), lambda ps, i: (i, 0))],
        out_specs=[pl.BlockSpec((tm, LANES), lambda ps, i: (i * ps, 0)),
                   pl.BlockSpec((tm, LANES), lambda ps, i: (i * ps, 0)),
                   pl.BlockSpec((8, LANES), lambda ps, i: (0, 0))],
        out_shape=[jax.ShapeDtypeStruct((t, LANES), jnp.int32),
                   jax.ShapeDtypeStruct((t, LANES), F32),
                   jax.ShapeDtypeStruct((8, LANES), F32)],
        scratch_shapes=[pltpu.VMEM((8, LANES), F32), pltpu.VMEM((8, LANES), F32)],
        compiler_params=_cparams(2),
    )(logits)


def _dispatch_body(meta_ref, dest_ref, h_ref, rows_ref, zbuf, zsem, sem):
    tm = h_ref.shape[0]

    def zero_copy(e):
        start = pl.multiple_of(meta_ref[e] - MOE_BLOCK, MOE_BLOCK)
        return pltpu.make_async_copy(zbuf, rows_ref.at[pl.ds(start, MOE_BLOCK)], zsem)

    def tail_copy(blk):
        start = pl.multiple_of(blk * MOE_BLOCK, MOE_BLOCK)
        return pltpu.make_async_copy(zbuf, rows_ref.at[pl.ds(start, MOE_BLOCK)], zsem)

    @pl.when(pl.program_id(0) == 0)
    def _():
        zbuf[...] = jnp.zeros_like(zbuf)

        def start_zero(e, carry):
            @pl.when(meta_ref[N_EXPERTS + e] > 0)
            def _():
                zero_copy(e).start()
            return carry

        def wait_zero(e, carry):
            @pl.when(meta_ref[N_EXPERTS + e] > 0)
            def _():
                zero_copy(e).wait()
            return carry

        def start_tail(blk, carry):
            tail_copy(blk).start()
            return carry

        def wait_tail(blk, carry):
            tail_copy(blk).wait()
            return carry

        n_used = meta_ref[2 * N_EXPERTS]
        n_blocks = rows_ref.shape[0] // MOE_BLOCK
        lax.fori_loop(0, N_EXPERTS, start_zero, 0)
        lax.fori_loop(n_used, n_blocks, start_tail, 0)
        lax.fori_loop(0, N_EXPERTS, wait_zero, 0)
        lax.fori_loop(n_used, n_blocks, wait_tail, 0)

    def issue(j, carry):
        for k in range(2):
            d = dest_ref[0, 2 * j + k]
            pltpu.make_async_copy(h_ref.at[j], rows_ref.at[d], sem).start()
        return carry

    lax.fori_loop(0, tm, issue, 0, unroll=8)
    for _ in range(2):
        pltpu.make_async_copy(h_ref, rows_ref.at[pl.ds(0, tm)], sem).wait()


def _dispatch(meta, dest3, h2, n_rows):
    t = h2.shape[0]
    tm = dest3.shape[2] // 2
    return pl.pallas_call(
        _dispatch_body,
        grid_spec=pltpu.PrefetchScalarGridSpec(
            num_scalar_prefetch=1,
            grid=(t // tm,),
            in_specs=[pl.BlockSpec((None, 1, 2 * tm), lambda i, m: (i, 0, 0), memory_space=pltpu.SMEM),
                      pl.BlockSpec((tm, 1, D_MODEL), lambda i, m: (i, 0, 0))],
            out_specs=pl.BlockSpec(memory_space=pl.ANY),
            scratch_shapes=[pltpu.VMEM((MOE_BLOCK, 1, D_MODEL), F32),
                            pltpu.SemaphoreType.DMA(()),
                            pltpu.SemaphoreType.DMA(())]),
        out_shape=jax.ShapeDtypeStruct((n_rows, 1, D_MODEL), F32),
        compiler_params=_cparams(1),
    )(meta, dest3, h2)


def _expert_body(be_ref, nu_ref, x_ref, wg_ref, wu_ref, wd_ref, y_ref, x2d, wg_b, wu_b, wd_b):
    blk = pl.program_id(0)

    @pl.when(blk < nu_ref[0])
    def _():
        prev = be_ref[jnp.maximum(blk - 1, 0)]

        @pl.when((blk == 0) | (be_ref[blk] != prev))
        def _():
            wg_b[...] = wg_ref[0].astype(BF16)
            wu_b[...] = wu_ref[0].astype(BF16)
            wd_b[...] = wd_ref[0].astype(BF16)

        x2d[...] = x_ref[...].reshape(MOE_BLOCK, D_MODEL)
        x = x2d[...].astype(BF16)
        g = jnp.dot(x, wg_b[...], preferred_element_type=F32)
        u = jnp.dot(x, wu_b[...], preferred_element_type=F32)
        hid = (g * _sigmoid(g)) * u
        y = jnp.dot(hid.astype(BF16), wd_b[...], preferred_element_type=F32)
        y_ref[...] = y.reshape(MOE_BLOCK, 1, D_MODEL)

    @pl.when(blk >= nu_ref[0])
    def _():
        y_ref[...] = jnp.zeros_like(y_ref)


def _experts(block_expert, n_used, rows, w_gate, w_up, w_down):
    n_rows = rows.shape[0]
    n_blocks = n_rows // MOE_BLOCK

    def row_map(b, be, nu):
        return (b, 0, 0)

    def w_map(b, be, nu):
        return (be[jnp.minimum(b, nu[0] - 1)], 0, 0)

    return pl.pallas_call(
        _expert_body,
        grid_spec=pltpu.PrefetchScalarGridSpec(
            num_scalar_prefetch=2,
            grid=(n_blocks,),
            in_specs=[pl.BlockSpec((MOE_BLOCK, 1, D_MODEL), row_map),
                      pl.BlockSpec((1, D_MODEL, D_EXPERT), w_map),
                      pl.BlockSpec((1, D_MODEL, D_EXPERT), w_map),
                      pl.BlockSpec((1, D_EXPERT, D_MODEL), w_map)],
            out_specs=pl.BlockSpec((MOE_BLOCK, 1, D_MODEL), row_map),
            scratch_shapes=[pltpu.VMEM((MOE_BLOCK, D_MODEL), F32),
                            pltpu.VMEM((D_MODEL, D_EXPERT), BF16),
                            pltpu.VMEM((D_MODEL, D_EXPERT), BF16),
                            pltpu.VMEM((D_EXPERT, D_MODEL), BF16)]),
        out_shape=jax.ShapeDtypeStruct((n_rows, 1, D_MODEL), F32),
        compiler_params=_cparams(1),
    )(block_expert, n_used, rows, w_gate, w_up, w_down)


def _combine_body(dcur_ref, dnext_ref, gate_ref, x2_ref, y_ref, o_ref, buf_a, buf_b, y2d, sem_a, sem_b):
    tm = x2_ref.shape[0]
    i = pl.program_id(0)
    n = pl.num_programs(0)

    def issue(dref, buf, sem):
        def body(j, carry):
            for k in range(2):
                d = dref[0, 2 * j + k]
                pltpu.make_async_copy(y_ref.at[d], buf.at[k * tm + j], sem).start()
            return carry
        lax.fori_loop(0, tm, body, 0, unroll=8)

    def finish(buf, sem):
        pltpu.make_async_copy(y_ref.at[pl.ds(0, 2 * tm)], buf, sem).wait()
        y2d[...] = buf[...].reshape(2 * tm, D_MODEL)
        g = gate_ref[...]
        o_ref[...] = x2_ref[...] + g[:, 0:1] * y2d[0:tm, :] + g[:, 1:2] * y2d[tm:2 * tm, :]

    @pl.when(i == 0)
    def _():
        issue(dcur_ref, buf_a, sem_a)

    for par, (cur, cur_sem, nxt, nxt_sem) in enumerate(((buf_a, sem_a, buf_b, sem_b),
                                                        (buf_b, sem_b, buf_a, sem_a))):
        @pl.when(i % 2 == par)
        def _(cur=cur, cur_sem=cur_sem, nxt=nxt, nxt_sem=nxt_sem):
            @pl.when(i + 1 < n)
            def _():
                issue(dnext_ref, nxt, nxt_sem)
            finish(cur, cur_sem)


def _combine(dest3, gates, x2, y_rows):
    t = x2.shape[0]
    tm = dest3.shape[2] // 2
    nt = t // tm
    return pl.pallas_call(
        _combine_body,
        grid=(nt,),
        in_specs=[pl.BlockSpec((None, 1, 2 * tm), lambda i: (i, 0, 0), memory_space=pltpu.SMEM),
                  pl.BlockSpec((None, 1, 2 * tm), lambda i: (jnp.minimum(i + 1, nt - 1), 0, 0),
                               memory_space=pltpu.SMEM),
                  pl.BlockSpec((tm, LANES), lambda i: (i, 0)),
                  pl.BlockSpec((tm, D_MODEL), lambda i: (i, 0)),
                  pl.BlockSpec(memory_space=pl.ANY)],
        out_specs=pl.BlockSpec((tm, D_MODEL), lambda i: (i, 0)),
        out_shape=jax.ShapeDtypeStruct((t, D_MODEL), F32),
        scratch_shapes=[pltpu.VMEM((2 * tm, 1, D_MODEL), F32),
                        pltpu.VMEM((2 * tm, 1, D_MODEL), F32),
                        pltpu.VMEM((2 * tm, D_MODEL), F32),
                        pltpu.SemaphoreType.DMA(()),
                        pltpu.SemaphoreType.DMA(())],
        compiler_params=_cparams(1),
    )(dest3, dest3, gates, x2, y_rows)


def _layer(x, norm_mix_w, w_in, conv_w, conv_b, dt_bias, a_log, d_skip, ssm_norm_w, w_ssm_proj,
           q_norm_w, k_norm_w, rel_bias, w_attn_proj, w_out, norm_ffn_w, w_coarse, b_coarse,
           w_fine, b_fine, w_gate_exp, w_up_exp, w_down_exp):
    b, s, d = x.shape
    t = b * s
    x2d = x.reshape(t, d)

    dt_lo = D_INNER + D_INNER + BC_WIDTH
    w_packed = jnp.concatenate(
        [w_in[:, :dt_lo], w_in[:, dt_lo + SSM_HEADS:], w_in[:, dt_lo:dt_lo + SSM_HEADS],
         jnp.zeros((d, PROJ_WIDTH - COL_DT - SSM_HEADS), w_in.dtype)], axis=1).astype(BF16)
    proj = _in_proj(x2d, norm_mix_w[None], w_packed)
    proj3 = proj.reshape(b, s, PROJ_WIDTH)

    y_ssm = _ssd(proj3, conv_w, conv_b, dt_bias, a_log, d_skip, ssm_norm_w).reshape(t, D_INNER)

    qw = jnp.tile(q_norm_w, HEADS_PER_GROUP)[None]
    kw = jnp.tile(k_norm_w, HEADS_PER_GROUP)[None]
    lane_head = np.arange(GROUP_WIDTH) // ATTN_HEAD_DIM
    bd = jnp.asarray((lane_head[:, None] == lane_head[None, :]).astype(np.float32), dtype=BF16)
    attn_outs, attn_lses = [], []
    for gi, (window, dilation) in enumerate(DILATED_CONFIGS):
        assert window // dilation == ATTN_BLK and s % window == 0
        bias = _band_bias(rel_bias[:, gi * HEADS_PER_GROUP:(gi + 1) * HEADS_PER_GROUP], dilation)
        o, l = _attn_group(proj3, gi, dilation, bias, qw, kw, bd)
        attn_outs.append(o)
        attn_lses.append(l)

    n_route = N_EXPERT_GROUPS + N_EXPERTS
    w_router = jnp.pad(jnp.concatenate([w_coarse, w_fine], axis=1), ((0, 0), (0, LANES - n_route)))
    b_router = jnp.pad(jnp.concatenate([b_coarse, b_fine]), (0, LANES - n_route))[None]
    x2, h2, logits = _mix_out(y_ssm, attn_outs, attn_lses, proj, x2d, w_ssm_proj.astype(BF16),
                              w_attn_proj.astype(BF16), w_out.astype(BF16), norm_ffn_w,
                              w_router.astype(BF16), b_router)

    dest, gates, counts = _route(logits)

    cnt = counts[0, :N_EXPERTS].astype(jnp.int32)
    padded = (cnt + MOE_BLOCK - 1) // MOE_BLOCK * MOE_BLOCK
    pad_end = jnp.cumsum(padded)
    n_blocks = -(-(2 * t + N_EXPERTS * (MOE_BLOCK - 1)) // MOE_BLOCK)
    block_expert = jnp.minimum(
        jnp.searchsorted(pad_end, jnp.arange(n_blocks, dtype=jnp.int32) * MOE_BLOCK, side='right'),
        N_EXPERTS - 1).astype(jnp.int32)
    n_used = (pad_end[-1:] // MOE_BLOCK).astype(jnp.int32)
    meta = jnp.concatenate([pad_end, padded, n_used]).astype(jnp.int32)

    tm_d = min(256, t)
    dest_d = dest[:, :2].reshape(t // tm_d, 1, 2 * tm_d)
    rows = _dispatch(meta, dest_d, h2, n_blocks * MOE_BLOCK)
    y_rows = _experts(block_expert, n_used, rows, w_gate_exp, w_up_exp, w_down_exp)
    tm_c = min(128, t)
    dest_c = dest[:, :2].reshape(t // tm_c, 1, 2 * tm_c)
    out = _combine(dest_c, gates, x2, y_rows)
    return out.reshape(b, s, d)


def kernel(x, norm_mix_w, w_in, conv_w, conv_b, dt_bias, a_log, d_skip, ssm_norm_w, w_ssm_proj,
           q_norm_w, k_norm_w, rel_bias, w_attn_proj, w_out, norm_ffn_w, w_coarse, b_coarse,
           w_fine, b_fine, w_gate_exp, w_up_exp, w_down_exp):
    depth = norm_mix_w.shape[0]
    for layer in range(depth):
        x = _layer(x, norm_mix_w[layer], w_in[layer], conv_w[layer], conv_b[layer], dt_bias[layer],
                   a_log[layer], d_skip[layer], ssm_norm_w[layer], w_ssm_proj[layer],
                   q_norm_w[layer], k_norm_w[layer], rel_bias, w_attn_proj[layer], w_out[layer],
                   norm_ffn_w[layer], w_coarse[layer], b_coarse[layer], w_fine[layer], b_fine[layer],
                   w_gate_exp[layer], w_up_exp[layer], w_down_exp[layer])
    return x
```

```python
import functools
import math

import jax
import jax.numpy as jnp
import numpy as np
from jax import lax
from jax.experimental import pallas as pl
from jax.experimental.pallas import tpu as pltpu

F32 = jnp.float32
BF16 = jnp.bfloat16
HIGHEST = lax.Precision.HIGHEST

LANES = 128
NORM_EPS = 1e-6
NEG_BIG = -1e30

D_MODEL = 1024
D_INNER = 2048
SSM_HEAD_DIM = 64
SSM_HEADS = 32
SSM_GROUPS = 2
D_STATE = 128
CONV_K = 4
BC_WIDTH = 2 * SSM_GROUPS * D_STATE
SSD_CHUNK = 128
ATTN_HEAD_DIM = 64
DILATED_CONFIGS = ((128, 1), (512, 4), (2048, 16))
HEADS_PER_GROUP = 8
GROUP_WIDTH = HEADS_PER_GROUP * ATTN_HEAD_DIM
ATTN_WIDTH = 3 * GROUP_WIDTH
ATTN_BLK = 128
NUM_BUCKETS = 32
MAX_DISTANCE = 2048
N_EXPERT_GROUPS = 8
EXPERTS_PER_GROUP = 8
N_EXPERTS = 64
D_EXPERT = 512
MOE_BLOCK = 128

COL_Z = 0
COL_XS = D_INNER
COL_BC = 2 * D_INNER
COL_Q = COL_BC + BC_WIDTH
COL_K = COL_Q + ATTN_WIDTH
COL_V = COL_K + ATTN_WIDTH
COL_GS = COL_V + ATTN_WIDTH
COL_GA = COL_GS + D_MODEL
COL_DT = COL_GA + D_MODEL
PROJ_WIDTH = COL_DT + 512

VMEM_LIMIT = 56 * 1024 * 1024


def _sigmoid(x):
    return 1.0 / (1.0 + jnp.exp(-x))


def _cparams(n_axes):
    return pltpu.CompilerParams(dimension_semantics=("arbitrary",) * n_axes,
                                vmem_limit_bytes=VMEM_LIMIT)


def _in_proj_body(x_ref, nw_ref, w_ref, o_ref, h_scr):
    @pl.when(pl.program_id(1) == 0)
    def _():
        x = x_ref[...]
        ms = jnp.mean(x * x, axis=-1, keepdims=True)
        h_scr[...] = (x * lax.rsqrt(ms + NORM_EPS) * nw_ref[...]).astype(BF16)

    o_ref[...] = jnp.dot(h_scr[...], w_ref[...], preferred_element_type=F32)


def _in_proj(x2d, norm_w, w):
    t, d = x2d.shape
    n = w.shape[1]
    tm = min(1024, t)
    tn = 512
    return pl.pallas_call(
        _in_proj_body,
        grid=(t // tm, n // tn),
        in_specs=[pl.BlockSpec((tm, d), lambda i, j: (i, 0)),
                  pl.BlockSpec((1, d), lambda i, j: (0, 0)),
                  pl.BlockSpec((d, tn), lambda i, j: (0, j))],
        out_specs=pl.BlockSpec((tm, tn), lambda i, j: (i, j)),
        out_shape=jax.ShapeDtypeStruct((t, n), F32),
        scratch_shapes=[pltpu.VMEM((tm, d), BF16)],
        compiler_params=_cparams(2),
    )(x2d, norm_w, w)


def _conv_silu(x, tail, w_ref, b_ref):
    b = b_ref[...]
    w3 = w_ref[3:4, :]
    acc = b + w3 * x
    xc = jnp.concatenate([tail, x[0:8]], axis=0)
    hacc = b + w3 * xc
    for k in range(CONV_K - 1):
        s = CONV_K - 1 - k
        wk = w_ref[k:k + 1, :]
        acc = acc + wk * pltpu.roll(x, s, 0)
        hacc = hacc + wk * pltpu.roll(xc, s, 0)
    out = jnp.concatenate([hacc[8:16], acc[8:]], axis=0)
    return out * _sigmoid(out)


def _ssd_body(z_ref, xs_ref, bc_ref, dt_ref, cwx_ref, cbx_ref, cwb_ref, cbb_ref, dtb_ref, alog_ref,
              dskip_ref, nw_ref, e_ref, y_ref, state, tail_x, tail_bc):
    L = SSD_CHUNK
    half = D_INNER // SSM_GROUPS

    @pl.when(pl.program_id(1) == 0)
    def _():
        state[...] = jnp.zeros_like(state)
        tail_x[...] = jnp.zeros_like(tail_x)
        tail_bc[...] = jnp.zeros_like(tail_bc)

    xs_raw = xs_ref[0]
    bc_raw = bc_ref[0]
    xs = _conv_silu(xs_raw, tail_x[...], cwx_ref, cbx_ref)
    bc = _conv_silu(bc_raw, tail_bc[...], cwb_ref, cbb_ref)
    tail_x[...] = xs_raw[L - 8:L]
    tail_bc[...] = bc_raw[L - 8:L]

    lane = lax.broadcasted_iota(jnp.int32, (L, LANES), 1)
    row = lax.broadcasted_iota(jnp.int32, (L, L), 0)
    col = lax.broadcasted_iota(jnp.int32, (L, L), 1)
    causal = row >= col

    v = dt_ref[0] + dtb_ref[...]
    dt = jnp.maximum(v, 0.0) + jnp.log1p(jnp.exp(-jnp.abs(v)))
    head_lane = lane < SSM_HEADS
    dt = jnp.where(head_lane, dt, 0.0)
    adt = dt * (-jnp.exp(alog_ref[...]))
    a_cs = jnp.dot(causal.astype(F32), adt, precision=HIGHEST, preferred_element_type=F32)
    a_cs_t = a_cs.T

    expand = e_ref[...]
    a_full = jnp.dot(a_cs, expand, precision=HIGHEST, preferred_element_type=F32)
    dt_full = jnp.dot(dt, expand, precision=HIGHEST, preferred_element_type=F32)
    a_tot = a_full[L - 1:L, :]
    decay_from_start = jnp.exp(a_full)
    decay_to_end = jnp.exp(a_tot - a_full)
    decay_chunk = jnp.exp(a_tot)

    xdt = xs * dt_full
    xdt_b = xdt.astype(BF16)
    xw_b = (xdt * decay_to_end).astype(BF16)

    y_groups = []
    for g in range(SSM_GROUPS):
        gs = slice(g * half, (g + 1) * half)
        bg = bc[:, g * D_STATE:(g + 1) * D_STATE]
        cg = bc[:, (SSM_GROUPS + g) * D_STATE:(SSM_GROUPS + g + 1) * D_STATE]
        cb16 = cg.astype(BF16)
        cb = lax.dot_general(cb16, bg.astype(BF16), (((1,), (1,)), ((), ())),
                             preferred_element_type=F32)
        s_prev = state[:, gs]
        y_off = jnp.dot(cb16, s_prev.astype(BF16), preferred_element_type=F32) * decay_from_start[:, gs]

        pair_cols = []
        for pr in range(half // LANES):
            h0 = g * (SSM_HEADS // SSM_GROUPS) + 2 * pr
            xp = xdt_b[:, h0 * SSM_HEAD_DIM:(h0 + 2) * SSM_HEAD_DIM]
            ys = []
            for h in (h0, h0 + 1):
                seg = a_cs[:, h:h + 1] - a_cs_t[h:h + 1, :]
                m = (cb * jnp.exp(jnp.where(causal, seg, NEG_BIG))).astype(BF16)
                ys.append(jnp.dot(m, xp, preferred_element_type=F32))
            pair_cols.append(jnp.where(lane < SSM_HEAD_DIM, ys[0], ys[1]))
        y_diag = jnp.concatenate(pair_cols, axis=1)

        state[:, gs] = decay_chunk[:, gs] * s_prev + jnp.dot(
            bg.T.astype(BF16), xw_b[:, gs], preferred_element_type=F32)
        y_groups.append(y_diag + y_off)

    y = jnp.concatenate(y_groups, axis=1) + dskip_ref[...] * xs
    z = z_ref[0]
    y = y * (z * _sigmoid(z))
    normed = []
    for g in range(SSM_GROUPS):
        yg = y[:, g * half:(g + 1) * half]
        ms = jnp.mean(yg * yg, axis=-1, keepdims=True)
        normed.append(yg * lax.rsqrt(ms + NORM_EPS))
    y_ref[0] = jnp.concatenate(normed, axis=1) * nw_ref[...]


def _ssd(proj3, conv_w, conv_b, dt_bias, a_log, d_skip, ssm_norm_w):
    b, s, _ = proj3.shape
    L = SSD_CHUNK
    pad = LANES - SSM_HEADS
    cwx, cwb = conv_w[:, :D_INNER], conv_w[:, D_INNER:]
    cbx, cbb = conv_b[None, :D_INNER], conv_b[None, D_INNER:]
    dtb = jnp.pad(dt_bias, (0, pad))[None]
    alog = jnp.pad(a_log, (0, pad))[None]
    dskip = jnp.repeat(d_skip, SSM_HEAD_DIM)[None]
    expand = (np.arange(LANES)[:, None] == np.arange(D_INNER)[None, :] // SSM_HEAD_DIM).astype(np.float32)

    def const(shape):
        return pl.BlockSpec(shape, lambda i, c: (0,) * len(shape))

    return pl.pallas_call(
        _ssd_body,
        grid=(b, s // L),
        in_specs=[pl.BlockSpec((1, L, D_INNER), lambda i, c: (i, c, COL_Z // D_INNER)),
                  pl.BlockSpec((1, L, D_INNER), lambda i, c: (i, c, COL_XS // D_INNER)),
                  pl.BlockSpec((1, L, BC_WIDTH), lambda i, c: (i, c, COL_BC // BC_WIDTH)),
                  pl.BlockSpec((1, L, LANES), lambda i, c: (i, c, COL_DT // LANES)),
                  const((CONV_K, D_INNER)), const((1, D_INNER)),
                  const((CONV_K, BC_WIDTH)), const((1, BC_WIDTH)),
                  const((1, LANES)), const((1, LANES)),
                  const((1, D_INNER)), const((1, D_INNER)),
                  const((LANES, D_INNER))],
        out_specs=pl.BlockSpec((1, L, D_INNER), lambda i, c: (i, c, 0)),
        out_shape=jax.ShapeDtypeStruct((b, s, D_INNER), F32),
        scratch_shapes=[pltpu.VMEM((D_STATE, D_INNER), F32),
                        pltpu.VMEM((8, D_INNER), F32),
                        pltpu.VMEM((8, BC_WIDTH), F32)],
        compiler_params=_cparams(2),
    )(proj3, proj3, proj3, proj3, cwx, cbx, cwb, cbb, dtb, alog, dskip, ssm_norm_w[None],
      jnp.asarray(expand))


def _t5_causal_bucket(dist):
    max_exact = NUM_BUCKETS // 2
    large = max_exact + (np.log(np.maximum(dist, max_exact) / max_exact)
                         / math.log(MAX_DISTANCE / max_exact) * (NUM_BUCKETS - max_exact)).astype(np.int32)
    return np.where(dist < max_exact, dist, np.minimum(large, NUM_BUCKETS - 1)).astype(np.int32)


def _band_bias(rel_bias_group, dilation):
    blk = ATTN_BLK
    off = np.arange(blk)[:, None] + blk - np.arange(2 * blk)[None, :]
    in_win = (off >= 0) & (off <= blk)
    bucket = _t5_causal_bucket(np.clip(off, 0, None) * dilation)
    onehot = (bucket.reshape(-1, 1) == np.arange(NUM_BUCKETS)[None, :]).astype(np.float32)
    bias = jnp.dot(jnp.asarray(onehot), rel_bias_group.astype(F32), precision=HIGHEST)
    bias = jnp.transpose(bias.reshape(blk, 2 * blk, HEADS_PER_GROUP), (2, 0, 1))
    bias = jnp.where(in_win[None], bias, NEG_BIG)
    return bias.reshape(HEADS_PER_GROUP // 2, 2 * blk, 2 * blk)


def _attn_body(*refs, dilation):
    n_pairs = HEADS_PER_GROUP // 2
    q_refs, k_refs, v_refs = refs[0:n_pairs], refs[n_pairs:2 * n_pairs], refs[2 * n_pairs:3 * n_pairs]
    bias_ref, qw_ref, kw_ref, bd_ref = refs[3 * n_pairs:3 * n_pairs + 4]
    o_refs = refs[3 * n_pairs + 4:4 * n_pairs + 4]
    lse_ref, kbuf, vbuf = refs[4 * n_pairs + 4:]
    blk = ATTN_BLK
    n = pl.program_id(1)
    slot = n % 2

    @pl.when(n == 0)
    def _():
        kbuf[...] = jnp.zeros_like(kbuf)
        vbuf[...] = jnp.zeros_like(vbuf)

    bd = bd_ref[...]

    def head_norm(x, w_ref):
        xx = x * x
        hi = xx.astype(BF16)
        lo = (xx - hi.astype(F32)).astype(BF16)
        ss = (jnp.dot(hi, bd, preferred_element_type=F32)
              + jnp.dot(lo, bd, preferred_element_type=F32))
        return x * lax.rsqrt(ss * (1.0 / ATTN_HEAD_DIM) + NORM_EPS) * w_ref[...]

    lane = lax.broadcasted_iota(jnp.int32, (blk, LANES), 1)
    first_pen = jnp.where(n == 0, NEG_BIG, 0.0)
    lo_half = lane < ATTN_HEAD_DIM
    nt = (((1,), (1,)), ((), ()))

    def one_residue(res, carry):
        rows = pl.ds(res, blk, stride=dilation)
        lse_tile = jnp.zeros((blk, LANES), F32)
        for p in range(n_pairs):
            qp = head_norm(q_refs[p][0, rows, :], qw_ref) * (ATTN_HEAD_DIM ** -0.5)
            k = head_norm(k_refs[p][0, rows, :], kw_ref).astype(BF16)
            v = v_refs[p][0, rows, :].astype(BF16)
            idx = res * n_pairs + p
            kbuf[slot, idx] = k
            vbuf[slot, idx] = v
            q2 = jnp.concatenate([jnp.where(lo_half, qp, 0.0), jnp.where(lo_half, 0.0, qp)],
                                 axis=0).astype(BF16)
            s_prev = lax.dot_general(q2, kbuf[1 - slot, idx], nt, preferred_element_type=F32)
            s_cur = lax.dot_general(q2, k, nt, preferred_element_type=F32)
            s_prev = s_prev + bias_ref[p, :, 0:blk] + first_pen
            s_cur = s_cur + bias_ref[p, :, blk:2 * blk]
            m = jnp.maximum(jnp.max(s_prev, axis=-1, keepdims=True), jnp.max(s_cur, axis=-1, keepdims=True))
            e_prev = jnp.exp(s_prev - m)
            e_cur = jnp.exp(s_cur - m)
            d = jnp.sum(e_prev, axis=-1, keepdims=True) + jnp.sum(e_cur, axis=-1, keepdims=True)
            pv = (jnp.dot(e_prev.astype(BF16), vbuf[1 - slot, idx], preferred_element_type=F32)
                  + jnp.dot(e_cur.astype(BF16), v, preferred_element_type=F32)) / d
            o_refs[p][0, rows, :] = jnp.where(lo_half, pv[0:blk], pv[blk:2 * blk])
            lse = m + jnp.log(d)
            lse_tile = jnp.where(lane == 2 * p, lse[0:blk], lse_tile)
            lse_tile = jnp.where(lane == 2 * p + 1, lse[blk:2 * blk], lse_tile)
        lse_ref[0, rows, :] = lse_tile
        return carry

    lax.fori_loop(0, dilation, one_residue, 0)


def _attn_group(proj3, gi, dilation, bias, qw, kw, bd):
    b, s, _ = proj3.shape
    blk = ATTN_BLK
    span = blk * dilation
    n_pairs = HEADS_PER_GROUP // 2

    def pair_spec(base, p):
        cb = (base + gi * GROUP_WIDTH) // LANES + p
        return pl.BlockSpec((1, span, LANES), lambda i, n: (i, n, cb))

    def const(shape):
        return pl.BlockSpec(shape, lambda i, n: (0,) * len(shape))

    token_spec = pl.BlockSpec((1, span, LANES), lambda i, n: (i, n, 0))
    res = pl.pallas_call(
        functools.partial(_attn_body, dilation=dilation),
        grid=(b, s // span),
        in_specs=([pair_spec(COL_Q, p) for p in range(n_pairs)]
                  + [pair_spec(COL_K, p) for p in range(n_pairs)]
                  + [pair_spec(COL_V, p) for p in range(n_pairs)]
                  + [const((n_pairs, 2 * blk, 2 * blk)), const((1, LANES)), const((1, LANES)),
                     const((LANES, LANES))]),
        out_specs=[token_spec] * (n_pairs + 1),
        out_shape=[jax.ShapeDtypeStruct((b, s, LANES), F32)] * (n_pairs + 1),
        scratch_shapes=[pltpu.VMEM((2, dilation * n_pairs, blk, LANES), BF16),
                        pltpu.VMEM((2, dilation * n_pairs, blk, LANES), BF16)],
        compiler_params=_cparams(2),
    )(*([proj3] * (3 * n_pairs)), bias, qw, kw, bd)
    outs = [o.reshape(b * s, LANES) for o in res[:n_pairs]]
    return outs, res[n_pairs].reshape(b * s, LANES)


def _mix_body(*refs):
    n_pairs = HEADS_PER_GROUP // 2
    y_ref = refs[0]
    o_refs = refs[1:1 + 3 * n_pairs]
    l_refs = refs[1 + 3 * n_pairs:4 + 3 * n_pairs]
    (gs_ref, ga_ref, x_ref, wssm_ref, wattn_ref, wout_ref, e8_ref, nfw_ref, wr_ref, br_ref,
     x2_ref, h2_ref, lg_ref) = refs[4 + 3 * n_pairs:]
    tm = x_ref.shape[0]
    lses = [l[...] for l in l_refs]
    mx = jnp.maximum(jnp.maximum(lses[0], lses[1]), lses[2])
    es = [jnp.exp(l - mx) for l in lses]
    inv = 1.0 / (es[0] + es[1] + es[2])
    e8 = e8_ref[...]
    att = jnp.zeros((tm, GROUP_WIDTH), F32)
    for g in range(3):
        w = es[g] * inv
        w_hi = w.astype(BF16)
        w_lo = (w - w_hi.astype(F32)).astype(BF16)
        w_full = (jnp.dot(w_hi, e8, preferred_element_type=F32)
                  + jnp.dot(w_lo, e8, preferred_element_type=F32))
        o_g = jnp.concatenate([o_refs[g * n_pairs + p][...] for p in range(n_pairs)], axis=1)
        att = att + w_full * o_g
    y_attn = jnp.dot(att.astype(BF16), wattn_ref[...], preferred_element_type=F32)
    y_ssm = jnp.dot(y_ref[...].astype(BF16), wssm_ref[...], preferred_element_type=F32)
    merged = _sigmoid(gs_ref[...]) * y_ssm + _sigmoid(ga_ref[...]) * y_attn
    x2 = x_ref[...] + jnp.dot(merged.astype(BF16), wout_ref[...], preferred_element_type=F32)
    x2_ref[...] = x2
    ms = jnp.mean(x2 * x2, axis=-1, keepdims=True)
    h2 = x2 * lax.rsqrt(ms + NORM_EPS) * nfw_ref[...]
    h2_ref[...] = h2.reshape(tm, 1, D_MODEL)
    lg_ref[...] = jnp.dot(h2.astype(BF16), wr_ref[...], preferred_element_type=F32) + br_ref[...]


def _mix_out(y_ssm, attn_outs, attn_lses, proj, x2d, w_ssm, w_attn, w_out, norm_ffn_w, w_router, b_router):
    t = x2d.shape[0]
    tm = min(256, t)
    e8 = (np.arange(LANES)[:, None] == np.arange(GROUP_WIDTH)[None, :] // ATTN_HEAD_DIM)
    e8 = jnp.asarray(e8.astype(np.float32), dtype=BF16)

    def rows(width, cb=0):
        return pl.BlockSpec((tm, width), lambda i: (i, cb))

    def const(shape):
        return pl.BlockSpec(shape, lambda i: (0,) * len(shape))

    return pl.pallas_call(
        _mix_body,
        grid=(t // tm,),
        in_specs=[rows(D_INNER)] + [rows(LANES)] * (len(attn_outs) + len(attn_lses)) + [
                  rows(D_MODEL, COL_GS // D_MODEL), rows(D_MODEL, COL_GA // D_MODEL), rows(D_MODEL),
                  const((D_INNER, D_MODEL)), const((GROUP_WIDTH, D_MODEL)), const((D_MODEL, D_MODEL)),
                  const((LANES, GROUP_WIDTH)), const((1, D_MODEL)),
                  const((D_MODEL, LANES)), const((1, LANES))],
        out_specs=[rows(D_MODEL),
                   pl.BlockSpec((tm, 1, D_MODEL), lambda i: (i, 0, 0)),
                   rows(LANES)],
        out_shape=[jax.ShapeDtypeStruct((t, D_MODEL), F32),
                   jax.ShapeDtypeStruct((t, 1, D_MODEL), F32),
                   jax.ShapeDtypeStruct((t, LANES), F32)],
        compiler_params=_cparams(1),
    )(y_ssm, *attn_outs, *attn_lses, proj, proj, x2d, w_ssm, w_attn, w_out, e8,
      norm_ffn_w[None], w_router, b_router)


def _route_body(lg_ref, dest_ref, gate_ref, cnt_ref, counts, pstart):
    tm = lg_ref.shape[0]
    ps = pl.program_id(0)
    i = pl.program_id(1)

    @pl.when((ps == 0) & (i == 0))
    def _():
        counts[...] = jnp.zeros_like(counts)
        pstart[...] = jnp.zeros_like(pstart)

    @pl.when((ps == 1) & (i == 0))
    def _():
        cnt = counts[...]
        padded = jnp.floor((cnt + (MOE_BLOCK - 1)) * (1.0 / MOE_BLOCK)) * MOE_BLOCK
        a = lax.broadcasted_iota(jnp.int32, (LANES, LANES), 0)
        bcol = lax.broadcasted_iota(jnp.int32, (LANES, LANES), 1)
        pad_end = jnp.dot(padded, (a <= bcol).astype(F32), precision=HIGHEST, preferred_element_type=F32)
        pstart[...] = pad_end - padded
        cnt_ref[...] = cnt
        counts[...] = jnp.zeros_like(counts)

    lg = lg_ref[...]
    lane = lax.broadcasted_iota(jnp.int32, (tm, LANES), 1)
    is_coarse = lane < N_EXPERT_GROUPS
    cmax = jnp.max(jnp.where(is_coarse, lg, NEG_BIG), axis=-1, keepdims=True)
    grp = jnp.min(jnp.where(is_coarse & (lg == cmax), lane, LANES), axis=-1, keepdims=True)
    group_p = 1.0 / jnp.sum(jnp.where(is_coarse, jnp.exp(lg - cmax), 0.0), axis=-1, keepdims=True)
    f_lo = N_EXPERT_GROUPS + EXPERTS_PER_GROUP * grp
    in_grp = (lane >= f_lo) & (lane < f_lo + EXPERTS_PER_GROUP)
    f1 = jnp.max(jnp.where(in_grp, lg, NEG_BIG), axis=-1, keepdims=True)
    i1 = jnp.min(jnp.where(in_grp & (lg == f1), lane, LANES), axis=-1, keepdims=True)
    rest = in_grp & (lane != i1)
    f2 = jnp.max(jnp.where(rest, lg, NEG_BIG), axis=-1, keepdims=True)
    i2 = jnp.min(jnp.where(rest & (lg == f2), lane, LANES), axis=-1, keepdims=True)
    e2 = jnp.exp(f2 - f1)
    g1 = group_p / (1.0 + e2)
    g2 = group_p * e2 / (1.0 + e2)

    oh1 = lane == i1 - N_EXPERT_GROUPS
    oh2 = lane == i2 - N_EXPERT_GROUPS
    onehot = jnp.where(oh1 | oh2, 1.0, 0.0)
    r = lax.broadcasted_iota(jnp.int32, (tm, tm), 0)
    c = lax.broadcasted_iota(jnp.int32, (tm, tm), 1)
    before = jnp.dot((r > c).astype(BF16), onehot.astype(BF16), preferred_element_type=F32)
    pos = pstart[0:1, :] + counts[0:1, :] + before
    d1 = jnp.sum(jnp.where(oh1, pos, 0.0), axis=-1, keepdims=True)
    d2 = jnp.sum(jnp.where(oh2, pos, 0.0), axis=-1, keepdims=True)
    counts[...] = counts[...] + jnp.sum(onehot, axis=0, keepdims=True)
    dest_ref[...] = jnp.where(lane == 0, d1, jnp.where(lane == 1, d2, 0.0)).astype(jnp.int32)
    gate_ref[...] = jnp.where(lane == 0, g1, jnp.where(lane == 1, g2, 0.0))


def _route(logits):
    t = logits.shape[0]
    tm = min(512, t)
    return pl.pallas_call(
        _route_body,
        grid=(2, t // tm),
        in_specs=[pl.BlockSpec((tm, LANES), lambda ps, i: (i, 0))],
        out_specs=[pl.BlockSpec((tm, LANES), lambda ps, i: (i * ps, 0)),
                   pl.BlockSpec((tm, LANES), lambda ps, i: (i * ps, 0)),
                   pl.BlockSpec((8, LANES), lambda ps, i: (0, 0))],
        out_shape=[jax.ShapeDtypeStruct((t, LANES), jnp.int32),
                   jax.ShapeDtypeStruct((t, LANES), F32),
                   jax.ShapeDtypeStruct((8, LANES), F32)],
        scratch_shapes=[pltpu.VMEM((8, LANES), F32), pltpu.VMEM((8, LANES), F32)],
        compiler_params=_cparams(2),
    )(logits)


def _dispatch_body(meta_ref, dest_ref, h_ref, rows_ref, zbuf, zsem, sem):
    tm = h_ref.shape[0]

    def zero_copy(e):
        start = pl.multiple_of(meta_ref[e] - MOE_BLOCK, MOE_BLOCK)
        return pltpu.make_async_copy(zbuf, rows_ref.at[pl.ds(start, MOE_BLOCK)], zsem)

    def tail_copy(blk):
        start = pl.multiple_of(blk * MOE_BLOCK, MOE_BLOCK)
        return pltpu.make_async_copy(zbuf, rows_ref.at[pl.ds(start, MOE_BLOCK)], zsem)

    @pl.when(pl.program_id(0) == 0)
    def _():
        zbuf[...] = jnp.zeros_like(zbuf)

        def start_zero(e, carry):
            @pl.when(meta_ref[N_EXPERTS + e] > 0)
            def _():
                zero_copy(e).start()
            return carry

        def wait_zero(e, carry):
            @pl.when(meta_ref[N_EXPERTS + e] > 0)
            def _():
                zero_copy(e).wait()
            return carry

        def start_tail(blk, carry):
            tail_copy(blk).start()
            return carry

        def wait_tail(blk, carry):
            tail_copy(blk).wait()
            return carry

        n_used = meta_ref[2 * N_EXPERTS]
        n_blocks = rows_ref.shape[0] // MOE_BLOCK
        lax.fori_loop(0, N_EXPERTS, start_zero, 0)
        lax.fori_loop(n_used, n_blocks, start_tail, 0)
        lax.fori_loop(0, N_EXPERTS, wait_zero, 0)
        lax.fori_loop(n_used, n_blocks, wait_tail, 0)

    def issue(j, carry):
        for k in range(2):
            d = dest_ref[0, 2 * j + k]
            pltpu.make_async_copy(h_ref.at[j], rows_ref.at[d], sem).start()
        return carry

    lax.fori_loop(0, tm, issue, 0, unroll=8)
    for _ in range(2):
        pltpu.make_async_copy(h_ref, rows_ref.at[pl.ds(0, tm)], sem).wait()


def _dispatch(meta, dest3, h2, n_rows):
    t = h2.shape[0]
    tm = dest3.shape[2] // 2
    return pl.pallas_call(
        _dispatch_body,
        grid_spec=pltpu.PrefetchScalarGridSpec(
            num_scalar_prefetch=1,
            grid=(t // tm,),
            in_specs=[pl.BlockSpec((None, 1, 2 * tm), lambda i, m: (i, 0, 0), memory_space=pltpu.SMEM),
                      pl.BlockSpec((tm, 1, D_MODEL), lambda i, m: (i, 0, 0))],
            out_specs=pl.BlockSpec(memory_space=pl.ANY),
            scratch_shapes=[pltpu.VMEM((MOE_BLOCK, 1, D_MODEL), F32),
                            pltpu.SemaphoreType.DMA(()),
                            pltpu.SemaphoreType.DMA(())]),
        out_shape=jax.ShapeDtypeStruct((n_rows, 1, D_MODEL), F32),
        compiler_params=_cparams(1),
    )(meta, dest3, h2)


def _expert_body(be_ref, nu_ref, x_ref, wg_ref, wu_ref, wd_ref, y_ref, x2d, wg_b, wu_b, wd_b):
    blk = pl.program_id(0)

    @pl.when(blk < nu_ref[0])
    def _():
        prev = be_ref[jnp.maximum(blk - 1, 0)]

        @pl.when((blk == 0) | (be_ref[blk] != prev))
        def _():
            wg_b[...] = wg_ref[0].astype(BF16)
            wu_b[...] = wu_ref[0].astype(BF16)
            wd_b[...] = wd_ref[0].astype(BF16)

        x2d[...] = x_ref[...].reshape(MOE_BLOCK, D_MODEL)
        x = x2d[...].astype(BF16)
        g = jnp.dot(x, wg_b[...], preferred_element_type=F32)
        u = jnp.dot(x, wu_b[...], preferred_element_type=F32)
        hid = (g * _sigmoid(g)) * u
        y = jnp.dot(hid.astype(BF16), wd_b[...], preferred_element_type=F32)
        y_ref[...] = y.reshape(MOE_BLOCK, 1, D_MODEL)

    @pl.when(blk >= nu_ref[0])
    def _():
        y_ref[...] = jnp.zeros_like(y_ref)


def _experts(block_expert, n_used, rows, w_gate, w_up, w_down):
    n_rows = rows.shape[0]
    n_blocks = n_rows // MOE_BLOCK

    def row_map(b, be, nu):
        return (b, 0, 0)

    def w_map(b, be, nu):
        return (be[jnp.minimum(b, nu[0] - 1)], 0, 0)

    return pl.pallas_call(
        _expert_body,
        grid_spec=pltpu.PrefetchScalarGridSpec(
            num_scalar_prefetch=2,
            grid=(n_blocks,),
            in_specs=[pl.BlockSpec((MOE_BLOCK, 1, D_MODEL), row_map),
                      pl.BlockSpec((1, D_MODEL, D_EXPERT), w_map),
                      pl.BlockSpec((1, D_MODEL, D_EXPERT), w_map),
                      pl.BlockSpec((1, D_EXPERT, D_MODEL), w_map)],
            out_specs=pl.BlockSpec((MOE_BLOCK, 1, D_MODEL), row_map),
            scratch_shapes=[pltpu.VMEM((MOE_BLOCK, D_MODEL), F32),
                            pltpu.VMEM((D_MODEL, D_EXPERT), BF16),
                            pltpu.VMEM((D_MODEL, D_EXPERT), BF16),
                            pltpu.VMEM((D_EXPERT, D_MODEL), BF16)]),
        out_shape=jax.ShapeDtypeStruct((n_rows, 1, D_MODEL), F32),
        compiler_params=_cparams(1),
    )(block_expert, n_used, rows, w_gate, w_up, w_down)


def _combine_body(dcur_ref, dnext_ref, gate_ref, x2_ref, y_ref, o_ref, buf_a, buf_b, y2d, sem_a, sem_b):
    tm = x2_ref.shape[0]
    i = pl.program_id(0)
    n = pl.num_programs(0)

    def issue(dref, buf, sem):
        def body(j, carry):
            for k in range(2):
                d = dref[0, 2 * j + k]
                pltpu.make_async_copy(y_ref.at[d], buf.at[k * tm + j], sem).start()
            return carry
        lax.fori_loop(0, tm, body, 0, unroll=8)

    def finish(buf, sem):
        pltpu.make_async_copy(y_ref.at[pl.ds(0, 2 * tm)], buf, sem).wait()
        y2d[...] = buf[...].reshape(2 * tm, D_MODEL)
        g = gate_ref[...]
        o_ref[...] = x2_ref[...] + g[:, 0:1] * y2d[0:tm, :] + g[:, 1:2] * y2d[tm:2 * tm, :]

    @pl.when(i == 0)
    def _():
        issue(dcur_ref, buf_a, sem_a)

    for par, (cur, cur_sem, nxt, nxt_sem) in enumerate(((buf_a, sem_a, buf_b, sem_b),
                                                        (buf_b, sem_b, buf_a, sem_a))):
        @pl.when(i % 2 == par)
        def _(cur=cur, cur_sem=cur_sem, nxt=nxt, nxt_sem=nxt_sem):
            @pl.when(i + 1 < n)
            def _():
                issue(dnext_ref, nxt, nxt_sem)
            finish(cur, cur_sem)


def _combine(dest3, gates, x2, y_rows):
    t = x2.shape[0]
    tm = dest3.shape[2] // 2
    nt = t // tm
    return pl.pallas_call(
        _combine_body,
        grid=(nt,),
        in_specs=[pl.BlockSpec((None, 1, 2 * tm), lambda i: (i, 0, 0), memory_space=pltpu.SMEM),
                  pl.BlockSpec((None, 1, 2 * tm), lambda i: (jnp.minimum(i + 1, nt - 1), 0, 0),
                               memory_space=pltpu.SMEM),
                  pl.BlockSpec((tm, LANES), lambda i: (i, 0)),
                  pl.BlockSpec((tm, D_MODEL), lambda i: (i, 0)),
                  pl.BlockSpec(memory_space=pl.ANY)],
        out_specs=pl.BlockSpec((tm, D_MODEL), lambda i: (i, 0)),
        out_shape=jax.ShapeDtypeStruct((t, D_MODEL), F32),
        scratch_shapes=[pltpu.VMEM((2 * tm, 1, D_MODEL), F32),
                        pltpu.VMEM((2 * tm, 1, D_MODEL), F32),
                        pltpu.VMEM((2 * tm, D_MODEL), F32),
                        pltpu.SemaphoreType.DMA(()),
                        pltpu.SemaphoreType.DMA(())],
        compiler_params=_cparams(1),
    )(dest3, dest3, gates, x2, y_rows)


def _layer(x, norm_mix_w, w_in, conv_w, conv_b, dt_bias, a_log, d_skip, ssm_norm_w, w_ssm_proj,
           q_norm_w, k_norm_w, rel_bias, w_attn_proj, w_out, norm_ffn_w, w_coarse, b_coarse,
           w_fine, b_fine, w_gate_exp, w_up_exp, w_down_exp):
    b, s, d = x.shape
    t = b * s
    x2d = x.reshape(t, d)

    dt_lo = D_INNER + D_INNER + BC_WIDTH
    w_packed = jnp.concatenate(
        [w_in[:, :dt_lo], w_in[:, dt_lo + SSM_HEADS:], w_in[:, dt_lo:dt_lo + SSM_HEADS],
         jnp.zeros((d, PROJ_WIDTH - COL_DT - SSM_HEADS), w_in.dtype)], axis=1).astype(BF16)
    proj = _in_proj(x2d, norm_mix_w[None], w_packed)
    proj3 = proj.reshape(b, s, PROJ_WIDTH)

    y_ssm = _ssd(proj3, conv_w, conv_b, dt_bias, a_log, d_skip, ssm_norm_w).reshape(t, D_INNER)

    qw = jnp.tile(q_norm_w, 2)[None]
    kw = jnp.tile(k_norm_w, 2)[None]
    lane_head = np.arange(LANES) // ATTN_HEAD_DIM
    bd = jnp.asarray((lane_head[:, None] == lane_head[None, :]).astype(np.float32), dtype=BF16)
    attn_outs, attn_lses = [], []
    for gi, (window, dilation) in enumerate(DILATED_CONFIGS):
        assert window // dilation == ATTN_BLK and s % window == 0
        bias = _band_bias(rel_bias[:, gi * HEADS_PER_GROUP:(gi + 1) * HEADS_PER_GROUP], dilation)
        o, l = _attn_group(proj3, gi, dilation, bias, qw, kw, bd)
        attn_outs.extend(o)
        attn_lses.append(l)

    n_route = N_EXPERT_GROUPS + N_EXPERTS
    w_router = jnp.pad(jnp.concatenate([w_coarse, w_fine], axis=1), ((0, 0), (0, LANES - n_route)))
    b_router = jnp.pad(jnp.concatenate([b_coarse, b_fine]), (0, LANES - n_route))[None]
    x2, h2, logits = _mix_out(y_ssm, attn_outs, attn_lses, proj, x2d, w_ssm_proj.astype(BF16),
                              w_attn_proj.astype(BF16), w_out.astype(BF16), norm_ffn_w,
                              w_router.astype(BF16), b_router)

    dest, gates, counts = _route(logits)

    cnt = counts[0, :N_EXPERTS].astype(jnp.int32)
    padded = (cnt + MOE_BLOCK - 1) // MOE_BLOCK * MOE_BLOCK
    pad_end = jnp.cumsum(padded)
    n_blocks = -(-(2 * t + N_EXPERTS * (MOE_BLOCK - 1)) // MOE_BLOCK)
    block_start = jnp.arange(n_blocks, dtype=jnp.int32) * MOE_BLOCK
    block_expert = jnp.minimum(jnp.sum((pad_end[None, :] <= block_start[:, None]).astype(jnp.int32), axis=1),
                               N_EXPERTS - 1)
    n_used = (pad_end[-1:] // MOE_BLOCK).astype(jnp.int32)
    meta = jnp.concatenate([pad_end, padded, n_used]).astype(jnp.int32)

    tm_d = min(256, t)
    dest_d = dest[:, :2].reshape(t // tm_d, 1, 2 * tm_d)
    rows = _dispatch(meta, dest_d, h2, n_blocks * MOE_BLOCK)
    y_rows = _experts(block_expert, n_used, rows, w_gate_exp, w_up_exp, w_down_exp)
    tm_c = min(128, t)
    dest_c = dest[:, :2].reshape(t // tm_c, 1, 2 * tm_c)
    out = _combine(dest_c, gates, x2, y_rows)
    return out.reshape(b, s, d)


def kernel(x, norm_mix_w, w_in, conv_w, conv_b, dt_bias, a_log, d_skip, ssm_norm_w, w_ssm_proj,
           q_norm_w, k_norm_w, rel_bias, w_attn_proj, w_out, norm_ffn_w, w_coarse, b_coarse,
           w_fine, b_fine, w_gate_exp, w_up_exp, w_down_exp):
    depth = norm_mix_w.shape[0]
    for layer in range(depth):
        x = _layer(x, norm_mix_w[layer], w_in[layer], conv_w[layer], conv_b[layer], dt_bias[layer],
                   a_log[layer], d_skip[layer], ssm_norm_w[layer], w_ssm_proj[layer],
                   q_norm_w[layer], k_norm_w[layer], rel_bias, w_attn_proj[layer], w_out[layer],
                   norm_ffn_w[layer], w_coarse[layer], b_coarse[layer], w_fine[layer], b_fine[layer],
                   w_gate_exp[layer], w_up_exp[layer], w_down_exp[layer])
    return x
```

```python
import functools
import math

import jax
import jax.numpy as jnp
import numpy as np
from jax import lax
from jax.experimental import pallas as pl
from jax.experimental.pallas import tpu as pltpu

F32 = jnp.float32
BF16 = jnp.bfloat16
HIGHEST = lax.Precision.HIGHEST

LANES = 128
NORM_EPS = 1e-6
NEG_BIG = -1e30

D_MODEL = 1024
D_INNER = 2048
SSM_HEAD_DIM = 64
SSM_HEADS = 32
SSM_GROUPS = 2
D_STATE = 128
CONV_K = 4
BC_WIDTH = 2 * SSM_GROUPS * D_STATE
SSD_CHUNK = 128
ATTN_HEAD_DIM = 64
DILATED_CONFIGS = ((128, 1), (512, 4), (2048, 16))
HEADS_PER_GROUP = 8
GROUP_WIDTH = HEADS_PER_GROUP * ATTN_HEAD_DIM
ATTN_WIDTH = 3 * GROUP_WIDTH
ATTN_BLK = 128
NUM_BUCKETS = 32
MAX_DISTANCE = 2048
N_EXPERT_GROUPS = 8
EXPERTS_PER_GROUP = 8
N_EXPERTS = 64
D_EXPERT = 512
MOE_BLOCK = 256

COL_Z = 0
COL_XS = D_INNER
COL_BC = 2 * D_INNER
COL_Q = COL_BC + BC_WIDTH
COL_K = COL_Q + ATTN_WIDTH
COL_V = COL_K + ATTN_WIDTH
COL_GS = COL_V + ATTN_WIDTH
COL_GA = COL_GS + D_MODEL
COL_DT = COL_GA + D_MODEL
PROJ_WIDTH = COL_DT + 512

VMEM_LIMIT = 56 * 1024 * 1024


def _sigmoid(x):
    return 1.0 / (1.0 + jnp.exp(-x))


def _cparams(n_axes):
    return pltpu.CompilerParams(dimension_semantics=("arbitrary",) * n_axes,
                                vmem_limit_bytes=VMEM_LIMIT)


def _in_proj_body(x_ref, nw_ref, w_ref, o_ref, h_scr):
    @pl.when(pl.program_id(1) == 0)
    def _():
        x = x_ref[...]
        ms = jnp.mean(x * x, axis=-1, keepdims=True)
        h_scr[...] = (x * lax.rsqrt(ms + NORM_EPS) * nw_ref[...]).astype(BF16)

    o_ref[...] = jnp.dot(h_scr[...], w_ref[...], preferred_element_type=F32)


def _in_proj(x2d, norm_w, w):
    t, d = x2d.shape
    n = w.shape[1]
    tm = min(2048, t)
    tn = 512
    return pl.pallas_call(
        _in_proj_body,
        grid=(t // tm, n // tn),
        in_specs=[pl.BlockSpec((tm, d), lambda i, j: (i, 0)),
                  pl.BlockSpec((1, d), lambda i, j: (0, 0)),
                  pl.BlockSpec((d, tn), lambda i, j: (0, j))],
        out_specs=pl.BlockSpec((tm, tn), lambda i, j: (i, j)),
        out_shape=jax.ShapeDtypeStruct((t, n), F32),
        scratch_shapes=[pltpu.VMEM((tm, d), BF16)],
        compiler_params=_cparams(2),
    )(x2d, norm_w, w)


def _conv_silu(x, tail, w_ref, b_ref):
    b = b_ref[...]
    w3 = w_ref[3:4, :]
    acc = b + w3 * x
    xc = jnp.concatenate([tail, x[0:8]], axis=0)
    hacc = b + w3 * xc
    for k in range(CONV_K - 1):
        s = CONV_K - 1 - k
        wk = w_ref[k:k + 1, :]
        acc = acc + wk * pltpu.roll(x, s, 0)
        hacc = hacc + wk * pltpu.roll(xc, s, 0)
    out = jnp.concatenate([hacc[8:16], acc[8:]], axis=0)
    return out * _sigmoid(out)


def _ssd_body(z_ref, xs_ref, bc_ref, dt_ref, cwx_ref, cbx_ref, cwb_ref, cbb_ref, dtb_ref, alog_ref,
              dskip_ref, nw_ref, e_ref, y_ref, state, tail_x, tail_bc):
    L = SSD_CHUNK
    half = D_INNER // SSM_GROUPS

    @pl.when(pl.program_id(1) == 0)
    def _():
        state[...] = jnp.zeros_like(state)
        tail_x[...] = jnp.zeros_like(tail_x)
        tail_bc[...] = jnp.zeros_like(tail_bc)

    xs_raw = xs_ref[0]
    bc_raw = bc_ref[0]
    xs = _conv_silu(xs_raw, tail_x[...], cwx_ref, cbx_ref)
    bc = _conv_silu(bc_raw, tail_bc[...], cwb_ref, cbb_ref)
    tail_x[...] = xs_raw[L - 8:L]
    tail_bc[...] = bc_raw[L - 8:L]

    lane = lax.broadcasted_iota(jnp.int32, (L, LANES), 1)
    row = lax.broadcasted_iota(jnp.int32, (L, L), 0)
    col = lax.broadcasted_iota(jnp.int32, (L, L), 1)
    causal = row >= col

    v = dt_ref[0] + dtb_ref[...]
    dt = jnp.maximum(v, 0.0) + jnp.log1p(jnp.exp(-jnp.abs(v)))
    head_lane = lane < SSM_HEADS
    dt = jnp.where(head_lane, dt, 0.0)
    adt = dt * (-jnp.exp(alog_ref[...]))
    a_cs = jnp.dot(causal.astype(F32), adt, precision=HIGHEST, preferred_element_type=F32)
    a_cs_t = a_cs.T

    expand = e_ref[...]
    a_full = jnp.dot(a_cs, expand, precision=HIGHEST, preferred_element_type=F32)
    dt_full = jnp.dot(dt, expand, precision=HIGHEST, preferred_element_type=F32)
    a_tot = a_full[L - 1:L, :]
    decay_from_start = jnp.exp(a_full)
    decay_to_end = jnp.exp(a_tot - a_full)
    decay_chunk = jnp.exp(a_tot)

    xdt = xs * dt_full
    xdt_b = xdt.astype(BF16)
    xw_b = (xdt * decay_to_end).astype(BF16)

    y_groups = []
    for g in range(SSM_GROUPS):
        gs = slice(g * half, (g + 1) * half)
        bg = bc[:, g * D_STATE:(g + 1) * D_STATE]
        cg = bc[:, (SSM_GROUPS + g) * D_STATE:(SSM_GROUPS + g + 1) * D_STATE]
        cb16 = cg.astype(BF16)
        cb = lax.dot_general(cb16, bg.astype(BF16), (((1,), (1,)), ((), ())),
                             preferred_element_type=F32)
        s_prev = state[:, gs]
        y_off = jnp.dot(cb16, s_prev.astype(BF16), preferred_element_type=F32) * decay_from_start[:, gs]

        pair_cols = []
        for pr in range(half // LANES):
            h0 = g * (SSM_HEADS // SSM_GROUPS) + 2 * pr
            xp = xdt_b[:, h0 * SSM_HEAD_DIM:(h0 + 2) * SSM_HEAD_DIM]
            ys = []
            for h in (h0, h0 + 1):
                seg = a_cs[:, h:h + 1] - a_cs_t[h:h + 1, :]
                m = (cb * jnp.exp(jnp.where(causal, seg, NEG_BIG))).astype(BF16)
                ys.append(jnp.dot(m, xp, preferred_element_type=F32))
            pair_cols.append(jnp.where(lane < SSM_HEAD_DIM, ys[0], ys[1]))
        y_diag = jnp.concatenate(pair_cols, axis=1)

        state[:, gs] = decay_chunk[:, gs] * s_prev + jnp.dot(
            bg.T.astype(BF16), xw_b[:, gs], preferred_element_type=F32)
        y_groups.append(y_diag + y_off)

    y = jnp.concatenate(y_groups, axis=1) + dskip_ref[...] * xs
    z = z_ref[0]
    y = y * (z * _sigmoid(z))
    normed = []
    for g in range(SSM_GROUPS):
        yg = y[:, g * half:(g + 1) * half]
        ms = jnp.mean(yg * yg, axis=-1, keepdims=True)
        normed.append(yg * lax.rsqrt(ms + NORM_EPS))
    y_ref[0] = jnp.concatenate(normed, axis=1) * nw_ref[...]


def _ssd(proj3, conv_w, conv_b, dt_bias, a_log, d_skip, ssm_norm_w):
    b, s, _ = proj3.shape
    L = SSD_CHUNK
    pad = LANES - SSM_HEADS
    cwx, cwb = conv_w[:, :D_INNER], conv_w[:, D_INNER:]
    cbx, cbb = conv_b[None, :D_INNER], conv_b[None, D_INNER:]
    dtb = jnp.pad(dt_bias, (0, pad))[None]
    alog = jnp.pad(a_log, (0, pad))[None]
    dskip = jnp.repeat(d_skip, SSM_HEAD_DIM)[None]
    expand = (np.arange(LANES)[:, None] == np.arange(D_INNER)[None, :] // SSM_HEAD_DIM).astype(np.float32)

    def const(shape):
        return pl.BlockSpec(shape, lambda i, c: (0,) * len(shape))

    return pl.pallas_call(
        _ssd_body,
        grid=(b, s // L),
        in_specs=[pl.BlockSpec((1, L, D_INNER), lambda i, c: (i, c, COL_Z // D_INNER)),
                  pl.BlockSpec((1, L, D_INNER), lambda i, c: (i, c, COL_XS // D_INNER)),
                  pl.BlockSpec((1, L, BC_WIDTH), lambda i, c: (i, c, COL_BC // BC_WIDTH)),
                  pl.BlockSpec((1, L, LANES), lambda i, c: (i, c, COL_DT // LANES)),
                  const((CONV_K, D_INNER)), const((1, D_INNER)),
                  const((CONV_K, BC_WIDTH)), const((1, BC_WIDTH)),
                  const((1, LANES)), const((1, LANES)),
                  const((1, D_INNER)), const((1, D_INNER)),
                  const((LANES, D_INNER))],
        out_specs=pl.BlockSpec((1, L, D_INNER), lambda i, c: (i, c, 0)),
        out_shape=jax.ShapeDtypeStruct((b, s, D_INNER), F32),
        scratch_shapes=[pltpu.VMEM((D_STATE, D_INNER), F32),
                        pltpu.VMEM((8, D_INNER), F32),
                        pltpu.VMEM((8, BC_WIDTH), F32)],
        compiler_params=_cparams(2),
    )(proj3, proj3, proj3, proj3, cwx, cbx, cwb, cbb, dtb, alog, dskip, ssm_norm_w[None],
      jnp.asarray(expand))


def _t5_causal_bucket(dist):
    max_exact = NUM_BUCKETS // 2
    large = max_exact + (np.log(np.maximum(dist, max_exact) / max_exact)
                         / math.log(MAX_DISTANCE / max_exact) * (NUM_BUCKETS - max_exact)).astype(np.int32)
    return np.where(dist < max_exact, dist, np.minimum(large, NUM_BUCKETS - 1)).astype(np.int32)


def _band_bias(rel_bias_group, dilation):
    blk = ATTN_BLK
    off = np.arange(blk)[:, None] + blk - np.arange(2 * blk)[None, :]
    in_win = (off >= 0) & (off <= blk)
    bucket = _t5_causal_bucket(np.clip(off, 0, None) * dilation)
    onehot = (bucket.reshape(-1, 1) == np.arange(NUM_BUCKETS)[None, :]).astype(np.float32)
    bias = jnp.dot(jnp.asarray(onehot), rel_bias_group.astype(F32), precision=HIGHEST)
    bias = jnp.transpose(bias.reshape(blk, 2 * blk, HEADS_PER_GROUP), (2, 0, 1))
    bias = jnp.where(in_win[None], bias, NEG_BIG)
    return bias.reshape(HEADS_PER_GROUP // 2, 2 * blk, 2 * blk)


def _attn_body(*refs, dilation):
    n_pairs = HEADS_PER_GROUP // 2
    q_refs, k_refs, v_refs = refs[0:n_pairs], refs[n_pairs:2 * n_pairs], refs[2 * n_pairs:3 * n_pairs]
    bias_ref, qw_ref, kw_ref, bd_ref = refs[3 * n_pairs:3 * n_pairs + 4]
    o_refs = refs[3 * n_pairs + 4:4 * n_pairs + 4]
    lse_ref, kbuf, vbuf = refs[4 * n_pairs + 4:]
    blk = ATTN_BLK
    n = pl.program_id(1)
    slot = n % 2

    @pl.when(n == 0)
    def _():
        kbuf[...] = jnp.zeros_like(kbuf)
        vbuf[...] = jnp.zeros_like(vbuf)

    bd = bd_ref[...]

    def head_norm(x, w_ref):
        xx = x * x
        hi = xx.astype(BF16)
        lo = (xx - hi.astype(F32)).astype(BF16)
        ss = (jnp.dot(hi, bd, preferred_element_type=F32)
              + jnp.dot(lo, bd, preferred_element_type=F32))
        return x * lax.rsqrt(ss * (1.0 / ATTN_HEAD_DIM) + NORM_EPS) * w_ref[...]

    lane = lax.broadcasted_iota(jnp.int32, (blk, LANES), 1)
    first_pen = jnp.where(n == 0, NEG_BIG, 0.0)
    lo_half = lane < ATTN_HEAD_DIM
    nt = (((1,), (1,)), ((), ()))

    def one_residue(res, carry):
        rows = pl.ds(res, blk, stride=dilation)
        lse_tile = jnp.zeros((blk, LANES), F32)
        for p in range(n_pairs):
            qp = head_norm(q_refs[p][0, rows, :], qw_ref) * (ATTN_HEAD_DIM ** -0.5)
            k = head_norm(k_refs[p][0, rows, :], kw_ref).astype(BF16)
            v = v_refs[p][0, rows, :].astype(BF16)
            idx = res * n_pairs + p
            kbuf[slot, idx] = k
            vbuf[slot, idx] = v
            q2 = jnp.concatenate([jnp.where(lo_half, qp, 0.0), jnp.where(lo_half, 0.0, qp)],
                                 axis=0).astype(BF16)
            s_prev = lax.dot_general(q2, kbuf[1 - slot, idx], nt, preferred_element_type=F32)
            s_cur = lax.dot_general(q2, k, nt, preferred_element_type=F32)
            s_prev = s_prev + bias_ref[p, :, 0:blk] + first_pen
            s_cur = s_cur + bias_ref[p, :, blk:2 * blk]
            m = jnp.maximum(jnp.max(s_prev, axis=-1, keepdims=True), jnp.max(s_cur, axis=-1, keepdims=True))
            e_prev = jnp.exp(s_prev - m)
            e_cur = jnp.exp(s_cur - m)
            d = jnp.sum(e_prev, axis=-1, keepdims=True) + jnp.sum(e_cur, axis=-1, keepdims=True)
            pv = (jnp.dot(e_prev.astype(BF16), vbuf[1 - slot, idx], preferred_element_type=F32)
                  + jnp.dot(e_cur.astype(BF16), v, preferred_element_type=F32)) / d
            o_refs[p][0, rows, :] = jnp.where(lo_half, pv[0:blk], pv[blk:2 * blk])
            lse = m + jnp.log(d)
            lse_tile = jnp.where(lane == 2 * p, lse[0:blk], lse_tile)
            lse_tile = jnp.where(lane == 2 * p + 1, lse[blk:2 * blk], lse_tile)
        lse_ref[0, rows, :] = lse_tile
        return carry

    lax.fori_loop(0, dilation, one_residue, 0)


def _attn_group(proj3, gi, dilation, bias, qw, kw, bd):
    b, s, _ = proj3.shape
    blk = ATTN_BLK
    span = blk * dilation
    n_pairs = HEADS_PER_GROUP // 2

    def pair_spec(base, p):
        cb = (base + gi * GROUP_WIDTH) // LANES + p
        return pl.BlockSpec((1, span, LANES), lambda i, n: (i, n, cb))

    def const(shape):
        return pl.BlockSpec(shape, lambda i, n: (0,) * len(shape))

    token_spec = pl.BlockSpec((1, span, LANES), lambda i, n: (i, n, 0))
    res = pl.pallas_call(
        functools.partial(_attn_body, dilation=dilation),
        grid=(b, s // span),
        in_specs=([pair_spec(COL_Q, p) for p in range(n_pairs)]
                  + [pair_spec(COL_K, p) for p in range(n_pairs)]
                  + [pair_spec(COL_V, p) for p in range(n_pairs)]
                  + [const((n_pairs, 2 * blk, 2 * blk)), const((1, LANES)), const((1, LANES)),
                     const((LANES, LANES))]),
        out_specs=[token_spec] * (n_pairs + 1),
        out_shape=[jax.ShapeDtypeStruct((b, s, LANES), F32)] * (n_pairs + 1),
        scratch_shapes=[pltpu.VMEM((2, dilation * n_pairs, blk, LANES), BF16),
                        pltpu.VMEM((2, dilation * n_pairs, blk, LANES), BF16)],
        compiler_params=_cparams(2),
    )(*([proj3] * (3 * n_pairs)), bias, qw, kw, bd)
    outs = [o.reshape(b * s, LANES) for o in res[:n_pairs]]
    return outs, res[n_pairs].reshape(b * s, LANES)


def _mix_body(*refs):
    n_pairs = HEADS_PER_GROUP // 2
    y_ref = refs[0]
    o_refs = refs[1:1 + 3 * n_pairs]
    l_refs = refs[1 + 3 * n_pairs:4 + 3 * n_pairs]
    (gs_ref, ga_ref, x_ref, wssm_ref, wattn_ref, wout_ref, e8_ref, nfw_ref, wr_ref, br_ref,
     x2_ref, h2_ref, lg_ref) = refs[4 + 3 * n_pairs:]
    tm = x_ref.shape[0]
    lses = [l[...] for l in l_refs]
    mx = jnp.maximum(jnp.maximum(lses[0], lses[1]), lses[2])
    es = [jnp.exp(l - mx) for l in lses]
    inv = 1.0 / (es[0] + es[1] + es[2])
    e8 = e8_ref[...]
    att = jnp.zeros((tm, GROUP_WIDTH), F32)
    for g in range(3):
        w = es[g] * inv
        w_hi = w.astype(BF16)
        w_lo = (w - w_hi.astype(F32)).astype(BF16)
        w_full = (jnp.dot(w_hi, e8, preferred_element_type=F32)
                  + jnp.dot(w_lo, e8, preferred_element_type=F32))
        o_g = jnp.concatenate([o_refs[g * n_pairs + p][...] for p in range(n_pairs)], axis=1)
        att = att + w_full * o_g
    y_attn = jnp.dot(att.astype(BF16), wattn_ref[...], preferred_element_type=F32)
    y_ssm = jnp.dot(y_ref[...].astype(BF16), wssm_ref[...], preferred_element_type=F32)
    merged = _sigmoid(gs_ref[...]) * y_ssm + _sigmoid(ga_ref[...]) * y_attn
    x2 = x_ref[...] + jnp.dot(merged.astype(BF16), wout_ref[...], preferred_element_type=F32)
    x2_ref[...] = x2
    ms = jnp.mean(x2 * x2, axis=-1, keepdims=True)
    h2 = x2 * lax.rsqrt(ms + NORM_EPS) * nfw_ref[...]
    h2_ref[...] = h2.reshape(tm, 1, D_MODEL)
    lg_ref[...] = jnp.dot(h2.astype(BF16), wr_ref[...], preferred_element_type=F32) + br_ref[...]


def _mix_out(y_ssm, attn_outs, attn_lses, proj, x2d, w_ssm, w_attn, w_out, norm_ffn_w, w_router, b_router):
    t = x2d.shape[0]
    tm = min(256, t)
    e8 = (np.arange(LANES)[:, None] == np.arange(GROUP_WIDTH)[None, :] // ATTN_HEAD_DIM)
    e8 = jnp.asarray(e8.astype(np.float32), dtype=BF16)

    def rows(width, cb=0):
        return pl.BlockSpec((tm, width), lambda i: (i, cb))

    def const(shape):
        return pl.BlockSpec(shape, lambda i: (0,) * len(shape))

    return pl.pallas_call(
        _mix_body,
        grid=(t // tm,),
        in_specs=[rows(D_INNER)] + [rows(LANES)] * (len(attn_outs) + len(attn_lses)) + [
                  rows(D_MODEL, COL_GS // D_MODEL), rows(D_MODEL, COL_GA // D_MODEL), rows(D_MODEL),
                  const((D_INNER, D_MODEL)), const((GROUP_WIDTH, D_MODEL)), const((D_MODEL, D_MODEL)),
                  const((LANES, GROUP_WIDTH)), const((1, D_MODEL)),
                  const((D_MODEL, LANES)), const((1, LANES))],
        out_specs=[rows(D_MODEL),
                   pl.BlockSpec((tm, 1, D_MODEL), lambda i: (i, 0, 0)),
                   rows(LANES)],
        out_shape=[jax.ShapeDtypeStruct((t, D_MODEL), F32),
                   jax.ShapeDtypeStruct((t, 1, D_MODEL), F32),
                   jax.ShapeDtypeStruct((t, LANES), F32)],
        compiler_params=_cparams(1),
    )(y_ssm, *attn_outs, *attn_lses, proj, proj, x2d, w_ssm, w_attn, w_out, e8,
      norm_ffn_w[None], w_router, b_router)


def _route_body(lg_ref, dest_ref, gate_ref, cnt_ref, counts, pstart):
    tm = lg_ref.shape[0]
    ps = pl.program_id(0)
    i = pl.program_id(1)

    @pl.when((ps == 0) & (i == 0))
    def _():
        counts[...] = jnp.zeros_like(counts)
        pstart[...] = jnp.zeros_like(pstart)

    @pl.when((ps == 1) & (i == 0))
    def _():
        cnt = counts[...]
        padded = jnp.floor((cnt + (MOE_BLOCK - 1)) * (1.0 / MOE_BLOCK)) * MOE_BLOCK
        a = lax.broadcasted_iota(jnp.int32, (LANES, LANES), 0)
        bcol = lax.broadcasted_iota(jnp.int32, (LANES, LANES), 1)
        pad_end = jnp.dot(padded, (a <= bcol).astype(F32), precision=HIGHEST, preferred_element_type=F32)
        pstart[...] = pad_end - padded
        cnt_ref[...] = cnt
        counts[...] = jnp.zeros_like(counts)

    lg = lg_ref[...]
    lane = lax.broadcasted_iota(jnp.int32, (tm, LANES), 1)
    is_coarse = lane < N_EXPERT_GROUPS
    cmax = jnp.max(jnp.where(is_coarse, lg, NEG_BIG), axis=-1, keepdims=True)
    grp = jnp.min(jnp.where(is_coarse & (lg == cmax), lane, LANES), axis=-1, keepdims=True)
    group_p = 1.0 / jnp.sum(jnp.where(is_coarse, jnp.exp(lg - cmax), 0.0), axis=-1, keepdims=True)
    f_lo = N_EXPERT_GROUPS + EXPERTS_PER_GROUP * grp
    in_grp = (lane >= f_lo) & (lane < f_lo + EXPERTS_PER_GROUP)
    f1 = jnp.max(jnp.where(in_grp, lg, NEG_BIG), axis=-1, keepdims=True)
    i1 = jnp.min(jnp.where(in_grp & (lg == f1), lane, LANES), axis=-1, keepdims=True)
    rest = in_grp & (lane != i1)
    f2 = jnp.max(jnp.where(rest, lg, NEG_BIG), axis=-1, keepdims=True)
    i2 = jnp.min(jnp.where(rest & (lg == f2), lane, LANES), axis=-1, keepdims=True)
    e2 = jnp.exp(f2 - f1)
    g1 = group_p / (1.0 + e2)
    g2 = group_p * e2 / (1.0 + e2)

    oh1 = lane == i1 - N_EXPERT_GROUPS
    oh2 = lane == i2 - N_EXPERT_GROUPS
    onehot = jnp.where(oh1 | oh2, 1.0, 0.0)
    r = lax.broadcasted_iota(jnp.int32, (tm, tm), 0)
    c = lax.broadcasted_iota(jnp.int32, (tm, tm), 1)
    before = jnp.dot((r > c).astype(BF16), onehot.astype(BF16), preferred_element_type=F32)
    pos = pstart[0:1, :] + counts[0:1, :] + before
    d1 = jnp.sum(jnp.where(oh1, pos, 0.0), axis=-1, keepdims=True)
    d2 = jnp.sum(jnp.where(oh2, pos, 0.0), axis=-1, keepdims=True)
    counts[...] = counts[...] + jnp.sum(onehot, axis=0, keepdims=True)
    dest_ref[...] = jnp.where(lane == 0, d1, jnp.where(lane == 1, d2, 0.0)).astype(jnp.int32)
    gate_ref[...] = jnp.where(lane == 0, g1, jnp.where(lane == 1, g2, 0.0))


def _route(logits):
    t = logits.shape[0]
    tm = min(512, t)
    return pl.pallas_call(
        _route_body,
        grid=(2, t // tm),
        in_specs=[pl.BlockSpec((tm, LANES), lambda ps, i: (i, 0))],
        out_specs=[pl.BlockSpec((tm, LANES), lambda ps, i: (i * ps, 0)),
                   pl.BlockSpec((tm, LANES), lambda ps, i: (i * ps, 0)),
                   pl.BlockSpec((8, LANES), lambda ps, i: (0, 0))],
        out_shape=[jax.ShapeDtypeStruct((t, LANES), jnp.int32),
                   jax.ShapeDtypeStruct((t, LANES), F32),
                   jax.ShapeDtypeStruct((8, LANES), F32)],
        scratch_shapes=[pltpu.VMEM((8, LANES), F32), pltpu.VMEM((8, LANES), F32)],
        compiler_params=_cparams(2),
    )(logits)


def _dispatch_body(meta_ref, dest_ref, h_ref, rows_ref, zbuf, zsem, sem):
    tm = h_ref.shape[0]

    def zero_copy(e):
        start = pl.multiple_of(meta_ref[e] - MOE_BLOCK, MOE_BLOCK)
        return pltpu.make_async_copy(zbuf, rows_ref.at[pl.ds(start, MOE_BLOCK)], zsem)

    def tail_copy(blk):
        start = pl.multiple_of(blk * MOE_BLOCK, MOE_BLOCK)
        return pltpu.make_async_copy(zbuf, rows_ref.at[pl.ds(start, MOE_BLOCK)], zsem)

    @pl.when(pl.program_id(0) == 0)
    def _():
        zbuf[...] = jnp.zeros_like(zbuf)

        def start_zero(e, carry):
            @pl.when(meta_ref[N_EXPERTS + e] > 0)
            def _():
                zero_copy(e).start()
            return carry

        def wait_zero(e, carry):
            @pl.when(meta_ref[N_EXPERTS + e] > 0)
            def _():
                zero_copy(e).wait()
            return carry

        def start_tail(blk, carry):
            tail_copy(blk).start()
            return carry

        def wait_tail(blk, carry):
            tail_copy(blk).wait()
            return carry

        n_used = meta_ref[2 * N_EXPERTS]
        n_blocks = rows_ref.shape[0] // MOE_BLOCK
        lax.fori_loop(0, N_EXPERTS, start_zero, 0)
        lax.fori_loop(n_used, n_blocks, start_tail, 0)
        lax.fori_loop(0, N_EXPERTS, wait_zero, 0)
        lax.fori_loop(n_used, n_blocks, wait_tail, 0)

    def issue(j, carry):
        for k in range(2):
            d = dest_ref[0, 2 * j + k]
            pltpu.make_async_copy(h_ref.at[j], rows_ref.at[d], sem).start()
        return carry

    lax.fori_loop(0, tm, issue, 0, unroll=8)
    for _ in range(2):
        pltpu.make_async_copy(h_ref, rows_ref.at[pl.ds(0, tm)], sem).wait()


def _dispatch(meta, dest3, h2, n_rows):
    t = h2.shape[0]
    tm = dest3.shape[2] // 2
    return pl.pallas_call(
        _dispatch_body,
        grid_spec=pltpu.PrefetchScalarGridSpec(
            num_scalar_prefetch=1,
            grid=(t // tm,),
            in_specs=[pl.BlockSpec((None, 1, 2 * tm), lambda i, m: (i, 0, 0), memory_space=pltpu.SMEM),
                      pl.BlockSpec((tm, 1, D_MODEL), lambda i, m: (i, 0, 0))],
            out_specs=pl.BlockSpec(memory_space=pl.ANY),
            scratch_shapes=[pltpu.VMEM((MOE_BLOCK, 1, D_MODEL), F32),
                            pltpu.SemaphoreType.DMA(()),
                            pltpu.SemaphoreType.DMA(())]),
        out_shape=jax.ShapeDtypeStruct((n_rows, 1, D_MODEL), F32),
        compiler_params=_cparams(1),
    )(meta, dest3, h2)


def _expert_body(be_ref, nu_ref, x_ref, wg_ref, wu_ref, wd_ref, y_ref, x2d, wg_b, wu_b, wd_b):
    blk = pl.program_id(0)

    @pl.when(blk < nu_ref[0])
    def _():
        prev = be_ref[jnp.maximum(blk - 1, 0)]

        @pl.when((blk == 0) | (be_ref[blk] != prev))
        def _():
            wg_b[...] = wg_ref[0].astype(BF16)
            wu_b[...] = wu_ref[0].astype(BF16)
            wd_b[...] = wd_ref[0].astype(BF16)

        x2d[...] = x_ref[...].reshape(MOE_BLOCK, D_MODEL)
        x = x2d[...].astype(BF16)
        g = jnp.dot(x, wg_b[...], preferred_element_type=F32)
        u = jnp.dot(x, wu_b[...], preferred_element_type=F32)
        hid = (g * _sigmoid(g)) * u
        y = jnp.dot(hid.astype(BF16), wd_b[...], preferred_element_type=F32)
        y_ref[...] = y.reshape(MOE_BLOCK, 1, D_MODEL)

    @pl.when(blk >= nu_ref[0])
    def _():
        y_ref[...] = jnp.zeros_like(y_ref)


def _experts(block_expert, n_used, rows, w_gate, w_up, w_down):
    n_rows = rows.shape[0]
    n_blocks = n_rows // MOE_BLOCK

    def row_map(b, be, nu):
        return (b, 0, 0)

    def w_map(b, be, nu):
        return (be[jnp.minimum(b, nu[0] - 1)], 0, 0)

    return pl.pallas_call(
        _expert_body,
        grid_spec=pltpu.PrefetchScalarGridSpec(
            num_scalar_prefetch=2,
            grid=(n_blocks,),
            in_specs=[pl.BlockSpec((MOE_BLOCK, 1, D_MODEL), row_map),
                      pl.BlockSpec((1, D_MODEL, D_EXPERT), w_map),
                      pl.BlockSpec((1, D_MODEL, D_EXPERT), w_map),
                      pl.BlockSpec((1, D_EXPERT, D_MODEL), w_map)],
            out_specs=pl.BlockSpec((MOE_BLOCK, 1, D_MODEL), row_map),
            scratch_shapes=[pltpu.VMEM((MOE_BLOCK, D_MODEL), F32),
                            pltpu.VMEM((D_MODEL, D_EXPERT), BF16),
                            pltpu.VMEM((D_MODEL, D_EXPERT), BF16),
                            pltpu.VMEM((D_EXPERT, D_MODEL), BF16)]),
        out_shape=jax.ShapeDtypeStruct((n_rows, 1, D_MODEL), F32),
        compiler_params=_cparams(1),
    )(block_expert, n_used, rows, w_gate, w_up, w_down)


def _combine_body(dcur_ref, dnext_ref, gate_ref, x2_ref, y_ref, o_ref, buf_a, buf_b, y2d, sem_a, sem_b):
    tm = x2_ref.shape[0]
    i = pl.program_id(0)
    n = pl.num_programs(0)

    def issue(dref, buf, sem):
        def body(j, carry):
            for k in range(2):
                d = dref[0, 2 * j + k]
                pltpu.make_async_copy(y_ref.at[d], buf.at[k * tm + j], sem).start()
            return carry
        lax.fori_loop(0, tm, body, 0, unroll=8)

    def finish(buf, sem):
        pltpu.make_async_copy(y_ref.at[pl.ds(0, 2 * tm)], buf, sem).wait()
        y2d[...] = buf[...].reshape(2 * tm, D_MODEL)
        g = gate_ref[...]
        o_ref[...] = x2_ref[...] + g[:, 0:1] * y2d[0:tm, :] + g[:, 1:2] * y2d[tm:2 * tm, :]

    @pl.when(i == 0)
    def _():
        issue(dcur_ref, buf_a, sem_a)

    for par, (cur, cur_sem, nxt, nxt_sem) in enumerate(((buf_a, sem_a, buf_b, sem_b),
                                                        (buf_b, sem_b, buf_a, sem_a))):
        @pl.when(i % 2 == par)
        def _(cur=cur, cur_sem=cur_sem, nxt=nxt, nxt_sem=nxt_sem):
            @pl.when(i + 1 < n)
            def _():
                issue(dnext_ref, nxt, nxt_sem)
            finish(cur, cur_sem)


def _combine(dest3, gates, x2, y_rows):
    t = x2.shape[0]
    tm = dest3.shape[2] // 2
    nt = t // tm
    return pl.pallas_call(
        _combine_body,
        grid=(nt,),
        in_specs=[pl.BlockSpec((None, 1, 2 * tm), lambda i: (i, 0, 0), memory_space=pltpu.SMEM),
                  pl.BlockSpec((None, 1, 2 * tm), lambda i: (jnp.minimum(i + 1, nt - 1), 0, 0),
                               memory_space=pltpu.SMEM),
                  pl.BlockSpec((tm, LANES), lambda i: (i, 0)),
                  pl.BlockSpec((tm, D_MODEL), lambda i: (i, 0)),
                  pl.BlockSpec(memory_space=pl.ANY)],
        out_specs=pl.BlockSpec((tm, D_MODEL), lambda i: (i, 0)),
        out_shape=jax.ShapeDtypeStruct((t, D_MODEL), F32),
        scratch_shapes=[pltpu.VMEM((2 * tm, 1, D_MODEL), F32),
                        pltpu.VMEM((2 * tm, 1, D_MODEL), F32),
                        pltpu.VMEM((2 * tm, D_MODEL), F32),
                        pltpu.SemaphoreType.DMA(()),
                        pltpu.SemaphoreType.DMA(())],
        compiler_params=_cparams(1),
    )(dest3, dest3, gates, x2, y_rows)


def _layer(x, norm_mix_w, w_in, conv_w, conv_b, dt_bias, a_log, d_skip, ssm_norm_w, w_ssm_proj,
           q_norm_w, k_norm_w, rel_bias, w_attn_proj, w_out, norm_ffn_w, w_coarse, b_coarse,
           w_fine, b_fine, w_gate_exp, w_up_exp, w_down_exp):
    b, s, d = x.shape
    t = b * s
    x2d = x.reshape(t, d)

    dt_lo = D_INNER + D_INNER + BC_WIDTH
    w_packed = jnp.concatenate(
        [w_in[:, :dt_lo], w_in[:, dt_lo + SSM_HEADS:], w_in[:, dt_lo:dt_lo + SSM_HEADS],
         jnp.zeros((d, PROJ_WIDTH - COL_DT - SSM_HEADS), w_in.dtype)], axis=1).astype(BF16)
    proj = _in_proj(x2d, norm_mix_w[None], w_packed)
    proj3 = proj.reshape(b, s, PROJ_WIDTH)

    y_ssm = _ssd(proj3, conv_w, conv_b, dt_bias, a_log, d_skip, ssm_norm_w).reshape(t, D_INNER)

    qw = jnp.tile(q_norm_w, 2)[None]
    kw = jnp.tile(k_norm_w, 2)[None]
    lane_head = np.arange(LANES) // ATTN_HEAD_DIM
    bd = jnp.asarray((lane_head[:, None] == lane_head[None, :]).astype(np.float32), dtype=BF16)
    attn_outs, attn_lses = [], []
    for gi, (window, dilation) in enumerate(DILATED_CONFIGS):
        assert window // dilation == ATTN_BLK and s % window == 0
        bias = _band_bias(rel_bias[:, gi * HEADS_PER_GROUP:(gi + 1) * HEADS_PER_GROUP], dilation)
        o, l = _attn_group(proj3, gi, dilation, bias, qw, kw, bd)
        attn_outs.extend(o)
        attn_lses.append(l)

    n_route = N_EXPERT_GROUPS + N_EXPERTS
    w_router = jnp.pad(jnp.concatenate([w_coarse, w_fine], axis=1), ((0, 0), (0, LANES - n_route)))
    b_router = jnp.pad(jnp.concatenate([b_coarse, b_fine]), (0, LANES - n_route))[None]
    x2, h2, logits = _mix_out(y_ssm, attn_outs, attn_lses, proj, x2d, w_ssm_proj.astype(BF16),
                              w_attn_proj.astype(BF16), w_out.astype(BF16), norm_ffn_w,
                              w_router.astype(BF16), b_router)

    dest, gates, counts = _route(logits)

    cnt = counts[0, :N_EXPERTS].astype(jnp.int32)
    padded = (cnt + MOE_BLOCK - 1) // MOE_BLOCK * MOE_BLOCK
    pad_end = jnp.cumsum(padded)
    n_blocks = -(-(2 * t + N_EXPERTS * (MOE_BLOCK - 1)) // MOE_BLOCK)
    block_start = jnp.arange(n_blocks, dtype=jnp.int32) * MOE_BLOCK
    block_expert = jnp.minimum(jnp.sum((pad_end[None, :] <= block_start[:, None]).astype(jnp.int32), axis=1),
                               N_EXPERTS - 1)
    n_used = (pad_end[-1:] // MOE_BLOCK).astype(jnp.int32)
    meta = jnp.concatenate([pad_end, padded, n_used]).astype(jnp.int32)

    tm_d = min(256, t)
    dest_d = dest[:, :2].reshape(t // tm_d, 1, 2 * tm_d)
    rows = _dispatch(meta, dest_d, h2, n_blocks * MOE_BLOCK)
    y_rows = _experts(block_expert, n_used, rows, w_gate_exp, w_up_exp, w_down_exp)
    tm_c = min(128, t)
    dest_c = dest[:, :2].reshape(t // tm_c, 1, 2 * tm_c)
    out = _combine(dest_c, gates, x2, y_rows)
    return out.reshape(b, s, d)


def kernel(x, norm_mix_w, w_in, conv_w, conv_b, dt_bias, a_log, d_skip, ssm_norm_w, w_ssm_proj,
           q_norm_w, k_norm_w, rel_bias, w_attn_proj, w_out, norm_ffn_w, w_coarse, b_coarse,
           w_fine, b_fine, w_gate_exp, w_up_exp, w_down_exp):
    depth = norm_mix_w.shape[0]
    for layer in range(depth):
        x = _layer(x, norm_mix_w[layer], w_in[layer], conv_w[layer], conv_b[layer], dt_bias[layer],
                   a_log[layer], d_skip[layer], ssm_norm_w[layer], w_ssm_proj[layer],
                   q_norm_w[layer], k_norm_w[layer], rel_bias, w_attn_proj[layer], w_out[layer],
                   norm_ffn_w[layer], w_coarse[layer], b_coarse[layer], w_fine[layer], b_fine[layer],
                   w_gate_exp[layer], w_up_exp[layer], w_down_exp[layer])
    return x
```

```python
import functools
import math

import jax
import jax.numpy as jnp
import numpy as np
from jax import lax
from jax.experimental import pallas as pl
from jax.experimental.pallas import tpu as pltpu

F32 = jnp.float32
BF16 = jnp.bfloat16
HIGHEST = lax.Precision.HIGHEST

LANES = 128
NORM_EPS = 1e-6
NEG_BIG = -1e30

D_MODEL = 1024
D_INNER = 2048
SSM_HEAD_DIM = 64
SSM_HEADS = 32
SSM_GROUPS = 2
D_STATE = 128
CONV_K = 4
BC_WIDTH = 2 * SSM_GROUPS * D_STATE
SSD_CHUNK = 128
ATTN_HEAD_DIM = 64
DILATED_CONFIGS = ((128, 1), (512, 4), (2048, 16))
HEADS_PER_GROUP = 8
GROUP_WIDTH = HEADS_PER_GROUP * ATTN_HEAD_DIM
ATTN_WIDTH = 3 * GROUP_WIDTH
ATTN_BLK = 128
NUM_BUCKETS = 32
MAX_DISTANCE = 2048
N_EXPERT_GROUPS = 8
EXPERTS_PER_GROUP = 8
N_EXPERTS = 64
D_EXPERT = 512
MOE_BLOCK = 256

COL_Z = 0
COL_XS = D_INNER
COL_BC = 2 * D_INNER
COL_Q = COL_BC + BC_WIDTH
COL_K = COL_Q + ATTN_WIDTH
COL_V = COL_K + ATTN_WIDTH
COL_GS = COL_V + ATTN_WIDTH
COL_GA = COL_GS + D_MODEL
COL_DT = COL_GA + D_MODEL
PROJ_WIDTH = COL_DT + 512

VMEM_LIMIT = 56 * 1024 * 1024


def _sigmoid(x):
    return 1.0 / (1.0 + jnp.exp(-x))


def _cparams(n_axes):
    return pltpu.CompilerParams(dimension_semantics=("arbitrary",) * n_axes,
                                vmem_limit_bytes=VMEM_LIMIT)


def _in_proj_body(x_ref, nw_ref, w_ref, o_ref, h_scr):
    @pl.when(pl.program_id(1) == 0)
    def _():
        x = x_ref[...]
        ms = jnp.mean(x * x, axis=-1, keepdims=True)
        h_scr[...] = (x * lax.rsqrt(ms + NORM_EPS) * nw_ref[...]).astype(BF16)

    o_ref[...] = jnp.dot(h_scr[...], w_ref[...], preferred_element_type=F32)


def _in_proj(x2d, norm_w, w):
    t, d = x2d.shape
    n = w.shape[1]
    tm = min(2048, t)
    tn = 512
    return pl.pallas_call(
        _in_proj_body,
        grid=(t // tm, n // tn),
        in_specs=[pl.BlockSpec((tm, d), lambda i, j: (i, 0)),
                  pl.BlockSpec((1, d), lambda i, j: (0, 0)),
                  pl.BlockSpec((d, tn), lambda i, j: (0, j))],
        out_specs=pl.BlockSpec((tm, tn), lambda i, j: (i, j)),
        out_shape=jax.ShapeDtypeStruct((t, n), F32),
        scratch_shapes=[pltpu.VMEM((tm, d), BF16)],
        compiler_params=_cparams(2),
    )(x2d, norm_w, w)


def _conv_silu(x, tail, w_ref, b_ref):
    b = b_ref[...]
    w3 = w_ref[3:4, :]
    acc = b + w3 * x
    xc = jnp.concatenate([tail, x[0:8]], axis=0)
    hacc = b + w3 * xc
    for k in range(CONV_K - 1):
        s = CONV_K - 1 - k
        wk = w_ref[k:k + 1, :]
        acc = acc + wk * pltpu.roll(x, s, 0)
        hacc = hacc + wk * pltpu.roll(xc, s, 0)
    out = jnp.concatenate([hacc[8:16], acc[8:]], axis=0)
    return out * _sigmoid(out)


def _ssd_body(z_ref, xs_ref, bc_ref, dt_ref, cwx_ref, cbx_ref, cwb_ref, cbb_ref, dtb_ref, alog_ref,
              dskip_ref, nw_ref, e_ref, y_ref, state, tail_x, tail_bc):
    L = SSD_CHUNK
    half = D_INNER // SSM_GROUPS

    @pl.when(pl.program_id(1) == 0)
    def _():
        state[...] = jnp.zeros_like(state)
        tail_x[...] = jnp.zeros_like(tail_x)
        tail_bc[...] = jnp.zeros_like(tail_bc)

    xs_raw = xs_ref[0]
    bc_raw = bc_ref[0]
    xs = _conv_silu(xs_raw, tail_x[...], cwx_ref, cbx_ref)
    bc = _conv_silu(bc_raw, tail_bc[...], cwb_ref, cbb_ref)
    tail_x[...] = xs_raw[L - 8:L]
    tail_bc[...] = bc_raw[L - 8:L]

    lane = lax.broadcasted_iota(jnp.int32, (L, LANES), 1)
    row = lax.broadcasted_iota(jnp.int32, (L, L), 0)
    col = lax.broadcasted_iota(jnp.int32, (L, L), 1)
    causal = row >= col

    v = dt_ref[0] + dtb_ref[...]
    dt = jnp.maximum(v, 0.0) + jnp.log1p(jnp.exp(-jnp.abs(v)))
    head_lane = lane < SSM_HEADS
    dt = jnp.where(head_lane, dt, 0.0)
    adt = dt * (-jnp.exp(alog_ref[...]))
    a_cs = jnp.dot(causal.astype(F32), adt, precision=HIGHEST, preferred_element_type=F32)
    a_cs_t = a_cs.T

    expand = e_ref[...]
    a_full = jnp.dot(a_cs, expand, precision=HIGHEST, preferred_element_type=F32)
    dt_full = jnp.dot(dt, expand, precision=HIGHEST, preferred_element_type=F32)
    a_tot = a_full[L - 1:L, :]
    decay_from_start = jnp.exp(a_full)
    decay_to_end = jnp.exp(a_tot - a_full)
    decay_chunk = jnp.exp(a_tot)

    xdt = xs * dt_full
    xdt_b = xdt.astype(BF16)
    xw_b = (xdt * decay_to_end).astype(BF16)

    y_groups = []
    for g in range(SSM_GROUPS):
        gs = slice(g * half, (g + 1) * half)
        bg = bc[:, g * D_STATE:(g + 1) * D_STATE]
        cg = bc[:, (SSM_GROUPS + g) * D_STATE:(SSM_GROUPS + g + 1) * D_STATE]
        cb16 = cg.astype(BF16)
        cb = lax.dot_general(cb16, bg.astype(BF16), (((1,), (1,)), ((), ())),
                             preferred_element_type=F32)
        s_prev = state[:, gs]
        y_off = jnp.dot(cb16, s_prev.astype(BF16), preferred_element_type=F32) * decay_from_start[:, gs]

        pair_cols = []
        for pr in range(half // LANES):
            h0 = g * (SSM_HEADS // SSM_GROUPS) + 2 * pr
            xp = xdt_b[:, h0 * SSM_HEAD_DIM:(h0 + 2) * SSM_HEAD_DIM]
            ys = []
            for h in (h0, h0 + 1):
                seg = a_cs[:, h:h + 1] - a_cs_t[h:h + 1, :]
                m = (cb * jnp.exp(jnp.where(causal, seg, NEG_BIG))).astype(BF16)
                ys.append(jnp.dot(m, xp, preferred_element_type=F32))
            pair_cols.append(jnp.where(lane < SSM_HEAD_DIM, ys[0], ys[1]))
        y_diag = jnp.concatenate(pair_cols, axis=1)

        state[:, gs] = decay_chunk[:, gs] * s_prev + jnp.dot(
            bg.T.astype(BF16), xw_b[:, gs], preferred_element_type=F32)
        y_groups.append(y_diag + y_off)

    y = jnp.concatenate(y_groups, axis=1) + dskip_ref[...] * xs
    z = z_ref[0]
    y = y * (z * _sigmoid(z))
    normed = []
    for g in range(SSM_GROUPS):
        yg = y[:, g * half:(g + 1) * half]
        ms = jnp.mean(yg * yg, axis=-1, keepdims=True)
        normed.append(yg * lax.rsqrt(ms + NORM_EPS))
    y_ref[0] = jnp.concatenate(normed, axis=1) * nw_ref[...]


def _ssd(proj3, conv_w, conv_b, dt_bias, a_log, d_skip, ssm_norm_w):
    b, s, _ = proj3.shape
    L = SSD_CHUNK
    pad = LANES - SSM_HEADS
    cwx, cwb = conv_w[:, :D_INNER], conv_w[:, D_INNER:]
    cbx, cbb = conv_b[None, :D_INNER], conv_b[None, D_INNER:]
    dtb = jnp.pad(dt_bias, (0, pad))[None]
    alog = jnp.pad(a_log, (0, pad))[None]
    dskip = jnp.repeat(d_skip, SSM_HEAD_DIM)[None]
    expand = (np.arange(LANES)[:, None] == np.arange(D_INNER)[None, :] // SSM_HEAD_DIM).astype(np.float32)

    def const(shape):
        return pl.BlockSpec(shape, lambda i, c: (0,) * len(shape))

    return pl.pallas_call(
        _ssd_body,
        grid=(b, s // L),
        in_specs=[pl.BlockSpec((1, L, D_INNER), lambda i, c: (i, c, COL_Z // D_INNER)),
                  pl.BlockSpec((1, L, D_INNER), lambda i, c: (i, c, COL_XS // D_INNER)),
                  pl.BlockSpec((1, L, BC_WIDTH), lambda i, c: (i, c, COL_BC // BC_WIDTH)),
                  pl.BlockSpec((1, L, LANES), lambda i, c: (i, c, COL_DT // LANES)),
                  const((CONV_K, D_INNER)), const((1, D_INNER)),
                  const((CONV_K, BC_WIDTH)), const((1, BC_WIDTH)),
                  const((1, LANES)), const((1, LANES)),
                  const((1, D_INNER)), const((1, D_INNER)),
                  const((LANES, D_INNER))],
        out_specs=pl.BlockSpec((1, L, D_INNER), lambda i, c: (i, c, 0)),
        out_shape=jax.ShapeDtypeStruct((b, s, D_INNER), F32),
        scratch_shapes=[pltpu.VMEM((D_STATE, D_INNER), F32),
                        pltpu.VMEM((8, D_INNER), F32),
                        pltpu.VMEM((8, BC_WIDTH), F32)],
        compiler_params=_cparams(2),
    )(proj3, proj3, proj3, proj3, cwx, cbx, cwb, cbb, dtb, alog, dskip, ssm_norm_w[None],
      jnp.asarray(expand))


def _t5_causal_bucket(dist):
    max_exact = NUM_BUCKETS // 2
    large = max_exact + (np.log(np.maximum(dist, max_exact) / max_exact)
                         / math.log(MAX_DISTANCE / max_exact) * (NUM_BUCKETS - max_exact)).astype(np.int32)
    return np.where(dist < max_exact, dist, np.minimum(large, NUM_BUCKETS - 1)).astype(np.int32)


def _band_bias(rel_bias_group, dilation):
    blk = ATTN_BLK
    off = np.arange(blk)[:, None] + blk - np.arange(2 * blk)[None, :]
    in_win = (off >= 0) & (off <= blk)
    bucket = _t5_causal_bucket(np.clip(off, 0, None) * dilation)
    onehot = (bucket.reshape(-1, 1) == np.arange(NUM_BUCKETS)[None, :]).astype(np.float32)
    bias = jnp.dot(jnp.asarray(onehot), rel_bias_group.astype(F32), precision=HIGHEST)
    bias = jnp.transpose(bias.reshape(blk, 2 * blk, HEADS_PER_GROUP), (2, 0, 1))
    bias = jnp.where(in_win[None], bias, NEG_BIG)
    return bias.reshape(HEADS_PER_GROUP // 2, 2 * blk, 2 * blk)


def _attn_body(*refs, dilation):
    n_pairs = HEADS_PER_GROUP // 2
    q_refs, k_refs, v_refs = refs[0:n_pairs], refs[n_pairs:2 * n_pairs], refs[2 * n_pairs:3 * n_pairs]
    bias_ref, qw_ref, kw_ref, bd_ref = refs[3 * n_pairs:3 * n_pairs + 4]
    o_refs = refs[3 * n_pairs + 4:4 * n_pairs + 4]
    lse_ref, kbuf, vbuf = refs[4 * n_pairs + 4:]
    blk = ATTN_BLK
    n = pl.program_id(1)
    slot = n % 2

    @pl.when(n == 0)
    def _():
        kbuf[...] = jnp.zeros_like(kbuf)
        vbuf[...] = jnp.zeros_like(vbuf)

    bd = bd_ref[...]

    def head_norm(x, w_ref):
        xx = x * x
        hi = xx.astype(BF16)
        lo = (xx - hi.astype(F32)).astype(BF16)
        ss = (jnp.dot(hi, bd, preferred_element_type=F32)
              + jnp.dot(lo, bd, preferred_element_type=F32))
        return x * lax.rsqrt(ss * (1.0 / ATTN_HEAD_DIM) + NORM_EPS) * w_ref[...]

    lane = lax.broadcasted_iota(jnp.int32, (blk, LANES), 1)
    first_pen = jnp.where(n == 0, NEG_BIG, 0.0)
    lo_half = lane < ATTN_HEAD_DIM
    nt = (((1,), (1,)), ((), ()))

    def one_residue(res, carry):
        rows = pl.ds(res, blk, stride=dilation)
        q_raw = [q_refs[p][0, rows, :] for p in range(n_pairs)]
        k_raw = [k_refs[p][0, rows, :] for p in range(n_pairs)]
        v_raw = [v_refs[p][0, rows, :] for p in range(n_pairs)]
        k_old = [kbuf[1 - slot, res * n_pairs + p] for p in range(n_pairs)]
        v_old = [vbuf[1 - slot, res * n_pairs + p] for p in range(n_pairs)]
        pairs = range(n_pairs)
        qn = head_norm(jnp.concatenate(q_raw, axis=0), qw_ref) * (ATTN_HEAD_DIM ** -0.5)
        kn = head_norm(jnp.concatenate(k_raw, axis=0), kw_ref).astype(BF16)
        k_new = [kn[p * blk:(p + 1) * blk] for p in pairs]
        v_new = [v_raw[p].astype(BF16) for p in pairs]
        q2 = []
        for p in pairs:
            qp = qn[p * blk:(p + 1) * blk]
            q2.append(jnp.concatenate([jnp.where(lo_half, qp, 0.0), jnp.where(lo_half, 0.0, qp)],
                                      axis=0).astype(BF16))
        s_prev = [lax.dot_general(q2[p], k_old[p], nt, preferred_element_type=F32) for p in pairs]
        s_cur = [lax.dot_general(q2[p], k_new[p], nt, preferred_element_type=F32) for p in pairs]
        e_prev, e_cur, m, d = [], [], [], []
        for p in pairs:
            sp = s_prev[p] + (bias_ref[p, :, 0:blk] + first_pen)
            sc = s_cur[p] + bias_ref[p, :, blk:2 * blk]
            mp = jnp.max(jnp.maximum(sp, sc), axis=-1, keepdims=True)
            ep = jnp.exp(sp - mp)
            ec = jnp.exp(sc - mp)
            m.append(mp)
            d.append(jnp.sum(ep + ec, axis=-1, keepdims=True))
            e_prev.append(ep.astype(BF16))
            e_cur.append(ec.astype(BF16))
        pv_prev = [jnp.dot(e_prev[p], v_old[p], preferred_element_type=F32) for p in pairs]
        pv_cur = [jnp.dot(e_cur[p], v_new[p], preferred_element_type=F32) for p in pairs]
        lse_tile = jnp.zeros((blk, LANES), F32)
        o_new = []
        for p in pairs:
            pv = (pv_prev[p] + pv_cur[p]) / d[p]
            o_new.append(jnp.where(lo_half, pv[0:blk], pv[blk:2 * blk]))
            lse = m[p] + jnp.log(d[p])
            lse_tile = jnp.where(lane == 2 * p, lse[0:blk], lse_tile)
            lse_tile = jnp.where(lane == 2 * p + 1, lse[blk:2 * blk], lse_tile)
        for p in pairs:
            kbuf[slot, res * n_pairs + p] = k_new[p]
            vbuf[slot, res * n_pairs + p] = v_new[p]
            o_refs[p][0, rows, :] = o_new[p]
        lse_ref[0, rows, :] = lse_tile
        return carry

    lax.fori_loop(0, dilation, one_residue, 0)


def _attn_group(proj3, gi, dilation, bias, qw, kw, bd):
    b, s, _ = proj3.shape
    blk = ATTN_BLK
    span = blk * dilation
    n_pairs = HEADS_PER_GROUP // 2

    def pair_spec(base, p):
        cb = (base + gi * GROUP_WIDTH) // LANES + p
        return pl.BlockSpec((1, span, LANES), lambda i, n: (i, n, cb))

    def const(shape):
        return pl.BlockSpec(shape, lambda i, n: (0,) * len(shape))

    token_spec = pl.BlockSpec((1, span, LANES), lambda i, n: (i, n, 0))
    res = pl.pallas_call(
        functools.partial(_attn_body, dilation=dilation),
        grid=(b, s // span),
        in_specs=([pair_spec(COL_Q, p) for p in range(n_pairs)]
                  + [pair_spec(COL_K, p) for p in range(n_pairs)]
                  + [pair_spec(COL_V, p) for p in range(n_pairs)]
                  + [const((n_pairs, 2 * blk, 2 * blk)), const((1, LANES)), const((1, LANES)),
                     const((LANES, LANES))]),
        out_specs=[token_spec] * (n_pairs + 1),
        out_shape=[jax.ShapeDtypeStruct((b, s, LANES), F32)] * (n_pairs + 1),
        scratch_shapes=[pltpu.VMEM((2, dilation * n_pairs, blk, LANES), BF16),
                        pltpu.VMEM((2, dilation * n_pairs, blk, LANES), BF16)],
        compiler_params=_cparams(2),
    )(*([proj3] * (3 * n_pairs)), bias, qw, kw, bd)
    outs = [o.reshape(b * s, LANES) for o in res[:n_pairs]]
    return outs, res[n_pairs].reshape(b * s, LANES)


def _mix_body(*refs):
    n_pairs = HEADS_PER_GROUP // 2
    y_ref = refs[0]
    o_refs = refs[1:1 + 3 * n_pairs]
    l_refs = refs[1 + 3 * n_pairs:4 + 3 * n_pairs]
    (gs_ref, ga_ref, x_ref, wssm_ref, wattn_ref, wout_ref, e8_ref, nfw_ref, wr_ref, br_ref,
     x2_ref, h2_ref, lg_ref) = refs[4 + 3 * n_pairs:]
    tm = x_ref.shape[0]
    lses = [l[...] for l in l_refs]
    mx = jnp.maximum(jnp.maximum(lses[0], lses[1]), lses[2])
    es = [jnp.exp(l - mx) for l in lses]
    inv = 1.0 / (es[0] + es[1] + es[2])
    e8 = e8_ref[...]
    att = jnp.zeros((tm, GROUP_WIDTH), F32)
    for g in range(3):
        w = es[g] * inv
        w_hi = w.astype(BF16)
        w_lo = (w - w_hi.astype(F32)).astype(BF16)
        w_full = (jnp.dot(w_hi, e8, preferred_element_type=F32)
                  + jnp.dot(w_lo, e8, preferred_element_type=F32))
        o_g = jnp.concatenate([o_refs[g * n_pairs + p][...] for p in range(n_pairs)], axis=1)
        att = att + w_full * o_g
    y_attn = jnp.dot(att.astype(BF16), wattn_ref[...], preferred_element_type=F32)
    y_ssm = jnp.dot(y_ref[...].astype(BF16), wssm_ref[...], preferred_element_type=F32)
    merged = _sigmoid(gs_ref[...]) * y_ssm + _sigmoid(ga_ref[...]) * y_attn
    x2 = x_ref[...] + jnp.dot(merged.astype(BF16), wout_ref[...], preferred_element_type=F32)
    x2_ref[...] = x2
    ms = jnp.mean(x2 * x2, axis=-1, keepdims=True)
    h2 = x2 * lax.rsqrt(ms + NORM_EPS) * nfw_ref[...]
    h2_ref[...] = h2.reshape(tm, 1, D_MODEL)
    lg_ref[...] = jnp.dot(h2.astype(BF16), wr_ref[...], preferred_element_type=F32) + br_ref[...]


def _mix_out(y_ssm, attn_outs, attn_lses, proj, x2d, w_ssm, w_attn, w_out, norm_ffn_w, w_router, b_router):
    t = x2d.shape[0]
    tm = min(256, t)
    e8 = (np.arange(LANES)[:, None] == np.arange(GROUP_WIDTH)[None, :] // ATTN_HEAD_DIM)
    e8 = jnp.asarray(e8.astype(np.float32), dtype=BF16)

    def rows(width, cb=0):
        return pl.BlockSpec((tm, width), lambda i: (i, cb))

    def const(shape):
        return pl.BlockSpec(shape, lambda i: (0,) * len(shape))

    return pl.pallas_call(
        _mix_body,
        grid=(t // tm,),
        in_specs=[rows(D_INNER)] + [rows(LANES)] * (len(attn_outs) + len(attn_lses)) + [
                  rows(D_MODEL, COL_GS // D_MODEL), rows(D_MODEL, COL_GA // D_MODEL), rows(D_MODEL),
                  const((D_INNER, D_MODEL)), const((GROUP_WIDTH, D_MODEL)), const((D_MODEL, D_MODEL)),
                  const((LANES, GROUP_WIDTH)), const((1, D_MODEL)),
                  const((D_MODEL, LANES)), const((1, LANES))],
        out_specs=[rows(D_MODEL),
                   pl.BlockSpec((tm, 1, D_MODEL), lambda i: (i, 0, 0)),
                   rows(LANES)],
        out_shape=[jax.ShapeDtypeStruct((t, D_MODEL), F32),
                   jax.ShapeDtypeStruct((t, 1, D_MODEL), F32),
                   jax.ShapeDtypeStruct((t, LANES), F32)],
        compiler_params=_cparams(1),
    )(y_ssm, *attn_outs, *attn_lses, proj, proj, x2d, w_ssm, w_attn, w_out, e8,
      norm_ffn_w[None], w_router, b_router)


def _route_body(lg_ref, dest_ref, gate_ref, cnt_ref, counts, pstart):
    tm = lg_ref.shape[0]
    ps = pl.program_id(0)
    i = pl.program_id(1)

    @pl.when((ps == 0) & (i == 0))
    def _():
        counts[...] = jnp.zeros_like(counts)
        pstart[...] = jnp.zeros_like(pstart)

    @pl.when((ps == 1) & (i == 0))
    def _():
        cnt = counts[...]
        padded = jnp.floor((cnt + (MOE_BLOCK - 1)) * (1.0 / MOE_BLOCK)) * MOE_BLOCK
        a = lax.broadcasted_iota(jnp.int32, (LANES, LANES), 0)
        bcol = lax.broadcasted_iota(jnp.int32, (LANES, LANES), 1)
        pad_end = jnp.dot(padded, (a <= bcol).astype(F32), precision=HIGHEST, preferred_element_type=F32)
        pstart[...] = pad_end - padded
        cnt_ref[...] = cnt
        counts[...] = jnp.zeros_like(counts)

    lg = lg_ref[...]
    lane = lax.broadcasted_iota(jnp.int32, (tm, LANES), 1)
    is_coarse = lane < N_EXPERT_GROUPS
    cmax = jnp.max(jnp.where(is_coarse, lg, NEG_BIG), axis=-1, keepdims=True)
    grp = jnp.min(jnp.where(is_coarse & (lg == cmax), lane, LANES), axis=-1, keepdims=True)
    group_p = 1.0 / jnp.sum(jnp.where(is_coarse, jnp.exp(lg - cmax), 0.0), axis=-1, keepdims=True)
    f_lo = N_EXPERT_GROUPS + EXPERTS_PER_GROUP * grp
    in_grp = (lane >= f_lo) & (lane < f_lo + EXPERTS_PER_GROUP)
    f1 = jnp.max(jnp.where(in_grp, lg, NEG_BIG), axis=-1, keepdims=True)
    i1 = jnp.min(jnp.where(in_grp & (lg == f1), lane, LANES), axis=-1, keepdims=True)
    rest = in_grp & (lane != i1)
    f2 = jnp.max(jnp.where(rest, lg, NEG_BIG), axis=-1, keepdims=True)
    i2 = jnp.min(jnp.where(rest & (lg == f2), lane, LANES), axis=-1, keepdims=True)
    e2 = jnp.exp(f2 - f1)
    g1 = group_p / (1.0 + e2)
    g2 = group_p * e2 / (1.0 + e2)

    oh1 = lane == i1 - N_EXPERT_GROUPS
    oh2 = lane == i2 - N_EXPERT_GROUPS
    onehot = jnp.where(oh1 | oh2, 1.0, 0.0)
    r = lax.broadcasted_iota(jnp.int32, (tm, tm), 0)
    c = lax.broadcasted_iota(jnp.int32, (tm, tm), 1)
    before = jnp.dot((r > c).astype(BF16), onehot.astype(BF16), preferred_element_type=F32)
    pos = pstart[0:1, :] + counts[0:1, :] + before
    d1 = jnp.sum(jnp.where(oh1, pos, 0.0), axis=-1, keepdims=True)
    d2 = jnp.sum(jnp.where(oh2, pos, 0.0), axis=-1, keepdims=True)
    counts[...] = counts[...] + jnp.sum(onehot, axis=0, keepdims=True)
    dest_ref[...] = jnp.where(lane == 0, d1, jnp.where(lane == 1, d2, 0.0)).astype(jnp.int32)
    gate_ref[...] = jnp.where(lane == 0, g1, jnp.where(lane == 1, g2, 0.0))


def _route(logits):
    t = logits.shape[0]
    tm = min(512, t)
    return pl.pallas_call(
        _route_body,
        grid=(2, t // tm),
        in_specs=[pl.BlockSpec((tm, LANES), lambda ps, i: (i, 0))],
        out_specs=[pl.BlockSpec((tm, LANES), lambda ps, i: (i * ps, 0)),
                   pl.BlockSpec((tm, LANES), lambda ps, i: (i * ps, 0)),
                   pl.BlockSpec((8, LANES), lambda ps, i: (0, 0))],
        out_shape=[jax.ShapeDtypeStruct((t, LANES), jnp.int32),
                   jax.ShapeDtypeStruct((t, LANES), F32),
                   jax.ShapeDtypeStruct((8, LANES), F32)],
        scratch_shapes=[pltpu.VMEM((8, LANES), F32), pltpu.VMEM((8, LANES), F32)],
        compiler_params=_cparams(2),
    )(logits)


def _dispatch_body(meta_ref, dest_ref, h_ref, rows_ref, zbuf, zsem, sem):
    tm = h_ref.shape[0]

    def zero_copy(e):
        start = pl.multiple_of(meta_ref[e] - MOE_BLOCK, MOE_BLOCK)
        return pltpu.make_async_copy(zbuf, rows_ref.at[pl.ds(start, MOE_BLOCK)], zsem)

    def tail_copy(blk):
        start = pl.multiple_of(blk * MOE_BLOCK, MOE_BLOCK)
        return pltpu.make_async_copy(zbuf, rows_ref.at[pl.ds(start, MOE_BLOCK)], zsem)

    @pl.when(pl.program_id(0) == 0)
    def _():
        zbuf[...] = jnp.zeros_like(zbuf)

        def start_zero(e, carry):
            @pl.when(meta_ref[N_EXPERTS + e] > 0)
            def _():
                zero_copy(e).start()
            return carry

        def wait_zero(e, carry):
            @pl.when(meta_ref[N_EXPERTS + e] > 0)
            def _():
                zero_copy(e).wait()
            return carry

        def start_tail(blk, carry):
            tail_copy(blk).start()
            return carry

        def wait_tail(blk, carry):
            tail_copy(blk).wait()
            return carry

        n_used = meta_ref[2 * N_EXPERTS]
        n_blocks = rows_ref.shape[0] // MOE_BLOCK
        lax.fori_loop(0, N_EXPERTS, start_zero, 0)
        lax.fori_loop(n_used, n_blocks, start_tail, 0)
        lax.fori_loop(0, N_EXPERTS, wait_zero, 0)
        lax.fori_loop(n_used, n_blocks, wait_tail, 0)

    def issue(j, carry):
        for k in range(2):
            d = dest_ref[0, 2 * j + k]
            pltpu.make_async_copy(h_ref.at[j], rows_ref.at[d], sem).start()
        return carry

    lax.fori_loop(0, tm, issue, 0, unroll=8)
    for _ in range(2):
        pltpu.make_async_copy(h_ref, rows_ref.at[pl.ds(0, tm)], sem).wait()


def _dispatch(meta, dest3, h2, n_rows):
    t = h2.shape[0]
    tm = dest3.shape[2] // 2
    return pl.pallas_call(
        _dispatch_body,
        grid_spec=pltpu.PrefetchScalarGridSpec(
            num_scalar_prefetch=1,
            grid=(t // tm,),
            in_specs=[pl.BlockSpec((None, 1, 2 * tm), lambda i, m: (i, 0, 0), memory_space=pltpu.SMEM),
                      pl.BlockSpec((tm, 1, D_MODEL), lambda i, m: (i, 0, 0))],
            out_specs=pl.BlockSpec(memory_space=pl.ANY),
            scratch_shapes=[pltpu.VMEM((MOE_BLOCK, 1, D_MODEL), F32),
                            pltpu.SemaphoreType.DMA(()),
                            pltpu.SemaphoreType.DMA(())]),
        out_shape=jax.ShapeDtypeStruct((n_rows, 1, D_MODEL), F32),
        compiler_params=_cparams(1),
    )(meta, dest3, h2)


def _expert_body(be_ref, nu_ref, x_ref, wg_ref, wu_ref, wd_ref, y_ref, x2d, wg_b, wu_b, wd_b):
    blk = pl.program_id(0)

    @pl.when(blk < nu_ref[0])
    def _():
        prev = be_ref[jnp.maximum(blk - 1, 0)]

        @pl.when((blk == 0) | (be_ref[blk] != prev))
        def _():
            wg_b[...] = wg_ref[0].astype(BF16)
            wu_b[...] = wu_ref[0].astype(BF16)
            wd_b[...] = wd_ref[0].astype(BF16)

        x2d[...] = x_ref[...].reshape(MOE_BLOCK, D_MODEL)
        x = x2d[...].astype(BF16)
        g = jnp.dot(x, wg_b[...], preferred_element_type=F32)
        u = jnp.dot(x, wu_b[...], preferred_element_type=F32)
        hid = (g * _sigmoid(g)) * u
        y = jnp.dot(hid.astype(BF16), wd_b[...], preferred_element_type=F32)
        y_ref[...] = y.reshape(MOE_BLOCK, 1, D_MODEL)

    @pl.when(blk >= nu_ref[0])
    def _():
        y_ref[...] = jnp.zeros_like(y_ref)


def _experts(block_expert, n_used, rows, w_gate, w_up, w_down):
    n_rows = rows.shape[0]
    n_blocks = n_rows // MOE_BLOCK

    def row_map(b, be, nu):
        return (b, 0, 0)

    def w_map(b, be, nu):
        return (be[jnp.minimum(b, nu[0] - 1)], 0, 0)

    return pl.pallas_call(
        _expert_body,
        grid_spec=pltpu.PrefetchScalarGridSpec(
            num_scalar_prefetch=2,
            grid=(n_blocks,),
            in_specs=[pl.BlockSpec((MOE_BLOCK, 1, D_MODEL), row_map),
                      pl.BlockSpec((1, D_MODEL, D_EXPERT), w_map),
                      pl.BlockSpec((1, D_MODEL, D_EXPERT), w_map),
                      pl.BlockSpec((1, D_EXPERT, D_MODEL), w_map)],
            out_specs=pl.BlockSpec((MOE_BLOCK, 1, D_MODEL), row_map),
            scratch_shapes=[pltpu.VMEM((MOE_BLOCK, D_MODEL), F32),
                            pltpu.VMEM((D_MODEL, D_EXPERT), BF16),
                            pltpu.VMEM((D_MODEL, D_EXPERT), BF16),
                            pltpu.VMEM((D_EXPERT, D_MODEL), BF16)]),
        out_shape=jax.ShapeDtypeStruct((n_rows, 1, D_MODEL), F32),
        compiler_params=_cparams(1),
    )(block_expert, n_used, rows, w_gate, w_up, w_down)


def _combine_body(dcur_ref, dnext_ref, gate_ref, x2_ref, y_ref, o_ref, buf_a, buf_b, y2d, sem_a, sem_b):
    tm = x2_ref.shape[0]
    i = pl.program_id(0)
    n = pl.num_programs(0)

    def issue(dref, buf, sem):
        def body(j, carry):
            for k in range(2):
                d = dref[0, 2 * j + k]
                pltpu.make_async_copy(y_ref.at[d], buf.at[k * tm + j], sem).start()
            return carry
        lax.fori_loop(0, tm, body, 0, unroll=8)

    def finish(buf, sem):
        pltpu.make_async_copy(y_ref.at[pl.ds(0, 2 * tm)], buf, sem).wait()
        y2d[...] = buf[...].reshape(2 * tm, D_MODEL)
        g = gate_ref[...]
        o_ref[...] = x2_ref[...] + g[:, 0:1] * y2d[0:tm, :] + g[:, 1:2] * y2d[tm:2 * tm, :]

    @pl.when(i == 0)
    def _():
        issue(dcur_ref, buf_a, sem_a)

    for par, (cur, cur_sem, nxt, nxt_sem) in enumerate(((buf_a, sem_a, buf_b, sem_b),
                                                        (buf_b, sem_b, buf_a, sem_a))):
        @pl.when(i % 2 == par)
        def _(cur=cur, cur_sem=cur_sem, nxt=nxt, nxt_sem=nxt_sem):
            @pl.when(i + 1 < n)
            def _():
                issue(dnext_ref, nxt, nxt_sem)
            finish(cur, cur_sem)


def _combine(dest3, gates, x2, y_rows):
    t = x2.shape[0]
    tm = dest3.shape[2] // 2
    nt = t // tm
    return pl.pallas_call(
        _combine_body,
        grid=(nt,),
        in_specs=[pl.BlockSpec((None, 1, 2 * tm), lambda i: (i, 0, 0), memory_space=pltpu.SMEM),
                  pl.BlockSpec((None, 1, 2 * tm), lambda i: (jnp.minimum(i + 1, nt - 1), 0, 0),
                               memory_space=pltpu.SMEM),
                  pl.BlockSpec((tm, LANES), lambda i: (i, 0)),
                  pl.BlockSpec((tm, D_MODEL), lambda i: (i, 0)),
                  pl.BlockSpec(memory_space=pl.ANY)],
        out_specs=pl.BlockSpec((tm, D_MODEL), lambda i: (i, 0)),
        out_shape=jax.ShapeDtypeStruct((t, D_MODEL), F32),
        scratch_shapes=[pltpu.VMEM((2 * tm, 1, D_MODEL), F32),
                        pltpu.VMEM((2 * tm, 1, D_MODEL), F32),
                        pltpu.VMEM((2 * tm, D_MODEL), F32),
                        pltpu.SemaphoreType.DMA(()),
                        pltpu.SemaphoreType.DMA(())],
        compiler_params=_cparams(1),
    )(dest3, dest3, gates, x2, y_rows)


def _layer(x, norm_mix_w, w_in, conv_w, conv_b, dt_bias, a_log, d_skip, ssm_norm_w, w_ssm_proj,
           q_norm_w, k_norm_w, rel_bias, w_attn_proj, w_out, norm_ffn_w, w_coarse, b_coarse,
           w_fine, b_fine, w_gate_exp, w_up_exp, w_down_exp):
    b, s, d = x.shape
    t = b * s
    x2d = x.reshape(t, d)

    dt_lo = D_INNER + D_INNER + BC_WIDTH
    w_packed = jnp.concatenate(
        [w_in[:, :dt_lo], w_in[:, dt_lo + SSM_HEADS:], w_in[:, dt_lo:dt_lo + SSM_HEADS],
         jnp.zeros((d, PROJ_WIDTH - COL_DT - SSM_HEADS), w_in.dtype)], axis=1).astype(BF16)
    proj = _in_proj(x2d, norm_mix_w[None], w_packed)
    proj3 = proj.reshape(b, s, PROJ_WIDTH)

    y_ssm = _ssd(proj3, conv_w, conv_b, dt_bias, a_log, d_skip, ssm_norm_w).reshape(t, D_INNER)

    qw = jnp.tile(q_norm_w, 2)[None]
    kw = jnp.tile(k_norm_w, 2)[None]
    lane_head = np.arange(LANES) // ATTN_HEAD_DIM
    bd = jnp.asarray((lane_head[:, None] == lane_head[None, :]).astype(np.float32), dtype=BF16)
    attn_outs, attn_lses = [], []
    for gi, (window, dilation) in enumerate(DILATED_CONFIGS):
        assert window // dilation == ATTN_BLK and s % window == 0
        bias = _band_bias(rel_bias[:, gi * HEADS_PER_GROUP:(gi + 1) * HEADS_PER_GROUP], dilation)
        o, l = _attn_group(proj3, gi, dilation, bias, qw, kw, bd)
        attn_outs.extend(o)
        attn_lses.append(l)

    n_route = N_EXPERT_GROUPS + N_EXPERTS
    w_router = jnp.pad(jnp.concatenate([w_coarse, w_fine], axis=1), ((0, 0), (0, LANES - n_route)))
    b_router = jnp.pad(jnp.concatenate([b_coarse, b_fine]), (0, LANES - n_route))[None]
    x2, h2, logits = _mix_out(y_ssm, attn_outs, attn_lses, proj, x2d, w_ssm_proj.astype(BF16),
                              w_attn_proj.astype(BF16), w_out.astype(BF16), norm_ffn_w,
                              w_router.astype(BF16), b_router)

    dest, gates, counts = _route(logits)

    cnt = counts[0, :N_EXPERTS].astype(jnp.int32)
    padded = (cnt + MOE_BLOCK - 1) // MOE_BLOCK * MOE_BLOCK
    pad_end = jnp.cumsum(padded)
    n_blocks = -(-(2 * t + N_EXPERTS * (MOE_BLOCK - 1)) // MOE_BLOCK)
    block_start = jnp.arange(n_blocks, dtype=jnp.int32) * MOE_BLOCK
    block_expert = jnp.minimum(jnp.sum((pad_end[None, :] <= block_start[:, None]).astype(jnp.int32), axis=1),
                               N_EXPERTS - 1)
    n_used = (pad_end[-1:] // MOE_BLOCK).astype(jnp.int32)
    meta = jnp.concatenate([pad_end, padded, n_used]).astype(jnp.int32)

    tm_d = min(256, t)
    dest_d = dest[:, :2].reshape(t // tm_d, 1, 2 * tm_d)
    rows = _dispatch(meta, dest_d, h2, n_blocks * MOE_BLOCK)
    y_rows = _experts(block_expert, n_used, rows, w_gate_exp, w_up_exp, w_down_exp)
    tm_c = min(128, t)
    dest_c = dest[:, :2].reshape(t // tm_c, 1, 2 * tm_c)
    out = _combine(dest_c, gates, x2, y_rows)
    return out.reshape(b, s, d)


def kernel(x, norm_mix_w, w_in, conv_w, conv_b, dt_bias, a_log, d_skip, ssm_norm_w, w_ssm_proj,
           q_norm_w, k_norm_w, rel_bias, w_attn_proj, w_out, norm_ffn_w, w_coarse, b_coarse,
           w_fine, b_fine, w_gate_exp, w_up_exp, w_down_exp):
    depth = norm_mix_w.shape[0]
    for layer in range(depth):
        x = _layer(x, norm_mix_w[layer], w_in[layer], conv_w[layer], conv_b[layer], dt_bias[layer],
                   a_log[layer], d_skip[layer], ssm_norm_w[layer], w_ssm_proj[layer],
                   q_norm_w[layer], k_norm_w[layer], rel_bias, w_attn_proj[layer], w_out[layer],
                   norm_ffn_w[layer], w_coarse[layer], b_coarse[layer], w_fine[layer], b_fine[layer],
                   w_gate_exp[layer], w_up_exp[layer], w_down_exp[layer])
    return x
```

```python
import functools
import math

import jax
import jax.numpy as jnp
import numpy as np
from jax import lax
from jax.experimental import pallas as pl
from jax.experimental.pallas import tpu as pltpu

F32 = jnp.float32
BF16 = jnp.bfloat16
HIGHEST = lax.Precision.HIGHEST

LANES = 128
NORM_EPS = 1e-6
NEG_BIG = -1e30

D_MODEL = 1024
D_INNER = 2048
SSM_HEAD_DIM = 64
SSM_HEADS = 32
SSM_GROUPS = 2
D_STATE = 128
CONV_K = 4
BC_WIDTH = 2 * SSM_GROUPS * D_STATE
SSD_CHUNK = 128
ATTN_HEAD_DIM = 64
DILATED_CONFIGS = ((128, 1), (512, 4), (2048, 16))
HEADS_PER_GROUP = 8
GROUP_WIDTH = HEADS_PER_GROUP * ATTN_HEAD_DIM
ATTN_WIDTH = 3 * GROUP_WIDTH
ATTN_BLK = 128
NUM_BUCKETS = 32
MAX_DISTANCE = 2048
N_EXPERT_GROUPS = 8
EXPERTS_PER_GROUP = 8
N_EXPERTS = 64
D_EXPERT = 512
MOE_BLOCK = 256

COL_Z = 0
COL_XS = D_INNER
COL_BC = 2 * D_INNER
COL_Q = COL_BC + BC_WIDTH
COL_K = COL_Q + ATTN_WIDTH
COL_V = COL_K + ATTN_WIDTH
COL_GS = COL_V + ATTN_WIDTH
COL_GA = COL_GS + D_MODEL
COL_DT = COL_GA + D_MODEL
PROJ_WIDTH = COL_DT + 512

VMEM_LIMIT = 56 * 1024 * 1024


def _sigmoid(x):
    return 1.0 / (1.0 + jnp.exp(-x))


def _cparams(n_axes):
    return pltpu.CompilerParams(dimension_semantics=("arbitrary",) * n_axes,
                                vmem_limit_bytes=VMEM_LIMIT)


def _in_proj_body(x_ref, nw_ref, w_ref, o_ref, h_scr):
    @pl.when(pl.program_id(1) == 0)
    def _():
        x = x_ref[...]
        ms = jnp.mean(x * x, axis=-1, keepdims=True)
        h_scr[...] = (x * lax.rsqrt(ms + NORM_EPS) * nw_ref[...]).astype(BF16)

    o_ref[...] = jnp.dot(h_scr[...], w_ref[...], preferred_element_type=F32)


def _in_proj(x2d, norm_w, w):
    t, d = x2d.shape
    n = w.shape[1]
    tm = min(2048, t)
    tn = 512
    return pl.pallas_call(
        _in_proj_body,
        grid=(t // tm, n // tn),
        in_specs=[pl.BlockSpec((tm, d), lambda i, j: (i, 0)),
                  pl.BlockSpec((1, d), lambda i, j: (0, 0)),
                  pl.BlockSpec((d, tn), lambda i, j: (0, j))],
        out_specs=pl.BlockSpec((tm, tn), lambda i, j: (i, j)),
        out_shape=jax.ShapeDtypeStruct((t, n), F32),
        scratch_shapes=[pltpu.VMEM((tm, d), BF16)],
        compiler_params=_cparams(2),
    )(x2d, norm_w, w)


def _conv_silu(x, tail, w_ref, b_ref):
    b = b_ref[...]
    w3 = w_ref[3:4, :]
    acc = b + w3 * x
    xc = jnp.concatenate([tail, x[0:8]], axis=0)
    hacc = b + w3 * xc
    for k in range(CONV_K - 1):
        s = CONV_K - 1 - k
        wk = w_ref[k:k + 1, :]
        acc = acc + wk * pltpu.roll(x, s, 0)
        hacc = hacc + wk * pltpu.roll(xc, s, 0)
    out = jnp.concatenate([hacc[8:16], acc[8:]], axis=0)
    return out * _sigmoid(out)


def _ssd_body(z_ref, xs_ref, bc_ref, dt_ref, cwx_ref, cbx_ref, cwb_ref, cbb_ref, dtb_ref, alog_ref,
              dskip_ref, nw_ref, e_ref, y_ref, state, tail_x, tail_bc):
    L = SSD_CHUNK
    half = D_INNER // SSM_GROUPS

    @pl.when(pl.program_id(1) == 0)
    def _():
        state[...] = jnp.zeros_like(state)
        tail_x[...] = jnp.zeros_like(tail_x)
        tail_bc[...] = jnp.zeros_like(tail_bc)

    xs_raw = xs_ref[0]
    bc_raw = bc_ref[0]
    xs = _conv_silu(xs_raw, tail_x[...], cwx_ref, cbx_ref)
    bc = _conv_silu(bc_raw, tail_bc[...], cwb_ref, cbb_ref)
    tail_x[...] = xs_raw[L - 8:L]
    tail_bc[...] = bc_raw[L - 8:L]

    lane = lax.broadcasted_iota(jnp.int32, (L, LANES), 1)
    row = lax.broadcasted_iota(jnp.int32, (L, L), 0)
    col = lax.broadcasted_iota(jnp.int32, (L, L), 1)
    causal = row >= col

    v = dt_ref[0] + dtb_ref[...]
    dt = jnp.maximum(v, 0.0) + jnp.log1p(jnp.exp(-jnp.abs(v)))
    head_lane = lane < SSM_HEADS
    dt = jnp.where(head_lane, dt, 0.0)
    adt = dt * (-jnp.exp(alog_ref[...]))
    a_cs = jnp.dot(causal.astype(F32), adt, precision=HIGHEST, preferred_element_type=F32)
    a_cs_t = a_cs.T

    expand = e_ref[...]
    a_full = jnp.dot(a_cs, expand, precision=HIGHEST, preferred_element_type=F32)
    dt_full = jnp.dot(dt, expand, precision=HIGHEST, preferred_element_type=F32)
    a_tot = a_full[L - 1:L, :]
    decay_from_start = jnp.exp(a_full)
    decay_to_end = jnp.exp(a_tot - a_full)
    decay_chunk = jnp.exp(a_tot)

    xdt = xs * dt_full
    xdt_b = xdt.astype(BF16)
    xw_b = (xdt * decay_to_end).astype(BF16)

    y_groups = []
    for g in range(SSM_GROUPS):
        gs = slice(g * half, (g + 1) * half)
        bg = bc[:, g * D_STATE:(g + 1) * D_STATE]
        cg = bc[:, (SSM_GROUPS + g) * D_STATE:(SSM_GROUPS + g + 1) * D_STATE]
        cb16 = cg.astype(BF16)
        cb = lax.dot_general(cb16, bg.astype(BF16), (((1,), (1,)), ((), ())),
                             preferred_element_type=F32)
        s_prev = state[:, gs]
        y_off = jnp.dot(cb16, s_prev.astype(BF16), preferred_element_type=F32) * decay_from_start[:, gs]

        pair_cols = []
        for pr in range(half // LANES):
            h0 = g * (SSM_HEADS // SSM_GROUPS) + 2 * pr
            xp = xdt_b[:, h0 * SSM_HEAD_DIM:(h0 + 2) * SSM_HEAD_DIM]
            ys = []
            for h in (h0, h0 + 1):
                seg = a_cs[:, h:h + 1] - a_cs_t[h:h + 1, :]
                m = (cb * jnp.exp(jnp.where(causal, seg, NEG_BIG))).astype(BF16)
                ys.append(jnp.dot(m, xp, preferred_element_type=F32))
            pair_cols.append(jnp.where(lane < SSM_HEAD_DIM, ys[0], ys[1]))
        y_diag = jnp.concatenate(pair_cols, axis=1)

        state[:, gs] = decay_chunk[:, gs] * s_prev + jnp.dot(
            bg.T.astype(BF16), xw_b[:, gs], preferred_element_type=F32)
        y_groups.append(y_diag + y_off)

    y = jnp.concatenate(y_groups, axis=1) + dskip_ref[...] * xs
    z = z_ref[0]
    y = y * (z * _sigmoid(z))
    normed = []
    for g in range(SSM_GROUPS):
        yg = y[:, g * half:(g + 1) * half]
        ms = jnp.mean(yg * yg, axis=-1, keepdims=True)
        normed.append(yg * lax.rsqrt(ms + NORM_EPS))
    y_ref[0] = jnp.concatenate(normed, axis=1) * nw_ref[...]


def _ssd(proj3, conv_w, conv_b, dt_bias, a_log, d_skip, ssm_norm_w):
    b, s, _ = proj3.shape
    L = SSD_CHUNK
    pad = LANES - SSM_HEADS
    cwx, cwb = conv_w[:, :D_INNER], conv_w[:, D_INNER:]
    cbx, cbb = conv_b[None, :D_INNER], conv_b[None, D_INNER:]
    dtb = jnp.pad(dt_bias, (0, pad))[None]
    alog = jnp.pad(a_log, (0, pad))[None]
    dskip = jnp.repeat(d_skip, SSM_HEAD_DIM)[None]
    expand = (np.arange(LANES)[:, None] == np.arange(D_INNER)[None, :] // SSM_HEAD_DIM).astype(np.float32)

    def const(shape):
        return pl.BlockSpec(shape, lambda i, c: (0,) * len(shape))

    return pl.pallas_call(
        _ssd_body,
        grid=(b, s // L),
        in_specs=[pl.BlockSpec((1, L, D_INNER), lambda i, c: (i, c, COL_Z // D_INNER)),
                  pl.BlockSpec((1, L, D_INNER), lambda i, c: (i, c, COL_XS // D_INNER)),
                  pl.BlockSpec((1, L, BC_WIDTH), lambda i, c: (i, c, COL_BC // BC_WIDTH)),
                  pl.BlockSpec((1, L, LANES), lambda i, c: (i, c, COL_DT // LANES)),
                  const((CONV_K, D_INNER)), const((1, D_INNER)),
                  const((CONV_K, BC_WIDTH)), const((1, BC_WIDTH)),
                  const((1, LANES)), const((1, LANES)),
                  const((1, D_INNER)), const((1, D_INNER)),
                  const((LANES, D_INNER))],
        out_specs=pl.BlockSpec((1, L, D_INNER), lambda i, c: (i, c, 0)),
        out_shape=jax.ShapeDtypeStruct((b, s, D_INNER), F32),
        scratch_shapes=[pltpu.VMEM((D_STATE, D_INNER), F32),
                        pltpu.VMEM((8, D_INNER), F32),
                        pltpu.VMEM((8, BC_WIDTH), F32)],
        compiler_params=_cparams(2),
    )(proj3, proj3, proj3, proj3, cwx, cbx, cwb, cbb, dtb, alog, dskip, ssm_norm_w[None],
      jnp.asarray(expand))


def _t5_causal_bucket(dist):
    max_exact = NUM_BUCKETS // 2
    large = max_exact + (np.log(np.maximum(dist, max_exact) / max_exact)
                         / math.log(MAX_DISTANCE / max_exact) * (NUM_BUCKETS - max_exact)).astype(np.int32)
    return np.where(dist < max_exact, dist, np.minimum(large, NUM_BUCKETS - 1)).astype(np.int32)


def _band_bias(rel_bias_group, dilation):
    blk = ATTN_BLK
    off = np.arange(blk)[:, None] + blk - np.arange(2 * blk)[None, :]
    in_win = (off >= 0) & (off <= blk)
    bucket = _t5_causal_bucket(np.clip(off, 0, None) * dilation)
    onehot = (bucket.reshape(-1, 1) == np.arange(NUM_BUCKETS)[None, :]).astype(np.float32)
    bias = jnp.dot(jnp.asarray(onehot), rel_bias_group.astype(F32), precision=HIGHEST)
    bias = jnp.transpose(bias.reshape(blk, 2 * blk, HEADS_PER_GROUP), (2, 0, 1))
    bias = jnp.where(in_win[None], bias, NEG_BIG)
    return bias.reshape(HEADS_PER_GROUP // 2, 2 * blk, 2 * blk)


def _attn_body(*refs, dilation):
    n_pairs = HEADS_PER_GROUP // 2
    q_refs, k_refs, v_refs = refs[0:n_pairs], refs[n_pairs:2 * n_pairs], refs[2 * n_pairs:3 * n_pairs]
    bias_ref, qw_ref, kw_ref, bd_ref = refs[3 * n_pairs:3 * n_pairs + 4]
    o_refs = refs[3 * n_pairs + 4:4 * n_pairs + 4]
    lse_ref, kbuf, vbuf = refs[4 * n_pairs + 4:]
    blk = ATTN_BLK
    n = pl.program_id(1)
    slot = n % 2

    @pl.when(n == 0)
    def _():
        kbuf[...] = jnp.zeros_like(kbuf)
        vbuf[...] = jnp.zeros_like(vbuf)

    bd = bd_ref[...]

    def head_norm(x, w_ref):
        xx = x * x
        hi = xx.astype(BF16)
        lo = (xx - hi.astype(F32)).astype(BF16)
        ss = (jnp.dot(hi, bd, preferred_element_type=F32)
              + jnp.dot(lo, bd, preferred_element_type=F32))
        return x * lax.rsqrt(ss * (1.0 / ATTN_HEAD_DIM) + NORM_EPS) * w_ref[...]

    lane = lax.broadcasted_iota(jnp.int32, (blk, LANES), 1)
    first_pen = jnp.where(n == 0, NEG_BIG, 0.0)
    lo_half = lane < ATTN_HEAD_DIM
    nt = (((1,), (1,)), ((), ()))

    def one_residue(res, carry):
        rows = pl.ds(res, blk, stride=dilation)
        q_raw = [q_refs[p][0, rows, :] for p in range(n_pairs)]
        k_raw = [k_refs[p][0, rows, :] for p in range(n_pairs)]
        v_raw = [v_refs[p][0, rows, :] for p in range(n_pairs)]
        k_old = [kbuf[1 - slot, res * n_pairs + p] for p in range(n_pairs)]
        v_old = [vbuf[1 - slot, res * n_pairs + p] for p in range(n_pairs)]
        pairs = range(n_pairs)
        qn = head_norm(jnp.concatenate(q_raw, axis=0), qw_ref) * (ATTN_HEAD_DIM ** -0.5)
        kn = head_norm(jnp.concatenate(k_raw, axis=0), kw_ref).astype(BF16)
        k_new = [kn[p * blk:(p + 1) * blk] for p in pairs]
        v_new = [v_raw[p].astype(BF16) for p in pairs]
        q2 = []
        for p in pairs:
            qp = qn[p * blk:(p + 1) * blk]
            q2.append(jnp.concatenate([jnp.where(lo_half, qp, 0.0), jnp.where(lo_half, 0.0, qp)],
                                      axis=0).astype(BF16))
        s_prev = [lax.dot_general(q2[p], k_old[p], nt, preferred_element_type=F32) for p in pairs]
        s_cur = [lax.dot_general(q2[p], k_new[p], nt, preferred_element_type=F32) for p in pairs]
        e_prev, e_cur, m, d = [], [], [], []
        for p in pairs:
            sp = s_prev[p] + (bias_ref[p, :, 0:blk] + first_pen)
            sc = s_cur[p] + bias_ref[p, :, blk:2 * blk]
            mp = jnp.max(jnp.maximum(sp, sc), axis=-1, keepdims=True)
            ep = jnp.exp(sp - mp)
            ec = jnp.exp(sc - mp)
            m.append(mp)
            d.append(jnp.sum(ep + ec, axis=-1, keepdims=True))
            e_prev.append(ep.astype(BF16))
            e_cur.append(ec.astype(BF16))
        pv_prev = [jnp.dot(e_prev[p], v_old[p], preferred_element_type=F32) for p in pairs]
        pv_cur = [jnp.dot(e_cur[p], v_new[p], preferred_element_type=F32) for p in pairs]
        lse_tile = jnp.zeros((blk, LANES), F32)
        o_new = []
        for p in pairs:
            pv = (pv_prev[p] + pv_cur[p]) / d[p]
            o_new.append(jnp.where(lo_half, pv[0:blk], pv[blk:2 * blk]))
            lse = m[p] + jnp.log(d[p])
            lse_tile = jnp.where(lane == 2 * p, lse[0:blk], lse_tile)
            lse_tile = jnp.where(lane == 2 * p + 1, lse[blk:2 * blk], lse_tile)
        for p in pairs:
            kbuf[slot, res * n_pairs + p] = k_new[p]
            vbuf[slot, res * n_pairs + p] = v_new[p]
            o_refs[p][0, rows, :] = o_new[p]
        lse_ref[0, rows, :] = lse_tile
        return carry

    lax.fori_loop(0, dilation, one_residue, 0)


def _attn_group(proj3, gi, dilation, bias, qw, kw, bd):
    b, s, _ = proj3.shape
    blk = ATTN_BLK
    span = blk * dilation
    n_pairs = HEADS_PER_GROUP // 2

    def pair_spec(base, p):
        cb = (base + gi * GROUP_WIDTH) // LANES + p
        return pl.BlockSpec((1, span, LANES), lambda i, n: (i, n, cb))

    def const(shape):
        return pl.BlockSpec(shape, lambda i, n: (0,) * len(shape))

    token_spec = pl.BlockSpec((1, span, LANES), lambda i, n: (i, n, 0))
    res = pl.pallas_call(
        functools.partial(_attn_body, dilation=dilation),
        grid=(b, s // span),
        in_specs=([pair_spec(COL_Q, p) for p in range(n_pairs)]
                  + [pair_spec(COL_K, p) for p in range(n_pairs)]
                  + [pair_spec(COL_V, p) for p in range(n_pairs)]
                  + [const((n_pairs, 2 * blk, 2 * blk)), const((1, LANES)), const((1, LANES)),
                     const((LANES, LANES))]),
        out_specs=[token_spec] * (n_pairs + 1),
        out_shape=[jax.ShapeDtypeStruct((b, s, LANES), F32)] * (n_pairs + 1),
        scratch_shapes=[pltpu.VMEM((2, dilation * n_pairs, blk, LANES), BF16),
                        pltpu.VMEM((2, dilation * n_pairs, blk, LANES), BF16)],
        compiler_params=_cparams(2),
    )(*([proj3] * (3 * n_pairs)), bias, qw, kw, bd)
    outs = [o.reshape(b * s, LANES) for o in res[:n_pairs]]
    return outs, res[n_pairs].reshape(b * s, LANES)


def _mix_body(*refs):
    n_pairs = HEADS_PER_GROUP // 2
    y_ref = refs[0]
    o_refs = refs[1:1 + 3 * n_pairs]
    l_refs = refs[1 + 3 * n_pairs:4 + 3 * n_pairs]
    (gs_ref, ga_ref, x_ref, wssm_ref, wattn_ref, wout_ref, e8_ref, nfw_ref, wr_ref, br_ref,
     x2_ref, h2_ref, lg_ref) = refs[4 + 3 * n_pairs:]
    tm = x_ref.shape[0]
    lses = [l[...] for l in l_refs]
    mx = jnp.maximum(jnp.maximum(lses[0], lses[1]), lses[2])
    es = [jnp.exp(l - mx) for l in lses]
    inv = 1.0 / (es[0] + es[1] + es[2])
    e8 = e8_ref[...]
    att = jnp.zeros((tm, GROUP_WIDTH), F32)
    for g in range(3):
        w = es[g] * inv
        w_hi = w.astype(BF16)
        w_lo = (w - w_hi.astype(F32)).astype(BF16)
        w_full = (jnp.dot(w_hi, e8, preferred_element_type=F32)
                  + jnp.dot(w_lo, e8, preferred_element_type=F32))
        o_g = jnp.concatenate([o_refs[g * n_pairs + p][...] for p in range(n_pairs)], axis=1)
        att = att + w_full * o_g
    y_attn = jnp.dot(att.astype(BF16), wattn_ref[...], preferred_element_type=F32)
    y_ssm = jnp.dot(y_ref[...].astype(BF16), wssm_ref[...], preferred_element_type=F32)
    merged = _sigmoid(gs_ref[...]) * y_ssm + _sigmoid(ga_ref[...]) * y_attn
    x2 = x_ref[...] + jnp.dot(merged.astype(BF16), wout_ref[...], preferred_element_type=F32)
    x2_ref[...] = x2
    ms = jnp.mean(x2 * x2, axis=-1, keepdims=True)
    h2 = x2 * lax.rsqrt(ms + NORM_EPS) * nfw_ref[...]
    h2_ref[...] = h2.reshape(tm, 1, D_MODEL)
    lg_ref[...] = jnp.dot(h2.astype(BF16), wr_ref[...], preferred_element_type=F32) + br_ref[...]


def _mix_out(y_ssm, attn_outs, attn_lses, proj, x2d, w_ssm, w_attn, w_out, norm_ffn_w, w_router, b_router):
    t = x2d.shape[0]
    tm = min(256, t)
    e8 = (np.arange(LANES)[:, None] == np.arange(GROUP_WIDTH)[None, :] // ATTN_HEAD_DIM)
    e8 = jnp.asarray(e8.astype(np.float32), dtype=BF16)

    def rows(width, cb=0):
        return pl.BlockSpec((tm, width), lambda i: (i, cb))

    def const(shape):
        return pl.BlockSpec(shape, lambda i: (0,) * len(shape))

    return pl.pallas_call(
        _mix_body,
        grid=(t // tm,),
        in_specs=[rows(D_INNER)] + [rows(LANES)] * (len(attn_outs) + len(attn_lses)) + [
                  rows(D_MODEL, COL_GS // D_MODEL), rows(D_MODEL, COL_GA // D_MODEL), rows(D_MODEL),
                  const((D_INNER, D_MODEL)), const((GROUP_WIDTH, D_MODEL)), const((D_MODEL, D_MODEL)),
                  const((LANES, GROUP_WIDTH)), const((1, D_MODEL)),
                  const((D_MODEL, LANES)), const((1, LANES))],
        out_specs=[rows(D_MODEL),
                   pl.BlockSpec((tm, 1, D_MODEL), lambda i: (i, 0, 0)),
                   rows(LANES)],
        out_shape=[jax.ShapeDtypeStruct((t, D_MODEL), F32),
                   jax.ShapeDtypeStruct((t, 1, D_MODEL), F32),
                   jax.ShapeDtypeStruct((t, LANES), F32)],
        compiler_params=_cparams(1),
    )(y_ssm, *attn_outs, *attn_lses, proj, proj, x2d, w_ssm, w_attn, w_out, e8,
      norm_ffn_w[None], w_router, b_router)


def _route_body(lg_ref, dest_ref, gate_ref, cnt_ref, counts, pstart):
    tm = lg_ref.shape[0]
    ps = pl.program_id(0)
    i = pl.program_id(1)

    @pl.when((ps == 0) & (i == 0))
    def _():
        counts[...] = jnp.zeros_like(counts)
        pstart[...] = jnp.zeros_like(pstart)

    @pl.when((ps == 1) & (i == 0))
    def _():
        cnt = counts[...]
        padded = jnp.floor((cnt + (MOE_BLOCK - 1)) * (1.0 / MOE_BLOCK)) * MOE_BLOCK
        a = lax.broadcasted_iota(jnp.int32, (LANES, LANES), 0)
        bcol = lax.broadcasted_iota(jnp.int32, (LANES, LANES), 1)
        pad_end = jnp.dot(padded, (a <= bcol).astype(F32), precision=HIGHEST, preferred_element_type=F32)
        pstart[...] = pad_end - padded
        cnt_ref[...] = cnt
        counts[...] = jnp.zeros_like(counts)

    lg = lg_ref[...]
    lane = lax.broadcasted_iota(jnp.int32, (tm, LANES), 1)
    is_coarse = lane < N_EXPERT_GROUPS
    cmax = jnp.max(jnp.where(is_coarse, lg, NEG_BIG), axis=-1, keepdims=True)
    grp = jnp.min(jnp.where(is_coarse & (lg == cmax), lane, LANES), axis=-1, keepdims=True)
    group_p = 1.0 / jnp.sum(jnp.where(is_coarse, jnp.exp(lg - cmax), 0.0), axis=-1, keepdims=True)
    f_lo = N_EXPERT_GROUPS + EXPERTS_PER_GROUP * grp
    in_grp = (lane >= f_lo) & (lane < f_lo + EXPERTS_PER_GROUP)
    f1 = jnp.max(jnp.where(in_grp, lg, NEG_BIG), axis=-1, keepdims=True)
    i1 = jnp.min(jnp.where(in_grp & (lg == f1), lane, LANES), axis=-1, keepdims=True)
    rest = in_grp & (lane != i1)
    f2 = jnp.max(jnp.where(rest, lg, NEG_BIG), axis=-1, keepdims=True)
    i2 = jnp.min(jnp.where(rest & (lg == f2), lane, LANES), axis=-1, keepdims=True)
    e2 = jnp.exp(f2 - f1)
    g1 = group_p / (1.0 + e2)
    g2 = group_p * e2 / (1.0 + e2)

    oh1 = lane == i1 - N_EXPERT_GROUPS
    oh2 = lane == i2 - N_EXPERT_GROUPS
    onehot = jnp.where(oh1 | oh2, 1.0, 0.0)
    r = lax.broadcasted_iota(jnp.int32, (tm, tm), 0)
    c = lax.broadcasted_iota(jnp.int32, (tm, tm), 1)
    before = jnp.dot((r > c).astype(BF16), onehot.astype(BF16), preferred_element_type=F32)
    pos = pstart[0:1, :] + counts[0:1, :] + before
    d1 = jnp.sum(jnp.where(oh1, pos, 0.0), axis=-1, keepdims=True)
    d2 = jnp.sum(jnp.where(oh2, pos, 0.0), axis=-1, keepdims=True)
    counts[...] = counts[...] + jnp.sum(onehot, axis=0, keepdims=True)
    dest_ref[...] = jnp.where(lane == 0, d1, jnp.where(lane == 1, d2, 0.0)).astype(jnp.int32)
    gate_ref[...] = jnp.where(lane == 0, g1, jnp.where(lane == 1, g2, 0.0))


def _route(logits):
    t = logits.shape[0]
    tm = min(512, t)
    return pl.pallas_call(
        _route_body,
        grid=(2, t // tm),
        in_specs=[pl.BlockSpec((tm, LANES), lambda ps, i: (i, 0))],
        out_specs=[pl.BlockSpec((tm, LANES), lambda ps, i: (i * ps, 0)),
                   pl.BlockSpec((tm, LANES), lambda ps, i: (i * ps, 0)),
                   pl.BlockSpec((8, LANES), lambda ps, i: (0, 0))],
        out_shape=[jax.ShapeDtypeStruct((t, LANES), jnp.int32),
                   jax.ShapeDtypeStruct((t, LANES), F32),
                   jax.ShapeDtypeStruct((8, LANES), F32)],
        scratch_shapes=[pltpu.VMEM((8, LANES), F32), pltpu.VMEM((8, LANES), F32)],
        compiler_params=_cparams(2),
    )(logits)


def _dispatch_body(meta_ref, dest_ref, h_ref, rows_ref, zbuf, zsem, sem):
    tm = h_ref.shape[0]

    def zero_copy(e):
        start = pl.multiple_of(meta_ref[e] - MOE_BLOCK, MOE_BLOCK)
        return pltpu.make_async_copy(zbuf, rows_ref.at[pl.ds(start, MOE_BLOCK)], zsem)

    def tail_copy(blk):
        start = pl.multiple_of(blk * MOE_BLOCK, MOE_BLOCK)
        return pltpu.make_async_copy(zbuf, rows_ref.at[pl.ds(start, MOE_BLOCK)], zsem)

    @pl.when(pl.program_id(0) == 0)
    def _():
        zbuf[...] = jnp.zeros_like(zbuf)

        def start_zero(e, carry):
            @pl.when(meta_ref[N_EXPERTS + e] > 0)
            def _():
                zero_copy(e).start()
            return carry

        def wait_zero(e, carry):
            @pl.when(meta_ref[N_EXPERTS + e] > 0)
            def _():
                zero_copy(e).wait()
            return carry

        def start_tail(blk, carry):
            tail_copy(blk).start()
            return carry

        def wait_tail(blk, carry):
            tail_copy(blk).wait()
            return carry

        n_used = meta_ref[2 * N_EXPERTS]
        n_blocks = rows_ref.shape[0] // MOE_BLOCK
        lax.fori_loop(0, N_EXPERTS, start_zero, 0)
        lax.fori_loop(n_used, n_blocks, start_tail, 0)
        lax.fori_loop(0, N_EXPERTS, wait_zero, 0)
        lax.fori_loop(n_used, n_blocks, wait_tail, 0)

    def issue(j, carry):
        for k in range(2):
            d = dest_ref[0, 2 * j + k]
            pltpu.make_async_copy(h_ref.at[j], rows_ref.at[d], sem).start()
        return carry

    lax.fori_loop(0, tm, issue, 0, unroll=8)
    for _ in range(2):
        pltpu.make_async_copy(h_ref, rows_ref.at[pl.ds(0, tm)], sem).wait()


def _dispatch(meta, dest3, h2, n_rows):
    t = h2.shape[0]
    tm = dest3.shape[2] // 2
    return pl.pallas_call(
        _dispatch_body,
        grid_spec=pltpu.PrefetchScalarGridSpec(
            num_scalar_prefetch=1,
            grid=(t // tm,),
            in_specs=[pl.BlockSpec((None, 1, 2 * tm), lambda i, m: (i, 0, 0), memory_space=pltpu.SMEM),
                      pl.BlockSpec((tm, 1, D_MODEL), lambda i, m: (i, 0, 0))],
            out_specs=pl.BlockSpec(memory_space=pl.ANY),
            scratch_shapes=[pltpu.VMEM((MOE_BLOCK, 1, D_MODEL), F32),
                            pltpu.SemaphoreType.DMA(()),
                            pltpu.SemaphoreType.DMA(())]),
        out_shape=jax.ShapeDtypeStruct((n_rows, 1, D_MODEL), F32),
        compiler_params=_cparams(1),
    )(meta, dest3, h2)


def _expert_body(sched_ref, x_ref, wg_hbm, wu_hbm, wd_hbm, y_ref, x2d, wg_f, wu_f, wd_f, wg_b, wu_b, wd_b, sems):
    blk = pl.program_id(0)
    n_used = sched_ref[4, 0]

    def weight_copies(expert, slot):
        return [pltpu.make_async_copy(src.at[expert], dst.at[slot], sems.at[slot])
                for src, dst in ((wg_hbm, wg_f), (wu_hbm, wu_f), (wd_hbm, wd_f))]

    @pl.when(blk < n_used)
    def _():
        expert = sched_ref[0, blk]
        slot = sched_ref[2, blk]
        nxt = sched_ref[3, blk]

        @pl.when(sched_ref[1, blk] == 1)
        def _():
            @pl.when(blk == 0)
            def _():
                for c in weight_copies(expert, slot):
                    c.start()

            for c in weight_copies(expert, slot):
                c.wait()

            @pl.when(nxt >= 0)
            def _():
                for c in weight_copies(nxt, 1 - slot):
                    c.start()

            wg_b[...] = wg_f[slot].astype(BF16)
            wu_b[...] = wu_f[slot].astype(BF16)
            wd_b[...] = wd_f[slot].astype(BF16)

        x2d[...] = x_ref[...].reshape(MOE_BLOCK, D_MODEL)
        x = x2d[...].astype(BF16)
        g = jnp.dot(x, wg_b[...], preferred_element_type=F32)
        u = jnp.dot(x, wu_b[...], preferred_element_type=F32)
        hid = (g * _sigmoid(g)) * u
        y = jnp.dot(hid.astype(BF16), wd_b[...], preferred_element_type=F32)
        y_ref[...] = y.reshape(MOE_BLOCK, 1, D_MODEL)

    @pl.when(blk >= n_used)
    def _():
        y_ref[...] = jnp.zeros_like(y_ref)


def _experts(sched, rows, w_gate, w_up, w_down):
    n_rows = rows.shape[0]
    n_blocks = n_rows // MOE_BLOCK

    def row_map(b, sched):
        return (b, 0, 0)

    any_spec = pl.BlockSpec(memory_space=pl.ANY)
    return pl.pallas_call(
        _expert_body,
        grid_spec=pltpu.PrefetchScalarGridSpec(
            num_scalar_prefetch=1,
            grid=(n_blocks,),
            in_specs=[pl.BlockSpec((MOE_BLOCK, 1, D_MODEL), row_map), any_spec, any_spec, any_spec],
            out_specs=pl.BlockSpec((MOE_BLOCK, 1, D_MODEL), row_map),
            scratch_shapes=[pltpu.VMEM((MOE_BLOCK, D_MODEL), F32),
                            pltpu.VMEM((2, D_MODEL, D_EXPERT), F32),
                            pltpu.VMEM((2, D_MODEL, D_EXPERT), F32),
                            pltpu.VMEM((2, D_EXPERT, D_MODEL), F32),
                            pltpu.VMEM((D_MODEL, D_EXPERT), BF16),
                            pltpu.VMEM((D_MODEL, D_EXPERT), BF16),
                            pltpu.VMEM((D_EXPERT, D_MODEL), BF16),
                            pltpu.SemaphoreType.DMA((2,))]),
        out_shape=jax.ShapeDtypeStruct((n_rows, 1, D_MODEL), F32),
        compiler_params=_cparams(1),
    )(sched, rows, w_gate, w_up, w_down)


def _combine_body(dcur_ref, dnext_ref, gate_ref, x2_ref, y_ref, o_ref, buf_a, buf_b, y2d, sem_a, sem_b):
    tm = x2_ref.shape[0]
    i = pl.program_id(0)
    n = pl.num_programs(0)

    def issue(dref, buf, sem):
        def body(j, carry):
            for k in range(2):
                d = dref[0, 2 * j + k]
                pltpu.make_async_copy(y_ref.at[d], buf.at[k * tm + j], sem).start()
            return carry
        lax.fori_loop(0, tm, body, 0, unroll=8)

    def finish(buf, sem):
        pltpu.make_async_copy(y_ref.at[pl.ds(0, 2 * tm)], buf, sem).wait()
        y2d[...] = buf[...].reshape(2 * tm, D_MODEL)
        g = gate_ref[...]
        o_ref[...] = x2_ref[...] + g[:, 0:1] * y2d[0:tm, :] + g[:, 1:2] * y2d[tm:2 * tm, :]

    @pl.when(i == 0)
    def _():
        issue(dcur_ref, buf_a, sem_a)

    for par, (cur, cur_sem, nxt, nxt_sem) in enumerate(((buf_a, sem_a, buf_b, sem_b),
                                                        (buf_b, sem_b, buf_a, sem_a))):
        @pl.when(i % 2 == par)
        def _(cur=cur, cur_sem=cur_sem, nxt=nxt, nxt_sem=nxt_sem):
            @pl.when(i + 1 < n)
            def _():
                issue(dnext_ref, nxt, nxt_sem)
            finish(cur, cur_sem)


def _combine(dest3, gates, x2, y_rows):
    t = x2.shape[0]
    tm = dest3.shape[2] // 2
    nt = t // tm
    return pl.pallas_call(
        _combine_body,
        grid=(nt,),
        in_specs=[pl.BlockSpec((None, 1, 2 * tm), lambda i: (i, 0, 0), memory_space=pltpu.SMEM),
                  pl.BlockSpec((None, 1, 2 * tm), lambda i: (jnp.minimum(i + 1, nt - 1), 0, 0),
                               memory_space=pltpu.SMEM),
                  pl.BlockSpec((tm, LANES), lambda i: (i, 0)),
                  pl.BlockSpec((tm, D_MODEL), lambda i: (i, 0)),
                  pl.BlockSpec(memory_space=pl.ANY)],
        out_specs=pl.BlockSpec((tm, D_MODEL), lambda i: (i, 0)),
        out_shape=jax.ShapeDtypeStruct((t, D_MODEL), F32),
        scratch_shapes=[pltpu.VMEM((2 * tm, 1, D_MODEL), F32),
                        pltpu.VMEM((2 * tm, 1, D_MODEL), F32),
                        pltpu.VMEM((2 * tm, D_MODEL), F32),
                        pltpu.SemaphoreType.DMA(()),
                        pltpu.SemaphoreType.DMA(())],
        compiler_params=_cparams(1),
    )(dest3, dest3, gates, x2, y_rows)


def _layer(x, norm_mix_w, w_in, conv_w, conv_b, dt_bias, a_log, d_skip, ssm_norm_w, w_ssm_proj,
           q_norm_w, k_norm_w, rel_bias, w_attn_proj, w_out, norm_ffn_w, w_coarse, b_coarse,
           w_fine, b_fine, w_gate_exp, w_up_exp, w_down_exp):
    b, s, d = x.shape
    t = b * s
    x2d = x.reshape(t, d)

    dt_lo = D_INNER + D_INNER + BC_WIDTH
    w_packed = jnp.concatenate(
        [w_in[:, :dt_lo], w_in[:, dt_lo + SSM_HEADS:], w_in[:, dt_lo:dt_lo + SSM_HEADS],
         jnp.zeros((d, PROJ_WIDTH - COL_DT - SSM_HEADS), w_in.dtype)], axis=1).astype(BF16)
    proj = _in_proj(x2d, norm_mix_w[None], w_packed)
    proj3 = proj.reshape(b, s, PROJ_WIDTH)

    y_ssm = _ssd(proj3, conv_w, conv_b, dt_bias, a_log, d_skip, ssm_norm_w).reshape(t, D_INNER)

    qw = jnp.tile(q_norm_w, 2)[None]
    kw = jnp.tile(k_norm_w, 2)[None]
    lane_head = np.arange(LANES) // ATTN_HEAD_DIM
    bd = jnp.asarray((lane_head[:, None] == lane_head[None, :]).astype(np.float32), dtype=BF16)
    attn_outs, attn_lses = [], []
    for gi, (window, dilation) in enumerate(DILATED_CONFIGS):
        assert window // dilation == ATTN_BLK and s % window == 0
        bias = _band_bias(rel_bias[:, gi * HEADS_PER_GROUP:(gi + 1) * HEADS_PER_GROUP], dilation)
        o, l = _attn_group(proj3, gi, dilation, bias, qw, kw, bd)
        attn_outs.extend(o)
        attn_lses.append(l)

    n_route = N_EXPERT_GROUPS + N_EXPERTS
    w_router = jnp.pad(jnp.concatenate([w_coarse, w_fine], axis=1), ((0, 0), (0, LANES - n_route)))
    b_router = jnp.pad(jnp.concatenate([b_coarse, b_fine]), (0, LANES - n_route))[None]
    x2, h2, logits = _mix_out(y_ssm, attn_outs, attn_lses, proj, x2d, w_ssm_proj.astype(BF16),
                              w_attn_proj.astype(BF16), w_out.astype(BF16), norm_ffn_w,
                              w_router.astype(BF16), b_router)

    dest, gates, counts = _route(logits)

    cnt = counts[0, :N_EXPERTS].astype(jnp.int32)
    padded = (cnt + MOE_BLOCK - 1) // MOE_BLOCK * MOE_BLOCK
    pad_end = jnp.cumsum(padded)
    n_blocks = -(-(2 * t + N_EXPERTS * (MOE_BLOCK - 1)) // MOE_BLOCK)
    block_start = jnp.arange(n_blocks, dtype=jnp.int32) * MOE_BLOCK
    block_expert = jnp.minimum(jnp.sum((pad_end[None, :] <= block_start[:, None]).astype(jnp.int32), axis=1),
                               N_EXPERTS - 1)
    n_used = (pad_end[-1:] // MOE_BLOCK).astype(jnp.int32)
    meta = jnp.concatenate([pad_end, padded, n_used]).astype(jnp.int32)
    blk_ids = jnp.arange(n_blocks, dtype=jnp.int32)
    first = ((blk_ids == 0) | (block_expert != jnp.roll(block_expert, 1))) & (blk_ids < n_used[0])
    slot = (jnp.cumsum(first.astype(jnp.int32)) - 1) % 2
    nxt_blk = blk_ids + padded[block_expert] // MOE_BLOCK
    nxt = jnp.where(nxt_blk < n_used[0], block_expert[jnp.minimum(nxt_blk, n_blocks - 1)], -1)
    sched = jnp.stack([block_expert, first.astype(jnp.int32), slot, nxt,
                       jnp.broadcast_to(n_used, (n_blocks,))]).astype(jnp.int32)

    tm_d = min(256, t)
    dest_d = dest[:, :2].reshape(t // tm_d, 1, 2 * tm_d)
    rows = _dispatch(meta, dest_d, h2, n_blocks * MOE_BLOCK)
    y_rows = _experts(sched, rows, w_gate_exp, w_up_exp, w_down_exp)
    tm_c = min(128, t)
    dest_c = dest[:, :2].reshape(t // tm_c, 1, 2 * tm_c)
    out = _combine(dest_c, gates, x2, y_rows)
    return out.reshape(b, s, d)


def kernel(x, norm_mix_w, w_in, conv_w, conv_b, dt_bias, a_log, d_skip, ssm_norm_w, w_ssm_proj,
           q_norm_w, k_norm_w, rel_bias, w_attn_proj, w_out, norm_ffn_w, w_coarse, b_coarse,
           w_fine, b_fine, w_gate_exp, w_up_exp, w_down_exp):
    depth = norm_mix_w.shape[0]
    for layer in range(depth):
        x = _layer(x, norm_mix_w[layer], w_in[layer], conv_w[layer], conv_b[layer], dt_bias[layer],
                   a_log[layer], d_skip[layer], ssm_norm_w[layer], w_ssm_proj[layer],
                   q_norm_w[layer], k_norm_w[layer], rel_bias, w_attn_proj[layer], w_out[layer],
                   norm_ffn_w[layer], w_coarse[layer], b_coarse[layer], w_fine[layer], b_fine[layer],
                   w_gate_exp[layer], w_up_exp[layer], w_down_exp[layer])
    return x
```

```python
import functools
import math

import jax
import jax.numpy as jnp
import numpy as np
from jax import lax
from jax.experimental import pallas as pl
from jax.experimental.pallas import tpu as pltpu

F32 = jnp.float32
BF16 = jnp.bfloat16
HIGHEST = lax.Precision.HIGHEST

LANES = 128
NORM_EPS = 1e-6
NEG_BIG = -1e30

D_MODEL = 1024
D_INNER = 2048
SSM_HEAD_DIM = 64
SSM_HEADS = 32
SSM_GROUPS = 2
D_STATE = 128
CONV_K = 4
BC_WIDTH = 2 * SSM_GROUPS * D_STATE
SSD_CHUNK = 128
ATTN_HEAD_DIM = 64
DILATED_CONFIGS = ((128, 1), (512, 4), (2048, 16))
HEADS_PER_GROUP = 8
GROUP_WIDTH = HEADS_PER_GROUP * ATTN_HEAD_DIM
ATTN_WIDTH = 3 * GROUP_WIDTH
ATTN_BLK = 128
NUM_BUCKETS = 32
MAX_DISTANCE = 2048
N_EXPERT_GROUPS = 8
EXPERTS_PER_GROUP = 8
N_EXPERTS = 64
D_EXPERT = 512
MOE_BLOCK = 256

COL_GS = 0
COL_GA = D_MODEL
COL_Q = 2 * D_MODEL
COL_K = COL_Q + ATTN_WIDTH
COL_V = COL_K + ATTN_WIDTH
PROJ_WIDTH = COL_V + ATTN_WIDTH
SSD_PROJ_WIDTH = 2 * D_INNER + BC_WIDTH + LANES

VMEM_LIMIT = 56 * 1024 * 1024


def _sigmoid(x):
    return 1.0 / (1.0 + jnp.exp(-x))


def _cparams(n_axes):
    return pltpu.CompilerParams(dimension_semantics=("arbitrary",) * n_axes,
                                vmem_limit_bytes=VMEM_LIMIT)


def _in_proj_body(x_ref, nw_ref, w_ref, o_ref, h_scr):
    @pl.when(pl.program_id(1) == 0)
    def _():
        x = x_ref[...]
        ms = jnp.mean(x * x, axis=-1, keepdims=True)
        h_scr[...] = (x * lax.rsqrt(ms + NORM_EPS) * nw_ref[...]).astype(BF16)

    o_ref[...] = jnp.dot(h_scr[...], w_ref[...], preferred_element_type=F32)


def _in_proj(x2d, norm_w, w):
    t, d = x2d.shape
    n = w.shape[1]
    tm = min(2048, t)
    tn = 512
    return pl.pallas_call(
        _in_proj_body,
        grid=(t // tm, n // tn),
        in_specs=[pl.BlockSpec((tm, d), lambda i, j: (i, 0)),
                  pl.BlockSpec((1, d), lambda i, j: (0, 0)),
                  pl.BlockSpec((d, tn), lambda i, j: (0, j))],
        out_specs=pl.BlockSpec((tm, tn), lambda i, j: (i, j)),
        out_shape=jax.ShapeDtypeStruct((t, n), F32),
        scratch_shapes=[pltpu.VMEM((tm, d), BF16)],
        compiler_params=_cparams(2),
    )(x2d, norm_w, w)


def _dot3(x, y, x_is_exact):
    v = y if x_is_exact else x
    hi = v.astype(BF16)
    r1 = v - hi.astype(F32)
    mid = r1.astype(BF16)
    lo = (r1 - mid.astype(F32)).astype(BF16)
    if x_is_exact:
        return sum(jnp.dot(x, part, preferred_element_type=F32) for part in (hi, mid, lo))
    return sum(jnp.dot(part, y, preferred_element_type=F32) for part in (hi, mid, lo))


def _ssd_chunk(z, xbuf, r0, dt_raw, cw_ref, cb_ref, dtb_ref, alog_ref, dskip_ref, nw_ref, e_ref, state):
    L = SSD_CHUNK
    half = D_INNER // SSM_GROUPS
    hg = SSM_HEADS // SSM_GROUPS

    conv = cb_ref[...] + cw_ref[CONV_K - 1:CONV_K, :] * xbuf[8 + r0:8 + r0 + L, :]
    for k in range(CONV_K - 1):
        lo = 8 + r0 - (CONV_K - 1 - k)
        conv = conv + cw_ref[k:k + 1, :] * xbuf[lo:lo + L, :]
    xbc = conv * _sigmoid(conv)
    xs = xbc[:, :D_INNER]

    lane = lax.broadcasted_iota(jnp.int32, (L, LANES), 1)
    row = lax.broadcasted_iota(jnp.int32, (L, L), 0)
    col = lax.broadcasted_iota(jnp.int32, (L, L), 1)
    causal = row >= col

    v = dt_raw + dtb_ref[...]
    dt = jnp.maximum(v, 0.0) + jnp.log1p(jnp.exp(-jnp.abs(v)))
    dt = jnp.where(lane < SSM_HEADS, dt, 0.0)
    adt = dt * (-jnp.exp(alog_ref[...]))
    a_cs = _dot3(causal.astype(BF16), adt, True)
    a_cs_t = a_cs.T
    expand = e_ref[...]
    a_full = _dot3(a_cs, expand, False)
    dt_full = _dot3(dt, expand, False)
    a_tot = a_full[L - 1:L, :]
    decay_from_start = jnp.exp(a_full)
    decay_to_end = jnp.exp(a_tot - a_full)
    decay_chunk = jnp.exp(a_tot)

    xdt = xs * dt_full
    xdt_b = xdt.astype(BF16)
    xw_b = (xdt * decay_to_end).astype(BF16)

    b16, c16, cbs, y_offs, s_prevs = [], [], [], [], []
    for g in range(SSM_GROUPS):
        bg = xbc[:, D_INNER + g * D_STATE:D_INNER + (g + 1) * D_STATE]
        cg = xbc[:, D_INNER + (SSM_GROUPS + g) * D_STATE:D_INNER + (SSM_GROUPS + g + 1) * D_STATE]
        b16.append(bg)
        c16.append(cg.astype(BF16))
        cbs.append(lax.dot_general(c16[g], bg.astype(BF16), (((1,), (1,)), ((), ())),
                                   preferred_element_type=F32))
    for g in range(SSM_GROUPS):
        s_prevs.append(state[:, g * half:(g + 1) * half])
        y_offs.append(jnp.dot(c16[g], s_prevs[g].astype(BF16), preferred_element_type=F32))
    y_cols = []
    for g in range(SSM_GROUPS):
        for pr in range(half // LANES):
            h0 = g * hg + 2 * pr
            xp = xdt_b[:, h0 * SSM_HEAD_DIM:(h0 + 2) * SSM_HEAD_DIM]
            ys = []
            for h in (h0, h0 + 1):
                seg = a_cs[:, h:h + 1] - a_cs_t[h:h + 1, :]
                m = (cbs[g] * jnp.exp(jnp.where(causal, seg, NEG_BIG))).astype(BF16)
                ys.append(jnp.dot(m, xp, preferred_element_type=F32))
            y_cols.append(jnp.where(lane < SSM_HEAD_DIM, ys[0], ys[1]))
    for g in range(SSM_GROUPS):
        gs = slice(g * half, (g + 1) * half)
        state[:, gs] = decay_chunk[:, gs] * s_prevs[g] + jnp.dot(
            b16[g].T.astype(BF16), xw_b[:, gs], preferred_element_type=F32)

    y = (jnp.concatenate(y_cols, axis=1) + jnp.concatenate(y_offs, axis=1) * decay_from_start
         + dskip_ref[...] * xs)
    y = y * (z * _sigmoid(z))
    normed = []
    for g in range(SSM_GROUPS):
        yg = y[:, g * half:(g + 1) * half]
        ms = jnp.mean(yg * yg, axis=-1, keepdims=True)
        normed.append(yg * lax.rsqrt(ms + NORM_EPS))
    return jnp.concatenate(normed, axis=1) * nw_ref[...]


def _ssd_body(x_ref, nmw_ref, w_ref, cw_ref, cb_ref, dtb_ref, alog_ref, dskip_ref, nw_ref, e_ref,
              y_ref, state, xbuf, zbuf, dtbuf):
    L = SSD_CHUNK
    rt = x_ref.shape[1]
    conv_dim = D_INNER + BC_WIDTH

    @pl.when(pl.program_id(1) == 0)
    def _():
        state[...] = jnp.zeros_like(state)
        xbuf[0:8, :] = jnp.zeros((8, conv_dim), F32)

    x = x_ref[0]
    ms = jnp.mean(x * x, axis=-1, keepdims=True)
    h = (x * lax.rsqrt(ms + NORM_EPS) * nmw_ref[...]).astype(BF16)
    zbuf[...] = jnp.dot(h, w_ref[:, 0:D_INNER], preferred_element_type=F32)
    xbuf[8:8 + rt, :] = jnp.dot(h, w_ref[:, D_INNER:D_INNER + conv_dim], preferred_element_type=F32)
    dtbuf[...] = jnp.dot(h, w_ref[:, D_INNER + conv_dim:], preferred_element_type=F32)

    for c in range(rt // L):
        r0 = c * L
        y_ref[0, r0:r0 + L, :] = _ssd_chunk(zbuf[r0:r0 + L, :], xbuf, r0, dtbuf[r0:r0 + L, :], cw_ref, cb_ref,
                                            dtb_ref, alog_ref, dskip_ref, nw_ref, e_ref, state)
    xbuf[0:8, :] = xbuf[rt:rt + 8, :]


def _ssd(x, norm_mix_w, w_ssd, conv_w, conv_b, dt_bias, a_log, d_skip, ssm_norm_w):
    b, s, d = x.shape
    rt = min(4 * SSD_CHUNK, s)
    pad = LANES - SSM_HEADS
    conv_dim = D_INNER + BC_WIDTH
    dtb = jnp.pad(dt_bias, (0, pad))[None]
    alog = jnp.pad(a_log, (0, pad))[None]
    dskip = jnp.repeat(d_skip, SSM_HEAD_DIM)[None]
    expand = (np.arange(LANES)[:, None] == np.arange(D_INNER)[None, :] // SSM_HEAD_DIM).astype(np.float32)

    def const(shape):
        return pl.BlockSpec(shape, lambda i, c: (0,) * len(shape))

    return pl.pallas_call(
        _ssd_body,
        grid=(b, s // rt),
        in_specs=[pl.BlockSpec((1, rt, d), lambda i, c: (i, c, 0)),
                  const((1, d)), const((d, SSD_PROJ_WIDTH)),
                  const((CONV_K, conv_dim)), const((1, conv_dim)),
                  const((1, LANES)), const((1, LANES)),
                  const((1, D_INNER)), const((1, D_INNER)),
                  const((LANES, D_INNER))],
        out_specs=pl.BlockSpec((1, rt, D_INNER), lambda i, c: (i, c, 0)),
        out_shape=jax.ShapeDtypeStruct((b, s, D_INNER), F32),
        scratch_shapes=[pltpu.VMEM((D_STATE, D_INNER), F32),
                        pltpu.VMEM((rt + 8, conv_dim), F32),
                        pltpu.VMEM((rt, D_INNER), F32),
                        pltpu.VMEM((rt, LANES), F32)],
        compiler_params=_cparams(2),
    )(x, norm_mix_w[None], w_ssd, conv_w, conv_b[None], dtb, alog, dskip, ssm_norm_w[None],
      jnp.asarray(expand, dtype=BF16))


def _t5_causal_bucket(dist):
    max_exact = NUM_BUCKETS // 2
    large = max_exact + (np.log(np.maximum(dist, max_exact) / max_exact)
                         / math.log(MAX_DISTANCE / max_exact) * (NUM_BUCKETS - max_exact)).astype(np.int32)
    return np.where(dist < max_exact, dist, np.minimum(large, NUM_BUCKETS - 1)).astype(np.int32)


def _band_bias(rel_bias_group, dilation):
    blk = ATTN_BLK
    off = np.arange(blk)[:, None] + blk - np.arange(2 * blk)[None, :]
    in_win = (off >= 0) & (off <= blk)
    bucket = _t5_causal_bucket(np.clip(off, 0, None) * dilation)
    onehot = (bucket.reshape(-1, 1) == np.arange(NUM_BUCKETS)[None, :]).astype(np.float32)
    bias = jnp.dot(jnp.asarray(onehot), rel_bias_group.astype(F32), precision=HIGHEST)
    bias = jnp.transpose(bias.reshape(blk, 2 * blk, HEADS_PER_GROUP), (2, 0, 1))
    bias = jnp.where(in_win[None], bias, NEG_BIG)
    return bias.reshape(HEADS_PER_GROUP // 2, 2 * blk, 2 * blk)


def _attn_body(*refs, dilation):
    n_pairs = HEADS_PER_GROUP // 2
    q_refs, k_refs, v_refs = refs[0:n_pairs], refs[n_pairs:2 * n_pairs], refs[2 * n_pairs:3 * n_pairs]
    bias_ref, qw_ref, kw_ref, bd_ref = refs[3 * n_pairs:3 * n_pairs + 4]
    o_refs = refs[3 * n_pairs + 4:4 * n_pairs + 4]
    lse_ref, kbuf, vbuf = refs[4 * n_pairs + 4:]
    blk = ATTN_BLK
    n = pl.program_id(1)
    slot = n % 2

    @pl.when(n == 0)
    def _():
        kbuf[...] = jnp.zeros_like(kbuf)
        vbuf[...] = jnp.zeros_like(vbuf)

    bd = bd_ref[...]

    def head_norm(x, w_ref):
        xx = x * x
        hi = xx.astype(BF16)
        lo = (xx - hi.astype(F32)).astype(BF16)
        ss = (jnp.dot(hi, bd, preferred_element_type=F32)
              + jnp.dot(lo, bd, preferred_element_type=F32))
        return x * lax.rsqrt(ss * (1.0 / ATTN_HEAD_DIM) + NORM_EPS) * w_ref[...]

    lane = lax.broadcasted_iota(jnp.int32, (blk, LANES), 1)
    first_pen = jnp.where(n == 0, NEG_BIG, 0.0)
    lo_half = lane < ATTN_HEAD_DIM
    nt = (((1,), (1,)), ((), ()))

    def one_residue(res, carry):
        rows = pl.ds(res, blk, stride=dilation)
        q_raw = [q_refs[p][0, rows, :] for p in range(n_pairs)]
        k_raw = [k_refs[p][0, rows, :] for p in range(n_pairs)]
        v_raw = [v_refs[p][0, rows, :] for p in range(n_pairs)]
        k_old = [kbuf[1 - slot, res * n_pairs + p] for p in range(n_pairs)]
        v_old = [vbuf[1 - slot, res * n_pairs + p] for p in range(n_pairs)]
        pairs = range(n_pairs)
        qn = head_norm(jnp.concatenate(q_raw, axis=0), qw_ref) * (ATTN_HEAD_DIM ** -0.5)
        kn = head_norm(jnp.concatenate(k_raw, axis=0), kw_ref).astype(BF16)
        k_new = [kn[p * blk:(p + 1) * blk] for p in pairs]
        v_new = [v_raw[p].astype(BF16) for p in pairs]
        q2 = []
        for p in pairs:
            qp = qn[p * blk:(p + 1) * blk]
            q2.append(jnp.concatenate([jnp.where(lo_half, qp, 0.0), jnp.where(lo_half, 0.0, qp)],
                                      axis=0).astype(BF16))
        s_prev = [lax.dot_general(q2[p], k_old[p], nt, preferred_element_type=F32) for p in pairs]
        s_cur = [lax.dot_general(q2[p], k_new[p], nt, preferred_element_type=F32) for p in pairs]
        e_prev, e_cur, m, d = [], [], [], []
        for p in pairs:
            sp = s_prev[p] + (bias_ref[p, :, 0:blk] + first_pen)
            sc = s_cur[p] + bias_ref[p, :, blk:2 * blk]
            mp = jnp.max(jnp.maximum(sp, sc), axis=-1, keepdims=True)
            ep = jnp.exp(sp - mp)
            ec = jnp.exp(sc - mp)
            m.append(mp)
            d.append(jnp.sum(ep + ec, axis=-1, keepdims=True))
            e_prev.append(ep.astype(BF16))
            e_cur.append(ec.astype(BF16))
        pv_prev = [jnp.dot(e_prev[p], v_old[p], preferred_element_type=F32) for p in pairs]
        pv_cur = [jnp.dot(e_cur[p], v_new[p], preferred_element_type=F32) for p in pairs]
        lse_tile = jnp.zeros((blk, LANES), F32)
        o_new = []
        for p in pairs:
            pv = (pv_prev[p] + pv_cur[p]) / d[p]
            o_new.append(jnp.where(lo_half, pv[0:blk], pv[blk:2 * blk]))
            lse = m[p] + jnp.log(d[p])
            lse_tile = jnp.where(lane == 2 * p, lse[0:blk], lse_tile)
            lse_tile = jnp.where(lane == 2 * p + 1, lse[blk:2 * blk], lse_tile)
        for p in pairs:
            kbuf[slot, res * n_pairs + p] = k_new[p]
            vbuf[slot, res * n_pairs + p] = v_new[p]
            o_refs[p][0, rows, :] = o_new[p]
        lse_ref[0, rows, :] = lse_tile
        return carry

    lax.fori_loop(0, dilation, one_residue, 0)


def _attn_group(proj3, gi, dilation, bias, qw, kw, bd):
    b, s, _ = proj3.shape
    blk = ATTN_BLK
    span = blk * dilation
    n_pairs = HEADS_PER_GROUP // 2

    def pair_spec(base, p):
        cb = (base + gi * GROUP_WIDTH) // LANES + p
        return pl.BlockSpec((1, span, LANES), lambda i, n: (i, n, cb))

    def const(shape):
        return pl.BlockSpec(shape, lambda i, n: (0,) * len(shape))

    token_spec = pl.BlockSpec((1, span, LANES), lambda i, n: (i, n, 0))
    res = pl.pallas_call(
        functools.partial(_attn_body, dilation=dilation),
        grid=(b, s // span),
        in_specs=([pair_spec(COL_Q, p) for p in range(n_pairs)]
                  + [pair_spec(COL_K, p) for p in range(n_pairs)]
                  + [pair_spec(COL_V, p) for p in range(n_pairs)]
                  + [const((n_pairs, 2 * blk, 2 * blk)), const((1, LANES)), const((1, LANES)),
                     const((LANES, LANES))]),
        out_specs=[token_spec] * (n_pairs + 1),
        out_shape=[jax.ShapeDtypeStruct((b, s, LANES), F32)] * (n_pairs + 1),
        scratch_shapes=[pltpu.VMEM((2, dilation * n_pairs, blk, LANES), BF16),
                        pltpu.VMEM((2, dilation * n_pairs, blk, LANES), BF16)],
        compiler_params=_cparams(2),
    )(*([proj3] * (3 * n_pairs)), bias, qw, kw, bd)
    outs = [o.reshape(b * s, LANES) for o in res[:n_pairs]]
    return outs, res[n_pairs].reshape(b * s, LANES)


def _mix_body(*refs):
    n_pairs = HEADS_PER_GROUP // 2
    y_ref = refs[0]
    o_refs = refs[1:1 + 3 * n_pairs]
    l_refs = refs[1 + 3 * n_pairs:4 + 3 * n_pairs]
    (gs_ref, ga_ref, x_ref, wssm_ref, wattn_ref, wout_ref, e8_ref, nfw_ref, wr_ref, br_ref,
     x2_ref, h2_ref, lg_ref) = refs[4 + 3 * n_pairs:]
    tm = x_ref.shape[0]
    lses = [l[...] for l in l_refs]
    mx = jnp.maximum(jnp.maximum(lses[0], lses[1]), lses[2])
    es = [jnp.exp(l - mx) for l in lses]
    inv = 1.0 / (es[0] + es[1] + es[2])
    e8 = e8_ref[...]
    att = jnp.zeros((tm, GROUP_WIDTH), F32)
    for g in range(3):
        w = es[g] * inv
        w_hi = w.astype(BF16)
        w_lo = (w - w_hi.astype(F32)).astype(BF16)
        w_full = (jnp.dot(w_hi, e8, preferred_element_type=F32)
                  + jnp.dot(w_lo, e8, preferred_element_type=F32))
        o_g = jnp.concatenate([o_refs[g * n_pairs + p][...] for p in range(n_pairs)], axis=1)
        att = att + w_full * o_g
    y_attn = jnp.dot(att.astype(BF16), wattn_ref[...], preferred_element_type=F32)
    y_ssm = jnp.dot(y_ref[...].astype(BF16), wssm_ref[...], preferred_element_type=F32)
    merged = _sigmoid(gs_ref[...]) * y_ssm + _sigmoid(ga_ref[...]) * y_attn
    x2 = x_ref[...] + jnp.dot(merged.astype(BF16), wout_ref[...], preferred_element_type=F32)
    x2_ref[...] = x2
    ms = jnp.mean(x2 * x2, axis=-1, keepdims=True)
    h2 = x2 * lax.rsqrt(ms + NORM_EPS) * nfw_ref[...]
    h2_ref[...] = h2.reshape(tm, 1, D_MODEL)
    lg_ref[...] = jnp.dot(h2.astype(BF16), wr_ref[...], preferred_element_type=F32) + br_ref[...]


def _mix_out(y_ssm, attn_outs, attn_lses, proj, x2d, w_ssm, w_attn, w_out, norm_ffn_w, w_router, b_router):
    t = x2d.shape[0]
    tm = min(256, t)
    e8 = (np.arange(LANES)[:, None] == np.arange(GROUP_WIDTH)[None, :] // ATTN_HEAD_DIM)
    e8 = jnp.asarray(e8.astype(np.float32), dtype=BF16)

    def rows(width, cb=0):
        return pl.BlockSpec((tm, width), lambda i: (i, cb))

    def const(shape):
        return pl.BlockSpec(shape, lambda i: (0,) * len(shape))

    return pl.pallas_call(
        _mix_body,
        grid=(t // tm,),
        in_specs=[rows(D_INNER)] + [rows(LANES)] * (len(attn_outs) + len(attn_lses)) + [
                  rows(D_MODEL, COL_GS // D_MODEL), rows(D_MODEL, COL_GA // D_MODEL), rows(D_MODEL),
                  const((D_INNER, D_MODEL)), const((GROUP_WIDTH, D_MODEL)), const((D_MODEL, D_MODEL)),
                  const((LANES, GROUP_WIDTH)), const((1, D_MODEL)),
                  const((D_MODEL, LANES)), const((1, LANES))],
        out_specs=[rows(D_MODEL),
                   pl.BlockSpec((tm, 1, D_MODEL), lambda i: (i, 0, 0)),
                   rows(LANES)],
        out_shape=[jax.ShapeDtypeStruct((t, D_MODEL), F32),
                   jax.ShapeDtypeStruct((t, 1, D_MODEL), F32),
                   jax.ShapeDtypeStruct((t, LANES), F32)],
        compiler_params=_cparams(1),
    )(y_ssm, *attn_outs, *attn_lses, proj, proj, x2d, w_ssm, w_attn, w_out, e8,
      norm_ffn_w[None], w_router, b_router)


def _route_body(lg_ref, dest_ref, gate_ref, cnt_ref, counts, pstart):
    tm = lg_ref.shape[0]
    ps = pl.program_id(0)
    i = pl.program_id(1)

    @pl.when((ps == 0) & (i == 0))
    def _():
        counts[...] = jnp.zeros_like(counts)
        pstart[...] = jnp.zeros_like(pstart)

    @pl.when((ps == 1) & (i == 0))
    def _():
        cnt = counts[...]
        padded = jnp.floor((cnt + (MOE_BLOCK - 1)) * (1.0 / MOE_BLOCK)) * MOE_BLOCK
        a = lax.broadcasted_iota(jnp.int32, (LANES, LANES), 0)
        bcol = lax.broadcasted_iota(jnp.int32, (LANES, LANES), 1)
        pad_end = jnp.dot(padded, (a <= bcol).astype(F32), precision=HIGHEST, preferred_element_type=F32)
        pstart[...] = pad_end - padded
        cnt_ref[...] = cnt
        counts[...] = jnp.zeros_like(counts)

    lg = lg_ref[...]
    lane = lax.broadcasted_iota(jnp.int32, (tm, LANES), 1)
    is_coarse = lane < N_EXPERT_GROUPS
    cmax = jnp.max(jnp.where(is_coarse, lg, NEG_BIG), axis=-1, keepdims=True)
    grp = jnp.min(jnp.where(is_coarse & (lg == cmax), lane, LANES), axis=-1, keepdims=True)
    group_p = 1.0 / jnp.sum(jnp.where(is_coarse, jnp.exp(lg - cmax), 0.0), axis=-1, keepdims=True)
    f_lo = N_EXPERT_GROUPS + EXPERTS_PER_GROUP * grp
    in_grp = (lane >= f_lo) & (lane < f_lo + EXPERTS_PER_GROUP)
    f1 = jnp.max(jnp.where(in_grp, lg, NEG_BIG), axis=-1, keepdims=True)
    i1 = jnp.min(jnp.where(in_grp & (lg == f1), lane, LANES), axis=-1, keepdims=True)
    rest = in_grp & (lane != i1)
    f2 = jnp.max(jnp.where(rest, lg, NEG_BIG), axis=-1, keepdims=True)
    i2 = jnp.min(jnp.where(rest & (lg == f2), lane, LANES), axis=-1, keepdims=True)
    e2 = jnp.exp(f2 - f1)
    g1 = group_p / (1.0 + e2)
    g2 = group_p * e2 / (1.0 + e2)

    oh1 = lane == i1 - N_EXPERT_GROUPS
    oh2 = lane == i2 - N_EXPERT_GROUPS
    onehot = jnp.where(oh1 | oh2, 1.0, 0.0)
    r = lax.broadcasted_iota(jnp.int32, (tm, tm), 0)
    c = lax.broadcasted_iota(jnp.int32, (tm, tm), 1)
    before = jnp.dot((r > c).astype(BF16), onehot.astype(BF16), preferred_element_type=F32)
    pos = pstart[0:1, :] + counts[0:1, :] + before
    d1 = jnp.sum(jnp.where(oh1, pos, 0.0), axis=-1, keepdims=True)
    d2 = jnp.sum(jnp.where(oh2, pos, 0.0), axis=-1, keepdims=True)
    counts[...] = counts[...] + jnp.sum(onehot, axis=0, keepdims=True)
    dest_ref[...] = jnp.where(lane == 0, d1, jnp.where(lane == 1, d2, 0.0)).astype(jnp.int32)
    gate_ref[...] = jnp.where(lane == 0, g1, jnp.where(lane == 1, g2, 0.0))


def _route(logits):
    t = logits.shape[0]
    tm = min(512, t)
    return pl.pallas_call(
        _route_body,
        grid=(2, t // tm),
        in_specs=[pl.BlockSpec((tm, LANES), lambda ps, i: (i, 0))],
        out_specs=[pl.BlockSpec((tm, LANES), lambda ps, i: (i * ps, 0)),
                   pl.BlockSpec((tm, LANES), lambda ps, i: (i * ps, 0)),
                   pl.BlockSpec((8, LANES), lambda ps, i: (0, 0))],
        out_shape=[jax.ShapeDtypeStruct((t, LANES), jnp.int32),
                   jax.ShapeDtypeStruct((t, LANES), F32),
                   jax.ShapeDtypeStruct((8, LANES), F32)],
        scratch_shapes=[pltpu.VMEM((8, LANES), F32), pltpu.VMEM((8, LANES), F32)],
        compiler_params=_cparams(2),
    )(logits)


def _dispatch_body(meta_ref, dest_ref, h_ref, rows_ref, zbuf, zsem, sem):
    tm = h_ref.shape[0]

    def zero_copy(e):
        start = pl.multiple_of(meta_ref[e] - MOE_BLOCK, MOE_BLOCK)
        return pltpu.make_async_copy(zbuf, rows_ref.at[pl.ds(start, MOE_BLOCK)], zsem)

    def tail_copy(blk):
        start = pl.multiple_of(blk * MOE_BLOCK, MOE_BLOCK)
        return pltpu.make_async_copy(zbuf, rows_ref.at[pl.ds(start, MOE_BLOCK)], zsem)

    @pl.when(pl.program_id(0) == 0)
    def _():
        zbuf[...] = jnp.zeros_like(zbuf)

        def start_zero(e, carry):
            @pl.when(meta_ref[N_EXPERTS + e] > 0)
            def _():
                zero_copy(e).start()
            return carry

        def wait_zero(e, carry):
            @pl.when(meta_ref[N_EXPERTS + e] > 0)
            def _():
                zero_copy(e).wait()
            return carry

        def start_tail(blk, carry):
            tail_copy(blk).start()
            return carry

        def wait_tail(blk, carry):
            tail_copy(blk).wait()
            return carry

        n_used = meta_ref[2 * N_EXPERTS]
        n_blocks = rows_ref.shape[0] // MOE_BLOCK
        lax.fori_loop(0, N_EXPERTS, start_zero, 0)
        lax.fori_loop(n_used, n_blocks, start_tail, 0)
        lax.fori_loop(0, N_EXPERTS, wait_zero, 0)
        lax.fori_loop(n_used, n_blocks, wait_tail, 0)

    def issue(j, carry):
        for k in range(2):
            d = dest_ref[0, 2 * j + k]
            pltpu.make_async_copy(h_ref.at[j], rows_ref.at[d], sem).start()
        return carry

    lax.fori_loop(0, tm, issue, 0, unroll=8)
    for _ in range(2):
        pltpu.make_async_copy(h_ref, rows_ref.at[pl.ds(0, tm)], sem).wait()


def _dispatch(meta, dest3, h2, n_rows):
    t = h2.shape[0]
    tm = dest3.shape[2] // 2
    return pl.pallas_call(
        _dispatch_body,
        grid_spec=pltpu.PrefetchScalarGridSpec(
            num_scalar_prefetch=1,
            grid=(t // tm,),
            in_specs=[pl.BlockSpec((None, 1, 2 * tm), lambda i, m: (i, 0, 0), memory_space=pltpu.SMEM),
                      pl.BlockSpec((tm, 1, D_MODEL), lambda i, m: (i, 0, 0))],
            out_specs=pl.BlockSpec(memory_space=pl.ANY),
            scratch_shapes=[pltpu.VMEM((MOE_BLOCK, 1, D_MODEL), F32),
                            pltpu.SemaphoreType.DMA(()),
                            pltpu.SemaphoreType.DMA(())]),
        out_shape=jax.ShapeDtypeStruct((n_rows, 1, D_MODEL), F32),
        compiler_params=_cparams(1),
    )(meta, dest3, h2)


def _expert_body(sched_ref, x_ref, wg_hbm, wu_hbm, wd_hbm, y_ref, x2d, wg_f, wu_f, wd_f, wg_b, wu_b, wd_b, sems):
    blk = pl.program_id(0)
    n_used = sched_ref[4, 0]

    def weight_copies(expert, slot):
        return [pltpu.make_async_copy(src.at[expert], dst.at[slot], sems.at[slot])
                for src, dst in ((wg_hbm, wg_f), (wu_hbm, wu_f), (wd_hbm, wd_f))]

    @pl.when(blk < n_used)
    def _():
        expert = sched_ref[0, blk]
        slot = sched_ref[2, blk]
        nxt = sched_ref[3, blk]

        @pl.when(sched_ref[1, blk] == 1)
        def _():
            @pl.when(blk == 0)
            def _():
                for c in weight_copies(expert, slot):
                    c.start()

            for c in weight_copies(expert, slot):
                c.wait()

            @pl.when(nxt >= 0)
            def _():
                for c in weight_copies(nxt, 1 - slot):
                    c.start()

            wg_b[...] = wg_f[slot].astype(BF16)
            wu_b[...] = wu_f[slot].astype(BF16)
            wd_b[...] = wd_f[slot].astype(BF16)

        x2d[...] = x_ref[...].reshape(MOE_BLOCK, D_MODEL)
        x = x2d[...].astype(BF16)
        g = jnp.dot(x, wg_b[...], preferred_element_type=F32)
        u = jnp.dot(x, wu_b[...], preferred_element_type=F32)
        hid = (g * _sigmoid(g)) * u
        y = jnp.dot(hid.astype(BF16), wd_b[...], preferred_element_type=F32)
        y_ref[...] = y.reshape(MOE_BLOCK, 1, D_MODEL)

    @pl.when(blk >= n_used)
    def _():
        y_ref[...] = jnp.zeros_like(y_ref)


def _experts(sched, rows, w_gate, w_up, w_down):
    n_rows = rows.shape[0]
    n_blocks = n_rows // MOE_BLOCK

    def row_map(b, sched):
        return (b, 0, 0)

    any_spec = pl.BlockSpec(memory_space=pl.ANY)
    return pl.pallas_call(
        _expert_body,
        grid_spec=pltpu.PrefetchScalarGridSpec(
            num_scalar_prefetch=1,
            grid=(n_blocks,),
            in_specs=[pl.BlockSpec((MOE_BLOCK, 1, D_MODEL), row_map), any_spec, any_spec, any_spec],
            out_specs=pl.BlockSpec((MOE_BLOCK, 1, D_MODEL), row_map),
            scratch_shapes=[pltpu.VMEM((MOE_BLOCK, D_MODEL), F32),
                            pltpu.VMEM((2, D_MODEL, D_EXPERT), F32),
                            pltpu.VMEM((2, D_MODEL, D_EXPERT), F32),
                            pltpu.VMEM((2, D_EXPERT, D_MODEL), F32),
                            pltpu.VMEM((D_MODEL, D_EXPERT), BF16),
                            pltpu.VMEM((D_MODEL, D_EXPERT), BF16),
                            pltpu.VMEM((D_EXPERT, D_MODEL), BF16),
                            pltpu.SemaphoreType.DMA((2,))]),
        out_shape=jax.ShapeDtypeStruct((n_rows, 1, D_MODEL), F32),
        compiler_params=_cparams(1),
    )(sched, rows, w_gate, w_up, w_down)


def _combine_body(dcur_ref, dnext_ref, gate_ref, x2_ref, y_ref, o_ref, buf_a, buf_b, y2d, sem_a, sem_b):
    tm = x2_ref.shape[0]
    i = pl.program_id(0)
    n = pl.num_programs(0)

    def issue(dref, buf, sem):
        def body(j, carry):
            for k in range(2):
                d = dref[0, 2 * j + k]
                pltpu.make_async_copy(y_ref.at[d], buf.at[k * tm + j], sem).start()
            return carry
        lax.fori_loop(0, tm, body, 0, unroll=8)

    def finish(buf, sem):
        pltpu.make_async_copy(y_ref.at[pl.ds(0, 2 * tm)], buf, sem).wait()
        y2d[...] = buf[...].reshape(2 * tm, D_MODEL)
        g = gate_ref[...]
        o_ref[...] = x2_ref[...] + g[:, 0:1] * y2d[0:tm, :] + g[:, 1:2] * y2d[tm:2 * tm, :]

    @pl.when(i == 0)
    def _():
        issue(dcur_ref, buf_a, sem_a)

    for par, (cur, cur_sem, nxt, nxt_sem) in enumerate(((buf_a, sem_a, buf_b, sem_b),
                                                        (buf_b, sem_b, buf_a, sem_a))):
        @pl.when(i % 2 == par)
        def _(cur=cur, cur_sem=cur_sem, nxt=nxt, nxt_sem=nxt_sem):
            @pl.when(i + 1 < n)
            def _():
                issue(dnext_ref, nxt, nxt_sem)
            finish(cur, cur_sem)


def _combine(dest3, gates, x2, y_rows):
    t = x2.shape[0]
    tm = dest3.shape[2] // 2
    nt = t // tm
    return pl.pallas_call(
        _combine_body,
        grid=(nt,),
        in_specs=[pl.BlockSpec((None, 1, 2 * tm), lambda i: (i, 0, 0), memory_space=pltpu.SMEM),
                  pl.BlockSpec((None, 1, 2 * tm), lambda i: (jnp.minimum(i + 1, nt - 1), 0, 0),
                               memory_space=pltpu.SMEM),
                  pl.BlockSpec((tm, LANES), lambda i: (i, 0)),
                  pl.BlockSpec((tm, D_MODEL), lambda i: (i, 0)),
                  pl.BlockSpec(memory_space=pl.ANY)],
        out_specs=pl.BlockSpec((tm, D_MODEL), lambda i: (i, 0)),
        out_shape=jax.ShapeDtypeStruct((t, D_MODEL), F32),
        scratch_shapes=[pltpu.VMEM((2 * tm, 1, D_MODEL), F32),
                        pltpu.VMEM((2 * tm, 1, D_MODEL), F32),
                        pltpu.VMEM((2 * tm, D_MODEL), F32),
                        pltpu.SemaphoreType.DMA(()),
                        pltpu.SemaphoreType.DMA(())],
        compiler_params=_cparams(1),
    )(dest3, dest3, gates, x2, y_rows)


def _layer(x, norm_mix_w, w_in, conv_w, conv_b, dt_bias, a_log, d_skip, ssm_norm_w, w_ssm_proj,
           q_norm_w, k_norm_w, rel_bias, w_attn_proj, w_out, norm_ffn_w, w_coarse, b_coarse,
           w_fine, b_fine, w_gate_exp, w_up_exp, w_down_exp):
    b, s, d = x.shape
    t = b * s
    x2d = x.reshape(t, d)

    dt_lo = 2 * D_INNER + BC_WIDTH
    qkv_lo = dt_lo + SSM_HEADS
    gate_lo = qkv_lo + 3 * ATTN_WIDTH
    w_ssd = jnp.concatenate([w_in[:, :qkv_lo], jnp.zeros((d, LANES - SSM_HEADS), w_in.dtype)],
                            axis=1).astype(BF16)
    w_att = jnp.concatenate([w_in[:, gate_lo:], w_in[:, qkv_lo:gate_lo]], axis=1).astype(BF16)
    proj = _in_proj(x2d, norm_mix_w[None], w_att)
    proj3 = proj.reshape(b, s, PROJ_WIDTH)

    y_ssm = _ssd(x, norm_mix_w, w_ssd, conv_w, conv_b, dt_bias, a_log, d_skip, ssm_norm_w).reshape(t, D_INNER)

    qw = jnp.tile(q_norm_w, 2)[None]
    kw = jnp.tile(k_norm_w, 2)[None]
    lane_head = np.arange(LANES) // ATTN_HEAD_DIM
    bd = jnp.asarray((lane_head[:, None] == lane_head[None, :]).astype(np.float32), dtype=BF16)
    attn_outs, attn_lses = [], []
    for gi, (window, dilation) in enumerate(DILATED_CONFIGS):
        assert window // dilation == ATTN_BLK and s % window == 0
        bias = _band_bias(rel_bias[:, gi * HEADS_PER_GROUP:(gi + 1) * HEADS_PER_GROUP], dilation)
        o, l = _attn_group(proj3, gi, dilation, bias, qw, kw, bd)
        attn_outs.extend(o)
        attn_lses.append(l)

    n_route = N_EXPERT_GROUPS + N_EXPERTS
    w_router = jnp.pad(jnp.concatenate([w_coarse, w_fine], axis=1), ((0, 0), (0, LANES - n_route)))
    b_router = jnp.pad(jnp.concatenate([b_coarse, b_fine]), (0, LANES - n_route))[None]
    x2, h2, logits = _mix_out(y_ssm, attn_outs, attn_lses, proj, x2d, w_ssm_proj.astype(BF16),
                              w_attn_proj.astype(BF16), w_out.astype(BF16), norm_ffn_w,
                              w_router.astype(BF16), b_router)

    dest, gates, counts = _route(logits)

    cnt = counts[0, :N_EXPERTS].astype(jnp.int32)
    padded = (cnt + MOE_BLOCK - 1) // MOE_BLOCK * MOE_BLOCK
    pad_end = jnp.cumsum(padded)
    n_blocks = -(-(2 * t + N_EXPERTS * (MOE_BLOCK - 1)) // MOE_BLOCK)
    block_start = jnp.arange(n_blocks, dtype=jnp.int32) * MOE_BLOCK
    block_expert = jnp.minimum(jnp.sum((pad_end[None, :] <= block_start[:, None]).astype(jnp.int32), axis=1),
                               N_EXPERTS - 1)
    n_used = (pad_end[-1:] // MOE_BLOCK).astype(jnp.int32)
    meta = jnp.concatenate([pad_end, padded, n_used]).astype(jnp.int32)
    blk_ids = jnp.arange(n_blocks, dtype=jnp.int32)
    first = ((blk_ids == 0) | (block_expert != jnp.roll(block_expert, 1))) & (blk_ids < n_used[0])
    slot = (jnp.cumsum(first.astype(jnp.int32)) - 1) % 2
    nxt_blk = blk_ids + padded[block_expert] // MOE_BLOCK
    nxt = jnp.where(nxt_blk < n_used[0], block_expert[jnp.minimum(nxt_blk, n_blocks - 1)], -1)
    sched = jnp.stack([block_expert, first.astype(jnp.int32), slot, nxt,
                       jnp.broadcast_to(n_used, (n_blocks,))]).astype(jnp.int32)

    tm_d = min(256, t)
    dest_d = dest[:, :2].reshape(t // tm_d, 1, 2 * tm_d)
    rows = _dispatch(meta, dest_d, h2, n_blocks * MOE_BLOCK)
    y_rows = _experts(sched, rows, w_gate_exp, w_up_exp, w_down_exp)
    tm_c = min(128, t)
    dest_c = dest[:, :2].reshape(t // tm_c, 1, 2 * tm_c)
    out = _combine(dest_c, gates, x2, y_rows)
    return out.reshape(b, s, d)


def kernel(x, norm_mix_w, w_in, conv_w, conv_b, dt_bias, a_log, d_skip, ssm_norm_w, w_ssm_proj,
           q_norm_w, k_norm_w, rel_bias, w_attn_proj, w_out, norm_ffn_w, w_coarse, b_coarse,
           w_fine, b_fine, w_gate_exp, w_up_exp, w_down_exp):
    depth = norm_mix_w.shape[0]
    for layer in range(depth):
        x = _layer(x, norm_mix_w[layer], w_in[layer], conv_w[layer], conv_b[layer], dt_bias[layer],
                   a_log[layer], d_skip[layer], ssm_norm_w[layer], w_ssm_proj[layer],
                   q_norm_w[layer], k_norm_w[layer], rel_bias, w_attn_proj[layer], w_out[layer],
                   norm_ffn_w[layer], w_coarse[layer], b_coarse[layer], w_fine[layer], b_fine[layer],
                   w_gate_exp[layer], w_up_exp[layer], w_down_exp[layer])
    return x
```

```python
import functools
import math

import jax
import jax.numpy as jnp
import numpy as np
from jax import lax
from jax.experimental import pallas as pl
from jax.experimental.pallas import tpu as pltpu

F32 = jnp.float32
BF16 = jnp.bfloat16
HIGHEST = lax.Precision.HIGHEST

LANES = 128
NORM_EPS = 1e-6
NEG_BIG = -1e30

D_MODEL = 1024
D_INNER = 2048
SSM_HEAD_DIM = 64
SSM_HEADS = 32
SSM_GROUPS = 2
D_STATE = 128
CONV_K = 4
BC_WIDTH = 2 * SSM_GROUPS * D_STATE
SSD_CHUNK = 128
ATTN_HEAD_DIM = 64
DILATED_CONFIGS = ((128, 1), (512, 4), (2048, 16))
HEADS_PER_GROUP = 8
GROUP_WIDTH = HEADS_PER_GROUP * ATTN_HEAD_DIM
ATTN_WIDTH = 3 * GROUP_WIDTH
ATTN_BLK = 128
NUM_BUCKETS = 32
MAX_DISTANCE = 2048
N_EXPERT_GROUPS = 8
EXPERTS_PER_GROUP = 8
N_EXPERTS = 64
D_EXPERT = 512
MOE_BLOCK = 256

SSD_PROJ_WIDTH = 2 * D_INNER + BC_WIDTH + LANES

VMEM_LIMIT = 56 * 1024 * 1024


def _sigmoid(x):
    return 1.0 / (1.0 + jnp.exp(-x))


def _cparams(n_axes):
    return pltpu.CompilerParams(dimension_semantics=("arbitrary",) * n_axes,
                                vmem_limit_bytes=VMEM_LIMIT)


def _prenorm_body(x_ref, nw_ref, h_ref):
    x = x_ref[...]
    ms = jnp.mean(x * x, axis=-1, keepdims=True)
    h_ref[...] = (x * lax.rsqrt(ms + NORM_EPS) * nw_ref[...]).astype(BF16)


def _prenorm(x2d, norm_w):
    t, d = x2d.shape
    tm = min(1024, t)
    return pl.pallas_call(
        _prenorm_body,
        grid=(t // tm,),
        in_specs=[pl.BlockSpec((tm, d), lambda i: (i, 0)), pl.BlockSpec((1, d), lambda i: (0, 0))],
        out_specs=pl.BlockSpec((tm, d), lambda i: (i, 0)),
        out_shape=jax.ShapeDtypeStruct((t, d), BF16),
        compiler_params=_cparams(1),
    )(x2d, norm_w)


def _dot3(x, y, x_is_exact):
    v = y if x_is_exact else x
    hi = v.astype(BF16)
    r1 = v - hi.astype(F32)
    mid = r1.astype(BF16)
    lo = (r1 - mid.astype(F32)).astype(BF16)
    if x_is_exact:
        return sum(jnp.dot(x, part, preferred_element_type=F32) for part in (hi, mid, lo))
    return sum(jnp.dot(part, y, preferred_element_type=F32) for part in (hi, mid, lo))


def _ssd_chunk(z, xbuf, r0, dt_raw, cw_ref, cb_ref, dtb_ref, alog_ref, dskip_ref, nw_ref, e_ref, state):
    L = SSD_CHUNK
    half = D_INNER // SSM_GROUPS
    hg = SSM_HEADS // SSM_GROUPS

    conv = cb_ref[...] + cw_ref[CONV_K - 1:CONV_K, :] * xbuf[8 + r0:8 + r0 + L, :]
    for k in range(CONV_K - 1):
        lo = 8 + r0 - (CONV_K - 1 - k)
        conv = conv + cw_ref[k:k + 1, :] * xbuf[lo:lo + L, :]
    xbc = conv * _sigmoid(conv)
    xs = xbc[:, :D_INNER]

    lane = lax.broadcasted_iota(jnp.int32, (L, LANES), 1)
    row = lax.broadcasted_iota(jnp.int32, (L, L), 0)
    col = lax.broadcasted_iota(jnp.int32, (L, L), 1)
    causal = row >= col

    v = dt_raw + dtb_ref[...]
    dt = jnp.maximum(v, 0.0) + jnp.log1p(jnp.exp(-jnp.abs(v)))
    dt = jnp.where(lane < SSM_HEADS, dt, 0.0)
    adt = dt * (-jnp.exp(alog_ref[...]))
    a_cs = _dot3(causal.astype(BF16), adt, True)
    a_cs_t = a_cs.T
    expand = e_ref[...]
    a_full = _dot3(a_cs, expand, False)
    dt_full = _dot3(dt, expand, False)
    a_tot = a_full[L - 1:L, :]
    decay_from_start = jnp.exp(a_full)
    decay_to_end = jnp.exp(a_tot - a_full)
    decay_chunk = jnp.exp(a_tot)

    xdt = xs * dt_full
    xdt_b = xdt.astype(BF16)
    xw_b = (xdt * decay_to_end).astype(BF16)

    b16, c16, cbs, y_offs, s_prevs = [], [], [], [], []
    for g in range(SSM_GROUPS):
        bg = xbc[:, D_INNER + g * D_STATE:D_INNER + (g + 1) * D_STATE]
        cg = xbc[:, D_INNER + (SSM_GROUPS + g) * D_STATE:D_INNER + (SSM_GROUPS + g + 1) * D_STATE]
        b16.append(bg)
        c16.append(cg.astype(BF16))
        cbs.append(lax.dot_general(c16[g], bg.astype(BF16), (((1,), (1,)), ((), ())),
                                   preferred_element_type=F32))
    for g in range(SSM_GROUPS):
        s_prevs.append(state[:, g * half:(g + 1) * half])
        y_offs.append(jnp.dot(c16[g], s_prevs[g].astype(BF16), preferred_element_type=F32))
    y_cols = []
    for g in range(SSM_GROUPS):
        for pr in range(half // LANES):
            h0 = g * hg + 2 * pr
            xp = xdt_b[:, h0 * SSM_HEAD_DIM:(h0 + 2) * SSM_HEAD_DIM]
            ys = []
            for h in (h0, h0 + 1):
                seg = a_cs[:, h:h + 1] - a_cs_t[h:h + 1, :]
                m = (cbs[g] * jnp.exp(jnp.where(causal, seg, NEG_BIG))).astype(BF16)
                ys.append(jnp.dot(m, xp, preferred_element_type=F32))
            y_cols.append(jnp.where(lane < SSM_HEAD_DIM, ys[0], ys[1]))
    for g in range(SSM_GROUPS):
        gs = slice(g * half, (g + 1) * half)
        state[:, gs] = decay_chunk[:, gs] * s_prevs[g] + jnp.dot(
            b16[g].T.astype(BF16), xw_b[:, gs], preferred_element_type=F32)

    y = (jnp.concatenate(y_cols, axis=1) + jnp.concatenate(y_offs, axis=1) * decay_from_start
         + dskip_ref[...] * xs)
    y = y * (z * _sigmoid(z))
    normed = []
    for g in range(SSM_GROUPS):
        yg = y[:, g * half:(g + 1) * half]
        ms = jnp.mean(yg * yg, axis=-1, keepdims=True)
        normed.append(yg * lax.rsqrt(ms + NORM_EPS))
    return jnp.concatenate(normed, axis=1) * nw_ref[...]


def _ssd_body(h_ref, w_ref, cw_ref, cb_ref, dtb_ref, alog_ref, dskip_ref, nw_ref, e_ref,
              y_ref, state, xbuf, zbuf, dtbuf):
    L = SSD_CHUNK
    rt = h_ref.shape[1]
    conv_dim = D_INNER + BC_WIDTH

    @pl.when(pl.program_id(1) == 0)
    def _():
        state[...] = jnp.zeros_like(state)
        xbuf[0:8, :] = jnp.zeros((8, conv_dim), F32)

    h = h_ref[0]
    zbuf[...] = jnp.dot(h, w_ref[:, 0:D_INNER], preferred_element_type=F32)
    xbuf[8:8 + rt, :] = jnp.dot(h, w_ref[:, D_INNER:D_INNER + conv_dim], preferred_element_type=F32)
    dtbuf[...] = jnp.dot(h, w_ref[:, D_INNER + conv_dim:], preferred_element_type=F32)

    for c in range(rt // L):
        r0 = c * L
        y = _ssd_chunk(zbuf[r0:r0 + L, :], xbuf, r0, dtbuf[r0:r0 + L, :], cw_ref, cb_ref,
                       dtb_ref, alog_ref, dskip_ref, nw_ref, e_ref, state)
        y_ref[0, r0:r0 + L, :] = y.astype(BF16)
    xbuf[0:8, :] = xbuf[rt:rt + 8, :]


def _ssd(h3, w_ssd, conv_w, conv_b, dt_bias, a_log, d_skip, ssm_norm_w):
    b, s, d = h3.shape
    rt = min(4 * SSD_CHUNK, s)
    pad = LANES - SSM_HEADS
    conv_dim = D_INNER + BC_WIDTH
    dtb = jnp.pad(dt_bias, (0, pad))[None]
    alog = jnp.pad(a_log, (0, pad))[None]
    dskip = jnp.repeat(d_skip, SSM_HEAD_DIM)[None]
    expand = (np.arange(LANES)[:, None] == np.arange(D_INNER)[None, :] // SSM_HEAD_DIM).astype(np.float32)

    def const(shape):
        return pl.BlockSpec(shape, lambda i, c: (0,) * len(shape))

    return pl.pallas_call(
        _ssd_body,
        grid=(b, s // rt),
        in_specs=[pl.BlockSpec((1, rt, d), lambda i, c: (i, c, 0)),
                  const((d, SSD_PROJ_WIDTH)),
                  const((CONV_K, conv_dim)), const((1, conv_dim)),
                  const((1, LANES)), const((1, LANES)),
                  const((1, D_INNER)), const((1, D_INNER)),
                  const((LANES, D_INNER))],
        out_specs=pl.BlockSpec((1, rt, D_INNER), lambda i, c: (i, c, 0)),
        out_shape=jax.ShapeDtypeStruct((b, s, D_INNER), BF16),
        scratch_shapes=[pltpu.VMEM((D_STATE, D_INNER), F32),
                        pltpu.VMEM((rt + 8, conv_dim), F32),
                        pltpu.VMEM((rt, D_INNER), F32),
                        pltpu.VMEM((rt, LANES), F32)],
        compiler_params=_cparams(2),
    )(h3, w_ssd, conv_w, conv_b[None], dtb, alog, dskip, ssm_norm_w[None],
      jnp.asarray(expand, dtype=BF16))


def _t5_causal_bucket(dist):
    max_exact = NUM_BUCKETS // 2
    large = max_exact + (np.log(np.maximum(dist, max_exact) / max_exact)
                         / math.log(MAX_DISTANCE / max_exact) * (NUM_BUCKETS - max_exact)).astype(np.int32)
    return np.where(dist < max_exact, dist, np.minimum(large, NUM_BUCKETS - 1)).astype(np.int32)


def _band_bias(rel_bias_group, dilation):
    blk = ATTN_BLK
    off = np.arange(blk)[:, None] + blk - np.arange(2 * blk)[None, :]
    in_win = (off >= 0) & (off <= blk)
    bucket = _t5_causal_bucket(np.clip(off, 0, None) * dilation)
    onehot = (bucket.reshape(-1, 1) == np.arange(NUM_BUCKETS)[None, :]).astype(np.float32)
    bias = jnp.dot(jnp.asarray(onehot), rel_bias_group.astype(F32), precision=HIGHEST)
    bias = jnp.transpose(bias.reshape(blk, 2 * blk, HEADS_PER_GROUP), (2, 0, 1))
    bias = jnp.where(in_win[None], bias, NEG_BIG)
    return bias.reshape(HEADS_PER_GROUP // 2, 2 * blk, 2 * blk)


def _attn_body(h_ref, w_ref, bias_ref, qw_ref, kw_ref, bd_ref, *rest, dilation):
    n_pairs = HEADS_PER_GROUP // 2
    o_refs = rest[0:n_pairs]
    lse_ref, qkv, kbuf, vbuf = rest[n_pairs:]
    blk = ATTN_BLK
    rt = h_ref.shape[1]
    span = blk * dilation
    n_sub = rt // span
    assert n_sub == 1 or dilation == 1
    step = pl.program_id(1)

    @pl.when(step == 0)
    def _():
        kbuf[...] = jnp.zeros_like(kbuf)
        vbuf[...] = jnp.zeros_like(vbuf)

    h = h_ref[0]
    for j in range(3 * n_pairs // 2):
        piece = jnp.dot(h, w_ref[:, 2 * j * LANES:2 * (j + 1) * LANES], preferred_element_type=F32)
        qkv[2 * j] = piece[:, 0:LANES]
        qkv[2 * j + 1] = piece[:, LANES:2 * LANES]

    bd = bd_ref[...]

    def head_norm(x, w_ref):
        xx = x * x
        hi = xx.astype(BF16)
        lo = (xx - hi.astype(F32)).astype(BF16)
        ss = (jnp.dot(hi, bd, preferred_element_type=F32)
              + jnp.dot(lo, bd, preferred_element_type=F32))
        return x * lax.rsqrt(ss * (1.0 / ATTN_HEAD_DIM) + NORM_EPS) * w_ref[...]

    lane = lax.broadcasted_iota(jnp.int32, (blk, LANES), 1)
    lo_half = lane < ATTN_HEAD_DIM
    nt = (((1,), (1,)), ((), ()))

    def one_block(it, carry):
        sub, res = (it, 0) if dilation == 1 else (0, it)
        rows = pl.ds(sub * span + res, blk, stride=dilation)
        gblk = step * n_sub + sub
        slot = gblk % 2
        first_pen = jnp.where(gblk == 0, NEG_BIG, 0.0)
        q_raw = [qkv[p, rows, :] for p in range(n_pairs)]
        k_raw = [qkv[n_pairs + p, rows, :] for p in range(n_pairs)]
        v_raw = [qkv[2 * n_pairs + p, rows, :] for p in range(n_pairs)]
        k_old = [kbuf[1 - slot, res * n_pairs + p] for p in range(n_pairs)]
        v_old = [vbuf[1 - slot, res * n_pairs + p] for p in range(n_pairs)]
        pairs = range(n_pairs)
        qn = head_norm(jnp.concatenate(q_raw, axis=0), qw_ref) * (ATTN_HEAD_DIM ** -0.5)
        kn = head_norm(jnp.concatenate(k_raw, axis=0), kw_ref).astype(BF16)
        k_new = [kn[p * blk:(p + 1) * blk] for p in pairs]
        v_new = [v_raw[p].astype(BF16) for p in pairs]
        q2 = []
        for p in pairs:
            qp = qn[p * blk:(p + 1) * blk]
            q2.append(jnp.concatenate([jnp.where(lo_half, qp, 0.0), jnp.where(lo_half, 0.0, qp)],
                                      axis=0).astype(BF16))
        s_prev = [lax.dot_general(q2[p], k_old[p], nt, preferred_element_type=F32) for p in pairs]
        s_cur = [lax.dot_general(q2[p], k_new[p], nt, preferred_element_type=F32) for p in pairs]
        e_prev, e_cur, m, d = [], [], [], []
        for p in pairs:
            sp = s_prev[p] + (bias_ref[p, :, 0:blk] + first_pen)
            sc = s_cur[p] + bias_ref[p, :, blk:2 * blk]
            mp = jnp.max(jnp.maximum(sp, sc), axis=-1, keepdims=True)
            ep = jnp.exp(sp - mp)
            ec = jnp.exp(sc - mp)
            m.append(mp)
            d.append(jnp.sum(ep + ec, axis=-1, keepdims=True))
            e_prev.append(ep.astype(BF16))
            e_cur.append(ec.astype(BF16))
        pv_prev = [jnp.dot(e_prev[p], v_old[p], preferred_element_type=F32) for p in pairs]
        pv_cur = [jnp.dot(e_cur[p], v_new[p], preferred_element_type=F32) for p in pairs]
        lse_tile = jnp.zeros((blk, LANES), F32)
        o_new = []
        for p in pairs:
            pv = (pv_prev[p] + pv_cur[p]) / d[p]
            o_new.append(jnp.where(lo_half, pv[0:blk], pv[blk:2 * blk]))
            lse = m[p] + jnp.log(d[p])
            lse_tile = jnp.where(lane == 2 * p, lse[0:blk], lse_tile)
            lse_tile = jnp.where(lane == 2 * p + 1, lse[blk:2 * blk], lse_tile)
        for p in pairs:
            kbuf[slot, res * n_pairs + p] = k_new[p]
            vbuf[slot, res * n_pairs + p] = v_new[p]
            o_refs[p][0, rows, :] = o_new[p]
        lse_ref[0, rows, :] = lse_tile
        return carry

    lax.fori_loop(0, n_sub * dilation, one_block, 0)


def _attn_group(h3, w_qkv, dilation, bias, qw, kw, bd):
    b, s, d = h3.shape
    blk = ATTN_BLK
    n_pairs = HEADS_PER_GROUP // 2
    rt = max(4 * blk, blk * dilation)
    rt = min(rt, s)

    def const(shape):
        return pl.BlockSpec(shape, lambda i, n: (0,) * len(shape))

    token_spec = pl.BlockSpec((1, rt, LANES), lambda i, n: (i, n, 0))
    res = pl.pallas_call(
        functools.partial(_attn_body, dilation=dilation),
        grid=(b, s // rt),
        in_specs=[pl.BlockSpec((1, rt, d), lambda i, n: (i, n, 0)),
                  const((d, 3 * GROUP_WIDTH)),
                  const((n_pairs, 2 * blk, 2 * blk)), const((1, LANES)), const((1, LANES)),
                  const((LANES, LANES))],
        out_specs=[token_spec] * (n_pairs + 1),
        out_shape=[jax.ShapeDtypeStruct((b, s, LANES), F32)] * (n_pairs + 1),
        scratch_shapes=[pltpu.VMEM((3 * n_pairs, rt, LANES), F32),
                        pltpu.VMEM((2, dilation * n_pairs, blk, LANES), BF16),
                        pltpu.VMEM((2, dilation * n_pairs, blk, LANES), BF16)],
        compiler_params=_cparams(2),
    )(h3, w_qkv, bias, qw, kw, bd)
    outs = [o.reshape(b * s, LANES) for o in res[:n_pairs]]
    return outs, res[n_pairs].reshape(b * s, LANES)


def _mix_body(*refs):
    n_pairs = HEADS_PER_GROUP // 2
    y_ref = refs[0]
    o_refs = refs[1:1 + 3 * n_pairs]
    l_refs = refs[1 + 3 * n_pairs:4 + 3 * n_pairs]
    (h_ref, wgate_ref, x_ref, wssm_ref, wattn_ref, wout_ref, e8_ref, nfw_ref, wr_ref, br_ref,
     x2_ref, h2_ref, lg_ref) = refs[4 + 3 * n_pairs:]
    tm = x_ref.shape[0]
    lses = [l[...] for l in l_refs]
    mx = jnp.maximum(jnp.maximum(lses[0], lses[1]), lses[2])
    es = [jnp.exp(l - mx) for l in lses]
    inv = 1.0 / (es[0] + es[1] + es[2])
    e8 = e8_ref[...]
    att = jnp.zeros((tm, GROUP_WIDTH), F32)
    for g in range(3):
        w = es[g] * inv
        w_hi = w.astype(BF16)
        w_lo = (w - w_hi.astype(F32)).astype(BF16)
        w_full = (jnp.dot(w_hi, e8, preferred_element_type=F32)
                  + jnp.dot(w_lo, e8, preferred_element_type=F32))
        o_g = jnp.concatenate([o_refs[g * n_pairs + p][...] for p in range(n_pairs)], axis=1)
        att = att + w_full * o_g
    y_attn = jnp.dot(att.astype(BF16), wattn_ref[...], preferred_element_type=F32)
    y_ssm = jnp.dot(y_ref[...], wssm_ref[...], preferred_element_type=F32)
    h = h_ref[...]
    g_ssm = jnp.dot(h, wgate_ref[:, 0:D_MODEL], preferred_element_type=F32)
    g_attn = jnp.dot(h, wgate_ref[:, D_MODEL:2 * D_MODEL], preferred_element_type=F32)
    merged = _sigmoid(g_ssm) * y_ssm + _sigmoid(g_attn) * y_attn
    x2 = x_ref[...] + jnp.dot(merged.astype(BF16), wout_ref[...], preferred_element_type=F32)
    x2_ref[...] = x2
    ms = jnp.mean(x2 * x2, axis=-1, keepdims=True)
    h2 = x2 * lax.rsqrt(ms + NORM_EPS) * nfw_ref[...]
    h2_ref[...] = h2.reshape(tm, 1, D_MODEL)
    lg_ref[...] = jnp.dot(h2.astype(BF16), wr_ref[...], preferred_element_type=F32) + br_ref[...]


def _mix_out(y_ssm, attn_outs, attn_lses, h2d, w_gate, x2d, w_ssm, w_attn, w_out, norm_ffn_w, w_router, b_router):
    t = x2d.shape[0]
    tm = min(256, t)
    e8 = (np.arange(LANES)[:, None] == np.arange(GROUP_WIDTH)[None, :] // ATTN_HEAD_DIM)
    e8 = jnp.asarray(e8.astype(np.float32), dtype=BF16)

    def rows(width, cb=0):
        return pl.BlockSpec((tm, width), lambda i: (i, cb))

    def const(shape):
        return pl.BlockSpec(shape, lambda i: (0,) * len(shape))

    return pl.pallas_call(
        _mix_body,
        grid=(t // tm,),
        in_specs=[rows(D_INNER)] + [rows(LANES)] * (len(attn_outs) + len(attn_lses)) + [
                  rows(D_MODEL), const((D_MODEL, 2 * D_MODEL)), rows(D_MODEL),
                  const((D_INNER, D_MODEL)), const((GROUP_WIDTH, D_MODEL)), const((D_MODEL, D_MODEL)),
                  const((LANES, GROUP_WIDTH)), const((1, D_MODEL)),
                  const((D_MODEL, LANES)), const((1, LANES))],
        out_specs=[rows(D_MODEL),
                   pl.BlockSpec((tm, 1, D_MODEL), lambda i: (i, 0, 0)),
                   rows(LANES)],
        out_shape=[jax.ShapeDtypeStruct((t, D_MODEL), F32),
                   jax.ShapeDtypeStruct((t, 1, D_MODEL), F32),
                   jax.ShapeDtypeStruct((t, LANES), F32)],
        compiler_params=_cparams(1),
    )(y_ssm, *attn_outs, *attn_lses, h2d, w_gate, x2d, w_ssm, w_attn, w_out, e8,
      norm_ffn_w[None], w_router, b_router)


def _route_body(lg_ref, dest_ref, gate_ref, cnt_ref, counts, pstart):
    tm = lg_ref.shape[0]
    ps = pl.program_id(0)
    i = pl.program_id(1)

    @pl.when((ps == 0) & (i == 0))
    def _():
        counts[...] = jnp.zeros_like(counts)
        pstart[...] = jnp.zeros_like(pstart)

    @pl.when((ps == 1) & (i == 0))
    def _():
        cnt = counts[...]
        padded = jnp.floor((cnt + (MOE_BLOCK - 1)) * (1.0 / MOE_BLOCK)) * MOE_BLOCK
        a = lax.broadcasted_iota(jnp.int32, (LANES, LANES), 0)
        bcol = lax.broadcasted_iota(jnp.int32, (LANES, LANES), 1)
        pad_end = jnp.dot(padded, (a <= bcol).astype(F32), precision=HIGHEST, preferred_element_type=F32)
        pstart[...] = pad_end - padded
        cnt_ref[...] = cnt
        counts[...] = jnp.zeros_like(counts)

    lg = lg_ref[...]
    lane = lax.broadcasted_iota(jnp.int32, (tm, LANES), 1)
    is_coarse = lane < N_EXPERT_GROUPS
    cmax = jnp.max(jnp.where(is_coarse, lg, NEG_BIG), axis=-1, keepdims=True)
    grp = jnp.min(jnp.where(is_coarse & (lg == cmax), lane, LANES), axis=-1, keepdims=True)
    group_p = 1.0 / jnp.sum(jnp.where(is_coarse, jnp.exp(lg - cmax), 0.0), axis=-1, keepdims=True)
    f_lo = N_EXPERT_GROUPS + EXPERTS_PER_GROUP * grp
    in_grp = (lane >= f_lo) & (lane < f_lo + EXPERTS_PER_GROUP)
    f1 = jnp.max(jnp.where(in_grp, lg, NEG_BIG), axis=-1, keepdims=True)
    i1 = jnp.min(jnp.where(in_grp & (lg == f1), lane, LANES), axis=-1, keepdims=True)
    rest = in_grp & (lane != i1)
    f2 = jnp.max(jnp.where(rest, lg, NEG_BIG), axis=-1, keepdims=True)
    i2 = jnp.min(jnp.where(rest & (lg == f2), lane, LANES), axis=-1, keepdims=True)
    e2 = jnp.exp(f2 - f1)
    g1 = group_p / (1.0 + e2)
    g2 = group_p * e2 / (1.0 + e2)

    oh1 = lane == i1 - N_EXPERT_GROUPS
    oh2 = lane == i2 - N_EXPERT_GROUPS
    onehot = jnp.where(oh1 | oh2, 1.0, 0.0)
    r = lax.broadcasted_iota(jnp.int32, (tm, tm), 0)
    c = lax.broadcasted_iota(jnp.int32, (tm, tm), 1)
    before = jnp.dot((r > c).astype(BF16), onehot.astype(BF16), preferred_element_type=F32)
    pos = pstart[0:1, :] + counts[0:1, :] + before
    d1 = jnp.sum(jnp.where(oh1, pos, 0.0), axis=-1, keepdims=True)
    d2 = jnp.sum(jnp.where(oh2, pos, 0.0), axis=-1, keepdims=True)
    counts[...] = counts[...] + jnp.sum(onehot, axis=0, keepdims=True)
    dest_ref[...] = jnp.where(lane == 0, d1, jnp.where(lane == 1, d2, 0.0)).astype(jnp.int32)
    gate_ref[...] = jnp.where(lane == 0, g1, jnp.where(lane == 1, g2, 0.0))


def _route(logits):
    t = logits.shape[0]
    tm = min(512, t)
    return pl.pallas_call(
        _route_body,
        grid=(2, t // tm),
        in_specs=[pl.BlockSpec((tm, LANES), lambda ps, i: (i, 0))],
        out_specs=[pl.BlockSpec((tm, LANES), lambda ps, i: (i * ps, 0)),
                   pl.BlockSpec((tm, LANES), lambda ps, i: (i * ps, 0)),
                   pl.BlockSpec((8, LANES), lambda ps, i: (0, 0))],
        out_shape=[jax.ShapeDtypeStruct((t, LANES), jnp.int32),
                   jax.ShapeDtypeStruct((t, LANES), F32),
                   jax.ShapeDtypeStruct((8, LANES), F32)],
        scratch_shapes=[pltpu.VMEM((8, LANES), F32), pltpu.VMEM((8, LANES), F32)],
        compiler_params=_cparams(2),
    )(logits)


def _dispatch_body(meta_ref, dest_ref, h_ref, rows_ref, zbuf, zsem, sem):
    tm = h_ref.shape[0]

    def zero_copy(e):
        start = pl.multiple_of(meta_ref[e] - MOE_BLOCK, MOE_BLOCK)
        return pltpu.make_async_copy(zbuf, rows_ref.at[pl.ds(start, MOE_BLOCK)], zsem)

    def tail_copy(blk):
        start = pl.multiple_of(blk * MOE_BLOCK, MOE_BLOCK)
        return pltpu.make_async_copy(zbuf, rows_ref.at[pl.ds(start, MOE_BLOCK)], zsem)

    @pl.when(pl.program_id(0) == 0)
    def _():
        zbuf[...] = jnp.zeros_like(zbuf)

        def start_zero(e, carry):
            @pl.when(meta_ref[N_EXPERTS + e] > 0)
            def _():
                zero_copy(e).start()
            return carry

        def wait_zero(e, carry):
            @pl.when(meta_ref[N_EXPERTS + e] > 0)
            def _():
                zero_copy(e).wait()
            return carry

        def start_tail(blk, carry):
            tail_copy(blk).start()
            return carry

        def wait_tail(blk, carry):
            tail_copy(blk).wait()
            return carry

        n_used = meta_ref[2 * N_EXPERTS]
        n_blocks = rows_ref.shape[0] // MOE_BLOCK
        lax.fori_loop(0, N_EXPERTS, start_zero, 0)
        lax.fori_loop(n_used, n_blocks, start_tail, 0)
        lax.fori_loop(0, N_EXPERTS, wait_zero, 0)
        lax.fori_loop(n_used, n_blocks, wait_tail, 0)

    def issue(j, carry):
        for k in range(2):
            d = dest_ref[0, 2 * j + k]
            pltpu.make_async_copy(h_ref.at[j], rows_ref.at[d], sem).start()
        return carry

    lax.fori_loop(0, tm, issue, 0, unroll=8)
    for _ in range(2):
        pltpu.make_async_copy(h_ref, rows_ref.at[pl.ds(0, tm)], sem).wait()


def _dispatch(meta, dest3, h2, n_rows):
    t = h2.shape[0]
    tm = dest3.shape[2] // 2
    return pl.pallas_call(
        _dispatch_body,
        grid_spec=pltpu.PrefetchScalarGridSpec(
            num_scalar_prefetch=1,
            grid=(t // tm,),
            in_specs=[pl.BlockSpec((None, 1, 2 * tm), lambda i, m: (i, 0, 0), memory_space=pltpu.SMEM),
                      pl.BlockSpec((tm, 1, D_MODEL), lambda i, m: (i, 0, 0))],
            out_specs=pl.BlockSpec(memory_space=pl.ANY),
            scratch_shapes=[pltpu.VMEM((MOE_BLOCK, 1, D_MODEL), F32),
                            pltpu.SemaphoreType.DMA(()),
                            pltpu.SemaphoreType.DMA(())]),
        out_shape=jax.ShapeDtypeStruct((n_rows, 1, D_MODEL), F32),
        compiler_params=_cparams(1),
    )(meta, dest3, h2)


def _expert_body(sched_ref, x_ref, wg_hbm, wu_hbm, wd_hbm, y_ref, x2d, wg_f, wu_f, wd_f, wg_b, wu_b, wd_b, sems):
    blk = pl.program_id(0)
    n_used = sched_ref[4, 0]

    def weight_copies(expert, slot):
        return [pltpu.make_async_copy(src.at[expert], dst.at[slot], sems.at[slot])
                for src, dst in ((wg_hbm, wg_f), (wu_hbm, wu_f), (wd_hbm, wd_f))]

    @pl.when(blk < n_used)
    def _():
        expert = sched_ref[0, blk]
        slot = sched_ref[2, blk]
        nxt = sched_ref[3, blk]

        @pl.when(sched_ref[1, blk] == 1)
        def _():
            @pl.when(blk == 0)
            def _():
                for c in weight_copies(expert, slot):
                    c.start()

            for c in weight_copies(expert, slot):
                c.wait()

            @pl.when(nxt >= 0)
            def _():
                for c in weight_copies(nxt, 1 - slot):
                    c.start()

            wg_b[...] = wg_f[slot].astype(BF16)
            wu_b[...] = wu_f[slot].astype(BF16)
            wd_b[...] = wd_f[slot].astype(BF16)

        x2d[...] = x_ref[...].reshape(MOE_BLOCK, D_MODEL)
        x = x2d[...].astype(BF16)
        g = jnp.dot(x, wg_b[...], preferred_element_type=F32)
        u = jnp.dot(x, wu_b[...], preferred_element_type=F32)
        hid = (g * _sigmoid(g)) * u
        y = jnp.dot(hid.astype(BF16), wd_b[...], preferred_element_type=F32)
        y_ref[...] = y.reshape(MOE_BLOCK, 1, D_MODEL)

    @pl.when(blk >= n_used)
    def _():
        y_ref[...] = jnp.zeros_like(y_ref)


def _experts(sched, rows, w_gate, w_up, w_down):
    n_rows = rows.shape[0]
    n_blocks = n_rows // MOE_BLOCK

    def row_map(b, sched):
        return (b, 0, 0)

    any_spec = pl.BlockSpec(memory_space=pl.ANY)
    return pl.pallas_call(
        _expert_body,
        grid_spec=pltpu.PrefetchScalarGridSpec(
            num_scalar_prefetch=1,
            grid=(n_blocks,),
            in_specs=[pl.BlockSpec((MOE_BLOCK, 1, D_MODEL), row_map), any_spec, any_spec, any_spec],
            out_specs=pl.BlockSpec((MOE_BLOCK, 1, D_MODEL), row_map),
            scratch_shapes=[pltpu.VMEM((MOE_BLOCK, D_MODEL), F32),
                            pltpu.VMEM((2, D_MODEL, D_EXPERT), F32),
                            pltpu.VMEM((2, D_MODEL, D_EXPERT), F32),
                            pltpu.VMEM((2, D_EXPERT, D_MODEL), F32),
                            pltpu.VMEM((D_MODEL, D_EXPERT), BF16),
                            pltpu.VMEM((D_MODEL, D_EXPERT), BF16),
                            pltpu.VMEM((D_EXPERT, D_MODEL), BF16),
                            pltpu.SemaphoreType.DMA((2,))]),
        out_shape=jax.ShapeDtypeStruct((n_rows, 1, D_MODEL), F32),
        compiler_params=_cparams(1),
    )(sched, rows, w_gate, w_up, w_down)


def _combine_body(dcur_ref, dnext_ref, gate_ref, x2_ref, y_ref, o_ref, buf_a, buf_b, y2d, sem_a, sem_b):
    tm = x2_ref.shape[0]
    i = pl.program_id(0)
    n = pl.num_programs(0)

    def issue(dref, buf, sem):
        def body(j, carry):
            for k in range(2):
                d = dref[0, 2 * j + k]
                pltpu.make_async_copy(y_ref.at[d], buf.at[k * tm + j], sem).start()
            return carry
        lax.fori_loop(0, tm, body, 0, unroll=8)

    def finish(buf, sem):
        pltpu.make_async_copy(y_ref.at[pl.ds(0, 2 * tm)], buf, sem).wait()
        y2d[...] = buf[...].reshape(2 * tm, D_MODEL)
        g = gate_ref[...]
        o_ref[...] = x2_ref[...] + g[:, 0:1] * y2d[0:tm, :] + g[:, 1:2] * y2d[tm:2 * tm, :]

    @pl.when(i == 0)
    def _():
        issue(dcur_ref, buf_a, sem_a)

    for par, (cur, cur_sem, nxt, nxt_sem) in enumerate(((buf_a, sem_a, buf_b, sem_b),
                                                        (buf_b, sem_b, buf_a, sem_a))):
        @pl.when(i % 2 == par)
        def _(cur=cur, cur_sem=cur_sem, nxt=nxt, nxt_sem=nxt_sem):
            @pl.when(i + 1 < n)
            def _():
                issue(dnext_ref, nxt, nxt_sem)
            finish(cur, cur_sem)


def _combine(dest3, gates, x2, y_rows):
    t = x2.shape[0]
    tm = dest3.shape[2] // 2
    nt = t // tm
    return pl.pallas_call(
        _combine_body,
        grid=(nt,),
        in_specs=[pl.BlockSpec((None, 1, 2 * tm), lambda i: (i, 0, 0), memory_space=pltpu.SMEM),
                  pl.BlockSpec((None, 1, 2 * tm), lambda i: (jnp.minimum(i + 1, nt - 1), 0, 0),
                               memory_space=pltpu.SMEM),
                  pl.BlockSpec((tm, LANES), lambda i: (i, 0)),
                  pl.BlockSpec((tm, D_MODEL), lambda i: (i, 0)),
                  pl.BlockSpec(memory_space=pl.ANY)],
        out_specs=pl.BlockSpec((tm, D_MODEL), lambda i: (i, 0)),
        out_shape=jax.ShapeDtypeStruct((t, D_MODEL), F32),
        scratch_shapes=[pltpu.VMEM((2 * tm, 1, D_MODEL), F32),
                        pltpu.VMEM((2 * tm, 1, D_MODEL), F32),
                        pltpu.VMEM((2 * tm, D_MODEL), F32),
                        pltpu.SemaphoreType.DMA(()),
                        pltpu.SemaphoreType.DMA(())],
        compiler_params=_cparams(1),
    )(dest3, dest3, gates, x2, y_rows)


def _layer(x, norm_mix_w, w_in, conv_w, conv_b, dt_bias, a_log, d_skip, ssm_norm_w, w_ssm_proj,
           q_norm_w, k_norm_w, rel_bias, w_attn_proj, w_out, norm_ffn_w, w_coarse, b_coarse,
           w_fine, b_fine, w_gate_exp, w_up_exp, w_down_exp):
    b, s, d = x.shape
    t = b * s
    x2d = x.reshape(t, d)

    dt_lo = 2 * D_INNER + BC_WIDTH
    qkv_lo = dt_lo + SSM_HEADS
    gate_lo = qkv_lo + 3 * ATTN_WIDTH
    w_ssd = jnp.concatenate([w_in[:, :qkv_lo], jnp.zeros((d, LANES - SSM_HEADS), w_in.dtype)],
                            axis=1).astype(BF16)
    w_gate = w_in[:, gate_lo:].astype(BF16)
    h2d = _prenorm(x2d, norm_mix_w[None])
    h3 = h2d.reshape(b, s, d)

    y_ssm = _ssd(h3, w_ssd, conv_w, conv_b, dt_bias, a_log, d_skip, ssm_norm_w).reshape(t, D_INNER)

    qw = jnp.tile(q_norm_w, 2)[None]
    kw = jnp.tile(k_norm_w, 2)[None]
    lane_head = np.arange(LANES) // ATTN_HEAD_DIM
    bd = jnp.asarray((lane_head[:, None] == lane_head[None, :]).astype(np.float32), dtype=BF16)
    attn_outs, attn_lses = [], []
    for gi, (window, dilation) in enumerate(DILATED_CONFIGS):
        assert window // dilation == ATTN_BLK and s % window == 0
        hs = slice(gi * GROUP_WIDTH, (gi + 1) * GROUP_WIDTH)
        w_qkv = jnp.concatenate([w_in[:, qkv_lo + j * ATTN_WIDTH:qkv_lo + (j + 1) * ATTN_WIDTH][:, hs]
                                 for j in range(3)], axis=1).astype(BF16)
        bias = _band_bias(rel_bias[:, gi * HEADS_PER_GROUP:(gi + 1) * HEADS_PER_GROUP], dilation)
        o, l = _attn_group(h3, w_qkv, dilation, bias, qw, kw, bd)
        attn_outs.extend(o)
        attn_lses.append(l)

    n_route = N_EXPERT_GROUPS + N_EXPERTS
    w_router = jnp.pad(jnp.concatenate([w_coarse, w_fine], axis=1), ((0, 0), (0, LANES - n_route)))
    b_router = jnp.pad(jnp.concatenate([b_coarse, b_fine]), (0, LANES - n_route))[None]
    x2, h2, logits = _mix_out(y_ssm, attn_outs, attn_lses, h2d, w_gate, x2d, w_ssm_proj.astype(BF16),
                              w_attn_proj.astype(BF16), w_out.astype(BF16), norm_ffn_w,
                              w_router.astype(BF16), b_router)

    dest, gates, counts = _route(logits)

    cnt = counts[0, :N_EXPERTS].astype(jnp.int32)
    padded = (cnt + MOE_BLOCK - 1) // MOE_BLOCK * MOE_BLOCK
    pad_end = jnp.cumsum(padded)
    n_blocks = -(-(2 * t + N_EXPERTS * (MOE_BLOCK - 1)) // MOE_BLOCK)
    block_start = jnp.arange(n_blocks, dtype=jnp.int32) * MOE_BLOCK
    block_expert = jnp.minimum(jnp.sum((pad_end[None, :] <= block_start[:, None]).astype(jnp.int32), axis=1),
                               N_EXPERTS - 1)
    n_used = (pad_end[-1:] // MOE_BLOCK).astype(jnp.int32)
    meta = jnp.concatenate([pad_end, padded, n_used]).astype(jnp.int32)
    blk_ids = jnp.arange(n_blocks, dtype=jnp.int32)
    first = ((blk_ids == 0) | (block_expert != jnp.roll(block_expert, 1))) & (blk_ids < n_used[0])
    slot = (jnp.cumsum(first.astype(jnp.int32)) - 1) % 2
    nxt_blk = blk_ids + padded[block_expert] // MOE_BLOCK
    nxt = jnp.where(nxt_blk < n_used[0], block_expert[jnp.minimum(nxt_blk, n_blocks - 1)], -1)
    sched = jnp.stack([block_expert, first.astype(jnp.int32), slot, nxt,
                       jnp.broadcast_to(n_used, (n_blocks,))]).astype(jnp.int32)

    tm_d = min(256, t)
    dest_d = dest[:, :2].reshape(t // tm_d, 1, 2 * tm_d)
    rows = _dispatch(meta, dest_d, h2, n_blocks * MOE_BLOCK)
    y_rows = _experts(sched, rows, w_gate_exp, w_up_exp, w_down_exp)
    tm_c = min(128, t)
    dest_c = dest[:, :2].reshape(t // tm_c, 1, 2 * tm_c)
    out = _combine(dest_c, gates, x2, y_rows)
    return out.reshape(b, s, d)


def kernel(x, norm_mix_w, w_in, conv_w, conv_b, dt_bias, a_log, d_skip, ssm_norm_w, w_ssm_proj,
           q_norm_w, k_norm_w, rel_bias, w_attn_proj, w_out, norm_ffn_w, w_coarse, b_coarse,
           w_fine, b_fine, w_gate_exp, w_up_exp, w_down_exp):
    depth = norm_mix_w.shape[0]
    for layer in range(depth):
        x = _layer(x, norm_mix_w[layer], w_in[layer], conv_w[layer], conv_b[layer], dt_bias[layer],
                   a_log[layer], d_skip[layer], ssm_norm_w[layer], w_ssm_proj[layer],
                   q_norm_w[layer], k_norm_w[layer], rel_bias, w_attn_proj[layer], w_out[layer],
                   norm_ffn_w[layer], w_coarse[layer], b_coarse[layer], w_fine[layer], b_fine[layer],
                   w_gate_exp[layer], w_up_exp[layer], w_down_exp[layer])
    return x
```

```python
import functools
import math

import jax
import jax.numpy as jnp
import numpy as np
from jax import lax
from jax.experimental import pallas as pl
from jax.experimental.pallas import tpu as pltpu

F32 = jnp.float32
BF16 = jnp.bfloat16
HIGHEST = lax.Precision.HIGHEST

LANES = 128
NORM_EPS = 1e-6
NEG_BIG = -1e30

D_MODEL = 1024
D_INNER = 2048
SSM_HEAD_DIM = 64
SSM_HEADS = 32
SSM_GROUPS = 2
D_STATE = 128
CONV_K = 4
BC_WIDTH = 2 * SSM_GROUPS * D_STATE
SSD_CHUNK = 128
ATTN_HEAD_DIM = 64
DILATED_CONFIGS = ((128, 1), (512, 4), (2048, 16))
HEADS_PER_GROUP = 8
GROUP_WIDTH = HEADS_PER_GROUP * ATTN_HEAD_DIM
ATTN_WIDTH = 3 * GROUP_WIDTH
ATTN_BLK = 128
NUM_BUCKETS = 32
MAX_DISTANCE = 2048
N_EXPERT_GROUPS = 8
EXPERTS_PER_GROUP = 8
N_EXPERTS = 64
D_EXPERT = 512
MOE_BLOCK = 256

SSD_PROJ_WIDTH = 2 * D_INNER + BC_WIDTH + LANES

VMEM_LIMIT = 56 * 1024 * 1024


def _sigmoid(x):
    return 1.0 / (1.0 + jnp.exp(-x))


def _cparams(n_axes):
    return pltpu.CompilerParams(dimension_semantics=("arbitrary",) * n_axes,
                                vmem_limit_bytes=VMEM_LIMIT)


def _prenorm_body(x_ref, nw_ref, h_ref):
    x = x_ref[...]
    ms = jnp.mean(x * x, axis=-1, keepdims=True)
    h_ref[...] = (x * lax.rsqrt(ms + NORM_EPS) * nw_ref[...]).astype(BF16)


def _prenorm(x2d, norm_w):
    t, d = x2d.shape
    tm = min(1024, t)
    return pl.pallas_call(
        _prenorm_body,
        grid=(t // tm,),
        in_specs=[pl.BlockSpec((tm, d), lambda i: (i, 0)), pl.BlockSpec((1, d), lambda i: (0, 0))],
        out_specs=pl.BlockSpec((tm, d), lambda i: (i, 0)),
        out_shape=jax.ShapeDtypeStruct((t, d), BF16),
        compiler_params=_cparams(1),
    )(x2d, norm_w)


def _dot3(x, y, x_is_exact):
    v = y if x_is_exact else x
    hi = v.astype(BF16)
    r1 = v - hi.astype(F32)
    mid = r1.astype(BF16)
    lo = (r1 - mid.astype(F32)).astype(BF16)
    if x_is_exact:
        return sum(jnp.dot(x, part, preferred_element_type=F32) for part in (hi, mid, lo))
    return sum(jnp.dot(part, y, preferred_element_type=F32) for part in (hi, mid, lo))


def _ssd_chunk(z, xbuf, r0, dt_raw, cw_ref, cb_ref, dtb_ref, alog_ref, dskip_ref, nw_ref, e_ref, state):
    L = SSD_CHUNK
    half = D_INNER // SSM_GROUPS
    hg = SSM_HEADS // SSM_GROUPS

    conv = cb_ref[...] + cw_ref[CONV_K - 1:CONV_K, :] * xbuf[8 + r0:8 + r0 + L, :]
    for k in range(CONV_K - 1):
        lo = 8 + r0 - (CONV_K - 1 - k)
        conv = conv + cw_ref[k:k + 1, :] * xbuf[lo:lo + L, :]
    xbc = conv * _sigmoid(conv)
    xs = xbc[:, :D_INNER]

    lane = lax.broadcasted_iota(jnp.int32, (L, LANES), 1)
    row = lax.broadcasted_iota(jnp.int32, (L, L), 0)
    col = lax.broadcasted_iota(jnp.int32, (L, L), 1)
    causal = row >= col

    v = dt_raw + dtb_ref[...]
    dt = jnp.maximum(v, 0.0) + jnp.log1p(jnp.exp(-jnp.abs(v)))
    dt = jnp.where(lane < SSM_HEADS, dt, 0.0)
    adt = dt * (-jnp.exp(alog_ref[...]))
    a_cs = _dot3(causal.astype(BF16), adt, True)
    a_cs_t = a_cs.T
    expand = e_ref[...]
    a_full = _dot3(a_cs, expand, False)
    dt_full = _dot3(dt, expand, False)
    a_tot = a_full[L - 1:L, :]
    decay_from_start = jnp.exp(a_full)
    decay_to_end = jnp.exp(a_tot - a_full)
    decay_chunk = jnp.exp(a_tot)

    xdt = xs * dt_full
    xw_b = (xdt * decay_to_end).astype(BF16)
    head_lo = lax.broadcasted_iota(jnp.int32, (L, D_INNER), 1) % (2 * SSM_HEAD_DIM) < SSM_HEAD_DIM
    xdt_lo = jnp.where(head_lo, xdt, 0.0).astype(BF16)
    xdt_hi = jnp.where(head_lo, 0.0, xdt).astype(BF16)

    b16, c16, cbs, y_offs, s_prevs = [], [], [], [], []
    for g in range(SSM_GROUPS):
        bg = xbc[:, D_INNER + g * D_STATE:D_INNER + (g + 1) * D_STATE]
        cg = xbc[:, D_INNER + (SSM_GROUPS + g) * D_STATE:D_INNER + (SSM_GROUPS + g + 1) * D_STATE]
        b16.append(bg)
        c16.append(cg.astype(BF16))
        cbs.append(lax.dot_general(c16[g], bg.astype(BF16), (((1,), (1,)), ((), ())),
                                   preferred_element_type=F32))
    for g in range(SSM_GROUPS):
        s_prevs.append(state[:, g * half:(g + 1) * half])
        y_offs.append(jnp.dot(c16[g], s_prevs[g].astype(BF16), preferred_element_type=F32))
    y_cols = []
    for g in range(SSM_GROUPS):
        for pr in range(half // LANES):
            h0 = g * hg + 2 * pr
            ps = slice(h0 * SSM_HEAD_DIM, (h0 + 2) * SSM_HEAD_DIM)
            ms = []
            for h in (h0, h0 + 1):
                seg = a_cs[:, h:h + 1] - a_cs_t[h:h + 1, :]
                ms.append((cbs[g] * jnp.exp(jnp.where(causal, seg, NEG_BIG))).astype(BF16))
            y_cols.append(jnp.dot(jnp.concatenate(ms, axis=1),
                                  jnp.concatenate([xdt_lo[:, ps], xdt_hi[:, ps]], axis=0),
                                  preferred_element_type=F32))
    for g in range(SSM_GROUPS):
        gs = slice(g * half, (g + 1) * half)
        state[:, gs] = decay_chunk[:, gs] * s_prevs[g] + jnp.dot(
            b16[g].T.astype(BF16), xw_b[:, gs], preferred_element_type=F32)

    y = (jnp.concatenate(y_cols, axis=1) + jnp.concatenate(y_offs, axis=1) * decay_from_start
         + dskip_ref[...] * xs)
    y = y * (z * _sigmoid(z))
    normed = []
    for g in range(SSM_GROUPS):
        yg = y[:, g * half:(g + 1) * half]
        ms = jnp.mean(yg * yg, axis=-1, keepdims=True)
        normed.append(yg * lax.rsqrt(ms + NORM_EPS))
    return jnp.concatenate(normed, axis=1) * nw_ref[...]


def _ssd_body(h_ref, w_ref, cw_ref, cb_ref, dtb_ref, alog_ref, dskip_ref, nw_ref, e_ref,
              y_ref, state, xbuf, zbuf, dtbuf):
    L = SSD_CHUNK
    rt = h_ref.shape[1]
    conv_dim = D_INNER + BC_WIDTH

    @pl.when(pl.program_id(1) == 0)
    def _():
        state[...] = jnp.zeros_like(state)
        xbuf[0:8, :] = jnp.zeros((8, conv_dim), F32)

    h = h_ref[0]
    zbuf[...] = jnp.dot(h, w_ref[:, 0:D_INNER], preferred_element_type=F32)
    xbuf[8:8 + rt, :] = jnp.dot(h, w_ref[:, D_INNER:D_INNER + conv_dim], preferred_element_type=F32)
    dtbuf[...] = jnp.dot(h, w_ref[:, D_INNER + conv_dim:], preferred_element_type=F32)

    for c in range(rt // L):
        r0 = c * L
        y = _ssd_chunk(zbuf[r0:r0 + L, :], xbuf, r0, dtbuf[r0:r0 + L, :], cw_ref, cb_ref,
                       dtb_ref, alog_ref, dskip_ref, nw_ref, e_ref, state)
        y_ref[0, r0:r0 + L, :] = y.astype(BF16)
    xbuf[0:8, :] = xbuf[rt:rt + 8, :]


def _ssd(h3, w_ssd, conv_w, conv_b, dt_bias, a_log, d_skip, ssm_norm_w):
    b, s, d = h3.shape
    rt = min(4 * SSD_CHUNK, s)
    pad = LANES - SSM_HEADS
    conv_dim = D_INNER + BC_WIDTH
    dtb = jnp.pad(dt_bias, (0, pad))[None]
    alog = jnp.pad(a_log, (0, pad))[None]
    dskip = jnp.repeat(d_skip, SSM_HEAD_DIM)[None]
    expand = (np.arange(LANES)[:, None] == np.arange(D_INNER)[None, :] // SSM_HEAD_DIM).astype(np.float32)

    def const(shape):
        return pl.BlockSpec(shape, lambda i, c: (0,) * len(shape))

    return pl.pallas_call(
        _ssd_body,
        grid=(b, s // rt),
        in_specs=[pl.BlockSpec((1, rt, d), lambda i, c: (i, c, 0)),
                  const((d, SSD_PROJ_WIDTH)),
                  const((CONV_K, conv_dim)), const((1, conv_dim)),
                  const((1, LANES)), const((1, LANES)),
                  const((1, D_INNER)), const((1, D_INNER)),
                  const((LANES, D_INNER))],
        out_specs=pl.BlockSpec((1, rt, D_INNER), lambda i, c: (i, c, 0)),
        out_shape=jax.ShapeDtypeStruct((b, s, D_INNER), BF16),
        scratch_shapes=[pltpu.VMEM((D_STATE, D_INNER), F32),
                        pltpu.VMEM((rt + 8, conv_dim), F32),
                        pltpu.VMEM((rt, D_INNER), F32),
                        pltpu.VMEM((rt, LANES), F32)],
        compiler_params=_cparams(2),
    )(h3, w_ssd, conv_w, conv_b[None], dtb, alog, dskip, ssm_norm_w[None],
      jnp.asarray(expand, dtype=BF16))


def _t5_causal_bucket(dist):
    max_exact = NUM_BUCKETS // 2
    large = max_exact + (np.log(np.maximum(dist, max_exact) / max_exact)
                         / math.log(MAX_DISTANCE / max_exact) * (NUM_BUCKETS - max_exact)).astype(np.int32)
    return np.where(dist < max_exact, dist, np.minimum(large, NUM_BUCKETS - 1)).astype(np.int32)


def _band_bias(rel_bias_group, dilation):
    blk = ATTN_BLK
    off = np.arange(blk)[:, None] + blk - np.arange(2 * blk)[None, :]
    in_win = (off >= 0) & (off <= blk)
    bucket = _t5_causal_bucket(np.clip(off, 0, None) * dilation)
    onehot = (bucket.reshape(-1, 1) == np.arange(NUM_BUCKETS)[None, :]).astype(np.float32)
    bias = jnp.dot(jnp.asarray(onehot), rel_bias_group.astype(F32), precision=HIGHEST)
    bias = jnp.transpose(bias.reshape(blk, 2 * blk, HEADS_PER_GROUP), (2, 0, 1))
    bias = jnp.where(in_win[None], bias, NEG_BIG)
    return bias.reshape(HEADS_PER_GROUP // 2, 2 * blk, 2 * blk)


def _attn_body(h_ref, w_ref, bias_ref, qw_ref, kw_ref, bd_ref, *rest, dilation):
    n_pairs = HEADS_PER_GROUP // 2
    o_refs = rest[0:n_pairs]
    lse_ref, qkv, kbuf, vbuf = rest[n_pairs:]
    blk = ATTN_BLK
    rt = h_ref.shape[1]
    span = blk * dilation
    n_sub = rt // span
    assert n_sub == 1 or dilation == 1
    step = pl.program_id(1)

    @pl.when(step == 0)
    def _():
        kbuf[...] = jnp.zeros_like(kbuf)
        vbuf[...] = jnp.zeros_like(vbuf)

    h = h_ref[0]
    for j in range(3 * n_pairs // 2):
        piece = jnp.dot(h, w_ref[:, 2 * j * LANES:2 * (j + 1) * LANES], preferred_element_type=F32)
        qkv[2 * j] = piece[:, 0:LANES]
        qkv[2 * j + 1] = piece[:, LANES:2 * LANES]

    bd = bd_ref[...]

    def head_norm(x, w_ref):
        xx = x * x
        hi = xx.astype(BF16)
        lo = (xx - hi.astype(F32)).astype(BF16)
        ss = (jnp.dot(hi, bd, preferred_element_type=F32)
              + jnp.dot(lo, bd, preferred_element_type=F32))
        return x * lax.rsqrt(ss * (1.0 / ATTN_HEAD_DIM) + NORM_EPS) * w_ref[...]

    lane = lax.broadcasted_iota(jnp.int32, (blk, LANES), 1)
    lo_half = lane < ATTN_HEAD_DIM
    nt = (((1,), (1,)), ((), ()))

    def one_block(it, carry):
        sub, res = (it, 0) if dilation == 1 else (0, it)
        rows = pl.ds(sub * span + res, blk, stride=dilation)
        gblk = step * n_sub + sub
        slot = gblk % 2
        first_pen = jnp.where(gblk == 0, NEG_BIG, 0.0)
        q_raw = [qkv[p, rows, :] for p in range(n_pairs)]
        k_raw = [qkv[n_pairs + p, rows, :] for p in range(n_pairs)]
        v_raw = [qkv[2 * n_pairs + p, rows, :] for p in range(n_pairs)]
        k_old = [kbuf[1 - slot, res * n_pairs + p] for p in range(n_pairs)]
        v_old = [vbuf[1 - slot, res * n_pairs + p] for p in range(n_pairs)]
        pairs = range(n_pairs)
        qn = head_norm(jnp.concatenate(q_raw, axis=0), qw_ref) * (ATTN_HEAD_DIM ** -0.5)
        kn = head_norm(jnp.concatenate(k_raw, axis=0), kw_ref).astype(BF16)
        k_new = [kn[p * blk:(p + 1) * blk] for p in pairs]
        v_new = [v_raw[p].astype(BF16) for p in pairs]
        q2 = []
        for p in pairs:
            qp = qn[p * blk:(p + 1) * blk]
            q2.append(jnp.concatenate([jnp.where(lo_half, qp, 0.0), jnp.where(lo_half, 0.0, qp)],
                                      axis=0).astype(BF16))
        s_prev = [lax.dot_general(q2[p], k_old[p], nt, preferred_element_type=F32) for p in pairs]
        s_cur = [lax.dot_general(q2[p], k_new[p], nt, preferred_element_type=F32) for p in pairs]
        e_prev, e_cur, m, d = [], [], [], []
        for p in pairs:
            sp = s_prev[p] + (bias_ref[p, :, 0:blk] + first_pen)
            sc = s_cur[p] + bias_ref[p, :, blk:2 * blk]
            mp = jnp.max(jnp.maximum(sp, sc), axis=-1, keepdims=True)
            ep = jnp.exp(sp - mp)
            ec = jnp.exp(sc - mp)
            m.append(mp)
            d.append(jnp.sum(ep + ec, axis=-1, keepdims=True))
            e_prev.append(ep.astype(BF16))
            e_cur.append(ec.astype(BF16))
        pv_prev = [jnp.dot(e_prev[p], v_old[p], preferred_element_type=F32) for p in pairs]
        pv_cur = [jnp.dot(e_cur[p], v_new[p], preferred_element_type=F32) for p in pairs]
        lse_tile = jnp.zeros((blk, LANES), F32)
        o_new = []
        for p in pairs:
            pv = (pv_prev[p] + pv_cur[p]) / d[p]
            o_new.append(jnp.where(lo_half, pv[0:blk], pv[blk:2 * blk]))
            lse = m[p] + jnp.log(d[p])
            lse_tile = jnp.where(lane == 2 * p, lse[0:blk], lse_tile)
            lse_tile = jnp.where(lane == 2 * p + 1, lse[blk:2 * blk], lse_tile)
        for p in pairs:
            kbuf[slot, res * n_pairs + p] = k_new[p]
            vbuf[slot, res * n_pairs + p] = v_new[p]
            o_refs[p][0, rows, :] = o_new[p]
        lse_ref[0, rows, :] = lse_tile
        return carry

    lax.fori_loop(0, n_sub * dilation, one_block, 0)


def _attn_group(h3, w_qkv, dilation, bias, qw, kw, bd):
    b, s, d = h3.shape
    blk = ATTN_BLK
    n_pairs = HEADS_PER_GROUP // 2
    rt = max(4 * blk, blk * dilation)
    rt = min(rt, s)

    def const(shape):
        return pl.BlockSpec(shape, lambda i, n: (0,) * len(shape))

    token_spec = pl.BlockSpec((1, rt, LANES), lambda i, n: (i, n, 0))
    res = pl.pallas_call(
        functools.partial(_attn_body, dilation=dilation),
        grid=(b, s // rt),
        in_specs=[pl.BlockSpec((1, rt, d), lambda i, n: (i, n, 0)),
                  const((d, 3 * GROUP_WIDTH)),
                  const((n_pairs, 2 * blk, 2 * blk)), const((1, LANES)), const((1, LANES)),
                  const((LANES, LANES))],
        out_specs=[token_spec] * (n_pairs + 1),
        out_shape=[jax.ShapeDtypeStruct((b, s, LANES), F32)] * (n_pairs + 1),
        scratch_shapes=[pltpu.VMEM((3 * n_pairs, rt, LANES), F32),
                        pltpu.VMEM((2, dilation * n_pairs, blk, LANES), BF16),
                        pltpu.VMEM((2, dilation * n_pairs, blk, LANES), BF16)],
        compiler_params=_cparams(2),
    )(h3, w_qkv, bias, qw, kw, bd)
    outs = [o.reshape(b * s, LANES) for o in res[:n_pairs]]
    return outs, res[n_pairs].reshape(b * s, LANES)


def _mix_body(*refs):
    n_pairs = HEADS_PER_GROUP // 2
    y_ref = refs[0]
    o_refs = refs[1:1 + 3 * n_pairs]
    l_refs = refs[1 + 3 * n_pairs:4 + 3 * n_pairs]
    (h_ref, wgate_ref, x_ref, wssm_ref, wattn_ref, wout_ref, e8_ref, nfw_ref, wr_ref, br_ref,
     x2_ref, h2_ref, lg_ref) = refs[4 + 3 * n_pairs:]
    tm = x_ref.shape[0]
    lses = [l[...] for l in l_refs]
    mx = jnp.maximum(jnp.maximum(lses[0], lses[1]), lses[2])
    es = [jnp.exp(l - mx) for l in lses]
    inv = 1.0 / (es[0] + es[1] + es[2])
    e8 = e8_ref[...]
    att = jnp.zeros((tm, GROUP_WIDTH), F32)
    for g in range(3):
        w = es[g] * inv
        w_hi = w.astype(BF16)
        w_lo = (w - w_hi.astype(F32)).astype(BF16)
        w_full = (jnp.dot(w_hi, e8, preferred_element_type=F32)
                  + jnp.dot(w_lo, e8, preferred_element_type=F32))
        o_g = jnp.concatenate([o_refs[g * n_pairs + p][...] for p in range(n_pairs)], axis=1)
        att = att + w_full * o_g
    y_attn = jnp.dot(att.astype(BF16), wattn_ref[...], preferred_element_type=F32)
    y_ssm = jnp.dot(y_ref[...], wssm_ref[...], preferred_element_type=F32)
    h = h_ref[...]
    g_ssm = jnp.dot(h, wgate_ref[:, 0:D_MODEL], preferred_element_type=F32)
    g_attn = jnp.dot(h, wgate_ref[:, D_MODEL:2 * D_MODEL], preferred_element_type=F32)
    merged = _sigmoid(g_ssm) * y_ssm + _sigmoid(g_attn) * y_attn
    x2 = x_ref[...] + jnp.dot(merged.astype(BF16), wout_ref[...], preferred_element_type=F32)
    x2_ref[...] = x2
    ms = jnp.mean(x2 * x2, axis=-1, keepdims=True)
    h2 = x2 * lax.rsqrt(ms + NORM_EPS) * nfw_ref[...]
    h2_ref[...] = h2.reshape(tm, 1, D_MODEL)
    lg_ref[...] = jnp.dot(h2.astype(BF16), wr_ref[...], preferred_element_type=F32) + br_ref[...]


def _mix_out(y_ssm, attn_outs, attn_lses, h2d, w_gate, x2d, w_ssm, w_attn, w_out, norm_ffn_w, w_router, b_router):
    t = x2d.shape[0]
    tm = min(512, t)
    e8 = (np.arange(LANES)[:, None] == np.arange(GROUP_WIDTH)[None, :] // ATTN_HEAD_DIM)
    e8 = jnp.asarray(e8.astype(np.float32), dtype=BF16)

    def rows(width, cb=0):
        return pl.BlockSpec((tm, width), lambda i: (i, cb))

    def const(shape):
        return pl.BlockSpec(shape, lambda i: (0,) * len(shape), pipeline_mode=pl.Buffered(1))

    return pl.pallas_call(
        _mix_body,
        grid=(t // tm,),
        in_specs=[rows(D_INNER)] + [rows(LANES)] * (len(attn_outs) + len(attn_lses)) + [
                  rows(D_MODEL), const((D_MODEL, 2 * D_MODEL)), rows(D_MODEL),
                  const((D_INNER, D_MODEL)), const((GROUP_WIDTH, D_MODEL)), const((D_MODEL, D_MODEL)),
                  const((LANES, GROUP_WIDTH)), const((1, D_MODEL)),
                  const((D_MODEL, LANES)), const((1, LANES))],
        out_specs=[rows(D_MODEL),
                   pl.BlockSpec((tm, 1, D_MODEL), lambda i: (i, 0, 0)),
                   rows(LANES)],
        out_shape=[jax.ShapeDtypeStruct((t, D_MODEL), F32),
                   jax.ShapeDtypeStruct((t, 1, D_MODEL), F32),
                   jax.ShapeDtypeStruct((t, LANES), F32)],
        compiler_params=_cparams(1),
    )(y_ssm, *attn_outs, *attn_lses, h2d, w_gate, x2d, w_ssm, w_attn, w_out, e8,
      norm_ffn_w[None], w_router, b_router)


def _route_body(lg_ref, dest_ref, gate_ref, cnt_ref, counts, pstart):
    tm = lg_ref.shape[0]
    ps = pl.program_id(0)
    i = pl.program_id(1)

    @pl.when((ps == 0) & (i == 0))
    def _():
        counts[...] = jnp.zeros_like(counts)
        pstart[...] = jnp.zeros_like(pstart)

    @pl.when((ps == 1) & (i == 0))
    def _():
        cnt = counts[...]
        padded = jnp.floor((cnt + (MOE_BLOCK - 1)) * (1.0 / MOE_BLOCK)) * MOE_BLOCK
        a = lax.broadcasted_iota(jnp.int32, (LANES, LANES), 0)
        bcol = lax.broadcasted_iota(jnp.int32, (LANES, LANES), 1)
        pad_end = jnp.dot(padded, (a <= bcol).astype(F32), precision=HIGHEST, preferred_element_type=F32)
        pstart[...] = pad_end - padded
        cnt_ref[...] = cnt
        counts[...] = jnp.zeros_like(counts)

    lg = lg_ref[...]
    lane = lax.broadcasted_iota(jnp.int32, (tm, LANES), 1)
    is_coarse = lane < N_EXPERT_GROUPS
    cmax = jnp.max(jnp.where(is_coarse, lg, NEG_BIG), axis=-1, keepdims=True)
    grp = jnp.min(jnp.where(is_coarse & (lg == cmax), lane, LANES), axis=-1, keepdims=True)
    group_p = 1.0 / jnp.sum(jnp.where(is_coarse, jnp.exp(lg - cmax), 0.0), axis=-1, keepdims=True)
    f_lo = N_EXPERT_GROUPS + EXPERTS_PER_GROUP * grp
    in_grp = (lane >= f_lo) & (lane < f_lo + EXPERTS_PER_GROUP)
    f1 = jnp.max(jnp.where(in_grp, lg, NEG_BIG), axis=-1, keepdims=True)
    i1 = jnp.min(jnp.where(in_grp & (lg == f1), lane, LANES), axis=-1, keepdims=True)
    rest = in_grp & (lane != i1)
    f2 = jnp.max(jnp.where(rest, lg, NEG_BIG), axis=-1, keepdims=True)
    i2 = jnp.min(jnp.where(rest & (lg == f2), lane, LANES), axis=-1, keepdims=True)
    e2 = jnp.exp(f2 - f1)
    g1 = group_p / (1.0 + e2)
    g2 = group_p * e2 / (1.0 + e2)

    oh1 = lane == i1 - N_EXPERT_GROUPS
    oh2 = lane == i2 - N_EXPERT_GROUPS
    onehot = jnp.where(oh1 | oh2, 1.0, 0.0)
    r = lax.broadcasted_iota(jnp.int32, (tm, tm), 0)
    c = lax.broadcasted_iota(jnp.int32, (tm, tm), 1)
    before = jnp.dot((r > c).astype(BF16), onehot.astype(BF16), preferred_element_type=F32)
    pos = pstart[0:1, :] + counts[0:1, :] + before
    d1 = jnp.sum(jnp.where(oh1, pos, 0.0), axis=-1, keepdims=True)
    d2 = jnp.sum(jnp.where(oh2, pos, 0.0), axis=-1, keepdims=True)
    counts[...] = counts[...] + jnp.sum(onehot, axis=0, keepdims=True)
    dest_ref[...] = jnp.where(lane == 0, d1, jnp.where(lane == 1, d2, 0.0)).astype(jnp.int32)
    gate_ref[...] = jnp.where(lane == 0, g1, jnp.where(lane == 1, g2, 0.0))


def _route(logits):
    t = logits.shape[0]
    tm = min(512, t)
    return pl.pallas_call(
        _route_body,
        grid=(2, t // tm),
        in_specs=[pl.BlockSpec((tm, LANES), lambda ps, i: (i, 0))],
        out_specs=[pl.BlockSpec((tm, LANES), lambda ps, i: (i * ps, 0)),
                   pl.BlockSpec((tm, LANES), lambda ps, i: (i * ps, 0)),
                   pl.BlockSpec((8, LANES), lambda ps, i: (0, 0))],
        out_shape=[jax.ShapeDtypeStruct((t, LANES), jnp.int32),
                   jax.ShapeDtypeStruct((t, LANES), F32),
                   jax.ShapeDtypeStruct((8, LANES), F32)],
        scratch_shapes=[pltpu.VMEM((8, LANES), F32), pltpu.VMEM((8, LANES), F32)],
        compiler_params=_cparams(2),
    )(logits)


def _dispatch_body(meta_ref, dest_ref, h_ref, rows_ref, zbuf, zsem, sem):
    tm = h_ref.shape[0]

    def zero_copy(e):
        start = pl.multiple_of(meta_ref[e] - MOE_BLOCK, MOE_BLOCK)
        return pltpu.make_async_copy(zbuf, rows_ref.at[pl.ds(start, MOE_BLOCK)], zsem)

    def tail_copy(blk):
        start = pl.multiple_of(blk * MOE_BLOCK, MOE_BLOCK)
        return pltpu.make_async_copy(zbuf, rows_ref.at[pl.ds(start, MOE_BLOCK)], zsem)

    @pl.when(pl.program_id(0) == 0)
    def _():
        zbuf[...] = jnp.zeros_like(zbuf)

        def start_zero(e, carry):
            @pl.when(meta_ref[N_EXPERTS + e] > 0)
            def _():
                zero_copy(e).start()
            return carry

        def wait_zero(e, carry):
            @pl.when(meta_ref[N_EXPERTS + e] > 0)
            def _():
                zero_copy(e).wait()
            return carry

        def start_tail(blk, carry):
            tail_copy(blk).start()
            return carry

        def wait_tail(blk, carry):
            tail_copy(blk).wait()
            return carry

        n_used = meta_ref[2 * N_EXPERTS]
        n_blocks = rows_ref.shape[0] // MOE_BLOCK
        lax.fori_loop(0, N_EXPERTS, start_zero, 0)
        lax.fori_loop(n_used, n_blocks, start_tail, 0)
        lax.fori_loop(0, N_EXPERTS, wait_zero, 0)
        lax.fori_loop(n_used, n_blocks, wait_tail, 0)

    def issue(j, carry):
        for k in range(2):
            d = dest_ref[0, 2 * j + k]
            pltpu.make_async_copy(h_ref.at[j], rows_ref.at[d], sem).start(priority=k)
        return carry

    lax.fori_loop(0, tm, issue, 0, unroll=8)
    for _ in range(2):
        pltpu.make_async_copy(h_ref, rows_ref.at[pl.ds(0, tm)], sem).wait()


def _dispatch(meta, dest3, h2, n_rows):
    t = h2.shape[0]
    tm = dest3.shape[2] // 2
    return pl.pallas_call(
        _dispatch_body,
        grid_spec=pltpu.PrefetchScalarGridSpec(
            num_scalar_prefetch=1,
            grid=(t // tm,),
            in_specs=[pl.BlockSpec((None, 1, 2 * tm), lambda i, m: (i, 0, 0), memory_space=pltpu.SMEM),
                      pl.BlockSpec((tm, 1, D_MODEL), lambda i, m: (i, 0, 0))],
            out_specs=pl.BlockSpec(memory_space=pl.ANY),
            scratch_shapes=[pltpu.VMEM((MOE_BLOCK, 1, D_MODEL), F32),
                            pltpu.SemaphoreType.DMA(()),
                            pltpu.SemaphoreType.DMA(())]),
        out_shape=jax.ShapeDtypeStruct((n_rows, 1, D_MODEL), F32),
        compiler_params=_cparams(1),
    )(meta, dest3, h2)


def _expert_body(sched_ref, x_ref, wg_hbm, wu_hbm, wd_hbm, y_ref, x2d, wg_f, wu_f, wd_f, wg_b, wu_b, wd_b, sems):
    blk = pl.program_id(0)
    n_used = sched_ref[4, 0]

    def weight_copies(expert, slot):
        return [pltpu.make_async_copy(src.at[expert], dst.at[slot], sems.at[slot])
                for src, dst in ((wg_hbm, wg_f), (wu_hbm, wu_f), (wd_hbm, wd_f))]

    @pl.when(blk < n_used)
    def _():
        expert = sched_ref[0, blk]
        slot = sched_ref[2, blk]
        nxt = sched_ref[3, blk]

        @pl.when(sched_ref[1, blk] == 1)
        def _():
            @pl.when(blk == 0)
            def _():
                for c in weight_copies(expert, slot):
                    c.start()

            for c in weight_copies(expert, slot):
                c.wait()

            @pl.when(nxt >= 0)
            def _():
                for c in weight_copies(nxt, 1 - slot):
                    c.start()

            wg_b[...] = wg_f[slot].astype(BF16)
            wu_b[...] = wu_f[slot].astype(BF16)
            wd_b[...] = wd_f[slot].astype(BF16)

        x2d[...] = x_ref[...].reshape(MOE_BLOCK, D_MODEL)
        x = x2d[...].astype(BF16)
        g = jnp.dot(x, wg_b[...], preferred_element_type=F32)
        u = jnp.dot(x, wu_b[...], preferred_element_type=F32)
        hid = (g * _sigmoid(g)) * u
        y = jnp.dot(hid.astype(BF16), wd_b[...], preferred_element_type=F32)
        y_ref[...] = y.reshape(MOE_BLOCK, 1, D_MODEL)

    @pl.when(blk >= n_used)
    def _():
        y_ref[...] = jnp.zeros_like(y_ref)


def _experts(sched, rows, w_gate, w_up, w_down):
    n_rows = rows.shape[0]
    n_blocks = n_rows // MOE_BLOCK

    def row_map(b, sched):
        return (b, 0, 0)

    any_spec = pl.BlockSpec(memory_space=pl.ANY)
    return pl.pallas_call(
        _expert_body,
        grid_spec=pltpu.PrefetchScalarGridSpec(
            num_scalar_prefetch=1,
            grid=(n_blocks,),
            in_specs=[pl.BlockSpec((MOE_BLOCK, 1, D_MODEL), row_map), any_spec, any_spec, any_spec],
            out_specs=pl.BlockSpec((MOE_BLOCK, 1, D_MODEL), row_map),
            scratch_shapes=[pltpu.VMEM((MOE_BLOCK, D_MODEL), F32),
                            pltpu.VMEM((2, D_MODEL, D_EXPERT), F32),
                            pltpu.VMEM((2, D_MODEL, D_EXPERT), F32),
                            pltpu.VMEM((2, D_EXPERT, D_MODEL), F32),
                            pltpu.VMEM((D_MODEL, D_EXPERT), BF16),
                            pltpu.VMEM((D_MODEL, D_EXPERT), BF16),
                            pltpu.VMEM((D_EXPERT, D_MODEL), BF16),
                            pltpu.SemaphoreType.DMA((2,))]),
        out_shape=jax.ShapeDtypeStruct((n_rows, 1, D_MODEL), F32),
        compiler_params=_cparams(1),
    )(sched, rows, w_gate, w_up, w_down)


def _combine_body(dcur_ref, dnext_ref, gate_ref, x2_ref, y_ref, o_ref, buf_a, buf_b, y2d, sem_a, sem_b):
    tm = x2_ref.shape[0]
    i = pl.program_id(0)
    n = pl.num_programs(0)

    def issue(dref, buf, sem):
        def body(j, carry):
            for k in range(2):
                d = dref[0, 2 * j + k]
                pltpu.make_async_copy(y_ref.at[d], buf.at[k * tm + j], sem).start(priority=k)
            return carry
        lax.fori_loop(0, tm, body, 0, unroll=8)

    def finish(buf, sem):
        pltpu.make_async_copy(y_ref.at[pl.ds(0, 2 * tm)], buf, sem).wait()
        y2d[...] = buf[...].reshape(2 * tm, D_MODEL)
        g = gate_ref[...]
        o_ref[...] = x2_ref[...] + g[:, 0:1] * y2d[0:tm, :] + g[:, 1:2] * y2d[tm:2 * tm, :]

    @pl.when(i == 0)
    def _():
        issue(dcur_ref, buf_a, sem_a)

    for par, (cur, cur_sem, nxt, nxt_sem) in enumerate(((buf_a, sem_a, buf_b, sem_b),
                                                        (buf_b, sem_b, buf_a, sem_a))):
        @pl.when(i % 2 == par)
        def _(cur=cur, cur_sem=cur_sem, nxt=nxt, nxt_sem=nxt_sem):
            @pl.when(i + 1 < n)
            def _():
                issue(dnext_ref, nxt, nxt_sem)
            finish(cur, cur_sem)


def _combine(dest3, gates, x2, y_rows):
    t = x2.shape[0]
    tm = dest3.shape[2] // 2
    nt = t // tm
    return pl.pallas_call(
        _combine_body,
        grid=(nt,),
        in_specs=[pl.BlockSpec((None, 1, 2 * tm), lambda i: (i, 0, 0), memory_space=pltpu.SMEM),
                  pl.BlockSpec((None, 1, 2 * tm), lambda i: (jnp.minimum(i + 1, nt - 1), 0, 0),
                               memory_space=pltpu.SMEM),
                  pl.BlockSpec((tm, LANES), lambda i: (i, 0)),
                  pl.BlockSpec((tm, D_MODEL), lambda i: (i, 0)),
                  pl.BlockSpec(memory_space=pl.ANY)],
        out_specs=pl.BlockSpec((tm, D_MODEL), lambda i: (i, 0)),
        out_shape=jax.ShapeDtypeStruct((t, D_MODEL), F32),
        scratch_shapes=[pltpu.VMEM((2 * tm, 1, D_MODEL), F32),
                        pltpu.VMEM((2 * tm, 1, D_MODEL), F32),
                        pltpu.VMEM((2 * tm, D_MODEL), F32),
                        pltpu.SemaphoreType.DMA(()),
                        pltpu.SemaphoreType.DMA(())],
        compiler_params=_cparams(1),
    )(dest3, dest3, gates, x2, y_rows)


def _layer(x, norm_mix_w, w_in, conv_w, conv_b, dt_bias, a_log, d_skip, ssm_norm_w, w_ssm_proj,
           q_norm_w, k_norm_w, rel_bias, w_attn_proj, w_out, norm_ffn_w, w_coarse, b_coarse,
           w_fine, b_fine, w_gate_exp, w_up_exp, w_down_exp):
    b, s, d = x.shape
    t = b * s
    x2d = x.reshape(t, d)

    dt_lo = 2 * D_INNER + BC_WIDTH
    qkv_lo = dt_lo + SSM_HEADS
    gate_lo = qkv_lo + 3 * ATTN_WIDTH
    w_ssd = jnp.concatenate([w_in[:, :qkv_lo], jnp.zeros((d, LANES - SSM_HEADS), w_in.dtype)],
                            axis=1).astype(BF16)
    w_gate = w_in[:, gate_lo:].astype(BF16)
    h2d = _prenorm(x2d, norm_mix_w[None])
    h3 = h2d.reshape(b, s, d)

    y_ssm = _ssd(h3, w_ssd, conv_w, conv_b, dt_bias, a_log, d_skip, ssm_norm_w).reshape(t, D_INNER)

    qw = jnp.tile(q_norm_w, 2)[None]
    kw = jnp.tile(k_norm_w, 2)[None]
    lane_head = np.arange(LANES) // ATTN_HEAD_DIM
    bd = jnp.asarray((lane_head[:, None] == lane_head[None, :]).astype(np.float32), dtype=BF16)
    attn_outs, attn_lses = [], []
    for gi, (window, dilation) in enumerate(DILATED_CONFIGS):
        assert window // dilation == ATTN_BLK and s % window == 0
        hs = slice(gi * GROUP_WIDTH, (gi + 1) * GROUP_WIDTH)
        w_qkv = jnp.concatenate([w_in[:, qkv_lo + j * ATTN_WIDTH:qkv_lo + (j + 1) * ATTN_WIDTH][:, hs]
                                 for j in range(3)], axis=1).astype(BF16)
        bias = _band_bias(rel_bias[:, gi * HEADS_PER_GROUP:(gi + 1) * HEADS_PER_GROUP], dilation)
        o, l = _attn_group(h3, w_qkv, dilation, bias, qw, kw, bd)
        attn_outs.extend(o)
        attn_lses.append(l)

    n_route = N_EXPERT_GROUPS + N_EXPERTS
    w_router = jnp.pad(jnp.concatenate([w_coarse, w_fine], axis=1), ((0, 0), (0, LANES - n_route)))
    b_router = jnp.pad(jnp.concatenate([b_coarse, b_fine]), (0, LANES - n_route))[None]
    x2, h2, logits = _mix_out(y_ssm, attn_outs, attn_lses, h2d, w_gate, x2d, w_ssm_proj.astype(BF16),
                              w_attn_proj.astype(BF16), w_out.astype(BF16), norm_ffn_w,
                              w_router.astype(BF16), b_router)

    dest, gates, counts = _route(logits)

    cnt = counts[0, :N_EXPERTS].astype(jnp.int32)
    padded = (cnt + MOE_BLOCK - 1) // MOE_BLOCK * MOE_BLOCK
    pad_end = jnp.cumsum(padded)
    n_blocks = -(-(2 * t + N_EXPERTS * (MOE_BLOCK - 1)) // MOE_BLOCK)
    block_start = jnp.arange(n_blocks, dtype=jnp.int32) * MOE_BLOCK
    block_expert = jnp.minimum(jnp.sum((pad_end[None, :] <= block_start[:, None]).astype(jnp.int32), axis=1),
                               N_EXPERTS - 1)
    n_used = (pad_end[-1:] // MOE_BLOCK).astype(jnp.int32)
    meta = jnp.concatenate([pad_end, padded, n_used]).astype(jnp.int32)
    blk_ids = jnp.arange(n_blocks, dtype=jnp.int32)
    first = ((blk_ids == 0) | (block_expert != jnp.roll(block_expert, 1))) & (blk_ids < n_used[0])
    slot = (jnp.cumsum(first.astype(jnp.int32)) - 1) % 2
    nxt_blk = blk_ids + padded[block_expert] // MOE_BLOCK
    nxt = jnp.where(nxt_blk < n_used[0], block_expert[jnp.minimum(nxt_blk, n_blocks - 1)], -1)
    sched = jnp.stack([block_expert, first.astype(jnp.int32), slot, nxt,
                       jnp.broadcast_to(n_used, (n_blocks,))]).astype(jnp.int32)

    tm_d = min(256, t)
    dest_d = dest[:, :2].reshape(t // tm_d, 1, 2 * tm_d)
    rows = _dispatch(meta, dest_d, h2, n_blocks * MOE_BLOCK)
    y_rows = _experts(sched, rows, w_gate_exp, w_up_exp, w_down_exp)
    tm_c = min(128, t)
    dest_c = dest[:, :2].reshape(t // tm_c, 1, 2 * tm_c)
    out = _combine(dest_c, gates, x2, y_rows)
    return out.reshape(b, s, d)


def kernel(x, norm_mix_w, w_in, conv_w, conv_b, dt_bias, a_log, d_skip, ssm_norm_w, w_ssm_proj,
           q_norm_w, k_norm_w, rel_bias, w_attn_proj, w_out, norm_ffn_w, w_coarse, b_coarse,
           w_fine, b_fine, w_gate_exp, w_up_exp, w_down_exp):
    depth = norm_mix_w.shape[0]
    for layer in range(depth):
        x = _layer(x, norm_mix_w[layer], w_in[layer], conv_w[layer], conv_b[layer], dt_bias[layer],
                   a_log[layer], d_skip[layer], ssm_norm_w[layer], w_ssm_proj[layer],
                   q_norm_w[layer], k_norm_w[layer], rel_bias, w_attn_proj[layer], w_out[layer],
                   norm_ffn_w[layer], w_coarse[layer], b_coarse[layer], w_fine[layer], b_fine[layer],
                   w_gate_exp[layer], w_up_exp[layer], w_down_exp[layer])
    return x
```

```python
import functools
import math

import jax
import jax.numpy as jnp
import numpy as np
from jax import lax
from jax.experimental import pallas as pl
from jax.experimental.pallas import tpu as pltpu

F32 = jnp.float32
BF16 = jnp.bfloat16
HIGHEST = lax.Precision.HIGHEST

LANES = 128
NORM_EPS = 1e-6
NEG_BIG = -1e30

D_MODEL = 1024
D_INNER = 2048
SSM_HEAD_DIM = 64
SSM_HEADS = 32
SSM_GROUPS = 2
D_STATE = 128
CONV_K = 4
BC_WIDTH = 2 * SSM_GROUPS * D_STATE
SSD_CHUNK = 128
ATTN_HEAD_DIM = 64
DILATED_CONFIGS = ((128, 1), (512, 4), (2048, 16))
HEADS_PER_GROUP = 8
GROUP_WIDTH = HEADS_PER_GROUP * ATTN_HEAD_DIM
ATTN_WIDTH = 3 * GROUP_WIDTH
ATTN_BLK = 128
NUM_BUCKETS = 32
MAX_DISTANCE = 2048
N_EXPERT_GROUPS = 8
EXPERTS_PER_GROUP = 8
N_EXPERTS = 64
D_EXPERT = 512
MOE_BLOCK = 256

SSD_PROJ_WIDTH = 2 * D_INNER + BC_WIDTH + LANES

VMEM_LIMIT = 56 * 1024 * 1024


def _sigmoid(x):
    return 1.0 / (1.0 + jnp.exp(-x))


def _cparams(n_axes):
    return pltpu.CompilerParams(dimension_semantics=("arbitrary",) * n_axes,
                                vmem_limit_bytes=VMEM_LIMIT)


def _prenorm_body(x_ref, nw_ref, h_ref):
    x = x_ref[...]
    ms = jnp.mean(x * x, axis=-1, keepdims=True)
    h_ref[...] = (x * lax.rsqrt(ms + NORM_EPS) * nw_ref[...]).astype(BF16)


def _prenorm(x2d, norm_w):
    t, d = x2d.shape
    tm = min(1024, t)
    return pl.pallas_call(
        _prenorm_body,
        grid=(t // tm,),
        in_specs=[pl.BlockSpec((tm, d), lambda i: (i, 0)), pl.BlockSpec((1, d), lambda i: (0, 0))],
        out_specs=pl.BlockSpec((tm, d), lambda i: (i, 0)),
        out_shape=jax.ShapeDtypeStruct((t, d), BF16),
        compiler_params=_cparams(1),
    )(x2d, norm_w)


def _dot3(x, y, x_is_exact):
    v = y if x_is_exact else x
    hi = v.astype(BF16)
    r1 = v - hi.astype(F32)
    mid = r1.astype(BF16)
    lo = (r1 - mid.astype(F32)).astype(BF16)
    if x_is_exact:
        return sum(jnp.dot(x, part, preferred_element_type=F32) for part in (hi, mid, lo))
    return sum(jnp.dot(part, y, preferred_element_type=F32) for part in (hi, mid, lo))


def _ssd_chunk(z, xbuf, r0, dt_raw, cw_ref, cb_ref, dtb_ref, alog_ref, dskip_ref, nw_ref, e_ref, state):
    L = SSD_CHUNK
    half = D_INNER // SSM_GROUPS
    hg = SSM_HEADS // SSM_GROUPS

    conv = cb_ref[...] + cw_ref[CONV_K - 1:CONV_K, :] * xbuf[8 + r0:8 + r0 + L, :]
    for k in range(CONV_K - 1):
        lo = 8 + r0 - (CONV_K - 1 - k)
        conv = conv + cw_ref[k:k + 1, :] * xbuf[lo:lo + L, :]
    xbc = conv * _sigmoid(conv)
    xs = xbc[:, :D_INNER]

    lane = lax.broadcasted_iota(jnp.int32, (L, LANES), 1)
    row = lax.broadcasted_iota(jnp.int32, (L, L), 0)
    col = lax.broadcasted_iota(jnp.int32, (L, L), 1)
    causal = row >= col

    v = dt_raw + dtb_ref[...]
    dt = jnp.maximum(v, 0.0) + jnp.log1p(jnp.exp(-jnp.abs(v)))
    dt = jnp.where(lane < SSM_HEADS, dt, 0.0)
    adt = dt * (-jnp.exp(alog_ref[...]))
    a_cs = _dot3(causal.astype(BF16), adt, True)
    a_cs_t = a_cs.T
    expand = e_ref[...]
    a_full = _dot3(a_cs, expand, False)
    dt_full = _dot3(dt, expand, False)
    a_tot = a_full[L - 1:L, :]
    decay_from_start = jnp.exp(a_full)
    decay_to_end = jnp.exp(a_tot - a_full)
    decay_chunk = jnp.exp(a_tot)

    xdt = xs * dt_full
    xw_b = (xdt * decay_to_end).astype(BF16)
    head_lo = lax.broadcasted_iota(jnp.int32, (L, D_INNER), 1) % (2 * SSM_HEAD_DIM) < SSM_HEAD_DIM
    xdt_lo = jnp.where(head_lo, xdt, 0.0).astype(BF16)
    xdt_hi = jnp.where(head_lo, 0.0, xdt).astype(BF16)

    b16, c16, cbs, y_offs, s_prevs = [], [], [], [], []
    for g in range(SSM_GROUPS):
        bg = xbc[:, D_INNER + g * D_STATE:D_INNER + (g + 1) * D_STATE]
        cg = xbc[:, D_INNER + (SSM_GROUPS + g) * D_STATE:D_INNER + (SSM_GROUPS + g + 1) * D_STATE]
        b16.append(bg)
        c16.append(cg.astype(BF16))
        cbs.append(lax.dot_general(c16[g], bg.astype(BF16), (((1,), (1,)), ((), ())),
                                   preferred_element_type=F32))
    for g in range(SSM_GROUPS):
        s_prevs.append(state[:, g * half:(g + 1) * half])
        y_offs.append(jnp.dot(c16[g], s_prevs[g].astype(BF16), preferred_element_type=F32))
    y_cols = []
    for g in range(SSM_GROUPS):
        for pr in range(half // LANES):
            h0 = g * hg + 2 * pr
            ps = slice(h0 * SSM_HEAD_DIM, (h0 + 2) * SSM_HEAD_DIM)
            ms = []
            for h in (h0, h0 + 1):
                seg = a_cs[:, h:h + 1] - a_cs_t[h:h + 1, :]
                ms.append((cbs[g] * jnp.exp(jnp.where(causal, seg, NEG_BIG))).astype(BF16))
            y_cols.append(jnp.dot(jnp.concatenate(ms, axis=1),
                                  jnp.concatenate([xdt_lo[:, ps], xdt_hi[:, ps]], axis=0),
                                  preferred_element_type=F32))
    for g in range(SSM_GROUPS):
        gs = slice(g * half, (g + 1) * half)
        state[:, gs] = decay_chunk[:, gs] * s_prevs[g] + jnp.dot(
            b16[g].T.astype(BF16), xw_b[:, gs], preferred_element_type=F32)

    y = (jnp.concatenate(y_cols, axis=1) + jnp.concatenate(y_offs, axis=1) * decay_from_start
         + dskip_ref[...] * xs)
    y = y * (z * _sigmoid(z))
    normed = []
    for g in range(SSM_GROUPS):
        yg = y[:, g * half:(g + 1) * half]
        ms = jnp.mean(yg * yg, axis=-1, keepdims=True)
        normed.append(yg * lax.rsqrt(ms + NORM_EPS))
    return jnp.concatenate(normed, axis=1) * nw_ref[...]


def _ssd_body(h_ref, w_ref, cw_ref, cb_ref, dtb_ref, alog_ref, dskip_ref, nw_ref, e_ref,
              y_ref, state, xbuf, zbuf, dtbuf):
    L = SSD_CHUNK
    rt = h_ref.shape[1]
    conv_dim = D_INNER + BC_WIDTH

    @pl.when(pl.program_id(1) == 0)
    def _():
        state[...] = jnp.zeros_like(state)
        xbuf[0:8, :] = jnp.zeros((8, conv_dim), F32)

    h = h_ref[0]
    zbuf[...] = jnp.dot(h, w_ref[:, 0:D_INNER], preferred_element_type=F32)
    xbuf[8:8 + rt, :] = jnp.dot(h, w_ref[:, D_INNER:D_INNER + conv_dim], preferred_element_type=F32)
    dtbuf[...] = jnp.dot(h, w_ref[:, D_INNER + conv_dim:], preferred_element_type=F32)

    for c in range(rt // L):
        r0 = c * L
        y = _ssd_chunk(zbuf[r0:r0 + L, :], xbuf, r0, dtbuf[r0:r0 + L, :], cw_ref, cb_ref,
                       dtb_ref, alog_ref, dskip_ref, nw_ref, e_ref, state)
        y_ref[0, r0:r0 + L, :] = y.astype(BF16)
    xbuf[0:8, :] = xbuf[rt:rt + 8, :]


def _ssd(h3, w_ssd, conv_w, conv_b, dt_bias, a_log, d_skip, ssm_norm_w):
    b, s, d = h3.shape
    rt = min(4 * SSD_CHUNK, s)
    pad = LANES - SSM_HEADS
    conv_dim = D_INNER + BC_WIDTH
    dtb = jnp.pad(dt_bias, (0, pad))[None]
    alog = jnp.pad(a_log, (0, pad))[None]
    dskip = jnp.repeat(d_skip, SSM_HEAD_DIM)[None]
    expand = (np.arange(LANES)[:, None] == np.arange(D_INNER)[None, :] // SSM_HEAD_DIM).astype(np.float32)

    def const(shape):
        return pl.BlockSpec(shape, lambda i, c: (0,) * len(shape))

    return pl.pallas_call(
        _ssd_body,
        grid=(b, s // rt),
        in_specs=[pl.BlockSpec((1, rt, d), lambda i, c: (i, c, 0)),
                  const((d, SSD_PROJ_WIDTH)),
                  const((CONV_K, conv_dim)), const((1, conv_dim)),
                  const((1, LANES)), const((1, LANES)),
                  const((1, D_INNER)), const((1, D_INNER)),
                  const((LANES, D_INNER))],
        out_specs=pl.BlockSpec((1, rt, D_INNER), lambda i, c: (i, c, 0)),
        out_shape=jax.ShapeDtypeStruct((b, s, D_INNER), BF16),
        scratch_shapes=[pltpu.VMEM((D_STATE, D_INNER), F32),
                        pltpu.VMEM((rt + 8, conv_dim), F32),
                        pltpu.VMEM((rt, D_INNER), F32),
                        pltpu.VMEM((rt, LANES), F32)],
        compiler_params=_cparams(2),
    )(h3, w_ssd, conv_w, conv_b[None], dtb, alog, dskip, ssm_norm_w[None],
      jnp.asarray(expand, dtype=BF16))


def _t5_causal_bucket(dist):
    max_exact = NUM_BUCKETS // 2
    large = max_exact + (np.log(np.maximum(dist, max_exact) / max_exact)
                         / math.log(MAX_DISTANCE / max_exact) * (NUM_BUCKETS - max_exact)).astype(np.int32)
    return np.where(dist < max_exact, dist, np.minimum(large, NUM_BUCKETS - 1)).astype(np.int32)


def _band_bias(rel_bias_group, dilation):
    blk = ATTN_BLK
    off = np.arange(blk)[:, None] + blk - np.arange(2 * blk)[None, :]
    in_win = (off >= 0) & (off <= blk)
    bucket = _t5_causal_bucket(np.clip(off, 0, None) * dilation)
    onehot = (bucket.reshape(-1, 1) == np.arange(NUM_BUCKETS)[None, :]).astype(np.float32)
    bias = jnp.dot(jnp.asarray(onehot), rel_bias_group.astype(F32), precision=HIGHEST)
    bias = jnp.transpose(bias.reshape(blk, 2 * blk, HEADS_PER_GROUP), (2, 0, 1))
    bias = jnp.where(in_win[None], bias, NEG_BIG)
    return bias.reshape(HEADS_PER_GROUP // 2, 2 * blk, 2 * blk)


def _attn_body(h_ref, w_ref, bias_ref, qw_ref, kw_ref, *rest, dilation):
    n_pairs = HEADS_PER_GROUP // 2
    o_refs = rest[0:n_pairs]
    lse_ref, qkv, kbuf, vbuf = rest[n_pairs:]
    blk = ATTN_BLK
    rt = h_ref.shape[1]
    span = blk * dilation
    n_sub = rt // span
    assert n_sub == 1 or dilation == 1
    step = pl.program_id(1)

    @pl.when(step == 0)
    def _():
        kbuf[...] = jnp.zeros_like(kbuf)
        vbuf[...] = jnp.zeros_like(vbuf)

    first_head = lax.broadcasted_iota(jnp.int32, (rt, LANES), 1) < ATTN_HEAD_DIM

    def head_norm(x, w_ref):
        xx = x * x
        s0 = jnp.sum(jnp.where(first_head, xx, 0.0), axis=-1, keepdims=True)
        s1 = jnp.sum(jnp.where(first_head, 0.0, xx), axis=-1, keepdims=True)
        ss = jnp.where(first_head, s0, s1)
        return x * lax.rsqrt(ss * (1.0 / ATTN_HEAD_DIM) + NORM_EPS) * w_ref[...]

    h = h_ref[0]
    for j in range(3 * n_pairs // 2):
        piece = jnp.dot(h, w_ref[:, 2 * j * LANES:2 * (j + 1) * LANES], preferred_element_type=F32)
        for half in range(2):
            slab = piece[:, half * LANES:(half + 1) * LANES]
            if j < n_pairs // 2:
                slab = head_norm(slab, qw_ref) * (ATTN_HEAD_DIM ** -0.5)
            elif j < n_pairs:
                slab = head_norm(slab, kw_ref)
            qkv[2 * j + half] = slab

    lane = lax.broadcasted_iota(jnp.int32, (blk, LANES), 1)
    lo_half = lane < ATTN_HEAD_DIM
    nt = (((1,), (1,)), ((), ()))

    def one_block(it, carry):
        sub, res = (it, 0) if dilation == 1 else (0, it)
        rows = pl.ds(sub * span + res, blk, stride=dilation)
        gblk = step * n_sub + sub
        slot = gblk % 2
        first_pen = jnp.where(gblk == 0, NEG_BIG, 0.0)
        q_raw = [qkv[p, rows, :] for p in range(n_pairs)]
        k_raw = [qkv[n_pairs + p, rows, :] for p in range(n_pairs)]
        v_raw = [qkv[2 * n_pairs + p, rows, :] for p in range(n_pairs)]
        k_old = [kbuf[1 - slot, res * n_pairs + p] for p in range(n_pairs)]
        v_old = [vbuf[1 - slot, res * n_pairs + p] for p in range(n_pairs)]
        pairs = range(n_pairs)
        k_new = [k_raw[p].astype(BF16) for p in pairs]
        v_new = [v_raw[p].astype(BF16) for p in pairs]
        q2 = []
        for p in pairs:
            qp = q_raw[p]
            q2.append(jnp.concatenate([jnp.where(lo_half, qp, 0.0), jnp.where(lo_half, 0.0, qp)],
                                      axis=0).astype(BF16))
        s_prev = [lax.dot_general(q2[p], k_old[p], nt, preferred_element_type=F32) for p in pairs]
        s_cur = [lax.dot_general(q2[p], k_new[p], nt, preferred_element_type=F32) for p in pairs]
        e_prev, e_cur, m, d = [], [], [], []
        for p in pairs:
            sp = s_prev[p] + (bias_ref[p, :, 0:blk] + first_pen)
            sc = s_cur[p] + bias_ref[p, :, blk:2 * blk]
            mp = jnp.max(jnp.maximum(sp, sc), axis=-1, keepdims=True)
            ep = jnp.exp(sp - mp)
            ec = jnp.exp(sc - mp)
            m.append(mp)
            d.append(jnp.sum(ep + ec, axis=-1, keepdims=True))
            e_prev.append(ep.astype(BF16))
            e_cur.append(ec.astype(BF16))
        pv_prev = [jnp.dot(e_prev[p], v_old[p], preferred_element_type=F32) for p in pairs]
        pv_cur = [jnp.dot(e_cur[p], v_new[p], preferred_element_type=F32) for p in pairs]
        lse_tile = jnp.zeros((blk, LANES), F32)
        o_new = []
        for p in pairs:
            pv = (pv_prev[p] + pv_cur[p]) / d[p]
            o_new.append(jnp.where(lo_half, pv[0:blk], pv[blk:2 * blk]))
            lse = m[p] + jnp.log(d[p])
            lse_tile = jnp.where(lane == 2 * p, lse[0:blk], lse_tile)
            lse_tile = jnp.where(lane == 2 * p + 1, lse[blk:2 * blk], lse_tile)
        for p in pairs:
            kbuf[slot, res * n_pairs + p] = k_new[p]
            vbuf[slot, res * n_pairs + p] = v_new[p]
            o_refs[p][0, rows, :] = o_new[p]
        lse_ref[0, rows, :] = lse_tile
        return carry

    lax.fori_loop(0, n_sub * dilation, one_block, 0)


def _attn_group(h3, w_qkv, dilation, bias, qw, kw):
    b, s, d = h3.shape
    blk = ATTN_BLK
    n_pairs = HEADS_PER_GROUP // 2
    rt = max(4 * blk, blk * dilation)
    rt = min(rt, s)

    def const(shape):
        return pl.BlockSpec(shape, lambda i, n: (0,) * len(shape))

    token_spec = pl.BlockSpec((1, rt, LANES), lambda i, n: (i, n, 0))
    res = pl.pallas_call(
        functools.partial(_attn_body, dilation=dilation),
        grid=(b, s // rt),
        in_specs=[pl.BlockSpec((1, rt, d), lambda i, n: (i, n, 0)),
                  const((d, 3 * GROUP_WIDTH)),
                  const((n_pairs, 2 * blk, 2 * blk)), const((1, LANES)), const((1, LANES))],
        out_specs=[token_spec] * (n_pairs + 1),
        out_shape=[jax.ShapeDtypeStruct((b, s, LANES), F32)] * (n_pairs + 1),
        scratch_shapes=[pltpu.VMEM((3 * n_pairs, rt, LANES), F32),
                        pltpu.VMEM((2, dilation * n_pairs, blk, LANES), BF16),
                        pltpu.VMEM((2, dilation * n_pairs, blk, LANES), BF16)],
        compiler_params=_cparams(2),
    )(h3, w_qkv, bias, qw, kw)
    outs = [o.reshape(b * s, LANES) for o in res[:n_pairs]]
    return outs, res[n_pairs].reshape(b * s, LANES)


def _mix_body(*refs):
    n_pairs = HEADS_PER_GROUP // 2
    y_ref = refs[0]
    o_refs = refs[1:1 + 3 * n_pairs]
    l_refs = refs[1 + 3 * n_pairs:4 + 3 * n_pairs]
    (h_ref, wgate_ref, x_ref, wssm_ref, wattn_ref, wout_ref, e8_ref, nfw_ref, wr_ref, br_ref,
     x2_ref, h2_ref, lg_ref) = refs[4 + 3 * n_pairs:]
    tm = x_ref.shape[0]
    lses = [l[...] for l in l_refs]
    mx = jnp.maximum(jnp.maximum(lses[0], lses[1]), lses[2])
    es = [jnp.exp(l - mx) for l in lses]
    inv = 1.0 / (es[0] + es[1] + es[2])
    e8 = e8_ref[...]
    att = jnp.zeros((tm, GROUP_WIDTH), F32)
    for g in range(3):
        w = es[g] * inv
        w_hi = w.astype(BF16)
        w_lo = (w - w_hi.astype(F32)).astype(BF16)
        w_full = (jnp.dot(w_hi, e8, preferred_element_type=F32)
                  + jnp.dot(w_lo, e8, preferred_element_type=F32))
        o_g = jnp.concatenate([o_refs[g * n_pairs + p][...] for p in range(n_pairs)], axis=1)
        att = att + w_full * o_g
    y_attn = jnp.dot(att.astype(BF16), wattn_ref[...], preferred_element_type=F32)
    y_ssm = jnp.dot(y_ref[...], wssm_ref[...], preferred_element_type=F32)
    h = h_ref[...]
    g_ssm = jnp.dot(h, wgate_ref[:, 0:D_MODEL], preferred_element_type=F32)
    g_attn = jnp.dot(h, wgate_ref[:, D_MODEL:2 * D_MODEL], preferred_element_type=F32)
    merged = _sigmoid(g_ssm) * y_ssm + _sigmoid(g_attn) * y_attn
    x2 = x_ref[...] + jnp.dot(merged.astype(BF16), wout_ref[...], preferred_element_type=F32)
    x2_ref[...] = x2
    ms = jnp.mean(x2 * x2, axis=-1, keepdims=True)
    h2 = x2 * lax.rsqrt(ms + NORM_EPS) * nfw_ref[...]
    h2_ref[...] = h2.reshape(tm, 1, D_MODEL)
    lg_ref[...] = jnp.dot(h2.astype(BF16), wr_ref[...], preferred_element_type=F32) + br_ref[...]


def _mix_out(y_ssm, attn_outs, attn_lses, h2d, w_gate, x2d, w_ssm, w_attn, w_out, norm_ffn_w, w_router, b_router):
    t = x2d.shape[0]
    tm = min(512, t)
    e8 = (np.arange(LANES)[:, None] == np.arange(GROUP_WIDTH)[None, :] // ATTN_HEAD_DIM)
    e8 = jnp.asarray(e8.astype(np.float32), dtype=BF16)

    def rows(width, cb=0):
        return pl.BlockSpec((tm, width), lambda i: (i, cb))

    def const(shape):
        return pl.BlockSpec(shape, lambda i: (0,) * len(shape), pipeline_mode=pl.Buffered(1))

    return pl.pallas_call(
        _mix_body,
        grid=(t // tm,),
        in_specs=[rows(D_INNER)] + [rows(LANES)] * (len(attn_outs) + len(attn_lses)) + [
                  rows(D_MODEL), const((D_MODEL, 2 * D_MODEL)), rows(D_MODEL),
                  const((D_INNER, D_MODEL)), const((GROUP_WIDTH, D_MODEL)), const((D_MODEL, D_MODEL)),
                  const((LANES, GROUP_WIDTH)), const((1, D_MODEL)),
                  const((D_MODEL, LANES)), const((1, LANES))],
        out_specs=[rows(D_MODEL),
                   pl.BlockSpec((tm, 1, D_MODEL), lambda i: (i, 0, 0)),
                   rows(LANES)],
        out_shape=[jax.ShapeDtypeStruct((t, D_MODEL), F32),
                   jax.ShapeDtypeStruct((t, 1, D_MODEL), F32),
                   jax.ShapeDtypeStruct((t, LANES), F32)],
        compiler_params=_cparams(1),
    )(y_ssm, *attn_outs, *attn_lses, h2d, w_gate, x2d, w_ssm, w_attn, w_out, e8,
      norm_ffn_w[None], w_router, b_router)


def _route_body(lg_ref, dest_ref, gate_ref, cnt_ref, counts, pstart):
    tm = lg_ref.shape[0]
    ps = pl.program_id(0)
    i = pl.program_id(1)

    @pl.when((ps == 0) & (i == 0))
    def _():
        counts[...] = jnp.zeros_like(counts)
        pstart[...] = jnp.zeros_like(pstart)

    @pl.when((ps == 1) & (i == 0))
    def _():
        cnt = counts[...]
        padded = jnp.floor((cnt + (MOE_BLOCK - 1)) * (1.0 / MOE_BLOCK)) * MOE_BLOCK
        a = lax.broadcasted_iota(jnp.int32, (LANES, LANES), 0)
        bcol = lax.broadcasted_iota(jnp.int32, (LANES, LANES), 1)
        pad_end = jnp.dot(padded, (a <= bcol).astype(F32), precision=HIGHEST, preferred_element_type=F32)
        pstart[...] = pad_end - padded
        cnt_ref[...] = cnt
        counts[...] = jnp.zeros_like(counts)

    lg = lg_ref[...]
    lane = lax.broadcasted_iota(jnp.int32, (tm, LANES), 1)
    is_coarse = lane < N_EXPERT_GROUPS
    cmax = jnp.max(jnp.where(is_coarse, lg, NEG_BIG), axis=-1, keepdims=True)
    grp = jnp.min(jnp.where(is_coarse & (lg == cmax), lane, LANES), axis=-1, keepdims=True)
    group_p = 1.0 / jnp.sum(jnp.where(is_coarse, jnp.exp(lg - cmax), 0.0), axis=-1, keepdims=True)
    f_lo = N_EXPERT_GROUPS + EXPERTS_PER_GROUP * grp
    in_grp = (lane >= f_lo) & (lane < f_lo + EXPERTS_PER_GROUP)
    f1 = jnp.max(jnp.where(in_grp, lg, NEG_BIG), axis=-1, keepdims=True)
    i1 = jnp.min(jnp.where(in_grp & (lg == f1), lane, LANES), axis=-1, keepdims=True)
    rest = in_grp & (lane != i1)
    f2 = jnp.max(jnp.where(rest, lg, NEG_BIG), axis=-1, keepdims=True)
    i2 = jnp.min(jnp.where(rest & (lg == f2), lane, LANES), axis=-1, keepdims=True)
    e2 = jnp.exp(f2 - f1)
    g1 = group_p / (1.0 + e2)
    g2 = group_p * e2 / (1.0 + e2)

    oh1 = lane == i1 - N_EXPERT_GROUPS
    oh2 = lane == i2 - N_EXPERT_GROUPS
    onehot = jnp.where(oh1 | oh2, 1.0, 0.0)
    r = lax.broadcasted_iota(jnp.int32, (tm, tm), 0)
    c = lax.broadcasted_iota(jnp.int32, (tm, tm), 1)
    before = jnp.dot((r > c).astype(BF16), onehot.astype(BF16), preferred_element_type=F32)
    pos = pstart[0:1, :] + counts[0:1, :] + before
    d1 = jnp.sum(jnp.where(oh1, pos, 0.0), axis=-1, keepdims=True)
    d2 = jnp.sum(jnp.where(oh2, pos, 0.0), axis=-1, keepdims=True)
    counts[...] = counts[...] + jnp.sum(onehot, axis=0, keepdims=True)
    dest_ref[...] = jnp.where(lane == 0, d1, jnp.where(lane == 1, d2, 0.0)).astype(jnp.int32)
    gate_ref[...] = jnp.where(lane == 0, g1, jnp.where(lane == 1, g2, 0.0))


def _route(logits):
    t = logits.shape[0]
    tm = min(512, t)
    return pl.pallas_call(
        _route_body,
        grid=(2, t // tm),
        in_specs=[pl.BlockSpec((tm, LANES), lambda ps, i: (i, 0))],
        out_specs=[pl.BlockSpec((tm, LANES), lambda ps, i: (i * ps, 0)),
                   pl.BlockSpec((tm, LANES), lambda ps, i: (i * ps, 0)),
                   pl.BlockSpec((8, LANES), lambda ps, i: (0, 0))],
        out_shape=[jax.ShapeDtypeStruct((t, LANES), jnp.int32),
                   jax.ShapeDtypeStruct((t, LANES), F32),
                   jax.ShapeDtypeStruct((8, LANES), F32)],
        scratch_shapes=[pltpu.VMEM((8, LANES), F32), pltpu.VMEM((8, LANES), F32)],
        compiler_params=_cparams(2),
    )(logits)


def _dispatch_body(meta_ref, dest_ref, h_ref, rows_ref, zbuf, zsem, sem):
    tm = h_ref.shape[0]

    def zero_copy(e):
        start = pl.multiple_of(meta_ref[e] - MOE_BLOCK, MOE_BLOCK)
        return pltpu.make_async_copy(zbuf, rows_ref.at[pl.ds(start, MOE_BLOCK)], zsem)

    def tail_copy(blk):
        start = pl.multiple_of(blk * MOE_BLOCK, MOE_BLOCK)
        return pltpu.make_async_copy(zbuf, rows_ref.at[pl.ds(start, MOE_BLOCK)], zsem)

    @pl.when(pl.program_id(0) == 0)
    def _():
        zbuf[...] = jnp.zeros_like(zbuf)

        def start_zero(e, carry):
            @pl.when(meta_ref[N_EXPERTS + e] > 0)
            def _():
                zero_copy(e).start()
            return carry

        def wait_zero(e, carry):
            @pl.when(meta_ref[N_EXPERTS + e] > 0)
            def _():
                zero_copy(e).wait()
            return carry

        def start_tail(blk, carry):
            tail_copy(blk).start()
            return carry

        def wait_tail(blk, carry):
            tail_copy(blk).wait()
            return carry

        n_used = meta_ref[2 * N_EXPERTS]
        n_blocks = rows_ref.shape[0] // MOE_BLOCK
        lax.fori_loop(0, N_EXPERTS, start_zero, 0)
        lax.fori_loop(n_used, n_blocks, start_tail, 0)
        lax.fori_loop(0, N_EXPERTS, wait_zero, 0)
        lax.fori_loop(n_used, n_blocks, wait_tail, 0)

    def issue(j, carry):
        for k in range(2):
            d = dest_ref[0, 2 * j + k]
            pltpu.make_async_copy(h_ref.at[j], rows_ref.at[d], sem).start(priority=k)
        return carry

    lax.fori_loop(0, tm, issue, 0, unroll=8)
    for _ in range(2):
        pltpu.make_async_copy(h_ref, rows_ref.at[pl.ds(0, tm)], sem).wait()


def _dispatch(meta, dest3, h2, n_rows):
    t = h2.shape[0]
    tm = dest3.shape[2] // 2
    return pl.pallas_call(
        _dispatch_body,
        grid_spec=pltpu.PrefetchScalarGridSpec(
            num_scalar_prefetch=1,
            grid=(t // tm,),
            in_specs=[pl.BlockSpec((None, 1, 2 * tm), lambda i, m: (i, 0, 0), memory_space=pltpu.SMEM),
                      pl.BlockSpec((tm, 1, D_MODEL), lambda i, m: (i, 0, 0))],
            out_specs=pl.BlockSpec(memory_space=pl.ANY),
            scratch_shapes=[pltpu.VMEM((MOE_BLOCK, 1, D_MODEL), F32),
                            pltpu.SemaphoreType.DMA(()),
                            pltpu.SemaphoreType.DMA(())]),
        out_shape=jax.ShapeDtypeStruct((n_rows, 1, D_MODEL), F32),
        compiler_params=_cparams(1),
    )(meta, dest3, h2)


def _expert_body(sched_ref, x_ref, wg_hbm, wu_hbm, wd_hbm, y_ref, x2d, wg_f, wu_f, wd_f, wg_b, wu_b, wd_b, sems):
    blk = pl.program_id(0)
    n_used = sched_ref[4, 0]

    def weight_copies(expert, slot):
        return [pltpu.make_async_copy(src.at[expert], dst.at[slot], sems.at[slot])
                for src, dst in ((wg_hbm, wg_f), (wu_hbm, wu_f), (wd_hbm, wd_f))]

    @pl.when(blk < n_used)
    def _():
        expert = sched_ref[0, blk]
        slot = sched_ref[2, blk]
        nxt = sched_ref[3, blk]

        @pl.when(sched_ref[1, blk] == 1)
        def _():
            @pl.when(blk == 0)
            def _():
                for c in weight_copies(expert, slot):
                    c.start()

            for c in weight_copies(expert, slot):
                c.wait()

            @pl.when(nxt >= 0)
            def _():
                for c in weight_copies(nxt, 1 - slot):
                    c.start()

            wg_b[...] = wg_f[slot].astype(BF16)
            wu_b[...] = wu_f[slot].astype(BF16)
            wd_b[...] = wd_f[slot].astype(BF16)

        x2d[...] = x_ref[...].reshape(MOE_BLOCK, D_MODEL)
        x = x2d[...].astype(BF16)
        g = jnp.dot(x, wg_b[...], preferred_element_type=F32)
        u = jnp.dot(x, wu_b[...], preferred_element_type=F32)
        hid = (g * _sigmoid(g)) * u
        y = jnp.dot(hid.astype(BF16), wd_b[...], preferred_element_type=F32)
        y_ref[...] = y.reshape(MOE_BLOCK, 1, D_MODEL)

    @pl.when(blk >= n_used)
    def _():
        y_ref[...] = jnp.zeros_like(y_ref)


def _experts(sched, rows, w_gate, w_up, w_down):
    n_rows = rows.shape[0]
    n_blocks = n_rows // MOE_BLOCK

    def row_map(b, sched):
        return (b, 0, 0)

    any_spec = pl.BlockSpec(memory_space=pl.ANY)
    return pl.pallas_call(
        _expert_body,
        grid_spec=pltpu.PrefetchScalarGridSpec(
            num_scalar_prefetch=1,
            grid=(n_blocks,),
            in_specs=[pl.BlockSpec((MOE_BLOCK, 1, D_MODEL), row_map), any_spec, any_spec, any_spec],
            out_specs=pl.BlockSpec((MOE_BLOCK, 1, D_MODEL), row_map),
            scratch_shapes=[pltpu.VMEM((MOE_BLOCK, D_MODEL), F32),
                            pltpu.VMEM((2, D_MODEL, D_EXPERT), F32),
                            pltpu.VMEM((2, D_MODEL, D_EXPERT), F32),
                            pltpu.VMEM((2, D_EXPERT, D_MODEL), F32),
                            pltpu.VMEM((D_MODEL, D_EXPERT), BF16),
                            pltpu.VMEM((D_MODEL, D_EXPERT), BF16),
                            pltpu.VMEM((D_EXPERT, D_MODEL), BF16),
                            pltpu.SemaphoreType.DMA((2,))]),
        out_shape=jax.ShapeDtypeStruct((n_rows, 1, D_MODEL), F32),
        compiler_params=_cparams(1),
    )(sched, rows, w_gate, w_up, w_down)


def _combine_body(dcur_ref, dnext_ref, gate_ref, x2_ref, y_ref, o_ref, buf_a, buf_b, y2d, sem_a, sem_b):
    tm = x2_ref.shape[0]
    i = pl.program_id(0)
    n = pl.num_programs(0)

    def issue(dref, buf, sem):
        def body(j, carry):
            for k in range(2):
                d = dref[0, 2 * j + k]
                pltpu.make_async_copy(y_ref.at[d], buf.at[k * tm + j], sem).start(priority=k)
            return carry
        lax.fori_loop(0, tm, body, 0, unroll=8)

    def finish(buf, sem):
        pltpu.make_async_copy(y_ref.at[pl.ds(0, 2 * tm)], buf, sem).wait()
        y2d[...] = buf[...].reshape(2 * tm, D_MODEL)
        g = gate_ref[...]
        o_ref[...] = x2_ref[...] + g[:, 0:1] * y2d[0:tm, :] + g[:, 1:2] * y2d[tm:2 * tm, :]

    @pl.when(i == 0)
    def _():
        issue(dcur_ref, buf_a, sem_a)

    for par, (cur, cur_sem, nxt, nxt_sem) in enumerate(((buf_a, sem_a, buf_b, sem_b),
                                                        (buf_b, sem_b, buf_a, sem_a))):
        @pl.when(i % 2 == par)
        def _(cur=cur, cur_sem=cur_sem, nxt=nxt, nxt_sem=nxt_sem):
            @pl.when(i + 1 < n)
            def _():
                issue(dnext_ref, nxt, nxt_sem)
            finish(cur, cur_sem)


def _combine(dest3, gates, x2, y_rows):
    t = x2.shape[0]
    tm = dest3.shape[2] // 2
    nt = t // tm
    return pl.pallas_call(
        _combine_body,
        grid=(nt,),
        in_specs=[pl.BlockSpec((None, 1, 2 * tm), lambda i: (i, 0, 0), memory_space=pltpu.SMEM),
                  pl.BlockSpec((None, 1, 2 * tm), lambda i: (jnp.minimum(i + 1, nt - 1), 0, 0),
                               memory_space=pltpu.SMEM),
                  pl.BlockSpec((tm, LANES), lambda i: (i, 0)),
                  pl.BlockSpec((tm, D_MODEL), lambda i: (i, 0)),
                  pl.BlockSpec(memory_space=pl.ANY)],
        out_specs=pl.BlockSpec((tm, D_MODEL), lambda i: (i, 0)),
        out_shape=jax.ShapeDtypeStruct((t, D_MODEL), F32),
        scratch_shapes=[pltpu.VMEM((2 * tm, 1, D_MODEL), F32),
                        pltpu.VMEM((2 * tm, 1, D_MODEL), F32),
                        pltpu.VMEM((2 * tm, D_MODEL), F32),
                        pltpu.SemaphoreType.DMA(()),
                        pltpu.SemaphoreType.DMA(())],
        compiler_params=_cparams(1),
    )(dest3, dest3, gates, x2, y_rows)


def _layer(x, norm_mix_w, w_in, conv_w, conv_b, dt_bias, a_log, d_skip, ssm_norm_w, w_ssm_proj,
           q_norm_w, k_norm_w, rel_bias, w_attn_proj, w_out, norm_ffn_w, w_coarse, b_coarse,
           w_fine, b_fine, w_gate_exp, w_up_exp, w_down_exp):
    b, s, d = x.shape
    t = b * s
    x2d = x.reshape(t, d)

    dt_lo = 2 * D_INNER + BC_WIDTH
    qkv_lo = dt_lo + SSM_HEADS
    gate_lo = qkv_lo + 3 * ATTN_WIDTH
    w_ssd = jnp.concatenate([w_in[:, :qkv_lo], jnp.zeros((d, LANES - SSM_HEADS), w_in.dtype)],
                            axis=1).astype(BF16)
    w_gate = w_in[:, gate_lo:].astype(BF16)
    h2d = _prenorm(x2d, norm_mix_w[None])
    h3 = h2d.reshape(b, s, d)

    y_ssm = _ssd(h3, w_ssd, conv_w, conv_b, dt_bias, a_log, d_skip, ssm_norm_w).reshape(t, D_INNER)

    qw = jnp.tile(q_norm_w, 2)[None]
    kw = jnp.tile(k_norm_w, 2)[None]
    attn_outs, attn_lses = [], []
    for gi, (window, dilation) in enumerate(DILATED_CONFIGS):
        assert window // dilation == ATTN_BLK and s % window == 0
        hs = slice(gi * GROUP_WIDTH, (gi + 1) * GROUP_WIDTH)
        w_qkv = jnp.concatenate([w_in[:, qkv_lo + j * ATTN_WIDTH:qkv_lo + (j + 1) * ATTN_WIDTH][:, hs]
                                 for j in range(3)], axis=1).astype(BF16)
        bias = _band_bias(rel_bias[:, gi * HEADS_PER_GROUP:(gi + 1) * HEADS_PER_GROUP], dilation)
        o, l = _attn_group(h3, w_qkv, dilation, bias, qw, kw)
        attn_outs.extend(o)
        attn_lses.append(l)

    n_route = N_EXPERT_GROUPS + N_EXPERTS
    w_router = jnp.pad(jnp.concatenate([w_coarse, w_fine], axis=1), ((0, 0), (0, LANES - n_route)))
    b_router = jnp.pad(jnp.concatenate([b_coarse, b_fine]), (0, LANES - n_route))[None]
    x2, h2, logits = _mix_out(y_ssm, attn_outs, attn_lses, h2d, w_gate, x2d, w_ssm_proj.astype(BF16),
                              w_attn_proj.astype(BF16), w_out.astype(BF16), norm_ffn_w,
                              w_router.astype(BF16), b_router)

    dest, gates, counts = _route(logits)

    cnt = counts[0, :N_EXPERTS].astype(jnp.int32)
    padded = (cnt + MOE_BLOCK - 1) // MOE_BLOCK * MOE_BLOCK
    pad_end = jnp.cumsum(padded)
    n_blocks = -(-(2 * t + N_EXPERTS * (MOE_BLOCK - 1)) // MOE_BLOCK)
    block_start = jnp.arange(n_blocks, dtype=jnp.int32) * MOE_BLOCK
    block_expert = jnp.minimum(jnp.sum((pad_end[None, :] <= block_start[:, None]).astype(jnp.int32), axis=1),
                               N_EXPERTS - 1)
    n_used = (pad_end[-1:] // MOE_BLOCK).astype(jnp.int32)
    meta = jnp.concatenate([pad_end, padded, n_used]).astype(jnp.int32)
    blk_ids = jnp.arange(n_blocks, dtype=jnp.int32)
    first = ((blk_ids == 0) | (block_expert != jnp.roll(block_expert, 1))) & (blk_ids < n_used[0])
    slot = (jnp.cumsum(first.astype(jnp.int32)) - 1) % 2
    nxt_blk = blk_ids + padded[block_expert] // MOE_BLOCK
    nxt = jnp.where(nxt_blk < n_used[0], block_expert[jnp.minimum(nxt_blk, n_blocks - 1)], -1)
    sched = jnp.stack([block_expert, first.astype(jnp.int32), slot, nxt,
                       jnp.broadcast_to(n_used, (n_blocks,))]).astype(jnp.int32)

    tm_d = min(256, t)
    dest_d = dest[:, :2].reshape(t // tm_d, 1, 2 * tm_d)
    rows = _dispatch(meta, dest_d, h2, n_blocks * MOE_BLOCK)
    y_rows = _experts(sched, rows, w_gate_exp, w_up_exp, w_down_exp)
    tm_c = min(128, t)
    dest_c = dest[:, :2].reshape(t // tm_c, 1, 2 * tm_c)
    out = _combine(dest_c, gates, x2, y_rows)
    return out.reshape(b, s, d)


def kernel(x, norm_mix_w, w_in, conv_w, conv_b, dt_bias, a_log, d_skip, ssm_norm_w, w_ssm_proj,
           q_norm_w, k_norm_w, rel_bias, w_attn_proj, w_out, norm_ffn_w, w_coarse, b_coarse,
           w_fine, b_fine, w_gate_exp, w_up_exp, w_down_exp):
    depth = norm_mix_w.shape[0]
    for layer in range(depth):
        x = _layer(x, norm_mix_w[layer], w_in[layer], conv_w[layer], conv_b[layer], dt_bias[layer],
                   a_log[layer], d_skip[layer], ssm_norm_w[layer], w_ssm_proj[layer],
                   q_norm_w[layer], k_norm_w[layer], rel_bias, w_attn_proj[layer], w_out[layer],
                   norm_ffn_w[layer], w_coarse[layer], b_coarse[layer], w_fine[layer], b_fine[layer],
                   w_gate_exp[layer], w_up_exp[layer], w_down_exp[layer])
    return x
```

```python
import functools
import math

import jax
import jax.numpy as jnp
import numpy as np
from jax import lax
from jax.experimental import pallas as pl
from jax.experimental.pallas import tpu as pltpu

F32 = jnp.float32
BF16 = jnp.bfloat16
HIGHEST = lax.Precision.HIGHEST

LANES = 128
NORM_EPS = 1e-6
NEG_BIG = -1e30
LOG2_E = math.log2(math.e)

D_MODEL = 1024
D_INNER = 2048
SSM_HEAD_DIM = 64
SSM_HEADS = 32
SSM_GROUPS = 2
D_STATE = 128
CONV_K = 4
BC_WIDTH = 2 * SSM_GROUPS * D_STATE
SSD_CHUNK = 128
ATTN_HEAD_DIM = 64
DILATED_CONFIGS = ((128, 1), (512, 4), (2048, 16))
HEADS_PER_GROUP = 8
GROUP_WIDTH = HEADS_PER_GROUP * ATTN_HEAD_DIM
ATTN_WIDTH = 3 * GROUP_WIDTH
ATTN_BLK = 128
NUM_BUCKETS = 32
MAX_DISTANCE = 2048
N_EXPERT_GROUPS = 8
EXPERTS_PER_GROUP = 8
N_EXPERTS = 64
D_EXPERT = 512
MOE_BLOCK = 256

SSD_PROJ_WIDTH = 2 * D_INNER + BC_WIDTH + LANES

VMEM_LIMIT = 56 * 1024 * 1024


def _sigmoid(x):
    return 1.0 / (1.0 + jnp.exp(-x))


def _cparams(n_axes):
    return pltpu.CompilerParams(dimension_semantics=("arbitrary",) * n_axes,
                                vmem_limit_bytes=VMEM_LIMIT)


def _prenorm_body(x_ref, nw_ref, h_ref):
    x = x_ref[...]
    ms = jnp.mean(x * x, axis=-1, keepdims=True)
    h_ref[...] = (x * lax.rsqrt(ms + NORM_EPS) * nw_ref[...]).astype(BF16)


def _prenorm(x2d, norm_w):
    t, d = x2d.shape
    tm = min(1024, t)
    return pl.pallas_call(
        _prenorm_body,
        grid=(t // tm,),
        in_specs=[pl.BlockSpec((tm, d), lambda i: (i, 0)), pl.BlockSpec((1, d), lambda i: (0, 0))],
        out_specs=pl.BlockSpec((tm, d), lambda i: (i, 0)),
        out_shape=jax.ShapeDtypeStruct((t, d), BF16),
        compiler_params=_cparams(1),
    )(x2d, norm_w)


def _dot3(x, y, x_is_exact):
    v = y if x_is_exact else x
    hi = v.astype(BF16)
    r1 = v - hi.astype(F32)
    mid = r1.astype(BF16)
    lo = (r1 - mid.astype(F32)).astype(BF16)
    if x_is_exact:
        return sum(jnp.dot(x, part, preferred_element_type=F32) for part in (hi, mid, lo))
    return sum(jnp.dot(part, y, preferred_element_type=F32) for part in (hi, mid, lo))


def _ssd_chunk(z, xbuf, r0, dt_raw, cw_ref, cb_ref, dtb_ref, alog_ref, dskip_ref, nw_ref, e_ref, state):
    L = SSD_CHUNK
    half = D_INNER // SSM_GROUPS
    hg = SSM_HEADS // SSM_GROUPS

    conv = cb_ref[...] + cw_ref[CONV_K - 1:CONV_K, :] * xbuf[8 + r0:8 + r0 + L, :]
    for k in range(CONV_K - 1):
        lo = 8 + r0 - (CONV_K - 1 - k)
        conv = conv + cw_ref[k:k + 1, :] * xbuf[lo:lo + L, :]
    xbc = conv * _sigmoid(conv)
    xs = xbc[:, :D_INNER]

    lane = lax.broadcasted_iota(jnp.int32, (L, LANES), 1)
    row = lax.broadcasted_iota(jnp.int32, (L, L), 0)
    col = lax.broadcasted_iota(jnp.int32, (L, L), 1)
    causal = row >= col

    v = dt_raw + dtb_ref[...]
    dt = jnp.maximum(v, 0.0) + jnp.log1p(jnp.exp(-jnp.abs(v)))
    dt = jnp.where(lane < SSM_HEADS, dt, 0.0)
    adt = dt * (-jnp.exp(alog_ref[...]))
    a_cs = _dot3(causal.astype(BF16), adt, True) * LOG2_E
    a_cs_t = a_cs.T
    expand = e_ref[...]
    a_full = _dot3(a_cs, expand, False)
    dt_full = _dot3(dt, expand, False)
    a_tot = a_full[L - 1:L, :]
    decay_from_start = jnp.exp2(a_full)
    decay_to_end = jnp.exp2(a_tot - a_full)
    decay_chunk = jnp.exp2(a_tot)

    xdt = xs * dt_full
    xw_b = (xdt * decay_to_end).astype(BF16)
    head_lo = lax.broadcasted_iota(jnp.int32, (L, D_INNER), 1) % (2 * SSM_HEAD_DIM) < SSM_HEAD_DIM
    xdt_lo = jnp.where(head_lo, xdt, 0.0).astype(BF16)
    xdt_hi = jnp.where(head_lo, 0.0, xdt).astype(BF16)

    b16, c16, cbs, y_offs, s_prevs = [], [], [], [], []
    for g in range(SSM_GROUPS):
        bg = xbc[:, D_INNER + g * D_STATE:D_INNER + (g + 1) * D_STATE]
        cg = xbc[:, D_INNER + (SSM_GROUPS + g) * D_STATE:D_INNER + (SSM_GROUPS + g + 1) * D_STATE]
        b16.append(bg)
        c16.append(cg.astype(BF16))
        cbs.append(lax.dot_general(c16[g], bg.astype(BF16), (((1,), (1,)), ((), ())),
                                   preferred_element_type=F32))
    for g in range(SSM_GROUPS):
        s_prevs.append(state[:, g * half:(g + 1) * half])
        y_offs.append(jnp.dot(c16[g], s_prevs[g].astype(BF16), preferred_element_type=F32))
    y_cols = []
    for g in range(SSM_GROUPS):
        for pr in range(half // LANES):
            h0 = g * hg + 2 * pr
            ps = slice(h0 * SSM_HEAD_DIM, (h0 + 2) * SSM_HEAD_DIM)
            ms = []
            for h in (h0, h0 + 1):
                seg = a_cs[:, h:h + 1] - a_cs_t[h:h + 1, :]
                ms.append((cbs[g] * jnp.exp2(jnp.where(causal, seg, NEG_BIG))).astype(BF16))
            y_cols.append(jnp.dot(jnp.concatenate(ms, axis=1),
                                  jnp.concatenate([xdt_lo[:, ps], xdt_hi[:, ps]], axis=0),
                                  preferred_element_type=F32))
    for g in range(SSM_GROUPS):
        gs = slice(g * half, (g + 1) * half)
        state[:, gs] = decay_chunk[:, gs] * s_prevs[g] + jnp.dot(
            b16[g].T.astype(BF16), xw_b[:, gs], preferred_element_type=F32)

    y = (jnp.concatenate(y_cols, axis=1) + jnp.concatenate(y_offs, axis=1) * decay_from_start
         + dskip_ref[...] * xs)
    y = y * (z * _sigmoid(z))
    normed = []
    for g in range(SSM_GROUPS):
        yg = y[:, g * half:(g + 1) * half]
        ms = jnp.mean(yg * yg, axis=-1, keepdims=True)
        normed.append(yg * lax.rsqrt(ms + NORM_EPS))
    return jnp.concatenate(normed, axis=1) * nw_ref[...]


def _ssd_body(h_ref, w_ref, cw_ref, cb_ref, dtb_ref, alog_ref, dskip_ref, nw_ref, e_ref,
              y_ref, state, xbuf, zbuf, dtbuf):
    L = SSD_CHUNK
    rt = h_ref.shape[1]
    conv_dim = D_INNER + BC_WIDTH

    @pl.when(pl.program_id(1) == 0)
    def _():
        state[...] = jnp.zeros_like(state)
        xbuf[0:8, :] = jnp.zeros((8, conv_dim), F32)

    h = h_ref[0]
    zbuf[...] = jnp.dot(h, w_ref[:, 0:D_INNER], preferred_element_type=F32)
    xbuf[8:8 + rt, :] = jnp.dot(h, w_ref[:, D_INNER:D_INNER + conv_dim], preferred_element_type=F32)
    dtbuf[...] = jnp.dot(h, w_ref[:, D_INNER + conv_dim:], preferred_element_type=F32)

    for c in range(rt // L):
        r0 = c * L
        y = _ssd_chunk(zbuf[r0:r0 + L, :], xbuf, r0, dtbuf[r0:r0 + L, :], cw_ref, cb_ref,
                       dtb_ref, alog_ref, dskip_ref, nw_ref, e_ref, state)
        y_ref[0, r0:r0 + L, :] = y.astype(BF16)
    xbuf[0:8, :] = xbuf[rt:rt + 8, :]


def _ssd(h3, w_ssd, conv_w, conv_b, dt_bias, a_log, d_skip, ssm_norm_w):
    b, s, d = h3.shape
    rt = min(4 * SSD_CHUNK, s)
    pad = LANES - SSM_HEADS
    conv_dim = D_INNER + BC_WIDTH
    dtb = jnp.pad(dt_bias, (0, pad))[None]
    alog = jnp.pad(a_log, (0, pad))[None]
    dskip = jnp.repeat(d_skip, SSM_HEAD_DIM)[None]
    expand = (np.arange(LANES)[:, None] == np.arange(D_INNER)[None, :] // SSM_HEAD_DIM).astype(np.float32)

    def const(shape):
        return pl.BlockSpec(shape, lambda i, c: (0,) * len(shape))

    return pl.pallas_call(
        _ssd_body,
        grid=(b, s // rt),
        in_specs=[pl.BlockSpec((1, rt, d), lambda i, c: (i, c, 0)),
                  const((d, SSD_PROJ_WIDTH)),
                  const((CONV_K, conv_dim)), const((1, conv_dim)),
                  const((1, LANES)), const((1, LANES)),
                  const((1, D_INNER)), const((1, D_INNER)),
                  const((LANES, D_INNER))],
        out_specs=pl.BlockSpec((1, rt, D_INNER), lambda i, c: (i, c, 0)),
        out_shape=jax.ShapeDtypeStruct((b, s, D_INNER), BF16),
        scratch_shapes=[pltpu.VMEM((D_STATE, D_INNER), F32),
                        pltpu.VMEM((rt + 8, conv_dim), F32),
                        pltpu.VMEM((rt, D_INNER), F32),
                        pltpu.VMEM((rt, LANES), F32)],
        compiler_params=_cparams(2),
    )(h3, w_ssd, conv_w, conv_b[None], dtb, alog, dskip, ssm_norm_w[None],
      jnp.asarray(expand, dtype=BF16))


def _t5_causal_bucket(dist):
    max_exact = NUM_BUCKETS // 2
    large = max_exact + (np.log(np.maximum(dist, max_exact) / max_exact)
                         / math.log(MAX_DISTANCE / max_exact) * (NUM_BUCKETS - max_exact)).astype(np.int32)
    return np.where(dist < max_exact, dist, np.minimum(large, NUM_BUCKETS - 1)).astype(np.int32)


def _band_bias(rel_bias_group, dilation):
    blk = ATTN_BLK
    off = np.arange(blk)[:, None] + blk - np.arange(2 * blk)[None, :]
    in_win = (off >= 0) & (off <= blk)
    bucket = _t5_causal_bucket(np.clip(off, 0, None) * dilation)
    onehot = (bucket.reshape(-1, 1) == np.arange(NUM_BUCKETS)[None, :]).astype(np.float32)
    bias = jnp.dot(jnp.asarray(onehot), rel_bias_group.astype(F32), precision=HIGHEST)
    bias = jnp.transpose(bias.reshape(blk, 2 * blk, HEADS_PER_GROUP), (2, 0, 1))
    bias = jnp.where(in_win[None], bias, NEG_BIG)
    return bias.reshape(HEADS_PER_GROUP // 2, 2 * blk, 2 * blk)


def _attn_body(h_ref, w_ref, bias_ref, qw_ref, kw_ref, *rest, dilation):
    n_pairs = HEADS_PER_GROUP // 2
    o_refs = rest[0:n_pairs]
    lse_ref, qkv, kbuf, vbuf = rest[n_pairs:]
    blk = ATTN_BLK
    rt = h_ref.shape[1]
    span = blk * dilation
    n_sub = rt // span
    assert n_sub == 1 or dilation == 1
    step = pl.program_id(1)

    @pl.when(step == 0)
    def _():
        kbuf[...] = jnp.zeros_like(kbuf)
        vbuf[...] = jnp.zeros_like(vbuf)

    first_head = lax.broadcasted_iota(jnp.int32, (rt, LANES), 1) < ATTN_HEAD_DIM

    def head_norm(x, w_ref):
        xx = x * x
        s0 = jnp.sum(jnp.where(first_head, xx, 0.0), axis=-1, keepdims=True)
        s1 = jnp.sum(jnp.where(first_head, 0.0, xx), axis=-1, keepdims=True)
        ss = jnp.where(first_head, s0, s1)
        return x * lax.rsqrt(ss * (1.0 / ATTN_HEAD_DIM) + NORM_EPS) * w_ref[...]

    h = h_ref[0]
    for j in range(3 * n_pairs // 2):
        piece = jnp.dot(h, w_ref[:, 2 * j * LANES:2 * (j + 1) * LANES], preferred_element_type=F32)
        for half in range(2):
            slab = piece[:, half * LANES:(half + 1) * LANES]
            if j < n_pairs // 2:
                slab = head_norm(slab, qw_ref) * (ATTN_HEAD_DIM ** -0.5)
            elif j < n_pairs:
                slab = head_norm(slab, kw_ref)
            qkv[2 * j + half] = slab

    lane = lax.broadcasted_iota(jnp.int32, (blk, LANES), 1)
    lo_half = lane < ATTN_HEAD_DIM
    nt = (((1,), (1,)), ((), ()))

    def one_block(it, carry):
        sub, res = (it, 0) if dilation == 1 else (0, it)
        rows = pl.ds(sub * span + res, blk, stride=dilation)
        gblk = step * n_sub + sub
        slot = gblk % 2
        first_pen = jnp.where(gblk == 0, NEG_BIG, 0.0)
        q_raw = [qkv[p, rows, :] for p in range(n_pairs)]
        k_raw = [qkv[n_pairs + p, rows, :] for p in range(n_pairs)]
        v_raw = [qkv[2 * n_pairs + p, rows, :] for p in range(n_pairs)]
        k_old = [kbuf[1 - slot, res * n_pairs + p] for p in range(n_pairs)]
        v_old = [vbuf[1 - slot, res * n_pairs + p] for p in range(n_pairs)]
        pairs = range(n_pairs)
        k_new = [k_raw[p].astype(BF16) for p in pairs]
        v_new = [v_raw[p].astype(BF16) for p in pairs]
        q2 = []
        for p in pairs:
            qp = q_raw[p]
            q2.append(jnp.concatenate([jnp.where(lo_half, qp, 0.0), jnp.where(lo_half, 0.0, qp)],
                                      axis=0).astype(BF16))
        s_prev = [lax.dot_general(q2[p], k_old[p], nt, preferred_element_type=F32) for p in pairs]
        s_cur = [lax.dot_general(q2[p], k_new[p], nt, preferred_element_type=F32) for p in pairs]
        e_prev, e_cur, m, d = [], [], [], []
        for p in pairs:
            sp = s_prev[p] + (bias_ref[p, :, 0:blk] + first_pen)
            sc = s_cur[p] + bias_ref[p, :, blk:2 * blk]
            mp = jnp.max(jnp.maximum(sp, sc), axis=-1, keepdims=True)
            ep = jnp.exp(sp - mp)
            ec = jnp.exp(sc - mp)
            m.append(mp)
            d.append(jnp.sum(ep + ec, axis=-1, keepdims=True))
            e_prev.append(ep.astype(BF16))
            e_cur.append(ec.astype(BF16))
        pv_prev = [jnp.dot(e_prev[p], v_old[p], preferred_element_type=F32) for p in pairs]
        pv_cur = [jnp.dot(e_cur[p], v_new[p], preferred_element_type=F32) for p in pairs]
        lse_tile = jnp.zeros((blk, LANES), F32)
        o_new = []
        for p in pairs:
            pv = (pv_prev[p] + pv_cur[p]) / d[p]
            o_new.append(jnp.where(lo_half, pv[0:blk], pv[blk:2 * blk]))
            lse = m[p] + jnp.log(d[p])
            lse_tile = jnp.where(lane == 2 * p, lse[0:blk], lse_tile)
            lse_tile = jnp.where(lane == 2 * p + 1, lse[blk:2 * blk], lse_tile)
        for p in pairs:
            kbuf[slot, res * n_pairs + p] = k_new[p]
            vbuf[slot, res * n_pairs + p] = v_new[p]
            o_refs[p][0, rows, :] = o_new[p]
        lse_ref[0, rows, :] = lse_tile
        return carry

    lax.fori_loop(0, n_sub * dilation, one_block, 0)


def _attn_group(h3, w_qkv, dilation, bias, qw, kw):
    b, s, d = h3.shape
    blk = ATTN_BLK
    n_pairs = HEADS_PER_GROUP // 2
    rt = max(4 * blk, blk * dilation)
    rt = min(rt, s)

    def const(shape):
        return pl.BlockSpec(shape, lambda i, n: (0,) * len(shape))

    token_spec = pl.BlockSpec((1, rt, LANES), lambda i, n: (i, n, 0))
    res = pl.pallas_call(
        functools.partial(_attn_body, dilation=dilation),
        grid=(b, s // rt),
        in_specs=[pl.BlockSpec((1, rt, d), lambda i, n: (i, n, 0)),
                  const((d, 3 * GROUP_WIDTH)),
                  const((n_pairs, 2 * blk, 2 * blk)), const((1, LANES)), const((1, LANES))],
        out_specs=[token_spec] * (n_pairs + 1),
        out_shape=[jax.ShapeDtypeStruct((b, s, LANES), F32)] * (n_pairs + 1),
        scratch_shapes=[pltpu.VMEM((3 * n_pairs, rt, LANES), F32),
                        pltpu.VMEM((2, dilation * n_pairs, blk, LANES), BF16),
                        pltpu.VMEM((2, dilation * n_pairs, blk, LANES), BF16)],
        compiler_params=_cparams(2),
    )(h3, w_qkv, bias, qw, kw)
    outs = [o.reshape(b * s, LANES) for o in res[:n_pairs]]
    return outs, res[n_pairs].reshape(b * s, LANES)


def _mix_body(*refs):
    n_pairs = HEADS_PER_GROUP // 2
    y_ref = refs[0]
    o_refs = refs[1:1 + 3 * n_pairs]
    l_refs = refs[1 + 3 * n_pairs:4 + 3 * n_pairs]
    (h_ref, wgate_ref, x_ref, wssm_ref, wattn_ref, wout_ref, e8_ref, nfw_ref, wr_ref, br_ref,
     x2_ref, h2_ref, lg_ref) = refs[4 + 3 * n_pairs:]
    tm = x_ref.shape[0]
    lses = [l[...] for l in l_refs]
    mx = jnp.maximum(jnp.maximum(lses[0], lses[1]), lses[2])
    es = [jnp.exp(l - mx) for l in lses]
    inv = 1.0 / (es[0] + es[1] + es[2])
    e8 = e8_ref[...]
    att = jnp.zeros((tm, GROUP_WIDTH), F32)
    for g in range(3):
        w = es[g] * inv
        w_hi = w.astype(BF16)
        w_lo = (w - w_hi.astype(F32)).astype(BF16)
        w_full = (jnp.dot(w_hi, e8, preferred_element_type=F32)
                  + jnp.dot(w_lo, e8, preferred_element_type=F32))
        o_g = jnp.concatenate([o_refs[g * n_pairs + p][...] for p in range(n_pairs)], axis=1)
        att = att + w_full * o_g
    y_attn = jnp.dot(att.astype(BF16), wattn_ref[...], preferred_element_type=F32)
    y_ssm = jnp.dot(y_ref[...], wssm_ref[...], preferred_element_type=F32)
    h = h_ref[...]
    g_ssm = jnp.dot(h, wgate_ref[:, 0:D_MODEL], preferred_element_type=F32)
    g_attn = jnp.dot(h, wgate_ref[:, D_MODEL:2 * D_MODEL], preferred_element_type=F32)
    merged = _sigmoid(g_ssm) * y_ssm + _sigmoid(g_attn) * y_attn
    x2 = x_ref[...] + jnp.dot(merged.astype(BF16), wout_ref[...], preferred_element_type=F32)
    x2_ref[...] = x2
    ms = jnp.mean(x2 * x2, axis=-1, keepdims=True)
    h2 = x2 * lax.rsqrt(ms + NORM_EPS) * nfw_ref[...]
    h2_ref[...] = h2.reshape(tm, 1, D_MODEL)
    lg_ref[...] = jnp.dot(h2.astype(BF16), wr_ref[...], preferred_element_type=F32) + br_ref[...]


def _mix_out(y_ssm, attn_outs, attn_lses, h2d, w_gate, x2d, w_ssm, w_attn, w_out, norm_ffn_w, w_router, b_router):
    t = x2d.shape[0]
    tm = min(512, t)
    e8 = (np.arange(LANES)[:, None] == np.arange(GROUP_WIDTH)[None, :] // ATTN_HEAD_DIM)
    e8 = jnp.asarray(e8.astype(np.float32), dtype=BF16)

    def rows(width, cb=0):
        return pl.BlockSpec((tm, width), lambda i: (i, cb))

    def const(shape):
        return pl.BlockSpec(shape, lambda i: (0,) * len(shape), pipeline_mode=pl.Buffered(1))

    return pl.pallas_call(
        _mix_body,
        grid=(t // tm,),
        in_specs=[rows(D_INNER)] + [rows(LANES)] * (len(attn_outs) + len(attn_lses)) + [
                  rows(D_MODEL), const((D_MODEL, 2 * D_MODEL)), rows(D_MODEL),
                  const((D_INNER, D_MODEL)), const((GROUP_WIDTH, D_MODEL)), const((D_MODEL, D_MODEL)),
                  const((LANES, GROUP_WIDTH)), const((1, D_MODEL)),
                  const((D_MODEL, LANES)), const((1, LANES))],
        out_specs=[rows(D_MODEL),
                   pl.BlockSpec((tm, 1, D_MODEL), lambda i: (i, 0, 0)),
                   rows(LANES)],
        out_shape=[jax.ShapeDtypeStruct((t, D_MODEL), F32),
                   jax.ShapeDtypeStruct((t, 1, D_MODEL), F32),
                   jax.ShapeDtypeStruct((t, LANES), F32)],
        compiler_params=_cparams(1),
    )(y_ssm, *attn_outs, *attn_lses, h2d, w_gate, x2d, w_ssm, w_attn, w_out, e8,
      norm_ffn_w[None], w_router, b_router)


def _route_body(lg_ref, sel_ref, gate_ref, cnt_ref, counts):
    tm = lg_ref.shape[0]
    i = pl.program_id(0)

    @pl.when(i == 0)
    def _():
        counts[...] = jnp.zeros_like(counts)

    lg = lg_ref[...]
    lane = lax.broadcasted_iota(jnp.int32, (tm, LANES), 1)
    is_coarse = lane < N_EXPERT_GROUPS
    cmax = jnp.max(jnp.where(is_coarse, lg, NEG_BIG), axis=-1, keepdims=True)
    grp = jnp.min(jnp.where(is_coarse & (lg == cmax), lane, LANES), axis=-1, keepdims=True)
    group_p = 1.0 / jnp.sum(jnp.where(is_coarse, jnp.exp(lg - cmax), 0.0), axis=-1, keepdims=True)
    f_lo = N_EXPERT_GROUPS + EXPERTS_PER_GROUP * grp
    in_grp = (lane >= f_lo) & (lane < f_lo + EXPERTS_PER_GROUP)
    f1 = jnp.max(jnp.where(in_grp, lg, NEG_BIG), axis=-1, keepdims=True)
    i1 = jnp.min(jnp.where(in_grp & (lg == f1), lane, LANES), axis=-1, keepdims=True)
    rest = in_grp & (lane != i1)
    f2 = jnp.max(jnp.where(rest, lg, NEG_BIG), axis=-1, keepdims=True)
    i2 = jnp.min(jnp.where(rest & (lg == f2), lane, LANES), axis=-1, keepdims=True)
    e2 = jnp.exp(f2 - f1)
    g1 = group_p / (1.0 + e2)
    g2 = group_p * e2 / (1.0 + e2)

    e1 = i1 - N_EXPERT_GROUPS
    e2i = i2 - N_EXPERT_GROUPS
    oh1 = lane == e1
    oh2 = lane == e2i
    onehot = jnp.where(oh1 | oh2, 1.0, 0.0)
    r = lax.broadcasted_iota(jnp.int32, (tm, tm), 0)
    c = lax.broadcasted_iota(jnp.int32, (tm, tm), 1)
    before = jnp.dot((r > c).astype(BF16), onehot.astype(BF16), preferred_element_type=F32)
    pos = counts[0:1, :] + before
    r1 = jnp.sum(jnp.where(oh1, pos, 0.0), axis=-1, keepdims=True).astype(jnp.int32)
    r2 = jnp.sum(jnp.where(oh2, pos, 0.0), axis=-1, keepdims=True).astype(jnp.int32)
    counts[...] = counts[...] + jnp.sum(onehot, axis=0, keepdims=True)
    sel_ref[...] = jnp.where(lane == 0, e1, jnp.where(lane == 1, e2i, jnp.where(lane == 2, r1,
                                                                                  jnp.where(lane == 3, r2, 0))))
    gate_ref[...] = jnp.where(lane == 0, g1, jnp.where(lane == 1, g2, 0.0))
    cnt_ref[...] = counts[...]


def _route(logits):
    t = logits.shape[0]
    tm = min(512, t)
    return pl.pallas_call(
        _route_body,
        grid=(t // tm,),
        in_specs=[pl.BlockSpec((tm, LANES), lambda i: (i, 0))],
        out_specs=[pl.BlockSpec((tm, LANES), lambda i: (i, 0)),
                   pl.BlockSpec((tm, LANES), lambda i: (i, 0)),
                   pl.BlockSpec((8, LANES), lambda i: (0, 0))],
        out_shape=[jax.ShapeDtypeStruct((t, LANES), jnp.int32),
                   jax.ShapeDtypeStruct((t, LANES), F32),
                   jax.ShapeDtypeStruct((8, LANES), F32)],
        scratch_shapes=[pltpu.VMEM((8, LANES), F32)],
        compiler_params=_cparams(1),
    )(logits)


def _dispatch_body(meta_ref, dest_ref, h_ref, rows_ref, zbuf, zsem, sem):
    tm = h_ref.shape[0]

    def zero_copy(e):
        start = pl.multiple_of(meta_ref[e] - MOE_BLOCK, MOE_BLOCK)
        return pltpu.make_async_copy(zbuf, rows_ref.at[pl.ds(start, MOE_BLOCK)], zsem)

    def tail_copy(blk):
        start = pl.multiple_of(blk * MOE_BLOCK, MOE_BLOCK)
        return pltpu.make_async_copy(zbuf, rows_ref.at[pl.ds(start, MOE_BLOCK)], zsem)

    @pl.when(pl.program_id(0) == 0)
    def _():
        zbuf[...] = jnp.zeros_like(zbuf)

        def start_zero(e, carry):
            @pl.when(meta_ref[N_EXPERTS + e] > 0)
            def _():
                zero_copy(e).start()
            return carry

        def wait_zero(e, carry):
            @pl.when(meta_ref[N_EXPERTS + e] > 0)
            def _():
                zero_copy(e).wait()
            return carry

        def start_tail(blk, carry):
            tail_copy(blk).start()
            return carry

        def wait_tail(blk, carry):
            tail_copy(blk).wait()
            return carry

        n_used = meta_ref[2 * N_EXPERTS]
        n_blocks = rows_ref.shape[0] // MOE_BLOCK
        lax.fori_loop(0, N_EXPERTS, start_zero, 0)
        lax.fori_loop(n_used, n_blocks, start_tail, 0)
        lax.fori_loop(0, N_EXPERTS, wait_zero, 0)
        lax.fori_loop(n_used, n_blocks, wait_tail, 0)

    def issue(j, carry):
        for k in range(2):
            d = dest_ref[0, 2 * j + k]
            pltpu.make_async_copy(h_ref.at[j], rows_ref.at[d], sem).start(priority=k)
        return carry

    lax.fori_loop(0, tm, issue, 0, unroll=8)
    for _ in range(2):
        pltpu.make_async_copy(h_ref, rows_ref.at[pl.ds(0, tm)], sem).wait()


def _dispatch(meta, dest3, h2, n_rows):
    t = h2.shape[0]
    tm = dest3.shape[2] // 2
    return pl.pallas_call(
        _dispatch_body,
        grid_spec=pltpu.PrefetchScalarGridSpec(
            num_scalar_prefetch=1,
            grid=(t // tm,),
            in_specs=[pl.BlockSpec((None, 1, 2 * tm), lambda i, m: (i, 0, 0), memory_space=pltpu.SMEM),
                      pl.BlockSpec((tm, 1, D_MODEL), lambda i, m: (i, 0, 0))],
            out_specs=pl.BlockSpec(memory_space=pl.ANY),
            scratch_shapes=[pltpu.VMEM((MOE_BLOCK, 1, D_MODEL), F32),
                            pltpu.SemaphoreType.DMA(()),
                            pltpu.SemaphoreType.DMA(())]),
        out_shape=jax.ShapeDtypeStruct((n_rows, 1, D_MODEL), F32),
        compiler_params=_cparams(1),
    )(meta, dest3, h2)


def _expert_body(sched_ref, x_ref, wg_hbm, wu_hbm, wd_hbm, y_ref, x2d, wg_f, wu_f, wd_f, wg_b, wu_b, wd_b, sems):
    blk = pl.program_id(0)
    n_used = sched_ref[4, 0]

    def weight_copies(expert, slot):
        return [pltpu.make_async_copy(src.at[expert], dst.at[slot], sems.at[slot])
                for src, dst in ((wg_hbm, wg_f), (wu_hbm, wu_f), (wd_hbm, wd_f))]

    @pl.when(blk < n_used)
    def _():
        expert = sched_ref[0, blk]
        slot = sched_ref[2, blk]
        nxt = sched_ref[3, blk]

        @pl.when(sched_ref[1, blk] == 1)
        def _():
            @pl.when(blk == 0)
            def _():
                for c in weight_copies(expert, slot):
                    c.start()

            for c in weight_copies(expert, slot):
                c.wait()

            @pl.when(nxt >= 0)
            def _():
                for c in weight_copies(nxt, 1 - slot):
                    c.start()

            wg_b[...] = wg_f[slot].astype(BF16)
            wu_b[...] = wu_f[slot].astype(BF16)
            wd_b[...] = wd_f[slot].astype(BF16)

        x2d[...] = x_ref[...].reshape(MOE_BLOCK, D_MODEL)
        x = x2d[...].astype(BF16)
        g = jnp.dot(x, wg_b[...], preferred_element_type=F32)
        u = jnp.dot(x, wu_b[...], preferred_element_type=F32)
        hid = (g * _sigmoid(g)) * u
        y = jnp.dot(hid.astype(BF16), wd_b[...], preferred_element_type=F32)
        y_ref[...] = y.reshape(MOE_BLOCK, 1, D_MODEL)

    @pl.when(blk >= n_used)
    def _():
        y_ref[...] = jnp.zeros_like(y_ref)


def _experts(sched, rows, w_gate, w_up, w_down):
    n_rows = rows.shape[0]
    n_blocks = n_rows // MOE_BLOCK

    def row_map(b, sched):
        return (b, 0, 0)

    any_spec = pl.BlockSpec(memory_space=pl.ANY)
    return pl.pallas_call(
        _expert_body,
        grid_spec=pltpu.PrefetchScalarGridSpec(
            num_scalar_prefetch=1,
            grid=(n_blocks,),
            in_specs=[pl.BlockSpec((MOE_BLOCK, 1, D_MODEL), row_map), any_spec, any_spec, any_spec],
            out_specs=pl.BlockSpec((MOE_BLOCK, 1, D_MODEL), row_map),
            scratch_shapes=[pltpu.VMEM((MOE_BLOCK, D_MODEL), F32),
                            pltpu.VMEM((2, D_MODEL, D_EXPERT), F32),
                            pltpu.VMEM((2, D_MODEL, D_EXPERT), F32),
                            pltpu.VMEM((2, D_EXPERT, D_MODEL), F32),
                            pltpu.VMEM((D_MODEL, D_EXPERT), BF16),
                            pltpu.VMEM((D_MODEL, D_EXPERT), BF16),
                            pltpu.VMEM((D_EXPERT, D_MODEL), BF16),
                            pltpu.SemaphoreType.DMA((2,))]),
        out_shape=jax.ShapeDtypeStruct((n_rows, 1, D_MODEL), F32),
        compiler_params=_cparams(1),
    )(sched, rows, w_gate, w_up, w_down)


def _combine_body(dcur_ref, dnext_ref, gate_ref, x2_ref, y_ref, o_ref, buf_a, buf_b, y2d, sem_a, sem_b):
    tm = x2_ref.shape[0]
    i = pl.program_id(0)
    n = pl.num_programs(0)

    def issue(dref, buf, sem):
        def body(j, carry):
            for k in range(2):
                d = dref[0, 2 * j + k]
                pltpu.make_async_copy(y_ref.at[d], buf.at[k * tm + j], sem).start(priority=k)
            return carry
        lax.fori_loop(0, tm, body, 0, unroll=8)

    def finish(buf, sem):
        pltpu.make_async_copy(y_ref.at[pl.ds(0, 2 * tm)], buf, sem).wait()
        y2d[...] = buf[...].reshape(2 * tm, D_MODEL)
        g = gate_ref[...]
        o_ref[...] = x2_ref[...] + g[:, 0:1] * y2d[0:tm, :] + g[:, 1:2] * y2d[tm:2 * tm, :]

    @pl.when(i == 0)
    def _():
        issue(dcur_ref, buf_a, sem_a)

    for par, (cur, cur_sem, nxt, nxt_sem) in enumerate(((buf_a, sem_a, buf_b, sem_b),
                                                        (buf_b, sem_b, buf_a, sem_a))):
        @pl.when(i % 2 == par)
        def _(cur=cur, cur_sem=cur_sem, nxt=nxt, nxt_sem=nxt_sem):
            @pl.when(i + 1 < n)
            def _():
                issue(dnext_ref, nxt, nxt_sem)
            finish(cur, cur_sem)


def _combine(dest3, gates, x2, y_rows):
    t = x2.shape[0]
    tm = dest3.shape[2] // 2
    nt = t // tm
    return pl.pallas_call(
        _combine_body,
        grid=(nt,),
        in_specs=[pl.BlockSpec((None, 1, 2 * tm), lambda i: (i, 0, 0), memory_space=pltpu.SMEM),
                  pl.BlockSpec((None, 1, 2 * tm), lambda i: (jnp.minimum(i + 1, nt - 1), 0, 0),
                               memory_space=pltpu.SMEM),
                  pl.BlockSpec((tm, LANES), lambda i: (i, 0)),
                  pl.BlockSpec((tm, D_MODEL), lambda i: (i, 0)),
                  pl.BlockSpec(memory_space=pl.ANY)],
        out_specs=pl.BlockSpec((tm, D_MODEL), lambda i: (i, 0)),
        out_shape=jax.ShapeDtypeStruct((t, D_MODEL), F32),
        scratch_shapes=[pltpu.VMEM((2 * tm, 1, D_MODEL), F32),
                        pltpu.VMEM((2 * tm, 1, D_MODEL), F32),
                        pltpu.VMEM((2 * tm, D_MODEL), F32),
                        pltpu.SemaphoreType.DMA(()),
                        pltpu.SemaphoreType.DMA(())],
        compiler_params=_cparams(1),
    )(dest3, dest3, gates, x2, y_rows)


def _layer(x, norm_mix_w, w_in, conv_w, conv_b, dt_bias, a_log, d_skip, ssm_norm_w, w_ssm_proj,
           q_norm_w, k_norm_w, rel_bias, w_attn_proj, w_out, norm_ffn_w, w_coarse, b_coarse,
           w_fine, b_fine, w_gate_exp, w_up_exp, w_down_exp):
    b, s, d = x.shape
    t = b * s
    x2d = x.reshape(t, d)

    dt_lo = 2 * D_INNER + BC_WIDTH
    qkv_lo = dt_lo + SSM_HEADS
    gate_lo = qkv_lo + 3 * ATTN_WIDTH
    w_ssd = jnp.concatenate([w_in[:, :qkv_lo], jnp.zeros((d, LANES - SSM_HEADS), w_in.dtype)],
                            axis=1).astype(BF16)
    w_gate = w_in[:, gate_lo:].astype(BF16)
    h2d = _prenorm(x2d, norm_mix_w[None])
    h3 = h2d.reshape(b, s, d)

    y_ssm = _ssd(h3, w_ssd, conv_w, conv_b, dt_bias, a_log, d_skip, ssm_norm_w).reshape(t, D_INNER)

    qw = jnp.tile(q_norm_w, 2)[None]
    kw = jnp.tile(k_norm_w, 2)[None]
    attn_outs, attn_lses = [], []
    for gi, (window, dilation) in enumerate(DILATED_CONFIGS):
        assert window // dilation == ATTN_BLK and s % window == 0
        hs = slice(gi * GROUP_WIDTH, (gi + 1) * GROUP_WIDTH)
        w_qkv = jnp.concatenate([w_in[:, qkv_lo + j * ATTN_WIDTH:qkv_lo + (j + 1) * ATTN_WIDTH][:, hs]
                                 for j in range(3)], axis=1).astype(BF16)
        bias = _band_bias(rel_bias[:, gi * HEADS_PER_GROUP:(gi + 1) * HEADS_PER_GROUP], dilation)
        o, l = _attn_group(h3, w_qkv, dilation, bias, qw, kw)
        attn_outs.extend(o)
        attn_lses.append(l)

    n_route = N_EXPERT_GROUPS + N_EXPERTS
    w_router = jnp.pad(jnp.concatenate([w_coarse, w_fine], axis=1), ((0, 0), (0, LANES - n_route)))
    b_router = jnp.pad(jnp.concatenate([b_coarse, b_fine]), (0, LANES - n_route))[None]
    x2, h2, logits = _mix_out(y_ssm, attn_outs, attn_lses, h2d, w_gate, x2d, w_ssm_proj.astype(BF16),
                              w_attn_proj.astype(BF16), w_out.astype(BF16), norm_ffn_w,
                              w_router.astype(BF16), b_router)

    sel, gates, counts = _route(logits)

    cnt = counts[0, :N_EXPERTS].astype(jnp.int32)
    padded = (cnt + MOE_BLOCK - 1) // MOE_BLOCK * MOE_BLOCK
    pad_end = jnp.cumsum(padded)
    n_blocks = -(-(2 * t + N_EXPERTS * (MOE_BLOCK - 1)) // MOE_BLOCK)
    block_start = jnp.arange(n_blocks, dtype=jnp.int32) * MOE_BLOCK
    block_expert = jnp.minimum(jnp.sum((pad_end[None, :] <= block_start[:, None]).astype(jnp.int32), axis=1),
                               N_EXPERTS - 1)
    n_used = (pad_end[-1:] // MOE_BLOCK).astype(jnp.int32)
    meta = jnp.concatenate([pad_end, padded, n_used]).astype(jnp.int32)
    dest = (pad_end - padded)[sel[:, 0:2]] + sel[:, 2:4]
    blk_ids = jnp.arange(n_blocks, dtype=jnp.int32)
    first = ((blk_ids == 0) | (block_expert != jnp.roll(block_expert, 1))) & (blk_ids < n_used[0])
    slot = (jnp.cumsum(first.astype(jnp.int32)) - 1) % 2
    nxt_blk = blk_ids + padded[block_expert] // MOE_BLOCK
    nxt = jnp.where(nxt_blk < n_used[0], block_expert[jnp.minimum(nxt_blk, n_blocks - 1)], -1)
    sched = jnp.stack([block_expert, first.astype(jnp.int32), slot, nxt,
                       jnp.broadcast_to(n_used, (n_blocks,))]).astype(jnp.int32)

    tm_d = min(256, t)
    dest_d = dest.reshape(t // tm_d, 1, 2 * tm_d)
    rows = _dispatch(meta, dest_d, h2, n_blocks * MOE_BLOCK)
    y_rows = _experts(sched, rows, w_gate_exp, w_up_exp, w_down_exp)
    tm_c = min(128, t)
    dest_c = dest.reshape(t // tm_c, 1, 2 * tm_c)
    out = _combine(dest_c, gates, x2, y_rows)
    return out.reshape(b, s, d)


def kernel(x, norm_mix_w, w_in, conv_w, conv_b, dt_bias, a_log, d_skip, ssm_norm_w, w_ssm_proj,
           q_norm_w, k_norm_w, rel_bias, w_attn_proj, w_out, norm_ffn_w, w_coarse, b_coarse,
           w_fine, b_fine, w_gate_exp, w_up_exp, w_down_exp):
    depth = norm_mix_w.shape[0]
    for layer in range(depth):
        x = _layer(x, norm_mix_w[layer], w_in[layer], conv_w[layer], conv_b[layer], dt_bias[layer],
                   a_log[layer], d_skip[layer], ssm_norm_w[layer], w_ssm_proj[layer],
                   q_norm_w[layer], k_norm_w[layer], rel_bias, w_attn_proj[layer], w_out[layer],
                   norm_ffn_w[layer], w_coarse[layer], b_coarse[layer], w_fine[layer], b_fine[layer],
                   w_gate_exp[layer], w_up_exp[layer], w_down_exp[layer])
    return x
```

```python
import functools
import math

import jax
import jax.numpy as jnp
import numpy as np
from jax import lax
from jax.experimental import pallas as pl
from jax.experimental.pallas import tpu as pltpu

F32 = jnp.float32
BF16 = jnp.bfloat16
HIGHEST = lax.Precision.HIGHEST

LANES = 128
NORM_EPS = 1e-6
NEG_BIG = -1e30
LOG2_E = math.log2(math.e)

D_MODEL = 1024
D_INNER = 2048
SSM_HEAD_DIM = 64
SSM_HEADS = 32
SSM_GROUPS = 2
D_STATE = 128
CONV_K = 4
BC_WIDTH = 2 * SSM_GROUPS * D_STATE
SSD_CHUNK = 128
ATTN_HEAD_DIM = 64
DILATED_CONFIGS = ((128, 1), (512, 4), (2048, 16))
HEADS_PER_GROUP = 8
GROUP_WIDTH = HEADS_PER_GROUP * ATTN_HEAD_DIM
ATTN_WIDTH = 3 * GROUP_WIDTH
ATTN_BLK = 128
NUM_BUCKETS = 32
MAX_DISTANCE = 2048
N_EXPERT_GROUPS = 8
EXPERTS_PER_GROUP = 8
N_EXPERTS = 64
D_EXPERT = 512
MOE_BLOCK = 256

SSD_PROJ_WIDTH = 2 * D_INNER + BC_WIDTH + LANES

VMEM_LIMIT = 56 * 1024 * 1024


def _sigmoid(x):
    return 1.0 / (1.0 + jnp.exp(-x))


def _cparams(n_axes):
    return pltpu.CompilerParams(dimension_semantics=("arbitrary",) * n_axes,
                                vmem_limit_bytes=VMEM_LIMIT)


def _prenorm_body(x_ref, nw_ref, h_ref):
    x = x_ref[...]
    ms = jnp.mean(x * x, axis=-1, keepdims=True)
    h_ref[...] = (x * lax.rsqrt(ms + NORM_EPS) * nw_ref[...]).astype(BF16)


def _prenorm(x2d, norm_w):
    t, d = x2d.shape
    tm = min(1024, t)
    return pl.pallas_call(
        _prenorm_body,
        grid=(t // tm,),
        in_specs=[pl.BlockSpec((tm, d), lambda i: (i, 0)), pl.BlockSpec((1, d), lambda i: (0, 0))],
        out_specs=pl.BlockSpec((tm, d), lambda i: (i, 0)),
        out_shape=jax.ShapeDtypeStruct((t, d), BF16),
        compiler_params=_cparams(1),
    )(x2d, norm_w)


def _dot3(x, y, x_is_exact):
    v = y if x_is_exact else x
    hi = v.astype(BF16)
    r1 = v - hi.astype(F32)
    mid = r1.astype(BF16)
    lo = (r1 - mid.astype(F32)).astype(BF16)
    if x_is_exact:
        return sum(jnp.dot(x, part, preferred_element_type=F32) for part in (hi, mid, lo))
    return sum(jnp.dot(part, y, preferred_element_type=F32) for part in (hi, mid, lo))


def _ssd_chunk(z, xbuf, r0, dt_raw, cw_ref, cb_ref, dtb_ref, alog_ref, dskip_ref, nw_ref, e_ref, state):
    L = SSD_CHUNK
    half = D_INNER // SSM_GROUPS
    hg = SSM_HEADS // SSM_GROUPS

    conv = cb_ref[...] + cw_ref[CONV_K - 1:CONV_K, :] * xbuf[8 + r0:8 + r0 + L, :]
    for k in range(CONV_K - 1):
        lo = 8 + r0 - (CONV_K - 1 - k)
        conv = conv + cw_ref[k:k + 1, :] * xbuf[lo:lo + L, :]
    xbc = conv * _sigmoid(conv)
    xs = xbc[:, :D_INNER]

    lane = lax.broadcasted_iota(jnp.int32, (L, LANES), 1)
    row = lax.broadcasted_iota(jnp.int32, (L, L), 0)
    col = lax.broadcasted_iota(jnp.int32, (L, L), 1)
    causal = row >= col

    v = dt_raw + dtb_ref[...]
    dt = jnp.maximum(v, 0.0) + jnp.log1p(jnp.exp(-jnp.abs(v)))
    dt = jnp.where(lane < SSM_HEADS, dt, 0.0)
    adt = dt * (-jnp.exp(alog_ref[...]))
    a_cs = _dot3(causal.astype(BF16), adt, True) * LOG2_E
    a_cs_t = a_cs.T
    expand = e_ref[...]
    a_full = _dot3(a_cs, expand, False)
    dt_full = _dot3(dt, expand, False)
    a_tot = a_full[L - 1:L, :]
    decay_from_start = jnp.exp2(a_full)
    decay_to_end = jnp.exp2(a_tot - a_full)
    decay_chunk = jnp.exp2(a_tot)

    xdt = xs * dt_full
    xw_b = (xdt * decay_to_end).astype(BF16)
    head_lo = lax.broadcasted_iota(jnp.int32, (L, D_INNER), 1) % (2 * SSM_HEAD_DIM) < SSM_HEAD_DIM
    xdt_lo = jnp.where(head_lo, xdt, 0.0).astype(BF16)
    xdt_hi = jnp.where(head_lo, 0.0, xdt).astype(BF16)

    b16, c16, cbs, y_offs, s_prevs = [], [], [], [], []
    for g in range(SSM_GROUPS):
        bg = xbc[:, D_INNER + g * D_STATE:D_INNER + (g + 1) * D_STATE]
        cg = xbc[:, D_INNER + (SSM_GROUPS + g) * D_STATE:D_INNER + (SSM_GROUPS + g + 1) * D_STATE]
        b16.append(bg)
        c16.append(cg.astype(BF16))
        cbs.append(lax.dot_general(c16[g], bg.astype(BF16), (((1,), (1,)), ((), ())),
                                   preferred_element_type=F32))
    for g in range(SSM_GROUPS):
        s_prevs.append(state[:, g * half:(g + 1) * half])
        y_offs.append(jnp.dot(c16[g], s_prevs[g].astype(BF16), preferred_element_type=F32))
    y_cols = []
    for g in range(SSM_GROUPS):
        for pr in range(half // LANES):
            h0 = g * hg + 2 * pr
            ps = slice(h0 * SSM_HEAD_DIM, (h0 + 2) * SSM_HEAD_DIM)
            ms = []
            for h in (h0, h0 + 1):
                seg = a_cs[:, h:h + 1] - a_cs_t[h:h + 1, :]
                ms.append((cbs[g] * jnp.exp2(jnp.where(causal, seg, NEG_BIG))).astype(BF16))
            y_cols.append(jnp.dot(jnp.concatenate(ms, axis=1),
                                  jnp.concatenate([xdt_lo[:, ps], xdt_hi[:, ps]], axis=0),
                                  preferred_element_type=F32))
    for g in range(SSM_GROUPS):
        gs = slice(g * half, (g + 1) * half)
        state[:, gs] = decay_chunk[:, gs] * s_prevs[g] + jnp.dot(
            b16[g].T.astype(BF16), xw_b[:, gs], preferred_element_type=F32)

    y = (jnp.concatenate(y_cols, axis=1) + jnp.concatenate(y_offs, axis=1) * decay_from_start
         + dskip_ref[...] * xs)
    y = y * (z * _sigmoid(z))
    normed = []
    for g in range(SSM_GROUPS):
        yg = y[:, g * half:(g + 1) * half]
        ms = jnp.mean(yg * yg, axis=-1, keepdims=True)
        normed.append(yg * lax.rsqrt(ms + NORM_EPS))
    return jnp.concatenate(normed, axis=1) * nw_ref[...]


def _ssd_body(h_ref, w_ref, cw_ref, cb_ref, dtb_ref, alog_ref, dskip_ref, nw_ref, e_ref,
              y_ref, state, xbuf, zbuf, dtbuf):
    L = SSD_CHUNK
    rt = h_ref.shape[1]
    conv_dim = D_INNER + BC_WIDTH

    @pl.when(pl.program_id(1) == 0)
    def _():
        state[...] = jnp.zeros_like(state)
        xbuf[0:8, :] = jnp.zeros((8, conv_dim), F32)

    h = h_ref[0]
    zbuf[...] = jnp.dot(h, w_ref[:, 0:D_INNER], preferred_element_type=F32)
    xbuf[8:8 + rt, :] = jnp.dot(h, w_ref[:, D_INNER:D_INNER + conv_dim], preferred_element_type=F32)
    dtbuf[...] = jnp.dot(h, w_ref[:, D_INNER + conv_dim:], preferred_element_type=F32)

    for c in range(rt // L):
        r0 = c * L
        y = _ssd_chunk(zbuf[r0:r0 + L, :], xbuf, r0, dtbuf[r0:r0 + L, :], cw_ref, cb_ref,
                       dtb_ref, alog_ref, dskip_ref, nw_ref, e_ref, state)
        y_ref[0, r0:r0 + L, :] = y.astype(BF16)
    xbuf[0:8, :] = xbuf[rt:rt + 8, :]


def _ssd(h3, w_ssd, conv_w, conv_b, dt_bias, a_log, d_skip, ssm_norm_w):
    b, s, d = h3.shape
    rt = min(4 * SSD_CHUNK, s)
    pad = LANES - SSM_HEADS
    conv_dim = D_INNER + BC_WIDTH
    dtb = jnp.pad(dt_bias, (0, pad))[None]
    alog = jnp.pad(a_log, (0, pad))[None]
    dskip = jnp.repeat(d_skip, SSM_HEAD_DIM)[None]
    expand = (np.arange(LANES)[:, None] == np.arange(D_INNER)[None, :] // SSM_HEAD_DIM).astype(np.float32)

    def const(shape):
        return pl.BlockSpec(shape, lambda i, c: (0,) * len(shape))

    return pl.pallas_call(
        _ssd_body,
        grid=(b, s // rt),
        in_specs=[pl.BlockSpec((1, rt, d), lambda i, c: (i, c, 0)),
                  const((d, SSD_PROJ_WIDTH)),
                  const((CONV_K, conv_dim)), const((1, conv_dim)),
                  const((1, LANES)), const((1, LANES)),
                  const((1, D_INNER)), const((1, D_INNER)),
                  const((LANES, D_INNER))],
        out_specs=pl.BlockSpec((1, rt, D_INNER), lambda i, c: (i, c, 0)),
        out_shape=jax.ShapeDtypeStruct((b, s, D_INNER), BF16),
        scratch_shapes=[pltpu.VMEM((D_STATE, D_INNER), F32),
                        pltpu.VMEM((rt + 8, conv_dim), F32),
                        pltpu.VMEM((rt, D_INNER), F32),
                        pltpu.VMEM((rt, LANES), F32)],
        compiler_params=_cparams(2),
    )(h3, w_ssd, conv_w, conv_b[None], dtb, alog, dskip, ssm_norm_w[None],
      jnp.asarray(expand, dtype=BF16))


def _t5_causal_bucket(dist):
    max_exact = NUM_BUCKETS // 2
    large = max_exact + (np.log(np.maximum(dist, max_exact) / max_exact)
                         / math.log(MAX_DISTANCE / max_exact) * (NUM_BUCKETS - max_exact)).astype(np.int32)
    return np.where(dist < max_exact, dist, np.minimum(large, NUM_BUCKETS - 1)).astype(np.int32)


def _band_bias(rel_bias_group, dilation):
    blk = ATTN_BLK
    off = np.arange(blk)[:, None] + blk - np.arange(2 * blk)[None, :]
    in_win = (off >= 0) & (off <= blk)
    bucket = _t5_causal_bucket(np.clip(off, 0, None) * dilation)
    onehot = (bucket.reshape(-1, 1) == np.arange(NUM_BUCKETS)[None, :]).astype(np.float32)
    bias = jnp.dot(jnp.asarray(onehot), rel_bias_group.astype(F32), precision=HIGHEST)
    bias = jnp.transpose(bias.reshape(blk, 2 * blk, HEADS_PER_GROUP), (2, 0, 1))
    bias = jnp.where(in_win[None], bias, NEG_BIG)
    return bias.reshape(HEADS_PER_GROUP // 2, 2 * blk, 2 * blk)


def _attn_body(h_ref, w_ref, bias_ref, qw_ref, kw_ref, *rest, dilation):
    n_pairs = HEADS_PER_GROUP // 2
    o_refs = rest[0:n_pairs]
    lse_ref, qkv, kbuf, vbuf = rest[n_pairs:]
    blk = ATTN_BLK
    rt = h_ref.shape[1]
    span = blk * dilation
    n_sub = rt // span
    assert n_sub == 1 or dilation == 1
    step = pl.program_id(1)

    @pl.when(step == 0)
    def _():
        kbuf[...] = jnp.zeros_like(kbuf)
        vbuf[...] = jnp.zeros_like(vbuf)

    first_head = lax.broadcasted_iota(jnp.int32, (rt, LANES), 1) < ATTN_HEAD_DIM

    def head_norm(x, w_ref):
        xx = x * x
        s0 = jnp.sum(jnp.where(first_head, xx, 0.0), axis=-1, keepdims=True)
        s1 = jnp.sum(jnp.where(first_head, 0.0, xx), axis=-1, keepdims=True)
        ss = jnp.where(first_head, s0, s1)
        return x * lax.rsqrt(ss * (1.0 / ATTN_HEAD_DIM) + NORM_EPS) * w_ref[...]

    h = h_ref[0]
    for j in range(3 * n_pairs // 2):
        piece = jnp.dot(h, w_ref[:, 2 * j * LANES:2 * (j + 1) * LANES], preferred_element_type=F32)
        for half in range(2):
            slab = piece[:, half * LANES:(half + 1) * LANES]
            if j < n_pairs // 2:
                slab = head_norm(slab, qw_ref) * (ATTN_HEAD_DIM ** -0.5)
            elif j < n_pairs:
                slab = head_norm(slab, kw_ref)
            qkv[2 * j + half] = slab

    lane = lax.broadcasted_iota(jnp.int32, (blk, LANES), 1)
    lo_half = lane < ATTN_HEAD_DIM
    nt = (((1,), (1,)), ((), ()))

    units = 2

    def block_pair(it, carry):
        pairs = range(n_pairs)
        rows, res, slot, pen, q2, k_new, v_new, k_old, v_old = [], [], [], [], [], [], [], [], []
        for u in range(units):
            blk_id = units * it + u
            sub, r = (blk_id, 0) if dilation == 1 else (0, blk_id)
            gblk = step * n_sub + sub
            rows.append(pl.ds(sub * span + r, blk, stride=dilation))
            res.append(r)
            slot.append(gblk % 2)
            pen.append(jnp.where(gblk == 0, NEG_BIG, 0.0))
            k_new.append([qkv[n_pairs + p, rows[u], :].astype(BF16) for p in pairs])
            v_new.append([qkv[2 * n_pairs + p, rows[u], :].astype(BF16) for p in pairs])
            if dilation == 1 and u > 0:
                k_old.append(k_new[u - 1])
                v_old.append(v_new[u - 1])
            else:
                k_old.append([kbuf[1 - slot[u], r * n_pairs + p] for p in pairs])
                v_old.append([vbuf[1 - slot[u], r * n_pairs + p] for p in pairs])
            q2.append([])
            for p in pairs:
                qp = qkv[p, rows[u], :]
                q2[u].append(jnp.concatenate([jnp.where(lo_half, qp, 0.0), jnp.where(lo_half, 0.0, qp)],
                                             axis=0).astype(BF16))
        todo = [(u, p) for u in range(units) for p in pairs]
        s_prev = {up: lax.dot_general(q2[up[0]][up[1]], k_old[up[0]][up[1]], nt, preferred_element_type=F32)
                  for up in todo}
        s_cur = {up: lax.dot_general(q2[up[0]][up[1]], k_new[up[0]][up[1]], nt, preferred_element_type=F32)
                 for up in todo}
        e_prev, e_cur, m, d = {}, {}, {}, {}
        for up in todo:
            u, p = up
            sp = s_prev[up] + (bias_ref[p, :, 0:blk] + pen[u])
            sc = s_cur[up] + bias_ref[p, :, blk:2 * blk]
            m[up] = jnp.max(jnp.maximum(sp, sc), axis=-1, keepdims=True)
            ep = jnp.exp(sp - m[up])
            ec = jnp.exp(sc - m[up])
            d[up] = jnp.sum(ep + ec, axis=-1, keepdims=True)
            e_prev[up] = ep.astype(BF16)
            e_cur[up] = ec.astype(BF16)
        pv_prev = {up: jnp.dot(e_prev[up], v_old[up[0]][up[1]], preferred_element_type=F32) for up in todo}
        pv_cur = {up: jnp.dot(e_cur[up], v_new[up[0]][up[1]], preferred_element_type=F32) for up in todo}
        for u in range(units):
            lse_tile = jnp.zeros((blk, LANES), F32)
            for p in pairs:
                pv = (pv_prev[u, p] + pv_cur[u, p]) / d[u, p]
                o_refs[p][0, rows[u], :] = jnp.where(lo_half, pv[0:blk], pv[blk:2 * blk])
                lse = m[u, p] + jnp.log(d[u, p])
                lse_tile = jnp.where(lane == 2 * p, lse[0:blk], lse_tile)
                lse_tile = jnp.where(lane == 2 * p + 1, lse[blk:2 * blk], lse_tile)
                kbuf[slot[u], res[u] * n_pairs + p] = k_new[u][p]
                vbuf[slot[u], res[u] * n_pairs + p] = v_new[u][p]
            lse_ref[0, rows[u], :] = lse_tile
        return carry

    lax.fori_loop(0, n_sub * dilation // units, block_pair, 0)


def _attn_group(h3, w_qkv, dilation, bias, qw, kw):
    b, s, d = h3.shape
    blk = ATTN_BLK
    n_pairs = HEADS_PER_GROUP // 2
    rt = max(4 * blk, blk * dilation)
    rt = min(rt, s)

    def const(shape):
        return pl.BlockSpec(shape, lambda i, n: (0,) * len(shape))

    token_spec = pl.BlockSpec((1, rt, LANES), lambda i, n: (i, n, 0))
    res = pl.pallas_call(
        functools.partial(_attn_body, dilation=dilation),
        grid=(b, s // rt),
        in_specs=[pl.BlockSpec((1, rt, d), lambda i, n: (i, n, 0)),
                  const((d, 3 * GROUP_WIDTH)),
                  const((n_pairs, 2 * blk, 2 * blk)), const((1, LANES)), const((1, LANES))],
        out_specs=[token_spec] * (n_pairs + 1),
        out_shape=[jax.ShapeDtypeStruct((b, s, LANES), F32)] * (n_pairs + 1),
        scratch_shapes=[pltpu.VMEM((3 * n_pairs, rt, LANES), F32),
                        pltpu.VMEM((2, dilation * n_pairs, blk, LANES), BF16),
                        pltpu.VMEM((2, dilation * n_pairs, blk, LANES), BF16)],
        compiler_params=_cparams(2),
    )(h3, w_qkv, bias, qw, kw)
    outs = [o.reshape(b * s, LANES) for o in res[:n_pairs]]
    return outs, res[n_pairs].reshape(b * s, LANES)


def _mix_body(*refs):
    n_pairs = HEADS_PER_GROUP // 2
    y_ref = refs[0]
    o_refs = refs[1:1 + 3 * n_pairs]
    l_refs = refs[1 + 3 * n_pairs:4 + 3 * n_pairs]
    (h_ref, wgate_ref, x_ref, wssm_ref, wattn_ref, wout_ref, e8_ref, nfw_ref, wr_ref, br_ref,
     x2_ref, h2_ref, lg_ref) = refs[4 + 3 * n_pairs:]
    tm = x_ref.shape[0]
    lses = [l[...] for l in l_refs]
    mx = jnp.maximum(jnp.maximum(lses[0], lses[1]), lses[2])
    es = [jnp.exp(l - mx) for l in lses]
    inv = 1.0 / (es[0] + es[1] + es[2])
    e8 = e8_ref[...]
    att = jnp.zeros((tm, GROUP_WIDTH), F32)
    for g in range(3):
        w = es[g] * inv
        w_hi = w.astype(BF16)
        w_lo = (w - w_hi.astype(F32)).astype(BF16)
        w_full = (jnp.dot(w_hi, e8, preferred_element_type=F32)
                  + jnp.dot(w_lo, e8, preferred_element_type=F32))
        o_g = jnp.concatenate([o_refs[g * n_pairs + p][...] for p in range(n_pairs)], axis=1)
        att = att + w_full * o_g
    y_attn = jnp.dot(att.astype(BF16), wattn_ref[...], preferred_element_type=F32)
    y_ssm = jnp.dot(y_ref[...], wssm_ref[...], preferred_element_type=F32)
    h = h_ref[...]
    g_ssm = jnp.dot(h, wgate_ref[:, 0:D_MODEL], preferred_element_type=F32)
    g_attn = jnp.dot(h, wgate_ref[:, D_MODEL:2 * D_MODEL], preferred_element_type=F32)
    merged = _sigmoid(g_ssm) * y_ssm + _sigmoid(g_attn) * y_attn
    x2 = x_ref[...] + jnp.dot(merged.astype(BF16), wout_ref[...], preferred_element_type=F32)
    x2_ref[...] = x2
    ms = jnp.mean(x2 * x2, axis=-1, keepdims=True)
    h2 = x2 * lax.rsqrt(ms + NORM_EPS) * nfw_ref[...]
    bits = lax.bitcast_convert_type(h2.astype(BF16).astype(F32), jnp.uint32)
    packed = bits[:, 0:D_MODEL // 2] | (bits[:, D_MODEL // 2:] >> 16)
    h2_ref[...] = packed.reshape(tm, 1, D_MODEL // 2)
    lg_ref[...] = jnp.dot(h2.astype(BF16), wr_ref[...], preferred_element_type=F32) + br_ref[...]


def _mix_out(y_ssm, attn_outs, attn_lses, h2d, w_gate, x2d, w_ssm, w_attn, w_out, norm_ffn_w, w_router, b_router):
    t = x2d.shape[0]
    tm = min(512, t)
    e8 = (np.arange(LANES)[:, None] == np.arange(GROUP_WIDTH)[None, :] // ATTN_HEAD_DIM)
    e8 = jnp.asarray(e8.astype(np.float32), dtype=BF16)

    def rows(width, cb=0):
        return pl.BlockSpec((tm, width), lambda i: (i, cb))

    def const(shape):
        return pl.BlockSpec(shape, lambda i: (0,) * len(shape), pipeline_mode=pl.Buffered(1))

    return pl.pallas_call(
        _mix_body,
        grid=(t // tm,),
        in_specs=[rows(D_INNER)] + [rows(LANES)] * (len(attn_outs) + len(attn_lses)) + [
                  rows(D_MODEL), const((D_MODEL, 2 * D_MODEL)), rows(D_MODEL),
                  const((D_INNER, D_MODEL)), const((GROUP_WIDTH, D_MODEL)), const((D_MODEL, D_MODEL)),
                  const((LANES, GROUP_WIDTH)), const((1, D_MODEL)),
                  const((D_MODEL, LANES)), const((1, LANES))],
        out_specs=[rows(D_MODEL),
                   pl.BlockSpec((tm, 1, D_MODEL // 2), lambda i: (i, 0, 0)),
                   rows(LANES)],
        out_shape=[jax.ShapeDtypeStruct((t, D_MODEL), F32),
                   jax.ShapeDtypeStruct((t, 1, D_MODEL // 2), jnp.uint32),
                   jax.ShapeDtypeStruct((t, LANES), F32)],
        compiler_params=_cparams(1),
    )(y_ssm, *attn_outs, *attn_lses, h2d, w_gate, x2d, w_ssm, w_attn, w_out, e8,
      norm_ffn_w[None], w_router, b_router)


def _route_body(lg_ref, sel_ref, gate_ref, cnt_ref, counts):
    tm = lg_ref.shape[0]
    i = pl.program_id(0)

    @pl.when(i == 0)
    def _():
        counts[...] = jnp.zeros_like(counts)

    lg = lg_ref[...]
    lane = lax.broadcasted_iota(jnp.int32, (tm, LANES), 1)
    is_coarse = lane < N_EXPERT_GROUPS
    cmax = jnp.max(jnp.where(is_coarse, lg, NEG_BIG), axis=-1, keepdims=True)
    grp = jnp.min(jnp.where(is_coarse & (lg == cmax), lane, LANES), axis=-1, keepdims=True)
    group_p = 1.0 / jnp.sum(jnp.where(is_coarse, jnp.exp(lg - cmax), 0.0), axis=-1, keepdims=True)
    f_lo = N_EXPERT_GROUPS + EXPERTS_PER_GROUP * grp
    in_grp = (lane >= f_lo) & (lane < f_lo + EXPERTS_PER_GROUP)
    f1 = jnp.max(jnp.where(in_grp, lg, NEG_BIG), axis=-1, keepdims=True)
    i1 = jnp.min(jnp.where(in_grp & (lg == f1), lane, LANES), axis=-1, keepdims=True)
    rest = in_grp & (lane != i1)
    f2 = jnp.max(jnp.where(rest, lg, NEG_BIG), axis=-1, keepdims=True)
    i2 = jnp.min(jnp.where(rest & (lg == f2), lane, LANES), axis=-1, keepdims=True)
    e2 = jnp.exp(f2 - f1)
    g1 = group_p / (1.0 + e2)
    g2 = group_p * e2 / (1.0 + e2)

    e1 = i1 - N_EXPERT_GROUPS
    e2i = i2 - N_EXPERT_GROUPS
    oh1 = lane == e1
    oh2 = lane == e2i
    onehot = jnp.where(oh1 | oh2, 1.0, 0.0)
    r = lax.broadcasted_iota(jnp.int32, (tm, tm), 0)
    c = lax.broadcasted_iota(jnp.int32, (tm, tm), 1)
    before = jnp.dot((r > c).astype(BF16), onehot.astype(BF16), preferred_element_type=F32)
    pos = counts[0:1, :] + before
    r1 = jnp.sum(jnp.where(oh1, pos, 0.0), axis=-1, keepdims=True).astype(jnp.int32)
    r2 = jnp.sum(jnp.where(oh2, pos, 0.0), axis=-1, keepdims=True).astype(jnp.int32)
    counts[...] = counts[...] + jnp.sum(onehot, axis=0, keepdims=True)
    sel_ref[...] = jnp.where(lane == 0, e1, jnp.where(lane == 1, e2i, jnp.where(lane == 2, r1,
                                                                                  jnp.where(lane == 3, r2, 0))))
    gate_ref[...] = jnp.where(lane == 0, g1, jnp.where(lane == 1, g2, 0.0))
    cnt_ref[...] = counts[...]


def _route(logits):
    t = logits.shape[0]
    tm = min(512, t)
    return pl.pallas_call(
        _route_body,
        grid=(t // tm,),
        in_specs=[pl.BlockSpec((tm, LANES), lambda i: (i, 0))],
        out_specs=[pl.BlockSpec((tm, LANES), lambda i: (i, 0)),
                   pl.BlockSpec((tm, LANES), lambda i: (i, 0)),
                   pl.BlockSpec((8, LANES), lambda i: (0, 0))],
        out_shape=[jax.ShapeDtypeStruct((t, LANES), jnp.int32),
                   jax.ShapeDtypeStruct((t, LANES), F32),
                   jax.ShapeDtypeStruct((8, LANES), F32)],
        scratch_shapes=[pltpu.VMEM((8, LANES), F32)],
        compiler_params=_cparams(1),
    )(logits)


def _dispatch_body(meta_ref, dest_ref, h_ref, rows_ref, zbuf, zsem, sem):
    tm = h_ref.shape[0]

    def zero_copy(e):
        start = pl.multiple_of(meta_ref[e] - MOE_BLOCK, MOE_BLOCK)
        return pltpu.make_async_copy(zbuf, rows_ref.at[pl.ds(start, MOE_BLOCK)], zsem)

    def tail_copy(blk):
        start = pl.multiple_of(blk * MOE_BLOCK, MOE_BLOCK)
        return pltpu.make_async_copy(zbuf, rows_ref.at[pl.ds(start, MOE_BLOCK)], zsem)

    @pl.when(pl.program_id(0) == 0)
    def _():
        zbuf[...] = jnp.zeros_like(zbuf)

        def start_zero(e, carry):
            @pl.when(meta_ref[N_EXPERTS + e] > 0)
            def _():
                zero_copy(e).start()
            return carry

        def wait_zero(e, carry):
            @pl.when(meta_ref[N_EXPERTS + e] > 0)
            def _():
                zero_copy(e).wait()
            return carry

        def start_tail(blk, carry):
            tail_copy(blk).start()
            return carry

        def wait_tail(blk, carry):
            tail_copy(blk).wait()
            return carry

        n_used = meta_ref[2 * N_EXPERTS]
        n_blocks = rows_ref.shape[0] // MOE_BLOCK
        lax.fori_loop(0, N_EXPERTS, start_zero, 0)
        lax.fori_loop(n_used, n_blocks, start_tail, 0)
        lax.fori_loop(0, N_EXPERTS, wait_zero, 0)
        lax.fori_loop(n_used, n_blocks, wait_tail, 0)

    def issue(j, carry):
        for k in range(2):
            d = dest_ref[0, 2 * j + k]
            pltpu.make_async_copy(h_ref.at[j], rows_ref.at[d], sem).start(priority=k)
        return carry

    lax.fori_loop(0, tm, issue, 0, unroll=8)
    for _ in range(2):
        pltpu.make_async_copy(h_ref, rows_ref.at[pl.ds(0, tm)], sem).wait()


def _dispatch(meta, dest3, h2, n_rows):
    t = h2.shape[0]
    tm = dest3.shape[2] // 2
    return pl.pallas_call(
        _dispatch_body,
        grid_spec=pltpu.PrefetchScalarGridSpec(
            num_scalar_prefetch=1,
            grid=(t // tm,),
            in_specs=[pl.BlockSpec((None, 1, 2 * tm), lambda i, m: (i, 0, 0), memory_space=pltpu.SMEM),
                      pl.BlockSpec((tm, 1, D_MODEL // 2), lambda i, m: (i, 0, 0))],
            out_specs=pl.BlockSpec(memory_space=pl.ANY),
            scratch_shapes=[pltpu.VMEM((MOE_BLOCK, 1, D_MODEL // 2), jnp.uint32),
                            pltpu.SemaphoreType.DMA(()),
                            pltpu.SemaphoreType.DMA(())]),
        out_shape=jax.ShapeDtypeStruct((n_rows, 1, D_MODEL // 2), jnp.uint32),
        compiler_params=_cparams(1),
    )(meta, dest3, h2)


def _expert_body(sched_ref, x_ref, wg_hbm, wu_hbm, wd_hbm, y_ref, x2d, wg_f, wu_f, wd_f, wg_b, wu_b, wd_b, sems):
    blk = pl.program_id(0)
    n_used = sched_ref[4, 0]

    def weight_copies(expert, slot):
        return [pltpu.make_async_copy(src.at[expert], dst.at[slot], sems.at[slot])
                for src, dst in ((wg_hbm, wg_f), (wu_hbm, wu_f), (wd_hbm, wd_f))]

    @pl.when(blk < n_used)
    def _():
        expert = sched_ref[0, blk]
        slot = sched_ref[2, blk]
        nxt = sched_ref[3, blk]

        @pl.when(sched_ref[1, blk] == 1)
        def _():
            @pl.when(blk == 0)
            def _():
                for c in weight_copies(expert, slot):
                    c.start()

            for c in weight_copies(expert, slot):
                c.wait()

            @pl.when(nxt >= 0)
            def _():
                for c in weight_copies(nxt, 1 - slot):
                    c.start()

            wg_b[...] = wg_f[slot].astype(BF16)
            wu_b[...] = wu_f[slot].astype(BF16)
            wd_b[...] = wd_f[slot].astype(BF16)

        x2d[...] = x_ref[...].reshape(MOE_BLOCK, D_MODEL // 2)
        words = x2d[...]
        x = jnp.concatenate([lax.bitcast_convert_type(words & jnp.uint32(0xFFFF0000), F32),
                             lax.bitcast_convert_type(words << 16, F32)], axis=1).astype(BF16)
        g = jnp.dot(x, wg_b[...], preferred_element_type=F32)
        u = jnp.dot(x, wu_b[...], preferred_element_type=F32)
        hid = (g * _sigmoid(g)) * u
        y = jnp.dot(hid.astype(BF16), wd_b[...], preferred_element_type=F32)
        y_ref[...] = y.reshape(MOE_BLOCK, 1, D_MODEL)

    @pl.when(blk >= n_used)
    def _():
        y_ref[...] = jnp.zeros_like(y_ref)


def _experts(sched, rows, w_gate, w_up, w_down):
    n_rows = rows.shape[0]
    n_blocks = n_rows // MOE_BLOCK

    def row_map(b, sched):
        return (b, 0, 0)

    any_spec = pl.BlockSpec(memory_space=pl.ANY)
    return pl.pallas_call(
        _expert_body,
        grid_spec=pltpu.PrefetchScalarGridSpec(
            num_scalar_prefetch=1,
            grid=(n_blocks,),
            in_specs=[pl.BlockSpec((MOE_BLOCK, 1, D_MODEL // 2), row_map), any_spec, any_spec, any_spec],
            out_specs=pl.BlockSpec((MOE_BLOCK, 1, D_MODEL), row_map),
            scratch_shapes=[pltpu.VMEM((MOE_BLOCK, D_MODEL // 2), jnp.uint32),
                            pltpu.VMEM((2, D_MODEL, D_EXPERT), F32),
                            pltpu.VMEM((2, D_MODEL, D_EXPERT), F32),
                            pltpu.VMEM((2, D_EXPERT, D_MODEL), F32),
                            pltpu.VMEM((D_MODEL, D_EXPERT), BF16),
                            pltpu.VMEM((D_MODEL, D_EXPERT), BF16),
                            pltpu.VMEM((D_EXPERT, D_MODEL), BF16),
                            pltpu.SemaphoreType.DMA((2,))]),
        out_shape=jax.ShapeDtypeStruct((n_rows, 1, D_MODEL), F32),
        compiler_params=_cparams(1),
    )(sched, rows, w_gate, w_up, w_down)


def _combine_body(dcur_ref, dnext_ref, gate_ref, x2_ref, y_ref, o_ref, buf_a, buf_b, y2d, sem_a, sem_b):
    tm = x2_ref.shape[0]
    i = pl.program_id(0)
    n = pl.num_programs(0)

    def issue(dref, buf, sem):
        def body(j, carry):
            for k in range(2):
                d = dref[0, 2 * j + k]
                pltpu.make_async_copy(y_ref.at[d], buf.at[k * tm + j], sem).start(priority=k)
            return carry
        lax.fori_loop(0, tm, body, 0, unroll=8)

    def finish(buf, sem):
        pltpu.make_async_copy(y_ref.at[pl.ds(0, 2 * tm)], buf, sem).wait()
        y2d[...] = buf[...].reshape(2 * tm, D_MODEL)
        g = gate_ref[...]
        o_ref[...] = x2_ref[...] + g[:, 0:1] * y2d[0:tm, :] + g[:, 1:2] * y2d[tm:2 * tm, :]

    @pl.when(i == 0)
    def _():
        issue(dcur_ref, buf_a, sem_a)

    for par, (cur, cur_sem, nxt, nxt_sem) in enumerate(((buf_a, sem_a, buf_b, sem_b),
                                                        (buf_b, sem_b, buf_a, sem_a))):
        @pl.when(i % 2 == par)
        def _(cur=cur, cur_sem=cur_sem, nxt=nxt, nxt_sem=nxt_sem):
            @pl.when(i + 1 < n)
            def _():
                issue(dnext_ref, nxt, nxt_sem)
            finish(cur, cur_sem)


def _combine(dest3, gates, x2, y_rows):
    t = x2.shape[0]
    tm = dest3.shape[2] // 2
    nt = t // tm
    return pl.pallas_call(
        _combine_body,
        grid=(nt,),
        in_specs=[pl.BlockSpec((None, 1, 2 * tm), lambda i: (i, 0, 0), memory_space=pltpu.SMEM),
                  pl.BlockSpec((None, 1, 2 * tm), lambda i: (jnp.minimum(i + 1, nt - 1), 0, 0),
                               memory_space=pltpu.SMEM),
                  pl.BlockSpec((tm, LANES), lambda i: (i, 0)),
                  pl.BlockSpec((tm, D_MODEL), lambda i: (i, 0)),
                  pl.BlockSpec(memory_space=pl.ANY)],
        out_specs=pl.BlockSpec((tm, D_MODEL), lambda i: (i, 0)),
        out_shape=jax.ShapeDtypeStruct((t, D_MODEL), F32),
        scratch_shapes=[pltpu.VMEM((2 * tm, 1, D_MODEL), F32),
                        pltpu.VMEM((2 * tm, 1, D_MODEL), F32),
                        pltpu.VMEM((2 * tm, D_MODEL), F32),
                        pltpu.SemaphoreType.DMA(()),
                        pltpu.SemaphoreType.DMA(())],
        compiler_params=_cparams(1),
    )(dest3, dest3, gates, x2, y_rows)


def _layer(x, norm_mix_w, w_in, conv_w, conv_b, dt_bias, a_log, d_skip, ssm_norm_w, w_ssm_proj,
           q_norm_w, k_norm_w, rel_bias, w_attn_proj, w_out, norm_ffn_w, w_coarse, b_coarse,
           w_fine, b_fine, w_gate_exp, w_up_exp, w_down_exp):
    b, s, d = x.shape
    t = b * s
    x2d = x.reshape(t, d)

    dt_lo = 2 * D_INNER + BC_WIDTH
    qkv_lo = dt_lo + SSM_HEADS
    gate_lo = qkv_lo + 3 * ATTN_WIDTH
    w_ssd = jnp.concatenate([w_in[:, :qkv_lo], jnp.zeros((d, LANES - SSM_HEADS), w_in.dtype)],
                            axis=1).astype(BF16)
    w_gate = w_in[:, gate_lo:].astype(BF16)
    h2d = _prenorm(x2d, norm_mix_w[None])
    h3 = h2d.reshape(b, s, d)

    y_ssm = _ssd(h3, w_ssd, conv_w, conv_b, dt_bias, a_log, d_skip, ssm_norm_w).reshape(t, D_INNER)

    qw = jnp.tile(q_norm_w, 2)[None]
    kw = jnp.tile(k_norm_w, 2)[None]
    attn_outs, attn_lses = [], []
    for gi, (window, dilation) in enumerate(DILATED_CONFIGS):
        assert window // dilation == ATTN_BLK and s % window == 0
        hs = slice(gi * GROUP_WIDTH, (gi + 1) * GROUP_WIDTH)
        w_qkv = jnp.concatenate([w_in[:, qkv_lo + j * ATTN_WIDTH:qkv_lo + (j + 1) * ATTN_WIDTH][:, hs]
                                 for j in range(3)], axis=1).astype(BF16)
        bias = _band_bias(rel_bias[:, gi * HEADS_PER_GROUP:(gi + 1) * HEADS_PER_GROUP], dilation)
        o, l = _attn_group(h3, w_qkv, dilation, bias, qw, kw)
        attn_outs.extend(o)
        attn_lses.append(l)

    n_route = N_EXPERT_GROUPS + N_EXPERTS
    w_router = jnp.pad(jnp.concatenate([w_coarse, w_fine], axis=1), ((0, 0), (0, LANES - n_route)))
    b_router = jnp.pad(jnp.concatenate([b_coarse, b_fine]), (0, LANES - n_route))[None]
    x2, h2, logits = _mix_out(y_ssm, attn_outs, attn_lses, h2d, w_gate, x2d, w_ssm_proj.astype(BF16),
                              w_attn_proj.astype(BF16), w_out.astype(BF16), norm_ffn_w,
                              w_router.astype(BF16), b_router)

    sel, gates, counts = _route(logits)

    cnt = counts[0, :N_EXPERTS].astype(jnp.int32)
    padded = (cnt + MOE_BLOCK - 1) // MOE_BLOCK * MOE_BLOCK
    pad_end = jnp.cumsum(padded)
    n_blocks = -(-(2 * t + N_EXPERTS * (MOE_BLOCK - 1)) // MOE_BLOCK)
    block_start = jnp.arange(n_blocks, dtype=jnp.int32) * MOE_BLOCK
    block_expert = jnp.minimum(jnp.sum((pad_end[None, :] <= block_start[:, None]).astype(jnp.int32), axis=1),
                               N_EXPERTS - 1)
    n_used = (pad_end[-1:] // MOE_BLOCK).astype(jnp.int32)
    meta = jnp.concatenate([pad_end, padded, n_used]).astype(jnp.int32)
    chosen = sel[:, 0:2, None] == jnp.arange(N_EXPERTS, dtype=jnp.int32)
    dest = jnp.sum(jnp.where(chosen, pad_end - padded, 0), axis=-1) + sel[:, 2:4]
    blk_ids = jnp.arange(n_blocks, dtype=jnp.int32)
    first = ((blk_ids == 0) | (block_expert != jnp.roll(block_expert, 1))) & (blk_ids < n_used[0])
    slot = (jnp.cumsum(first.astype(jnp.int32)) - 1) % 2
    nxt_blk = blk_ids + padded[block_expert] // MOE_BLOCK
    nxt = jnp.where(nxt_blk < n_used[0], block_expert[jnp.minimum(nxt_blk, n_blocks - 1)], -1)
    sched = jnp.stack([block_expert, first.astype(jnp.int32), slot, nxt,
                       jnp.broadcast_to(n_used, (n_blocks,))]).astype(jnp.int32)

    tm_d = min(256, t)
    dest_d = dest.reshape(t // tm_d, 1, 2 * tm_d)
    rows = _dispatch(meta, dest_d, h2, n_blocks * MOE_BLOCK)
    y_rows = _experts(sched, rows, w_gate_exp, w_up_exp, w_down_exp)
    tm_c = min(128, t)
    dest_c = dest.reshape(t // tm_c, 1, 2 * tm_c)
    out = _combine(dest_c, gates, x2, y_rows)
    return out.reshape(b, s, d)


def kernel(x, norm_mix_w, w_in, conv_w, conv_b, dt_bias, a_log, d_skip, ssm_norm_w, w_ssm_proj,
           q_norm_w, k_norm_w, rel_bias, w_attn_proj, w_out, norm_ffn_w, w_coarse, b_coarse,
           w_fine, b_fine, w_gate_exp, w_up_exp, w_down_exp):
    depth = norm_mix_w.shape[0]
    for layer in range(depth):
        x = _layer(x, norm_mix_w[layer], w_in[layer], conv_w[layer], conv_b[layer], dt_bias[layer],
                   a_log[layer], d_skip[layer], ssm_norm_w[layer], w_ssm_proj[layer],
                   q_norm_w[layer], k_norm_w[layer], rel_bias, w_attn_proj[layer], w_out[layer],
                   norm_ffn_w[layer], w_coarse[layer], b_coarse[layer], w_fine[layer], b_fine[layer],
                   w_gate_exp[layer], w_up_exp[layer], w_down_exp[layer])
    return x
```

```python
import functools
import math

import jax
import jax.numpy as jnp
import numpy as np
from jax import lax
from jax.experimental import pallas as pl
from jax.experimental.pallas import tpu as pltpu

F32 = jnp.float32
BF16 = jnp.bfloat16
HIGHEST = lax.Precision.HIGHEST

LANES = 128
NORM_EPS = 1e-6
NEG_BIG = -1e30
LOG2_E = math.log2(math.e)

D_MODEL = 1024
D_INNER = 2048
SSM_HEAD_DIM = 64
SSM_HEADS = 32
SSM_GROUPS = 2
D_STATE = 128
CONV_K = 4
BC_WIDTH = 2 * SSM_GROUPS * D_STATE
SSD_CHUNK = 128
ATTN_HEAD_DIM = 64
DILATED_CONFIGS = ((128, 1), (512, 4), (2048, 16))
HEADS_PER_GROUP = 8
GROUP_WIDTH = HEADS_PER_GROUP * ATTN_HEAD_DIM
ATTN_WIDTH = 3 * GROUP_WIDTH
ATTN_BLK = 128
NUM_BUCKETS = 32
MAX_DISTANCE = 2048
N_EXPERT_GROUPS = 8
EXPERTS_PER_GROUP = 8
N_EXPERTS = 64
D_EXPERT = 512
MOE_BLOCK = 256

SSD_PROJ_WIDTH = 2 * D_INNER + BC_WIDTH + LANES

VMEM_LIMIT = 56 * 1024 * 1024


def _sigmoid(x):
    return 1.0 / (1.0 + jnp.exp(-x))


def _cparams(n_axes):
    return pltpu.CompilerParams(dimension_semantics=("arbitrary",) * n_axes,
                                vmem_limit_bytes=VMEM_LIMIT)


def _prenorm_body(x_ref, nw_ref, h_ref):
    x = x_ref[...]
    ms = jnp.mean(x * x, axis=-1, keepdims=True)
    h_ref[...] = (x * lax.rsqrt(ms + NORM_EPS) * nw_ref[...]).astype(BF16)


def _prenorm(x2d, norm_w):
    t, d = x2d.shape
    tm = min(1024, t)
    return pl.pallas_call(
        _prenorm_body,
        grid=(t // tm,),
        in_specs=[pl.BlockSpec((tm, d), lambda i: (i, 0)), pl.BlockSpec((1, d), lambda i: (0, 0))],
        out_specs=pl.BlockSpec((tm, d), lambda i: (i, 0)),
        out_shape=jax.ShapeDtypeStruct((t, d), BF16),
        compiler_params=_cparams(1),
    )(x2d, norm_w)


def _dot3(x, y, x_is_exact):
    v = y if x_is_exact else x
    hi = v.astype(BF16)
    r1 = v - hi.astype(F32)
    mid = r1.astype(BF16)
    lo = (r1 - mid.astype(F32)).astype(BF16)
    if x_is_exact:
        return sum(jnp.dot(x, part, preferred_element_type=F32) for part in (hi, mid, lo))
    return sum(jnp.dot(part, y, preferred_element_type=F32) for part in (hi, mid, lo))


def _ssd_chunk(z, xbuf, r0, dt_raw, cw_ref, cb_ref, dtb_ref, alog_ref, dskip_ref, nw_ref, e_ref, state):
    L = SSD_CHUNK
    half = D_INNER // SSM_GROUPS
    hg = SSM_HEADS // SSM_GROUPS

    conv = cb_ref[...] + cw_ref[CONV_K - 1:CONV_K, :] * xbuf[8 + r0:8 + r0 + L, :]
    for k in range(CONV_K - 1):
        lo = 8 + r0 - (CONV_K - 1 - k)
        conv = conv + cw_ref[k:k + 1, :] * xbuf[lo:lo + L, :]
    xbc = conv * _sigmoid(conv)
    xs = xbc[:, :D_INNER]

    lane = lax.broadcasted_iota(jnp.int32, (L, LANES), 1)
    row = lax.broadcasted_iota(jnp.int32, (L, L), 0)
    col = lax.broadcasted_iota(jnp.int32, (L, L), 1)
    causal = row >= col

    v = dt_raw + dtb_ref[...]
    dt = jnp.maximum(v, 0.0) + jnp.log1p(jnp.exp(-jnp.abs(v)))
    dt = jnp.where(lane < SSM_HEADS, dt, 0.0)
    adt = dt * (-jnp.exp(alog_ref[...]))
    a_cs = _dot3(causal.astype(BF16), adt, True) * LOG2_E
    a_cs_t = a_cs.T
    expand = e_ref[...]
    a_full = _dot3(a_cs, expand, False)
    dt_full = _dot3(dt, expand, False)
    a_tot = a_full[L - 1:L, :]
    decay_from_start = jnp.exp2(a_full)
    decay_to_end = jnp.exp2(a_tot - a_full)
    decay_chunk = jnp.exp2(a_tot)

    xdt = xs * dt_full
    xw_b = (xdt * decay_to_end).astype(BF16)
    head_lo = lax.broadcasted_iota(jnp.int32, (L, D_INNER), 1) % (2 * SSM_HEAD_DIM) < SSM_HEAD_DIM
    xdt_lo = jnp.where(head_lo, xdt, 0.0).astype(BF16)
    xdt_hi = jnp.where(head_lo, 0.0, xdt).astype(BF16)

    b16, c16, cbs, y_offs, s_prevs = [], [], [], [], []
    for g in range(SSM_GROUPS):
        bg = xbc[:, D_INNER + g * D_STATE:D_INNER + (g + 1) * D_STATE]
        cg = xbc[:, D_INNER + (SSM_GROUPS + g) * D_STATE:D_INNER + (SSM_GROUPS + g + 1) * D_STATE]
        b16.append(bg)
        c16.append(cg.astype(BF16))
        cbs.append(lax.dot_general(c16[g], bg.astype(BF16), (((1,), (1,)), ((), ())),
                                   preferred_element_type=F32))
    for g in range(SSM_GROUPS):
        s_prevs.append(state[:, g * half:(g + 1) * half])
        y_offs.append(jnp.dot(c16[g], s_prevs[g].astype(BF16), preferred_element_type=F32))
    y_cols = []
    for g in range(SSM_GROUPS):
        for pr in range(half // LANES):
            h0 = g * hg + 2 * pr
            ps = slice(h0 * SSM_HEAD_DIM, (h0 + 2) * SSM_HEAD_DIM)
            ms = []
            for h in (h0, h0 + 1):
                seg = a_cs[:, h:h + 1] - a_cs_t[h:h + 1, :]
                ms.append((cbs[g] * jnp.exp2(jnp.where(causal, seg, NEG_BIG))).astype(BF16))
            y_cols.append(jnp.dot(jnp.concatenate(ms, axis=1),
                                  jnp.concatenate([xdt_lo[:, ps], xdt_hi[:, ps]], axis=0),
                                  preferred_element_type=F32))
    for g in range(SSM_GROUPS):
        gs = slice(g * half, (g + 1) * half)
        state[:, gs] = decay_chunk[:, gs] * s_prevs[g] + jnp.dot(
            b16[g].T.astype(BF16), xw_b[:, gs], preferred_element_type=F32)

    y = (jnp.concatenate(y_cols, axis=1) + jnp.concatenate(y_offs, axis=1) * decay_from_start
         + dskip_ref[...] * xs)
    y = y * (z * _sigmoid(z))
    normed = []
    for g in range(SSM_GROUPS):
        yg = y[:, g * half:(g + 1) * half]
        ms = jnp.mean(yg * yg, axis=-1, keepdims=True)
        normed.append(yg * lax.rsqrt(ms + NORM_EPS))
    return jnp.concatenate(normed, axis=1) * nw_ref[...]


def _ssd_body(h_ref, w_ref, cw_ref, cb_ref, dtb_ref, alog_ref, dskip_ref, nw_ref, e_ref,
              y_ref, state, xbuf, zbuf, dtbuf):
    L = SSD_CHUNK
    rt = h_ref.shape[1]
    conv_dim = D_INNER + BC_WIDTH

    @pl.when(pl.program_id(1) == 0)
    def _():
        state[...] = jnp.zeros_like(state)
        xbuf[0:8, :] = jnp.zeros((8, conv_dim), F32)

    h = h_ref[0]
    zbuf[...] = jnp.dot(h, w_ref[:, 0:D_INNER], preferred_element_type=F32)
    xbuf[8:8 + rt, :] = jnp.dot(h, w_ref[:, D_INNER:D_INNER + conv_dim], preferred_element_type=F32)
    dtbuf[...] = jnp.dot(h, w_ref[:, D_INNER + conv_dim:], preferred_element_type=F32)

    for c in range(rt // L):
        r0 = c * L
        y = _ssd_chunk(zbuf[r0:r0 + L, :], xbuf, r0, dtbuf[r0:r0 + L, :], cw_ref, cb_ref,
                       dtb_ref, alog_ref, dskip_ref, nw_ref, e_ref, state)
        y_ref[0, r0:r0 + L, :] = y.astype(BF16)
    xbuf[0:8, :] = xbuf[rt:rt + 8, :]


def _ssd(h3, w_ssd, conv_w, conv_b, dt_bias, a_log, d_skip, ssm_norm_w):
    b, s, d = h3.shape
    rt = min(4 * SSD_CHUNK, s)
    pad = LANES - SSM_HEADS
    conv_dim = D_INNER + BC_WIDTH
    dtb = jnp.pad(dt_bias, (0, pad))[None]
    alog = jnp.pad(a_log, (0, pad))[None]
    dskip = jnp.repeat(d_skip, SSM_HEAD_DIM)[None]
    expand = (np.arange(LANES)[:, None] == np.arange(D_INNER)[None, :] // SSM_HEAD_DIM).astype(np.float32)

    def const(shape):
        return pl.BlockSpec(shape, lambda i, c: (0,) * len(shape))

    return pl.pallas_call(
        _ssd_body,
        grid=(b, s // rt),
        in_specs=[pl.BlockSpec((1, rt, d), lambda i, c: (i, c, 0)),
                  const((d, SSD_PROJ_WIDTH)),
                  const((CONV_K, conv_dim)), const((1, conv_dim)),
                  const((1, LANES)), const((1, LANES)),
                  const((1, D_INNER)), const((1, D_INNER)),
                  const((LANES, D_INNER))],
        out_specs=pl.BlockSpec((1, rt, D_INNER), lambda i, c: (i, c, 0)),
        out_shape=jax.ShapeDtypeStruct((b, s, D_INNER), BF16),
        scratch_shapes=[pltpu.VMEM((D_STATE, D_INNER), F32),
                        pltpu.VMEM((rt + 8, conv_dim), F32),
                        pltpu.VMEM((rt, D_INNER), F32),
                        pltpu.VMEM((rt, LANES), F32)],
        compiler_params=_cparams(2),
    )(h3, w_ssd, conv_w, conv_b[None], dtb, alog, dskip, ssm_norm_w[None],
      jnp.asarray(expand, dtype=BF16))


def _t5_causal_bucket(dist):
    max_exact = NUM_BUCKETS // 2
    large = max_exact + (np.log(np.maximum(dist, max_exact) / max_exact)
                         / math.log(MAX_DISTANCE / max_exact) * (NUM_BUCKETS - max_exact)).astype(np.int32)
    return np.where(dist < max_exact, dist, np.minimum(large, NUM_BUCKETS - 1)).astype(np.int32)


def _band_bias(rel_bias_group, dilation):
    blk = ATTN_BLK
    off = np.arange(blk)[:, None] + blk - np.arange(2 * blk)[None, :]
    in_win = (off >= 0) & (off <= blk)
    bucket = _t5_causal_bucket(np.clip(off, 0, None) * dilation)
    onehot = (bucket.reshape(-1, 1) == np.arange(NUM_BUCKETS)[None, :]).astype(np.float32)
    bias = jnp.dot(jnp.asarray(onehot), rel_bias_group.astype(F32), precision=HIGHEST)
    bias = jnp.transpose(bias.reshape(blk, 2 * blk, HEADS_PER_GROUP), (2, 0, 1))
    bias = jnp.where(in_win[None], bias, NEG_BIG)
    return bias.reshape(HEADS_PER_GROUP // 2, 2 * blk, 2 * blk)


def _attn_body(h_ref, w_ref, bias_ref, qw_ref, kw_ref, *rest, dilation):
    n_pairs = HEADS_PER_GROUP // 2
    o_refs = rest[0:n_pairs]
    lse_ref, qkv, kbuf, vbuf = rest[n_pairs:]
    blk = ATTN_BLK
    rt = h_ref.shape[1]
    span = blk * dilation
    n_sub = rt // span
    assert n_sub == 1 or dilation == 1
    step = pl.program_id(1)

    @pl.when(step == 0)
    def _():
        kbuf[...] = jnp.zeros_like(kbuf)
        vbuf[...] = jnp.zeros_like(vbuf)

    first_head = lax.broadcasted_iota(jnp.int32, (rt, LANES), 1) < ATTN_HEAD_DIM

    def head_norm(x, w_ref):
        xx = x * x
        s0 = jnp.sum(jnp.where(first_head, xx, 0.0), axis=-1, keepdims=True)
        s1 = jnp.sum(jnp.where(first_head, 0.0, xx), axis=-1, keepdims=True)
        ss = jnp.where(first_head, s0, s1)
        return x * lax.rsqrt(ss * (1.0 / ATTN_HEAD_DIM) + NORM_EPS) * w_ref[...]

    h = h_ref[0]
    for j in range(3 * n_pairs // 2):
        piece = jnp.dot(h, w_ref[:, 2 * j * LANES:2 * (j + 1) * LANES], preferred_element_type=F32)
        for half in range(2):
            slab = piece[:, half * LANES:(half + 1) * LANES]
            if j < n_pairs // 2:
                slab = head_norm(slab, qw_ref) * (ATTN_HEAD_DIM ** -0.5)
            elif j < n_pairs:
                slab = head_norm(slab, kw_ref)
            qkv[2 * j + half] = slab

    lane = lax.broadcasted_iota(jnp.int32, (blk, LANES), 1)
    lo_half = lane < ATTN_HEAD_DIM
    nt = (((1,), (1,)), ((), ()))

    units = 2

    def block_pair(it, carry):
        pairs = range(n_pairs)
        rows, res, slot, pen, q2, k_new, v_new, k_old, v_old = [], [], [], [], [], [], [], [], []
        for u in range(units):
            blk_id = units * it + u
            sub, r = (blk_id, 0) if dilation == 1 else (0, blk_id)
            gblk = step * n_sub + sub
            rows.append(pl.ds(sub * span + r, blk, stride=dilation))
            res.append(r)
            slot.append(gblk % 2)
            pen.append(jnp.where(gblk == 0, NEG_BIG, 0.0))
            k_new.append([qkv[n_pairs + p, rows[u], :].astype(BF16) for p in pairs])
            v_new.append([qkv[2 * n_pairs + p, rows[u], :].astype(BF16) for p in pairs])
            if dilation == 1 and u > 0:
                k_old.append(k_new[u - 1])
                v_old.append(v_new[u - 1])
            else:
                k_old.append([kbuf[1 - slot[u], r * n_pairs + p] for p in pairs])
                v_old.append([vbuf[1 - slot[u], r * n_pairs + p] for p in pairs])
            q2.append([])
            for p in pairs:
                qp = qkv[p, rows[u], :]
                q2[u].append(jnp.concatenate([jnp.where(lo_half, qp, 0.0), jnp.where(lo_half, 0.0, qp)],
                                             axis=0).astype(BF16))
        todo = [(u, p) for u in range(units) for p in pairs]
        s_prev = {up: lax.dot_general(q2[up[0]][up[1]], k_old[up[0]][up[1]], nt, preferred_element_type=F32)
                  for up in todo}
        s_cur = {up: lax.dot_general(q2[up[0]][up[1]], k_new[up[0]][up[1]], nt, preferred_element_type=F32)
                 for up in todo}
        e_prev, e_cur, m, d = {}, {}, {}, {}
        for up in todo:
            u, p = up
            sp = s_prev[up] + (bias_ref[p, :, 0:blk] + pen[u])
            sc = s_cur[up] + bias_ref[p, :, blk:2 * blk]
            m[up] = jnp.max(jnp.maximum(sp, sc), axis=-1, keepdims=True)
            ep = jnp.exp(sp - m[up])
            ec = jnp.exp(sc - m[up])
            d[up] = jnp.sum(ep + ec, axis=-1, keepdims=True)
            e_prev[up] = ep.astype(BF16)
            e_cur[up] = ec.astype(BF16)
        pv_prev = {up: jnp.dot(e_prev[up], v_old[up[0]][up[1]], preferred_element_type=F32) for up in todo}
        pv_cur = {up: jnp.dot(e_cur[up], v_new[up[0]][up[1]], preferred_element_type=F32) for up in todo}
        for u in range(units):
            lse_tile = jnp.zeros((blk, LANES), F32)
            for p in pairs:
                pv = (pv_prev[u, p] + pv_cur[u, p]) / d[u, p]
                o_refs[p][0, rows[u], :] = jnp.where(lo_half, pv[0:blk], pv[blk:2 * blk])
                lse = m[u, p] + jnp.log(d[u, p])
                lse_tile = jnp.where(lane == 2 * p, lse[0:blk], lse_tile)
                lse_tile = jnp.where(lane == 2 * p + 1, lse[blk:2 * blk], lse_tile)
                kbuf[slot[u], res[u] * n_pairs + p] = k_new[u][p]
                vbuf[slot[u], res[u] * n_pairs + p] = v_new[u][p]
            lse_ref[0, rows[u], :] = lse_tile
        return carry

    lax.fori_loop(0, n_sub * dilation // units, block_pair, 0)


def _attn_group(h3, w_qkv, dilation, bias, qw, kw):
    b, s, d = h3.shape
    blk = ATTN_BLK
    n_pairs = HEADS_PER_GROUP // 2
    rt = max(4 * blk, blk * dilation)
    rt = min(rt, s)

    def const(shape):
        return pl.BlockSpec(shape, lambda i, n: (0,) * len(shape))

    token_spec = pl.BlockSpec((1, rt, LANES), lambda i, n: (i, n, 0))
    res = pl.pallas_call(
        functools.partial(_attn_body, dilation=dilation),
        grid=(b, s // rt),
        in_specs=[pl.BlockSpec((1, rt, d), lambda i, n: (i, n, 0)),
                  const((d, 3 * GROUP_WIDTH)),
                  const((n_pairs, 2 * blk, 2 * blk)), const((1, LANES)), const((1, LANES))],
        out_specs=[token_spec] * (n_pairs + 1),
        out_shape=[jax.ShapeDtypeStruct((b, s, LANES), F32)] * (n_pairs + 1),
        scratch_shapes=[pltpu.VMEM((3 * n_pairs, rt, LANES), F32),
                        pltpu.VMEM((2, dilation * n_pairs, blk, LANES), BF16),
                        pltpu.VMEM((2, dilation * n_pairs, blk, LANES), BF16)],
        compiler_params=_cparams(2),
    )(h3, w_qkv, bias, qw, kw)
    outs = [o.reshape(b * s, LANES) for o in res[:n_pairs]]
    return outs, res[n_pairs].reshape(b * s, LANES)


def _mix_body(*refs):
    n_pairs = HEADS_PER_GROUP // 2
    y_ref = refs[0]
    o_refs = refs[1:1 + 3 * n_pairs]
    l_refs = refs[1 + 3 * n_pairs:4 + 3 * n_pairs]
    (h_ref, wgate_ref, x_ref, wssm_ref, wattn_ref, wout_ref, e8_ref, nfw_ref, wr_ref, br_ref,
     x2_ref, h2_ref, lg_ref) = refs[4 + 3 * n_pairs:]
    tm = x_ref.shape[0]
    lses = [l[...] for l in l_refs]
    mx = jnp.maximum(jnp.maximum(lses[0], lses[1]), lses[2])
    es = [jnp.exp(l - mx) for l in lses]
    inv = 1.0 / (es[0] + es[1] + es[2])
    e8 = e8_ref[...]
    att = jnp.zeros((tm, GROUP_WIDTH), F32)
    for g in range(3):
        w = es[g] * inv
        w_hi = w.astype(BF16)
        w_lo = (w - w_hi.astype(F32)).astype(BF16)
        w_full = (jnp.dot(w_hi, e8, preferred_element_type=F32)
                  + jnp.dot(w_lo, e8, preferred_element_type=F32))
        o_g = jnp.concatenate([o_refs[g * n_pairs + p][...] for p in range(n_pairs)], axis=1)
        att = att + w_full * o_g
    y_attn = jnp.dot(att.astype(BF16), wattn_ref[...], preferred_element_type=F32)
    y_ssm = jnp.dot(y_ref[...], wssm_ref[...], preferred_element_type=F32)
    h = h_ref[...]
    g_ssm = jnp.dot(h, wgate_ref[:, 0:D_MODEL], preferred_element_type=F32)
    g_attn = jnp.dot(h, wgate_ref[:, D_MODEL:2 * D_MODEL], preferred_element_type=F32)
    merged = _sigmoid(g_ssm) * y_ssm + _sigmoid(g_attn) * y_attn
    x2 = x_ref[...] + jnp.dot(merged.astype(BF16), wout_ref[...], preferred_element_type=F32)
    x2_ref[...] = x2
    ms = jnp.mean(x2 * x2, axis=-1, keepdims=True)
    h2 = x2 * lax.rsqrt(ms + NORM_EPS) * nfw_ref[...]
    bits = lax.bitcast_convert_type(h2.astype(BF16).astype(F32), jnp.uint32)
    packed = bits[:, 0:D_MODEL // 2] | (bits[:, D_MODEL // 2:] >> 16)
    h2_ref[...] = packed.reshape(tm, 1, D_MODEL // 2)
    lg_ref[...] = jnp.dot(h2.astype(BF16), wr_ref[...], preferred_element_type=F32) + br_ref[...]


def _mix_out(y_ssm, attn_outs, attn_lses, h2d, w_gate, x2d, w_ssm, w_attn, w_out, norm_ffn_w, w_router, b_router):
    t = x2d.shape[0]
    tm = min(512, t)
    e8 = (np.arange(LANES)[:, None] == np.arange(GROUP_WIDTH)[None, :] // ATTN_HEAD_DIM)
    e8 = jnp.asarray(e8.astype(np.float32), dtype=BF16)

    def rows(width, cb=0):
        return pl.BlockSpec((tm, width), lambda i: (i, cb))

    def const(shape):
        return pl.BlockSpec(shape, lambda i: (0,) * len(shape), pipeline_mode=pl.Buffered(1))

    return pl.pallas_call(
        _mix_body,
        grid=(t // tm,),
        in_specs=[rows(D_INNER)] + [rows(LANES)] * (len(attn_outs) + len(attn_lses)) + [
                  rows(D_MODEL), const((D_MODEL, 2 * D_MODEL)), rows(D_MODEL),
                  const((D_INNER, D_MODEL)), const((GROUP_WIDTH, D_MODEL)), const((D_MODEL, D_MODEL)),
                  const((LANES, GROUP_WIDTH)), const((1, D_MODEL)),
                  const((D_MODEL, LANES)), const((1, LANES))],
        out_specs=[rows(D_MODEL),
                   pl.BlockSpec((tm, 1, D_MODEL // 2), lambda i: (i, 0, 0)),
                   rows(LANES)],
        out_shape=[jax.ShapeDtypeStruct((t, D_MODEL), F32),
                   jax.ShapeDtypeStruct((t, 1, D_MODEL // 2), jnp.uint32),
                   jax.ShapeDtypeStruct((t, LANES), F32)],
        compiler_params=_cparams(1),
    )(y_ssm, *attn_outs, *attn_lses, h2d, w_gate, x2d, w_ssm, w_attn, w_out, e8,
      norm_ffn_w[None], w_router, b_router)


def _route_body(lg_ref, sel_ref, gate_ref, cnt_ref, counts):
    tm = lg_ref.shape[0]
    i = pl.program_id(0)

    @pl.when(i == 0)
    def _():
        counts[...] = jnp.zeros_like(counts)

    lg = lg_ref[...]
    lane = lax.broadcasted_iota(jnp.int32, (tm, LANES), 1)
    is_coarse = lane < N_EXPERT_GROUPS
    cmax = jnp.max(jnp.where(is_coarse, lg, NEG_BIG), axis=-1, keepdims=True)
    grp = jnp.min(jnp.where(is_coarse & (lg == cmax), lane, LANES), axis=-1, keepdims=True)
    group_p = 1.0 / jnp.sum(jnp.where(is_coarse, jnp.exp(lg - cmax), 0.0), axis=-1, keepdims=True)
    f_lo = N_EXPERT_GROUPS + EXPERTS_PER_GROUP * grp
    in_grp = (lane >= f_lo) & (lane < f_lo + EXPERTS_PER_GROUP)
    f1 = jnp.max(jnp.where(in_grp, lg, NEG_BIG), axis=-1, keepdims=True)
    i1 = jnp.min(jnp.where(in_grp & (lg == f1), lane, LANES), axis=-1, keepdims=True)
    rest = in_grp & (lane != i1)
    f2 = jnp.max(jnp.where(rest, lg, NEG_BIG), axis=-1, keepdims=True)
    i2 = jnp.min(jnp.where(rest & (lg == f2), lane, LANES), axis=-1, keepdims=True)
    e2 = jnp.exp(f2 - f1)
    g1 = group_p / (1.0 + e2)
    g2 = group_p * e2 / (1.0 + e2)

    e1 = i1 - N_EXPERT_GROUPS
    e2i = i2 - N_EXPERT_GROUPS
    oh1 = lane == e1
    oh2 = lane == e2i
    onehot = jnp.where(oh1 | oh2, 1.0, 0.0)
    r = lax.broadcasted_iota(jnp.int32, (tm, tm), 0)
    c = lax.broadcasted_iota(jnp.int32, (tm, tm), 1)
    before = jnp.dot((r > c).astype(BF16), onehot.astype(BF16), preferred_element_type=F32)
    pos = counts[0:1, :] + before
    r1 = jnp.sum(jnp.where(oh1, pos, 0.0), axis=-1, keepdims=True).astype(jnp.int32)
    r2 = jnp.sum(jnp.where(oh2, pos, 0.0), axis=-1, keepdims=True).astype(jnp.int32)
    counts[...] = counts[...] + jnp.sum(onehot, axis=0, keepdims=True)
    sel_ref[...] = jnp.where(lane == 0, e1, jnp.where(lane == 1, e2i, jnp.where(lane == 2, r1,
                                                                                  jnp.where(lane == 3, r2, 0))))
    gate_ref[...] = jnp.where(lane == 0, g1, jnp.where(lane == 1, g2, 0.0))
    cnt_ref[...] = counts[...]


def _route(logits):
    t = logits.shape[0]
    tm = min(512, t)
    return pl.pallas_call(
        _route_body,
        grid=(t // tm,),
        in_specs=[pl.BlockSpec((tm, LANES), lambda i: (i, 0))],
        out_specs=[pl.BlockSpec((tm, LANES), lambda i: (i, 0)),
                   pl.BlockSpec((tm, LANES), lambda i: (i, 0)),
                   pl.BlockSpec((8, LANES), lambda i: (0, 0))],
        out_shape=[jax.ShapeDtypeStruct((t, LANES), jnp.int32),
                   jax.ShapeDtypeStruct((t, LANES), F32),
                   jax.ShapeDtypeStruct((8, LANES), F32)],
        scratch_shapes=[pltpu.VMEM((8, LANES), F32)],
        compiler_params=_cparams(1),
    )(logits)


def _dispatch_body(meta_ref, dest_ref, h_ref, rows_ref, zbuf, zsem, sem):
    tm = h_ref.shape[0]

    def zero_copy(e):
        start = pl.multiple_of(meta_ref[e] - MOE_BLOCK, MOE_BLOCK)
        return pltpu.make_async_copy(zbuf, rows_ref.at[pl.ds(start, MOE_BLOCK)], zsem)

    def tail_copy(blk):
        start = pl.multiple_of(blk * MOE_BLOCK, MOE_BLOCK)
        return pltpu.make_async_copy(zbuf, rows_ref.at[pl.ds(start, MOE_BLOCK)], zsem)

    @pl.when(pl.program_id(0) == 0)
    def _():
        zbuf[...] = jnp.zeros_like(zbuf)

        def start_zero(e, carry):
            @pl.when(meta_ref[N_EXPERTS + e] > 0)
            def _():
                zero_copy(e).start()
            return carry

        def wait_zero(e, carry):
            @pl.when(meta_ref[N_EXPERTS + e] > 0)
            def _():
                zero_copy(e).wait()
            return carry

        def start_tail(blk, carry):
            tail_copy(blk).start()
            return carry

        def wait_tail(blk, carry):
            tail_copy(blk).wait()
            return carry

        n_used = meta_ref[2 * N_EXPERTS]
        n_blocks = rows_ref.shape[0] // MOE_BLOCK
        lax.fori_loop(0, N_EXPERTS, start_zero, 0)
        lax.fori_loop(n_used, n_blocks, start_tail, 0)
        lax.fori_loop(0, N_EXPERTS, wait_zero, 0)
        lax.fori_loop(n_used, n_blocks, wait_tail, 0)

    def issue(j, carry):
        for k in range(2):
            d = dest_ref[0, 2 * j + k]
            pltpu.make_async_copy(h_ref.at[j], rows_ref.at[d], sem).start(priority=k)
        return carry

    lax.fori_loop(0, tm, issue, 0, unroll=8)
    for _ in range(2):
        pltpu.make_async_copy(h_ref, rows_ref.at[pl.ds(0, tm)], sem).wait()


def _dispatch(meta, dest3, h2, n_rows):
    t = h2.shape[0]
    tm = dest3.shape[2] // 2
    return pl.pallas_call(
        _dispatch_body,
        grid_spec=pltpu.PrefetchScalarGridSpec(
            num_scalar_prefetch=1,
            grid=(t // tm,),
            in_specs=[pl.BlockSpec((None, 1, 2 * tm), lambda i, m: (i, 0, 0), memory_space=pltpu.SMEM),
                      pl.BlockSpec((tm, 1, D_MODEL // 2), lambda i, m: (i, 0, 0))],
            out_specs=pl.BlockSpec(memory_space=pl.ANY),
            scratch_shapes=[pltpu.VMEM((MOE_BLOCK, 1, D_MODEL // 2), jnp.uint32),
                            pltpu.SemaphoreType.DMA(()),
                            pltpu.SemaphoreType.DMA(())]),
        out_shape=jax.ShapeDtypeStruct((n_rows, 1, D_MODEL // 2), jnp.uint32),
        compiler_params=_cparams(1),
    )(meta, dest3, h2)


def _expert_body(sched_ref, x_ref, wg_hbm, wu_hbm, wd_hbm, y_ref, x2d, wg_f, wu_f, wd_f, wg_b, wu_b, wd_b, sems):
    blk = pl.program_id(0)
    n_used = sched_ref[4, 0]

    def weight_copies(expert, slot):
        return [pltpu.make_async_copy(src.at[expert], dst.at[slot], sems.at[slot])
                for src, dst in ((wg_hbm, wg_f), (wu_hbm, wu_f), (wd_hbm, wd_f))]

    @pl.when(blk < n_used)
    def _():
        expert = sched_ref[0, blk]
        slot = sched_ref[2, blk]
        nxt = sched_ref[3, blk]

        @pl.when(sched_ref[1, blk] == 1)
        def _():
            @pl.when(blk == 0)
            def _():
                for c in weight_copies(expert, slot):
                    c.start()

            for c in weight_copies(expert, slot):
                c.wait()

            @pl.when(nxt >= 0)
            def _():
                for c in weight_copies(nxt, 1 - slot):
                    c.start()

            wg_b[...] = wg_f[slot].astype(BF16)
            wu_b[...] = wu_f[slot].astype(BF16)
            wd_b[...] = wd_f[slot].astype(BF16)

        x2d[...] = x_ref[...].reshape(MOE_BLOCK, D_MODEL // 2)
        words = x2d[...]
        x = jnp.concatenate([lax.bitcast_convert_type(words & jnp.uint32(0xFFFF0000), F32),
                             lax.bitcast_convert_type(words << 16, F32)], axis=1).astype(BF16)
        g = jnp.dot(x, wg_b[...], preferred_element_type=F32)
        u = jnp.dot(x, wu_b[...], preferred_element_type=F32)
        hid = (g * _sigmoid(g)) * u
        y = jnp.dot(hid.astype(BF16), wd_b[...], preferred_element_type=F32)
        bits = lax.bitcast_convert_type(y.astype(BF16).astype(F32), jnp.uint32)
        packed = bits[:, 0:D_MODEL // 2] | (bits[:, D_MODEL // 2:] >> 16)
        y_ref[...] = packed.reshape(MOE_BLOCK, 1, D_MODEL // 2)

    @pl.when(blk >= n_used)
    def _():
        y_ref[...] = jnp.zeros_like(y_ref)


def _experts(sched, rows, w_gate, w_up, w_down):
    n_rows = rows.shape[0]
    n_blocks = n_rows // MOE_BLOCK

    def row_map(b, sched):
        return (b, 0, 0)

    any_spec = pl.BlockSpec(memory_space=pl.ANY)
    return pl.pallas_call(
        _expert_body,
        grid_spec=pltpu.PrefetchScalarGridSpec(
            num_scalar_prefetch=1,
            grid=(n_blocks,),
            in_specs=[pl.BlockSpec((MOE_BLOCK, 1, D_MODEL // 2), row_map), any_spec, any_spec, any_spec],
            out_specs=pl.BlockSpec((MOE_BLOCK, 1, D_MODEL // 2), row_map),
            scratch_shapes=[pltpu.VMEM((MOE_BLOCK, D_MODEL // 2), jnp.uint32),
                            pltpu.VMEM((2, D_MODEL, D_EXPERT), F32),
                            pltpu.VMEM((2, D_MODEL, D_EXPERT), F32),
                            pltpu.VMEM((2, D_EXPERT, D_MODEL), F32),
                            pltpu.VMEM((D_MODEL, D_EXPERT), BF16),
                            pltpu.VMEM((D_MODEL, D_EXPERT), BF16),
                            pltpu.VMEM((D_EXPERT, D_MODEL), BF16),
                            pltpu.SemaphoreType.DMA((2,))]),
        out_shape=jax.ShapeDtypeStruct((n_rows, 1, D_MODEL // 2), jnp.uint32),
        compiler_params=_cparams(1),
    )(sched, rows, w_gate, w_up, w_down)


def _combine_body(dcur_ref, dnext_ref, gate_ref, x2_ref, y_ref, o_ref, buf_a, buf_b, y2d, sem_a, sem_b):
    tm = x2_ref.shape[0]
    i = pl.program_id(0)
    n = pl.num_programs(0)

    def issue(dref, buf, sem):
        def body(j, carry):
            for k in range(2):
                d = dref[0, 2 * j + k]
                pltpu.make_async_copy(y_ref.at[d], buf.at[k * tm + j], sem).start(priority=k)
            return carry
        lax.fori_loop(0, tm, body, 0, unroll=8)

    def finish(buf, sem):
        pltpu.make_async_copy(y_ref.at[pl.ds(0, 2 * tm)], buf, sem).wait()
        y2d[...] = buf[...].reshape(2 * tm, D_MODEL // 2)
        words = y2d[...]
        y = jnp.concatenate([lax.bitcast_convert_type(words & jnp.uint32(0xFFFF0000), F32),
                             lax.bitcast_convert_type(words << 16, F32)], axis=1)
        g = gate_ref[...]
        o_ref[...] = x2_ref[...] + g[:, 0:1] * y[0:tm, :] + g[:, 1:2] * y[tm:2 * tm, :]

    @pl.when(i == 0)
    def _():
        issue(dcur_ref, buf_a, sem_a)

    for par, (cur, cur_sem, nxt, nxt_sem) in enumerate(((buf_a, sem_a, buf_b, sem_b),
                                                        (buf_b, sem_b, buf_a, sem_a))):
        @pl.when(i % 2 == par)
        def _(cur=cur, cur_sem=cur_sem, nxt=nxt, nxt_sem=nxt_sem):
            @pl.when(i + 1 < n)
            def _():
                issue(dnext_ref, nxt, nxt_sem)
            finish(cur, cur_sem)


def _combine(dest3, gates, x2, y_rows):
    t = x2.shape[0]
    tm = dest3.shape[2] // 2
    nt = t // tm
    return pl.pallas_call(
        _combine_body,
        grid=(nt,),
        in_specs=[pl.BlockSpec((None, 1, 2 * tm), lambda i: (i, 0, 0), memory_space=pltpu.SMEM),
                  pl.BlockSpec((None, 1, 2 * tm), lambda i: (jnp.minimum(i + 1, nt - 1), 0, 0),
                               memory_space=pltpu.SMEM),
                  pl.BlockSpec((tm, LANES), lambda i: (i, 0)),
                  pl.BlockSpec((tm, D_MODEL), lambda i: (i, 0)),
                  pl.BlockSpec(memory_space=pl.ANY)],
        out_specs=pl.BlockSpec((tm, D_MODEL), lambda i: (i, 0)),
        out_shape=jax.ShapeDtypeStruct((t, D_MODEL), F32),
        scratch_shapes=[pltpu.VMEM((2 * tm, 1, D_MODEL // 2), jnp.uint32),
                        pltpu.VMEM((2 * tm, 1, D_MODEL // 2), jnp.uint32),
                        pltpu.VMEM((2 * tm, D_MODEL // 2), jnp.uint32),
                        pltpu.SemaphoreType.DMA(()),
                        pltpu.SemaphoreType.DMA(())],
        compiler_params=_cparams(1),
    )(dest3, dest3, gates, x2, y_rows)


def _layer(x, norm_mix_w, w_in, conv_w, conv_b, dt_bias, a_log, d_skip, ssm_norm_w, w_ssm_proj,
           q_norm_w, k_norm_w, rel_bias, w_attn_proj, w_out, norm_ffn_w, w_coarse, b_coarse,
           w_fine, b_fine, w_gate_exp, w_up_exp, w_down_exp):
    b, s, d = x.shape
    t = b * s
    x2d = x.reshape(t, d)

    dt_lo = 2 * D_INNER + BC_WIDTH
    qkv_lo = dt_lo + SSM_HEADS
    gate_lo = qkv_lo + 3 * ATTN_WIDTH
    w_ssd = jnp.concatenate([w_in[:, :qkv_lo], jnp.zeros((d, LANES - SSM_HEADS), w_in.dtype)],
                            axis=1).astype(BF16)
    w_gate = w_in[:, gate_lo:].astype(BF16)
    h2d = _prenorm(x2d, norm_mix_w[None])
    h3 = h2d.reshape(b, s, d)

    y_ssm = _ssd(h3, w_ssd, conv_w, conv_b, dt_bias, a_log, d_skip, ssm_norm_w).reshape(t, D_INNER)

    qw = jnp.tile(q_norm_w, 2)[None]
    kw = jnp.tile(k_norm_w, 2)[None]
    attn_outs, attn_lses = [], []
    for gi, (window, dilation) in enumerate(DILATED_CONFIGS):
        assert window // dilation == ATTN_BLK and s % window == 0
        hs = slice(gi * GROUP_WIDTH, (gi + 1) * GROUP_WIDTH)
        w_qkv = jnp.concatenate([w_in[:, qkv_lo + j * ATTN_WIDTH:qkv_lo + (j + 1) * ATTN_WIDTH][:, hs]
                                 for j in range(3)], axis=1).astype(BF16)
        bias = _band_bias(rel_bias[:, gi * HEADS_PER_GROUP:(gi + 1) * HEADS_PER_GROUP], dilation)
        o, l = _attn_group(h3, w_qkv, dilation, bias, qw, kw)
        attn_outs.extend(o)
        attn_lses.append(l)

    n_route = N_EXPERT_GROUPS + N_EXPERTS
    w_router = jnp.pad(jnp.concatenate([w_coarse, w_fine], axis=1), ((0, 0), (0, LANES - n_route)))
    b_router = jnp.pad(jnp.concatenate([b_coarse, b_fine]), (0, LANES - n_route))[None]
    x2, h2, logits = _mix_out(y_ssm, attn_outs, attn_lses, h2d, w_gate, x2d, w_ssm_proj.astype(BF16),
                              w_attn_proj.astype(BF16), w_out.astype(BF16), norm_ffn_w,
                              w_router.astype(BF16), b_router)

    sel, gates, counts = _route(logits)

    cnt = counts[0, :N_EXPERTS].astype(jnp.int32)
    padded = (cnt + MOE_BLOCK - 1) // MOE_BLOCK * MOE_BLOCK
    pad_end = jnp.cumsum(padded)
    n_blocks = -(-(2 * t + N_EXPERTS * (MOE_BLOCK - 1)) // MOE_BLOCK)
    block_start = jnp.arange(n_blocks, dtype=jnp.int32) * MOE_BLOCK
    block_expert = jnp.minimum(jnp.sum((pad_end[None, :] <= block_start[:, None]).astype(jnp.int32), axis=1),
                               N_EXPERTS - 1)
    n_used = (pad_end[-1:] // MOE_BLOCK).astype(jnp.int32)
    meta = jnp.concatenate([pad_end, padded, n_used]).astype(jnp.int32)
    chosen = sel[:, 0:2, None] == jnp.arange(N_EXPERTS, dtype=jnp.int32)
    dest = jnp.sum(jnp.where(chosen, pad_end - padded, 0), axis=-1) + sel[:, 2:4]
    blk_ids = jnp.arange(n_blocks, dtype=jnp.int32)
    first = ((blk_ids == 0) | (block_expert != jnp.roll(block_expert, 1))) & (blk_ids < n_used[0])
    slot = (jnp.cumsum(first.astype(jnp.int32)) - 1) % 2
    nxt_blk = blk_ids + padded[block_expert] // MOE_BLOCK
    nxt = jnp.where(nxt_blk < n_used[0], block_expert[jnp.minimum(nxt_blk, n_blocks - 1)], -1)
    sched = jnp.stack([block_expert, first.astype(jnp.int32), slot, nxt,
                       jnp.broadcast_to(n_used, (n_blocks,))]).astype(jnp.int32)

    tm_d = min(256, t)
    dest_d = dest.reshape(t // tm_d, 1, 2 * tm_d)
    rows = _dispatch(meta, dest_d, h2, n_blocks * MOE_BLOCK)
    y_rows = _experts(sched, rows, w_gate_exp, w_up_exp, w_down_exp)
    tm_c = min(128, t)
    dest_c = dest.reshape(t // tm_c, 1, 2 * tm_c)
    out = _combine(dest_c, gates, x2, y_rows)
    return out.reshape(b, s, d)


def kernel(x, norm_mix_w, w_in, conv_w, conv_b, dt_bias, a_log, d_skip, ssm_norm_w, w_ssm_proj,
           q_norm_w, k_norm_w, rel_bias, w_attn_proj, w_out, norm_ffn_w, w_coarse, b_coarse,
           w_fine, b_fine, w_gate_exp, w_up_exp, w_down_exp):
    depth = norm_mix_w.shape[0]
    for layer in range(depth):
        x = _layer(x, norm_mix_w[layer], w_in[layer], conv_w[layer], conv_b[layer], dt_bias[layer],
                   a_log[layer], d_skip[layer], ssm_norm_w[layer], w_ssm_proj[layer],
                   q_norm_w[layer], k_norm_w[layer], rel_bias, w_attn_proj[layer], w_out[layer],
                   norm_ffn_w[layer], w_coarse[layer], b_coarse[layer], w_fine[layer], b_fine[layer],
                   w_gate_exp[layer], w_up_exp[layer], w_down_exp[layer])
    return x
```

```python
import functools
import math

import jax
import jax.numpy as jnp
import numpy as np
from jax import lax
from jax.experimental import pallas as pl
from jax.experimental.pallas import tpu as pltpu

F32 = jnp.float32
BF16 = jnp.bfloat16
HIGHEST = lax.Precision.HIGHEST

LANES = 128
NORM_EPS = 1e-6
NEG_BIG = -1e30
LOG2_E = math.log2(math.e)

D_MODEL = 1024
D_INNER = 2048
SSM_HEAD_DIM = 64
SSM_HEADS = 32
SSM_GROUPS = 2
D_STATE = 128
CONV_K = 4
BC_WIDTH = 2 * SSM_GROUPS * D_STATE
SSD_CHUNK = 128
ATTN_HEAD_DIM = 64
DILATED_CONFIGS = ((128, 1), (512, 4), (2048, 16))
HEADS_PER_GROUP = 8
GROUP_WIDTH = HEADS_PER_GROUP * ATTN_HEAD_DIM
ATTN_WIDTH = 3 * GROUP_WIDTH
ATTN_BLK = 128
NUM_BUCKETS = 32
MAX_DISTANCE = 2048
N_EXPERT_GROUPS = 8
EXPERTS_PER_GROUP = 8
N_EXPERTS = 64
D_EXPERT = 512
MOE_BLOCK = 256

SSD_PROJ_WIDTH = 2 * D_INNER + BC_WIDTH + LANES

VMEM_LIMIT = 56 * 1024 * 1024


def _sigmoid(x):
    return 1.0 / (1.0 + jnp.exp(-x))


def _cparams(n_axes):
    return pltpu.CompilerParams(dimension_semantics=("arbitrary",) * n_axes,
                                vmem_limit_bytes=VMEM_LIMIT)


def _dot3(x, y, x_is_exact):
    v = y if x_is_exact else x
    hi = v.astype(BF16)
    r1 = v - hi.astype(F32)
    mid = r1.astype(BF16)
    lo = (r1 - mid.astype(F32)).astype(BF16)
    if x_is_exact:
        return sum(jnp.dot(x, part, preferred_element_type=F32) for part in (hi, mid, lo))
    return sum(jnp.dot(part, y, preferred_element_type=F32) for part in (hi, mid, lo))


def _ssd_chunk(z, xbuf, r0, dt_raw, cw_ref, cb_ref, dtb_ref, alog_ref, dskip_ref, nw_ref, e_ref, state):
    L = SSD_CHUNK
    half = D_INNER // SSM_GROUPS
    hg = SSM_HEADS // SSM_GROUPS

    conv = cb_ref[...] + cw_ref[CONV_K - 1:CONV_K, :] * xbuf[8 + r0:8 + r0 + L, :]
    for k in range(CONV_K - 1):
        lo = 8 + r0 - (CONV_K - 1 - k)
        conv = conv + cw_ref[k:k + 1, :] * xbuf[lo:lo + L, :]
    xbc = conv * _sigmoid(conv)
    xs = xbc[:, :D_INNER]

    lane = lax.broadcasted_iota(jnp.int32, (L, LANES), 1)
    row = lax.broadcasted_iota(jnp.int32, (L, L), 0)
    col = lax.broadcasted_iota(jnp.int32, (L, L), 1)
    causal = row >= col

    v = dt_raw + dtb_ref[...]
    dt = jnp.maximum(v, 0.0) + jnp.log1p(jnp.exp(-jnp.abs(v)))
    dt = jnp.where(lane < SSM_HEADS, dt, 0.0)
    adt = dt * (-jnp.exp(alog_ref[...]))
    a_cs = _dot3(causal.astype(BF16), adt, True) * LOG2_E
    a_cs_t = a_cs.T
    expand = e_ref[...]
    a_full = _dot3(a_cs, expand, False)
    dt_full = _dot3(dt, expand, False)
    a_tot = a_full[L - 1:L, :]
    decay_from_start = jnp.exp2(a_full)
    decay_to_end = jnp.exp2(a_tot - a_full)
    decay_chunk = jnp.exp2(a_tot)

    xdt = xs * dt_full
    xw_b = (xdt * decay_to_end).astype(BF16)
    head_lo = lax.broadcasted_iota(jnp.int32, (L, D_INNER), 1) % (2 * SSM_HEAD_DIM) < SSM_HEAD_DIM
    xdt_lo = jnp.where(head_lo, xdt, 0.0).astype(BF16)
    xdt_hi = jnp.where(head_lo, 0.0, xdt).astype(BF16)

    b16, c16, cbs, y_offs, s_prevs = [], [], [], [], []
    for g in range(SSM_GROUPS):
        bg = xbc[:, D_INNER + g * D_STATE:D_INNER + (g + 1) * D_STATE]
        cg = xbc[:, D_INNER + (SSM_GROUPS + g) * D_STATE:D_INNER + (SSM_GROUPS + g + 1) * D_STATE]
        b16.append(bg)
        c16.append(cg.astype(BF16))
        cbs.append(lax.dot_general(c16[g], bg.astype(BF16), (((1,), (1,)), ((), ())),
                                   preferred_element_type=F32))
    for g in range(SSM_GROUPS):
        s_prevs.append(state[:, g * half:(g + 1) * half])
        y_offs.append(jnp.dot(c16[g], s_prevs[g].astype(BF16), preferred_element_type=F32))
    y_cols = []
    for g in range(SSM_GROUPS):
        for pr in range(half // LANES):
            h0 = g * hg + 2 * pr
            ps = slice(h0 * SSM_HEAD_DIM, (h0 + 2) * SSM_HEAD_DIM)
            ms = []
            for h in (h0, h0 + 1):
                seg = a_cs[:, h:h + 1] - a_cs_t[h:h + 1, :]
                ms.append((cbs[g] * jnp.exp2(jnp.where(causal, seg, NEG_BIG))).astype(BF16))
            y_cols.append(jnp.dot(jnp.concatenate(ms, axis=1),
                                  jnp.concatenate([xdt_lo[:, ps], xdt_hi[:, ps]], axis=0),
                                  preferred_element_type=F32))
    for g in range(SSM_GROUPS):
        gs = slice(g * half, (g + 1) * half)
        state[:, gs] = decay_chunk[:, gs] * s_prevs[g] + jnp.dot(
            b16[g].T.astype(BF16), xw_b[:, gs], preferred_element_type=F32)

    y = (jnp.concatenate(y_cols, axis=1) + jnp.concatenate(y_offs, axis=1) * decay_from_start
         + dskip_ref[...] * xs)
    y = y * (z * _sigmoid(z))
    normed = []
    for g in range(SSM_GROUPS):
        yg = y[:, g * half:(g + 1) * half]
        ms = jnp.mean(yg * yg, axis=-1, keepdims=True)
        normed.append(yg * lax.rsqrt(ms + NORM_EPS))
    return jnp.concatenate(normed, axis=1) * nw_ref[...]


def _ssd_body(x_ref, nmw_ref, w_ref, cw_ref, cb_ref, dtb_ref, alog_ref, dskip_ref, nw_ref, e_ref,
              y_ref, h_ref, state, xbuf, zbuf, dtbuf):
    L = SSD_CHUNK
    rt = x_ref.shape[1]
    conv_dim = D_INNER + BC_WIDTH

    @pl.when(pl.program_id(1) == 0)
    def _():
        state[...] = jnp.zeros_like(state)
        xbuf[0:8, :] = jnp.zeros((8, conv_dim), F32)

    x = x_ref[0]
    ms = jnp.mean(x * x, axis=-1, keepdims=True)
    h = (x * lax.rsqrt(ms + NORM_EPS) * nmw_ref[...]).astype(BF16)
    h_ref[0] = h
    zbuf[...] = jnp.dot(h, w_ref[:, 0:D_INNER], preferred_element_type=F32)
    xbuf[8:8 + rt, :] = jnp.dot(h, w_ref[:, D_INNER:D_INNER + conv_dim], preferred_element_type=F32)
    dtbuf[...] = jnp.dot(h, w_ref[:, D_INNER + conv_dim:], preferred_element_type=F32)

    for c in range(rt // L):
        r0 = c * L
        y = _ssd_chunk(zbuf[r0:r0 + L, :], xbuf, r0, dtbuf[r0:r0 + L, :], cw_ref, cb_ref,
                       dtb_ref, alog_ref, dskip_ref, nw_ref, e_ref, state)
        y_ref[0, r0:r0 + L, :] = y.astype(BF16)
    xbuf[0:8, :] = xbuf[rt:rt + 8, :]


def _ssd(x, norm_mix_w, w_ssd, conv_w, conv_b, dt_bias, a_log, d_skip, ssm_norm_w):
    b, s, d = x.shape
    rt = min(4 * SSD_CHUNK, s)
    pad = LANES - SSM_HEADS
    conv_dim = D_INNER + BC_WIDTH
    dtb = jnp.pad(dt_bias, (0, pad))[None]
    alog = jnp.pad(a_log, (0, pad))[None]
    dskip = jnp.repeat(d_skip, SSM_HEAD_DIM)[None]
    expand = (np.arange(LANES)[:, None] == np.arange(D_INNER)[None, :] // SSM_HEAD_DIM).astype(np.float32)

    def const(shape):
        return pl.BlockSpec(shape, lambda i, c: (0,) * len(shape))

    return pl.pallas_call(
        _ssd_body,
        grid=(b, s // rt),
        in_specs=[pl.BlockSpec((1, rt, d), lambda i, c: (i, c, 0)),
                  const((1, d)), const((d, SSD_PROJ_WIDTH)),
                  const((CONV_K, conv_dim)), const((1, conv_dim)),
                  const((1, LANES)), const((1, LANES)),
                  const((1, D_INNER)), const((1, D_INNER)),
                  const((LANES, D_INNER))],
        out_specs=[pl.BlockSpec((1, rt, D_INNER), lambda i, c: (i, c, 0)),
                   pl.BlockSpec((1, rt, d), lambda i, c: (i, c, 0))],
        out_shape=[jax.ShapeDtypeStruct((b, s, D_INNER), BF16),
                   jax.ShapeDtypeStruct((b, s, d), BF16)],
        scratch_shapes=[pltpu.VMEM((D_STATE, D_INNER), F32),
                        pltpu.VMEM((rt + 8, conv_dim), F32),
                        pltpu.VMEM((rt, D_INNER), F32),
                        pltpu.VMEM((rt, LANES), F32)],
        compiler_params=_cparams(2),
    )(x, norm_mix_w[None], w_ssd, conv_w, conv_b[None], dtb, alog, dskip, ssm_norm_w[None],
      jnp.asarray(expand, dtype=BF16))


def _t5_causal_bucket(dist):
    max_exact = NUM_BUCKETS // 2
    large = max_exact + (np.log(np.maximum(dist, max_exact) / max_exact)
                         / math.log(MAX_DISTANCE / max_exact) * (NUM_BUCKETS - max_exact)).astype(np.int32)
    return np.where(dist < max_exact, dist, np.minimum(large, NUM_BUCKETS - 1)).astype(np.int32)


def _band_bias(rel_bias_group, dilation):
    blk = ATTN_BLK
    off = np.arange(blk)[:, None] + blk - np.arange(2 * blk)[None, :]
    in_win = (off >= 0) & (off <= blk)
    bucket = _t5_causal_bucket(np.clip(off, 0, None) * dilation)
    onehot = (bucket.reshape(-1, 1) == np.arange(NUM_BUCKETS)[None, :]).astype(np.float32)
    bias = jnp.dot(jnp.asarray(onehot), rel_bias_group.astype(F32), precision=HIGHEST)
    bias = jnp.transpose(bias.reshape(blk, 2 * blk, HEADS_PER_GROUP), (2, 0, 1))
    bias = jnp.where(in_win[None], bias, NEG_BIG)
    return bias.reshape(HEADS_PER_GROUP // 2, 2 * blk, 2 * blk)


def _attn_body(h_ref, w_ref, bias_ref, qw_ref, kw_ref, *rest, dilation):
    n_pairs = HEADS_PER_GROUP // 2
    o_refs = rest[0:n_pairs]
    lse_ref, qkv, kbuf, vbuf = rest[n_pairs:]
    blk = ATTN_BLK
    rt = h_ref.shape[1]
    span = blk * dilation
    n_sub = rt // span
    assert n_sub == 1 or dilation == 1
    step = pl.program_id(1)

    @pl.when(step == 0)
    def _():
        kbuf[...] = jnp.zeros_like(kbuf)
        vbuf[...] = jnp.zeros_like(vbuf)

    first_head = lax.broadcasted_iota(jnp.int32, (rt, LANES), 1) < ATTN_HEAD_DIM

    def head_norm(x, w_ref):
        xx = x * x
        s0 = jnp.sum(jnp.where(first_head, xx, 0.0), axis=-1, keepdims=True)
        s1 = jnp.sum(jnp.where(first_head, 0.0, xx), axis=-1, keepdims=True)
        ss = jnp.where(first_head, s0, s1)
        return x * lax.rsqrt(ss * (1.0 / ATTN_HEAD_DIM) + NORM_EPS) * w_ref[...]

    h = h_ref[0]
    for j in range(3 * n_pairs // 2):
        piece = jnp.dot(h, w_ref[:, 2 * j * LANES:2 * (j + 1) * LANES], preferred_element_type=F32)
        for half in range(2):
            slab = piece[:, half * LANES:(half + 1) * LANES]
            if j < n_pairs // 2:
                slab = head_norm(slab, qw_ref) * (ATTN_HEAD_DIM ** -0.5)
            elif j < n_pairs:
                slab = head_norm(slab, kw_ref)
            qkv[2 * j + half] = slab

    lane = lax.broadcasted_iota(jnp.int32, (blk, LANES), 1)
    lo_half = lane < ATTN_HEAD_DIM
    nt = (((1,), (1,)), ((), ()))

    units = 2

    def block_pair(it, carry):
        pairs = range(n_pairs)
        rows, res, slot, pen, q2, k_new, v_new, k_old, v_old = [], [], [], [], [], [], [], [], []
        for u in range(units):
            blk_id = units * it + u
            sub, r = (blk_id, 0) if dilation == 1 else (0, blk_id)
            gblk = step * n_sub + sub
            rows.append(pl.ds(sub * span + r, blk, stride=dilation))
            res.append(r)
            slot.append(gblk % 2)
            pen.append(jnp.where(gblk == 0, NEG_BIG, 0.0))
            k_new.append([qkv[n_pairs + p, rows[u], :].astype(BF16) for p in pairs])
            v_new.append([qkv[2 * n_pairs + p, rows[u], :].astype(BF16) for p in pairs])
            if dilation == 1 and u > 0:
                k_old.append(k_new[u - 1])
                v_old.append(v_new[u - 1])
            else:
                k_old.append([kbuf[1 - slot[u], r * n_pairs + p] for p in pairs])
                v_old.append([vbuf[1 - slot[u], r * n_pairs + p] for p in pairs])
            q2.append([])
            for p in pairs:
                qp = qkv[p, rows[u], :]
                q2[u].append(jnp.concatenate([jnp.where(lo_half, qp, 0.0), jnp.where(lo_half, 0.0, qp)],
                                             axis=0).astype(BF16))
        todo = [(u, p) for u in range(units) for p in pairs]
        s_prev = {up: lax.dot_general(q2[up[0]][up[1]], k_old[up[0]][up[1]], nt, preferred_element_type=F32)
                  for up in todo}
        s_cur = {up: lax.dot_general(q2[up[0]][up[1]], k_new[up[0]][up[1]], nt, preferred_element_type=F32)
                 for up in todo}
        e_prev, e_cur, m, d = {}, {}, {}, {}
        for up in todo:
            u, p = up
            sp = s_prev[up] + (bias_ref[p, :, 0:blk] + pen[u])
            sc = s_cur[up] + bias_ref[p, :, blk:2 * blk]
            m[up] = jnp.max(jnp.maximum(sp, sc), axis=-1, keepdims=True)
            ep = jnp.exp(sp - m[up])
            ec = jnp.exp(sc - m[up])
            d[up] = jnp.sum(ep + ec, axis=-1, keepdims=True)
            e_prev[up] = ep.astype(BF16)
            e_cur[up] = ec.astype(BF16)
        pv_prev = {up: jnp.dot(e_prev[up], v_old[up[0]][up[1]], preferred_element_type=F32) for up in todo}
        pv_cur = {up: jnp.dot(e_cur[up], v_new[up[0]][up[1]], preferred_element_type=F32) for up in todo}
        for u in range(units):
            lse_tile = jnp.zeros((blk, LANES), F32)
            for p in pairs:
                pv = (pv_prev[u, p] + pv_cur[u, p]) / d[u, p]
                o_refs[p][0, rows[u], :] = jnp.where(lo_half, pv[0:blk], pv[blk:2 * blk])
                lse = m[u, p] + jnp.log(d[u, p])
                lse_tile = jnp.where(lane == 2 * p, lse[0:blk], lse_tile)
                lse_tile = jnp.where(lane == 2 * p + 1, lse[blk:2 * blk], lse_tile)
                kbuf[slot[u], res[u] * n_pairs + p] = k_new[u][p]
                vbuf[slot[u], res[u] * n_pairs + p] = v_new[u][p]
            lse_ref[0, rows[u], :] = lse_tile
        return carry

    lax.fori_loop(0, n_sub * dilation // units, block_pair, 0)


def _attn_group(h3, w_qkv, dilation, bias, qw, kw):
    b, s, d = h3.shape
    blk = ATTN_BLK
    n_pairs = HEADS_PER_GROUP // 2
    rt = max(4 * blk, blk * dilation)
    rt = min(rt, s)

    def const(shape):
        return pl.BlockSpec(shape, lambda i, n: (0,) * len(shape))

    token_spec = pl.BlockSpec((1, rt, LANES), lambda i, n: (i, n, 0))
    res = pl.pallas_call(
        functools.partial(_attn_body, dilation=dilation),
        grid=(b, s // rt),
        in_specs=[pl.BlockSpec((1, rt, d), lambda i, n: (i, n, 0)),
                  const((d, 3 * GROUP_WIDTH)),
                  const((n_pairs, 2 * blk, 2 * blk)), const((1, LANES)), const((1, LANES))],
        out_specs=[token_spec] * (n_pairs + 1),
        out_shape=[jax.ShapeDtypeStruct((b, s, LANES), F32)] * (n_pairs + 1),
        scratch_shapes=[pltpu.VMEM((3 * n_pairs, rt, LANES), F32),
                        pltpu.VMEM((2, dilation * n_pairs, blk, LANES), BF16),
                        pltpu.VMEM((2, dilation * n_pairs, blk, LANES), BF16)],
        compiler_params=_cparams(2),
    )(h3, w_qkv, bias, qw, kw)
    outs = [o.reshape(b * s, LANES) for o in res[:n_pairs]]
    return outs, res[n_pairs].reshape(b * s, LANES)


def _mix_body(*refs):
    n_pairs = HEADS_PER_GROUP // 2
    y_ref = refs[0]
    o_refs = refs[1:1 + 3 * n_pairs]
    l_refs = refs[1 + 3 * n_pairs:4 + 3 * n_pairs]
    (h_ref, wgate_ref, x_ref, wssm_ref, wattn_ref, wout_ref, e8_ref, nfw_ref, wr_ref, br_ref,
     x2_ref, h2_ref, lg_ref) = refs[4 + 3 * n_pairs:]
    tm = x_ref.shape[0]
    lses = [l[...] for l in l_refs]
    mx = jnp.maximum(jnp.maximum(lses[0], lses[1]), lses[2])
    es = [jnp.exp(l - mx) for l in lses]
    inv = 1.0 / (es[0] + es[1] + es[2])
    e8 = e8_ref[...]
    att = jnp.zeros((tm, GROUP_WIDTH), F32)
    for g in range(3):
        w = es[g] * inv
        w_hi = w.astype(BF16)
        w_lo = (w - w_hi.astype(F32)).astype(BF16)
        w_full = (jnp.dot(w_hi, e8, preferred_element_type=F32)
                  + jnp.dot(w_lo, e8, preferred_element_type=F32))
        o_g = jnp.concatenate([o_refs[g * n_pairs + p][...] for p in range(n_pairs)], axis=1)
        att = att + w_full * o_g
    y_attn = jnp.dot(att.astype(BF16), wattn_ref[...], preferred_element_type=F32)
    y_ssm = jnp.dot(y_ref[...], wssm_ref[...], preferred_element_type=F32)
    h = h_ref[...]
    g_ssm = jnp.dot(h, wgate_ref[:, 0:D_MODEL], preferred_element_type=F32)
    g_attn = jnp.dot(h, wgate_ref[:, D_MODEL:2 * D_MODEL], preferred_element_type=F32)
    merged = _sigmoid(g_ssm) * y_ssm + _sigmoid(g_attn) * y_attn
    x2 = x_ref[...] + jnp.dot(merged.astype(BF16), wout_ref[...], preferred_element_type=F32)
    x2_ref[...] = x2
    ms = jnp.mean(x2 * x2, axis=-1, keepdims=True)
    h2 = x2 * lax.rsqrt(ms + NORM_EPS) * nfw_ref[...]
    bits = lax.bitcast_convert_type(h2.astype(BF16).astype(F32), jnp.uint32)
    packed = bits[:, 0:D_MODEL // 2] | (bits[:, D_MODEL // 2:] >> 16)
    h2_ref[...] = packed.reshape(tm, 1, D_MODEL // 2)
    lg_ref[...] = jnp.dot(h2.astype(BF16), wr_ref[...], preferred_element_type=F32) + br_ref[...]


def _mix_out(y_ssm, attn_outs, attn_lses, h2d, w_gate, x2d, w_ssm, w_attn, w_out, norm_ffn_w, w_router, b_router):
    t = x2d.shape[0]
    tm = min(512, t)
    e8 = (np.arange(LANES)[:, None] == np.arange(GROUP_WIDTH)[None, :] // ATTN_HEAD_DIM)
    e8 = jnp.asarray(e8.astype(np.float32), dtype=BF16)

    def rows(width, cb=0):
        return pl.BlockSpec((tm, width), lambda i: (i, cb))

    def const(shape):
        return pl.BlockSpec(shape, lambda i: (0,) * len(shape), pipeline_mode=pl.Buffered(1))

    return pl.pallas_call(
        _mix_body,
        grid=(t // tm,),
        in_specs=[rows(D_INNER)] + [rows(LANES)] * (len(attn_outs) + len(attn_lses)) + [
                  rows(D_MODEL), const((D_MODEL, 2 * D_MODEL)), rows(D_MODEL),
                  const((D_INNER, D_MODEL)), const((GROUP_WIDTH, D_MODEL)), const((D_MODEL, D_MODEL)),
                  const((LANES, GROUP_WIDTH)), const((1, D_MODEL)),
                  const((D_MODEL, LANES)), const((1, LANES))],
        out_specs=[rows(D_MODEL),
                   pl.BlockSpec((tm, 1, D_MODEL // 2), lambda i: (i, 0, 0)),
                   rows(LANES)],
        out_shape=[jax.ShapeDtypeStruct((t, D_MODEL), F32),
                   jax.ShapeDtypeStruct((t, 1, D_MODEL // 2), jnp.uint32),
                   jax.ShapeDtypeStruct((t, LANES), F32)],
        compiler_params=_cparams(1),
    )(y_ssm, *attn_outs, *attn_lses, h2d, w_gate, x2d, w_ssm, w_attn, w_out, e8,
      norm_ffn_w[None], w_router, b_router)


def _route_body(lg_ref, sel_ref, gate_ref, cnt_ref, counts):
    tm = lg_ref.shape[0]
    i = pl.program_id(0)

    @pl.when(i == 0)
    def _():
        counts[...] = jnp.zeros_like(counts)

    lg = lg_ref[...]
    lane = lax.broadcasted_iota(jnp.int32, (tm, LANES), 1)
    is_coarse = lane < N_EXPERT_GROUPS
    cmax = jnp.max(jnp.where(is_coarse, lg, NEG_BIG), axis=-1, keepdims=True)
    grp = jnp.min(jnp.where(is_coarse & (lg == cmax), lane, LANES), axis=-1, keepdims=True)
    group_p = 1.0 / jnp.sum(jnp.where(is_coarse, jnp.exp(lg - cmax), 0.0), axis=-1, keepdims=True)
    f_lo = N_EXPERT_GROUPS + EXPERTS_PER_GROUP * grp
    in_grp = (lane >= f_lo) & (lane < f_lo + EXPERTS_PER_GROUP)
    f1 = jnp.max(jnp.where(in_grp, lg, NEG_BIG), axis=-1, keepdims=True)
    i1 = jnp.min(jnp.where(in_grp & (lg == f1), lane, LANES), axis=-1, keepdims=True)
    rest = in_grp & (lane != i1)
    f2 = jnp.max(jnp.where(rest, lg, NEG_BIG), axis=-1, keepdims=True)
    i2 = jnp.min(jnp.where(rest & (lg == f2), lane, LANES), axis=-1, keepdims=True)
    e2 = jnp.exp(f2 - f1)
    g1 = group_p / (1.0 + e2)
    g2 = group_p * e2 / (1.0 + e2)

    e1 = i1 - N_EXPERT_GROUPS
    e2i = i2 - N_EXPERT_GROUPS
    oh1 = lane == e1
    oh2 = lane == e2i
    onehot = jnp.where(oh1 | oh2, 1.0, 0.0)
    r = lax.broadcasted_iota(jnp.int32, (tm, tm), 0)
    c = lax.broadcasted_iota(jnp.int32, (tm, tm), 1)
    before = jnp.dot((r > c).astype(BF16), onehot.astype(BF16), preferred_element_type=F32)
    pos = counts[0:1, :] + before
    r1 = jnp.sum(jnp.where(oh1, pos, 0.0), axis=-1, keepdims=True).astype(jnp.int32)
    r2 = jnp.sum(jnp.where(oh2, pos, 0.0), axis=-1, keepdims=True).astype(jnp.int32)
    counts[...] = counts[...] + jnp.sum(onehot, axis=0, keepdims=True)
    sel_ref[...] = jnp.where(lane == 0, e1, jnp.where(lane == 1, e2i, jnp.where(lane == 2, r1,
                                                                                  jnp.where(lane == 3, r2, 0))))
    gate_ref[...] = jnp.where(lane == 0, g1, jnp.where(lane == 1, g2, 0.0))
    cnt_ref[...] = counts[...]


def _route(logits):
    t = logits.shape[0]
    tm = min(512, t)
    return pl.pallas_call(
        _route_body,
        grid=(t // tm,),
        in_specs=[pl.BlockSpec((tm, LANES), lambda i: (i, 0))],
        out_specs=[pl.BlockSpec((tm, LANES), lambda i: (i, 0)),
                   pl.BlockSpec((tm, LANES), lambda i: (i, 0)),
                   pl.BlockSpec((8, LANES), lambda i: (0, 0))],
        out_shape=[jax.ShapeDtypeStruct((t, LANES), jnp.int32),
                   jax.ShapeDtypeStruct((t, LANES), F32),
                   jax.ShapeDtypeStruct((8, LANES), F32)],
        scratch_shapes=[pltpu.VMEM((8, LANES), F32)],
        compiler_params=_cparams(1),
    )(logits)


def _dispatch_body(meta_ref, dest_ref, h_ref, rows_ref, zbuf, zsem, sem):
    tm = h_ref.shape[0]

    def zero_copy(e):
        start = pl.multiple_of(meta_ref[e] - MOE_BLOCK, MOE_BLOCK)
        return pltpu.make_async_copy(zbuf, rows_ref.at[pl.ds(start, MOE_BLOCK)], zsem)

    def tail_copy(blk):
        start = pl.multiple_of(blk * MOE_BLOCK, MOE_BLOCK)
        return pltpu.make_async_copy(zbuf, rows_ref.at[pl.ds(start, MOE_BLOCK)], zsem)

    @pl.when(pl.program_id(0) == 0)
    def _():
        zbuf[...] = jnp.zeros_like(zbuf)

        def start_zero(e, carry):
            @pl.when(meta_ref[N_EXPERTS + e] > 0)
            def _():
                zero_copy(e).start()
            return carry

        def wait_zero(e, carry):
            @pl.when(meta_ref[N_EXPERTS + e] > 0)
            def _():
                zero_copy(e).wait()
            return carry

        def start_tail(blk, carry):
            tail_copy(blk).start()
            return carry

        def wait_tail(blk, carry):
            tail_copy(blk).wait()
            return carry

        n_used = meta_ref[2 * N_EXPERTS]
        n_blocks = rows_ref.shape[0] // MOE_BLOCK
        lax.fori_loop(0, N_EXPERTS, start_zero, 0)
        lax.fori_loop(n_used, n_blocks, start_tail, 0)
        lax.fori_loop(0, N_EXPERTS, wait_zero, 0)
        lax.fori_loop(n_used, n_blocks, wait_tail, 0)

    def issue(j, carry):
        for k in range(2):
            d = dest_ref[0, 2 * j + k]
            pltpu.make_async_copy(h_ref.at[j], rows_ref.at[d], sem).start(priority=k)
        return carry

    lax.fori_loop(0, tm, issue, 0, unroll=8)
    for _ in range(2):
        pltpu.make_async_copy(h_ref, rows_ref.at[pl.ds(0, tm)], sem).wait()


def _dispatch(meta, dest3, h2, n_rows):
    t = h2.shape[0]
    tm = dest3.shape[2] // 2
    return pl.pallas_call(
        _dispatch_body,
        grid_spec=pltpu.PrefetchScalarGridSpec(
            num_scalar_prefetch=1,
            grid=(t // tm,),
            in_specs=[pl.BlockSpec((None, 1, 2 * tm), lambda i, m: (i, 0, 0), memory_space=pltpu.SMEM),
                      pl.BlockSpec((tm, 1, D_MODEL // 2), lambda i, m: (i, 0, 0))],
            out_specs=pl.BlockSpec(memory_space=pl.ANY),
            scratch_shapes=[pltpu.VMEM((MOE_BLOCK, 1, D_MODEL // 2), jnp.uint32),
                            pltpu.SemaphoreType.DMA(()),
                            pltpu.SemaphoreType.DMA(())]),
        out_shape=jax.ShapeDtypeStruct((n_rows, 1, D_MODEL // 2), jnp.uint32),
        compiler_params=_cparams(1),
    )(meta, dest3, h2)


def _expert_body(sched_ref, x_ref, wg_hbm, wu_hbm, wd_hbm, y_ref, x2d, wg_f, wu_f, wd_f, wg_b, wu_b, wd_b, sems):
    blk = pl.program_id(0)
    n_used = sched_ref[4, 0]

    def weight_copies(expert, slot):
        return [pltpu.make_async_copy(src.at[expert], dst.at[slot], sems.at[slot])
                for src, dst in ((wg_hbm, wg_f), (wu_hbm, wu_f), (wd_hbm, wd_f))]

    @pl.when(blk < n_used)
    def _():
        expert = sched_ref[0, blk]
        slot = sched_ref[2, blk]
        nxt = sched_ref[3, blk]

        @pl.when(sched_ref[1, blk] == 1)
        def _():
            @pl.when(blk == 0)
            def _():
                for c in weight_copies(expert, slot):
                    c.start()

            for c in weight_copies(expert, slot):
                c.wait()

            @pl.when(nxt >= 0)
            def _():
                for c in weight_copies(nxt, 1 - slot):
                    c.start()

            wg_b[...] = wg_f[slot].astype(BF16)
            wu_b[...] = wu_f[slot].astype(BF16)
            wd_b[...] = wd_f[slot].astype(BF16)

        x2d[...] = x_ref[...].reshape(MOE_BLOCK, D_MODEL // 2)
        words = x2d[...]
        x = jnp.concatenate([lax.bitcast_convert_type(words & jnp.uint32(0xFFFF0000), F32),
                             lax.bitcast_convert_type(words << 16, F32)], axis=1).astype(BF16)
        g = jnp.dot(x, wg_b[...], preferred_element_type=F32)
        u = jnp.dot(x, wu_b[...], preferred_element_type=F32)
        hid = (g * _sigmoid(g)) * u
        y = jnp.dot(hid.astype(BF16), wd_b[...], preferred_element_type=F32)
        bits = lax.bitcast_convert_type(y.astype(BF16).astype(F32), jnp.uint32)
        packed = bits[:, 0:D_MODEL // 2] | (bits[:, D_MODEL // 2:] >> 16)
        y_ref[...] = packed.reshape(MOE_BLOCK, 1, D_MODEL // 2)

    @pl.when(blk >= n_used)
    def _():
        y_ref[...] = jnp.zeros_like(y_ref)


def _experts(sched, rows, w_gate, w_up, w_down):
    n_rows = rows.shape[0]
    n_blocks = n_rows // MOE_BLOCK

    def row_map(b, sched):
        return (b, 0, 0)

    any_spec = pl.BlockSpec(memory_space=pl.ANY)
    return pl.pallas_call(
        _expert_body,
        grid_spec=pltpu.PrefetchScalarGridSpec(
            num_scalar_prefetch=1,
            grid=(n_blocks,),
            in_specs=[pl.BlockSpec((MOE_BLOCK, 1, D_MODEL // 2), row_map), any_spec, any_spec, any_spec],
            out_specs=pl.BlockSpec((MOE_BLOCK, 1, D_MODEL // 2), row_map),
            scratch_shapes=[pltpu.VMEM((MOE_BLOCK, D_MODEL // 2), jnp.uint32),
                            pltpu.VMEM((2, D_MODEL, D_EXPERT), F32),
                            pltpu.VMEM((2, D_MODEL, D_EXPERT), F32),
                            pltpu.VMEM((2, D_EXPERT, D_MODEL), F32),
                            pltpu.VMEM((D_MODEL, D_EXPERT), BF16),
                            pltpu.VMEM((D_MODEL, D_EXPERT), BF16),
                            pltpu.VMEM((D_EXPERT, D_MODEL), BF16),
                            pltpu.SemaphoreType.DMA((2,))]),
        out_shape=jax.ShapeDtypeStruct((n_rows, 1, D_MODEL // 2), jnp.uint32),
        compiler_params=_cparams(1),
    )(sched, rows, w_gate, w_up, w_down)


def _combine_body(dcur_ref, dnext_ref, gate_ref, x2_ref, y_ref, o_ref, buf_a, buf_b, y2d, sem_a, sem_b):
    tm = x2_ref.shape[0]
    i = pl.program_id(0)
    n = pl.num_programs(0)

    def issue(dref, buf, sem):
        def body(j, carry):
            for k in range(2):
                d = dref[0, 2 * j + k]
                pltpu.make_async_copy(y_ref.at[d], buf.at[k * tm + j], sem).start(priority=k)
            return carry
        lax.fori_loop(0, tm, body, 0, unroll=8)

    def finish(buf, sem):
        pltpu.make_async_copy(y_ref.at[pl.ds(0, 2 * tm)], buf, sem).wait()
        y2d[...] = buf[...].reshape(2 * tm, D_MODEL // 2)
        words = y2d[...]
        y = jnp.concatenate([lax.bitcast_convert_type(words & jnp.uint32(0xFFFF0000), F32),
                             lax.bitcast_convert_type(words << 16, F32)], axis=1)
        g = gate_ref[...]
        o_ref[...] = x2_ref[...] + g[:, 0:1] * y[0:tm, :] + g[:, 1:2] * y[tm:2 * tm, :]

    @pl.when(i == 0)
    def _():
        issue(dcur_ref, buf_a, sem_a)

    for par, (cur, cur_sem, nxt, nxt_sem) in enumerate(((buf_a, sem_a, buf_b, sem_b),
                                                        (buf_b, sem_b, buf_a, sem_a))):
        @pl.when(i % 2 == par)
        def _(cur=cur, cur_sem=cur_sem, nxt=nxt, nxt_sem=nxt_sem):
            @pl.when(i + 1 < n)
            def _():
                issue(dnext_ref, nxt, nxt_sem)
            finish(cur, cur_sem)


def _combine(dest3, gates, x2, y_rows):
    t = x2.shape[0]
    tm = dest3.shape[2] // 2
    nt = t // tm
    return pl.pallas_call(
        _combine_body,
        grid=(nt,),
        in_specs=[pl.BlockSpec((None, 1, 2 * tm), lambda i: (i, 0, 0), memory_space=pltpu.SMEM),
                  pl.BlockSpec((None, 1, 2 * tm), lambda i: (jnp.minimum(i + 1, nt - 1), 0, 0),
                               memory_space=pltpu.SMEM),
                  pl.BlockSpec((tm, LANES), lambda i: (i, 0)),
                  pl.BlockSpec((tm, D_MODEL), lambda i: (i, 0)),
                  pl.BlockSpec(memory_space=pl.ANY)],
        out_specs=pl.BlockSpec((tm, D_MODEL), lambda i: (i, 0)),
        out_shape=jax.ShapeDtypeStruct((t, D_MODEL), F32),
        scratch_shapes=[pltpu.VMEM((2 * tm, 1, D_MODEL // 2), jnp.uint32),
                        pltpu.VMEM((2 * tm, 1, D_MODEL // 2), jnp.uint32),
                        pltpu.VMEM((2 * tm, D_MODEL // 2), jnp.uint32),
                        pltpu.SemaphoreType.DMA(()),
                        pltpu.SemaphoreType.DMA(())],
        compiler_params=_cparams(1),
    )(dest3, dest3, gates, x2, y_rows)


def _layer(x, norm_mix_w, w_in, conv_w, conv_b, dt_bias, a_log, d_skip, ssm_norm_w, w_ssm_proj,
           q_norm_w, k_norm_w, rel_bias, w_attn_proj, w_out, norm_ffn_w, w_coarse, b_coarse,
           w_fine, b_fine, w_gate_exp, w_up_exp, w_down_exp):
    b, s, d = x.shape
    t = b * s
    x2d = x.reshape(t, d)

    dt_lo = 2 * D_INNER + BC_WIDTH
    qkv_lo = dt_lo + SSM_HEADS
    gate_lo = qkv_lo + 3 * ATTN_WIDTH
    w_ssd = jnp.concatenate([w_in[:, :qkv_lo], jnp.zeros((d, LANES - SSM_HEADS), w_in.dtype)],
                            axis=1).astype(BF16)
    w_gate = w_in[:, gate_lo:].astype(BF16)
    y_ssm, h3 = _ssd(x, norm_mix_w, w_ssd, conv_w, conv_b, dt_bias, a_log, d_skip, ssm_norm_w)
    y_ssm = y_ssm.reshape(t, D_INNER)
    h2d = h3.reshape(t, d)

    qw = jnp.tile(q_norm_w, 2)[None]
    kw = jnp.tile(k_norm_w, 2)[None]
    attn_outs, attn_lses = [], []
    for gi, (window, dilation) in enumerate(DILATED_CONFIGS):
        assert window // dilation == ATTN_BLK and s % window == 0
        hs = slice(gi * GROUP_WIDTH, (gi + 1) * GROUP_WIDTH)
        w_qkv = jnp.concatenate([w_in[:, qkv_lo + j * ATTN_WIDTH:qkv_lo + (j + 1) * ATTN_WIDTH][:, hs]
                                 for j in range(3)], axis=1).astype(BF16)
        bias = _band_bias(rel_bias[:, gi * HEADS_PER_GROUP:(gi + 1) * HEADS_PER_GROUP], dilation)
        o, l = _attn_group(h3, w_qkv, dilation, bias, qw, kw)
        attn_outs.extend(o)
        attn_lses.append(l)

    n_route = N_EXPERT_GROUPS + N_EXPERTS
    w_router = jnp.pad(jnp.concatenate([w_coarse, w_fine], axis=1), ((0, 0), (0, LANES - n_route)))
    b_router = jnp.pad(jnp.concatenate([b_coarse, b_fine]), (0, LANES - n_route))[None]
    x2, h2, logits = _mix_out(y_ssm, attn_outs, attn_lses, h2d, w_gate, x2d, w_ssm_proj.astype(BF16),
                              w_attn_proj.astype(BF16), w_out.astype(BF16), norm_ffn_w,
                              w_router.astype(BF16), b_router)

    sel, gates, counts = _route(logits)

    cnt = counts[0, :N_EXPERTS].astype(jnp.int32)
    padded = (cnt + MOE_BLOCK - 1) // MOE_BLOCK * MOE_BLOCK
    pad_end = jnp.cumsum(padded)
    n_blocks = -(-(2 * t + N_EXPERTS * (MOE_BLOCK - 1)) // MOE_BLOCK)
    block_start = jnp.arange(n_blocks, dtype=jnp.int32) * MOE_BLOCK
    block_expert = jnp.minimum(jnp.sum((pad_end[None, :] <= block_start[:, None]).astype(jnp.int32), axis=1),
                               N_EXPERTS - 1)
    n_used = (pad_end[-1:] // MOE_BLOCK).astype(jnp.int32)
    meta = jnp.concatenate([pad_end, padded, n_used]).astype(jnp.int32)
    chosen = sel[:, 0:2, None] == jnp.arange(N_EXPERTS, dtype=jnp.int32)
    dest = jnp.sum(jnp.where(chosen, pad_end - padded, 0), axis=-1) + sel[:, 2:4]
    blk_ids = jnp.arange(n_blocks, dtype=jnp.int32)
    first = ((blk_ids == 0) | (block_expert != jnp.roll(block_expert, 1))) & (blk_ids < n_used[0])
    slot = (jnp.cumsum(first.astype(jnp.int32)) - 1) % 2
    nxt_blk = blk_ids + padded[block_expert] // MOE_BLOCK
    nxt = jnp.where(nxt_blk < n_used[0], block_expert[jnp.minimum(nxt_blk, n_blocks - 1)], -1)
    sched = jnp.stack([block_expert, first.astype(jnp.int32), slot, nxt,
                       jnp.broadcast_to(n_used, (n_blocks,))]).astype(jnp.int32)

    tm_d = min(256, t)
    dest_d = dest.reshape(t // tm_d, 1, 2 * tm_d)
    rows = _dispatch(meta, dest_d, h2, n_blocks * MOE_BLOCK)
    y_rows = _experts(sched, rows, w_gate_exp, w_up_exp, w_down_exp)
    tm_c = min(128, t)
    dest_c = dest.reshape(t // tm_c, 1, 2 * tm_c)
    out = _combine(dest_c, gates, x2, y_rows)
    return out.reshape(b, s, d)


def kernel(x, norm_mix_w, w_in, conv_w, conv_b, dt_bias, a_log, d_skip, ssm_norm_w, w_ssm_proj,
           q_norm_w, k_norm_w, rel_bias, w_attn_proj, w_out, norm_ffn_w, w_coarse, b_coarse,
           w_fine, b_fine, w_gate_exp, w_up_exp, w_down_exp):
    depth = norm_mix_w.shape[0]
    for layer in range(depth):
        x = _layer(x, norm_mix_w[layer], w_in[layer], conv_w[layer], conv_b[layer], dt_bias[layer],
                   a_log[layer], d_skip[layer], ssm_norm_w[layer], w_ssm_proj[layer],
                   q_norm_w[layer], k_norm_w[layer], rel_bias, w_attn_proj[layer], w_out[layer],
                   norm_ffn_w[layer], w_coarse[layer], b_coarse[layer], w_fine[layer], b_fine[layer],
                   w_gate_exp[layer], w_up_exp[layer], w_down_exp[layer])
    return x
```

```python
import functools
import math

import jax
import jax.numpy as jnp
import numpy as np
from jax import lax
from jax.experimental import pallas as pl
from jax.experimental.pallas import tpu as pltpu

F32 = jnp.float32
BF16 = jnp.bfloat16
HIGHEST = lax.Precision.HIGHEST

LANES = 128
NORM_EPS = 1e-6
NEG_BIG = -1e30
LOG2_E = math.log2(math.e)

D_MODEL = 1024
D_INNER = 2048
SSM_HEAD_DIM = 64
SSM_HEADS = 32
SSM_GROUPS = 2
D_STATE = 128
CONV_K = 4
BC_WIDTH = 2 * SSM_GROUPS * D_STATE
SSD_CHUNK = 128
ATTN_HEAD_DIM = 64
DILATED_CONFIGS = ((128, 1), (512, 4), (2048, 16))
HEADS_PER_GROUP = 8
GROUP_WIDTH = HEADS_PER_GROUP * ATTN_HEAD_DIM
ATTN_WIDTH = 3 * GROUP_WIDTH
ATTN_BLK = 128
NUM_BUCKETS = 32
MAX_DISTANCE = 2048
N_EXPERT_GROUPS = 8
EXPERTS_PER_GROUP = 8
N_EXPERTS = 64
D_EXPERT = 512
MOE_BLOCK = 256

SSD_PROJ_WIDTH = 2 * D_INNER + BC_WIDTH + LANES

VMEM_LIMIT = 56 * 1024 * 1024


def _sigmoid(x):
    return 1.0 / (1.0 + jnp.exp(-x))


def _cparams(n_axes):
    return pltpu.CompilerParams(dimension_semantics=("arbitrary",) * n_axes,
                                vmem_limit_bytes=VMEM_LIMIT)


def _dot3(x, y, x_is_exact):
    v = y if x_is_exact else x
    hi = v.astype(BF16)
    r1 = v - hi.astype(F32)
    mid = r1.astype(BF16)
    lo = (r1 - mid.astype(F32)).astype(BF16)
    if x_is_exact:
        return sum(jnp.dot(x, part, preferred_element_type=F32) for part in (hi, mid, lo))
    return sum(jnp.dot(part, y, preferred_element_type=F32) for part in (hi, mid, lo))


def _ssd_chunk(z, xbuf, r0, dt_raw, cw_ref, cb_ref, dtb_ref, alog_ref, dskip_ref, nw_ref, e_ref, state):
    L = SSD_CHUNK
    half = D_INNER // SSM_GROUPS
    hg = SSM_HEADS // SSM_GROUPS

    conv = cb_ref[...] + cw_ref[CONV_K - 1:CONV_K, :] * xbuf[8 + r0:8 + r0 + L, :]
    for k in range(CONV_K - 1):
        lo = 8 + r0 - (CONV_K - 1 - k)
        conv = conv + cw_ref[k:k + 1, :] * xbuf[lo:lo + L, :]
    xbc = conv * _sigmoid(conv)
    xs = xbc[:, :D_INNER]

    lane = lax.broadcasted_iota(jnp.int32, (L, LANES), 1)
    row = lax.broadcasted_iota(jnp.int32, (L, L), 0)
    col = lax.broadcasted_iota(jnp.int32, (L, L), 1)
    causal = row >= col

    v = dt_raw + dtb_ref[...]
    dt = jnp.maximum(v, 0.0) + jnp.log1p(jnp.exp(-jnp.abs(v)))
    dt = jnp.where(lane < SSM_HEADS, dt, 0.0)
    adt = dt * (-jnp.exp(alog_ref[...]))
    a_cs = _dot3(causal.astype(BF16), adt, True) * LOG2_E
    a_cs_t = a_cs.T
    expand = e_ref[...]
    a_full = _dot3(a_cs, expand, False)
    dt_full = _dot3(dt, expand, False)
    a_tot = a_full[L - 1:L, :]
    decay_from_start = jnp.exp2(a_full)
    decay_to_end = jnp.exp2(a_tot - a_full)
    decay_chunk = jnp.exp2(a_tot)

    xdt = xs * dt_full
    xw_b = (xdt * decay_to_end).astype(BF16)
    head_lo = lax.broadcasted_iota(jnp.int32, (L, D_INNER), 1) % (2 * SSM_HEAD_DIM) < SSM_HEAD_DIM
    xdt_lo = jnp.where(head_lo, xdt, 0.0).astype(BF16)
    xdt_hi = jnp.where(head_lo, 0.0, xdt).astype(BF16)

    b16, c16, cbs, y_offs, s_prevs = [], [], [], [], []
    for g in range(SSM_GROUPS):
        bg = xbc[:, D_INNER + g * D_STATE:D_INNER + (g + 1) * D_STATE]
        cg = xbc[:, D_INNER + (SSM_GROUPS + g) * D_STATE:D_INNER + (SSM_GROUPS + g + 1) * D_STATE]
        b16.append(bg)
        c16.append(cg.astype(BF16))
        cbs.append(lax.dot_general(c16[g], bg.astype(BF16), (((1,), (1,)), ((), ())),
                                   preferred_element_type=F32))
    for g in range(SSM_GROUPS):
        s_prevs.append(state[:, g * half:(g + 1) * half])
        y_offs.append(jnp.dot(c16[g], s_prevs[g].astype(BF16), preferred_element_type=F32))
    y_cols = []
    for g in range(SSM_GROUPS):
        for pr in range(half // LANES):
            h0 = g * hg + 2 * pr
            ps = slice(h0 * SSM_HEAD_DIM, (h0 + 2) * SSM_HEAD_DIM)
            ms = []
            for h in (h0, h0 + 1):
                seg = a_cs[:, h:h + 1] - a_cs_t[h:h + 1, :]
                ms.append((cbs[g] * jnp.exp2(jnp.where(causal, seg, NEG_BIG))).astype(BF16))
            y_cols.append(jnp.dot(jnp.concatenate(ms, axis=1),
                                  jnp.concatenate([xdt_lo[:, ps], xdt_hi[:, ps]], axis=0),
                                  preferred_element_type=F32))
    for g in range(SSM_GROUPS):
        gs = slice(g * half, (g + 1) * half)
        state[:, gs] = decay_chunk[:, gs] * s_prevs[g] + jnp.dot(
            b16[g].T.astype(BF16), xw_b[:, gs], preferred_element_type=F32)

    y = (jnp.concatenate(y_cols, axis=1) + jnp.concatenate(y_offs, axis=1) * decay_from_start
         + dskip_ref[...] * xs)
    y = y * (z * _sigmoid(z))
    normed = []
    for g in range(SSM_GROUPS):
        yg = y[:, g * half:(g + 1) * half]
        ms = jnp.mean(yg * yg, axis=-1, keepdims=True)
        normed.append(yg * lax.rsqrt(ms + NORM_EPS))
    return jnp.concatenate(normed, axis=1) * nw_ref[...]


def _ssd_body(x_ref, nmw_ref, w_ref, cw_ref, cb_ref, dtb_ref, alog_ref, dskip_ref, nw_ref, e_ref,
              y_ref, h_ref, state, xbuf, zbuf, dtbuf):
    L = SSD_CHUNK
    rt = x_ref.shape[1]
    conv_dim = D_INNER + BC_WIDTH

    @pl.when(pl.program_id(1) == 0)
    def _():
        state[...] = jnp.zeros_like(state)
        xbuf[0:8, :] = jnp.zeros((8, conv_dim), F32)

    x = x_ref[0]
    ms = jnp.mean(x * x, axis=-1, keepdims=True)
    h = (x * lax.rsqrt(ms + NORM_EPS) * nmw_ref[...]).astype(BF16)
    h_ref[0] = h
    zbuf[...] = jnp.dot(h, w_ref[:, 0:D_INNER], preferred_element_type=F32)
    xbuf[8:8 + rt, :] = jnp.dot(h, w_ref[:, D_INNER:D_INNER + conv_dim], preferred_element_type=F32)
    dtbuf[...] = jnp.dot(h, w_ref[:, D_INNER + conv_dim:], preferred_element_type=F32)

    for c in range(rt // L):
        r0 = c * L
        y = _ssd_chunk(zbuf[r0:r0 + L, :], xbuf, r0, dtbuf[r0:r0 + L, :], cw_ref, cb_ref,
                       dtb_ref, alog_ref, dskip_ref, nw_ref, e_ref, state)
        y_ref[0, r0:r0 + L, :] = y.astype(BF16)
    xbuf[0:8, :] = xbuf[rt:rt + 8, :]


def _ssd(x, norm_mix_w, w_ssd, conv_w, conv_b, dt_bias, a_log, d_skip, ssm_norm_w):
    b, s, d = x.shape
    rt = min(4 * SSD_CHUNK, s)
    pad = LANES - SSM_HEADS
    conv_dim = D_INNER + BC_WIDTH
    dtb = jnp.pad(dt_bias, (0, pad))[None]
    alog = jnp.pad(a_log, (0, pad))[None]
    dskip = jnp.repeat(d_skip, SSM_HEAD_DIM)[None]
    expand = (np.arange(LANES)[:, None] == np.arange(D_INNER)[None, :] // SSM_HEAD_DIM).astype(np.float32)

    def const(shape):
        return pl.BlockSpec(shape, lambda i, c: (0,) * len(shape))

    return pl.pallas_call(
        _ssd_body,
        grid=(b, s // rt),
        in_specs=[pl.BlockSpec((1, rt, d), lambda i, c: (i, c, 0)),
                  const((1, d)), const((d, SSD_PROJ_WIDTH)),
                  const((CONV_K, conv_dim)), const((1, conv_dim)),
                  const((1, LANES)), const((1, LANES)),
                  const((1, D_INNER)), const((1, D_INNER)),
                  const((LANES, D_INNER))],
        out_specs=[pl.BlockSpec((1, rt, D_INNER), lambda i, c: (i, c, 0)),
                   pl.BlockSpec((1, rt, d), lambda i, c: (i, c, 0))],
        out_shape=[jax.ShapeDtypeStruct((b, s, D_INNER), BF16),
                   jax.ShapeDtypeStruct((b, s, d), BF16)],
        scratch_shapes=[pltpu.VMEM((D_STATE, D_INNER), F32),
                        pltpu.VMEM((rt + 8, conv_dim), F32),
                        pltpu.VMEM((rt, D_INNER), F32),
                        pltpu.VMEM((rt, LANES), F32)],
        compiler_params=_cparams(2),
    )(x, norm_mix_w[None], w_ssd, conv_w, conv_b[None], dtb, alog, dskip, ssm_norm_w[None],
      jnp.asarray(expand, dtype=BF16))


def _t5_causal_bucket(dist):
    max_exact = NUM_BUCKETS // 2
    large = max_exact + (np.log(np.maximum(dist, max_exact) / max_exact)
                         / math.log(MAX_DISTANCE / max_exact) * (NUM_BUCKETS - max_exact)).astype(np.int32)
    return np.where(dist < max_exact, dist, np.minimum(large, NUM_BUCKETS - 1)).astype(np.int32)


def _band_bias(rel_bias_group, dilation):
    blk = ATTN_BLK
    off = np.arange(blk)[:, None] + blk - np.arange(2 * blk)[None, :]
    in_win = (off >= 0) & (off <= blk)
    bucket = _t5_causal_bucket(np.clip(off, 0, None) * dilation)
    onehot = (bucket.reshape(-1, 1) == np.arange(NUM_BUCKETS)[None, :]).astype(np.float32)
    bias = jnp.dot(jnp.asarray(onehot), rel_bias_group.astype(F32), precision=HIGHEST)
    bias = jnp.transpose(bias.reshape(blk, 2 * blk, HEADS_PER_GROUP), (2, 0, 1))
    bias = jnp.where(in_win[None], bias, NEG_BIG)
    return bias.reshape(HEADS_PER_GROUP // 2, 2 * blk, 2 * blk)


def _attn_body(h_ref, w_ref, bias_ref, qw_ref, kw_ref, *rest, dilation):
    n_pairs = HEADS_PER_GROUP // 2
    o_refs = rest[0:n_pairs]
    lse_ref, qkv, kbuf, vbuf = rest[n_pairs:]
    blk = ATTN_BLK
    rt = h_ref.shape[1]
    span = blk * dilation
    n_sub = rt // span
    assert n_sub == 1 or dilation == 1
    step = pl.program_id(1)

    @pl.when(step == 0)
    def _():
        kbuf[...] = jnp.zeros_like(kbuf)
        vbuf[...] = jnp.zeros_like(vbuf)

    first_head = lax.broadcasted_iota(jnp.int32, (rt, LANES), 1) < ATTN_HEAD_DIM

    def head_norm(x, w_ref):
        xx = x * x
        s0 = jnp.sum(jnp.where(first_head, xx, 0.0), axis=-1, keepdims=True)
        s1 = jnp.sum(jnp.where(first_head, 0.0, xx), axis=-1, keepdims=True)
        ss = jnp.where(first_head, s0, s1)
        return x * lax.rsqrt(ss * (1.0 / ATTN_HEAD_DIM) + NORM_EPS) * w_ref[...]

    h = h_ref[0]
    for j in range(3 * n_pairs // 2):
        piece = jnp.dot(h, w_ref[:, 2 * j * LANES:2 * (j + 1) * LANES], preferred_element_type=F32)
        for half in range(2):
            slab = piece[:, half * LANES:(half + 1) * LANES]
            if j < n_pairs // 2:
                slab = head_norm(slab, qw_ref) * (ATTN_HEAD_DIM ** -0.5)
            elif j < n_pairs:
                slab = head_norm(slab, kw_ref)
            qkv[2 * j + half] = slab

    lane = lax.broadcasted_iota(jnp.int32, (blk, LANES), 1)
    lo_half = lane < ATTN_HEAD_DIM
    nt = (((1,), (1,)), ((), ()))

    units = 2

    def block_pair(it, carry):
        pairs = range(n_pairs)
        rows, res, slot, pen, q2, k_new, v_new, k_old, v_old = [], [], [], [], [], [], [], [], []
        for u in range(units):
            blk_id = units * it + u
            sub, r = (blk_id, 0) if dilation == 1 else (0, blk_id)
            gblk = step * n_sub + sub
            rows.append(pl.ds(sub * span + r, blk, stride=dilation))
            res.append(r)
            slot.append(gblk % 2)
            pen.append(jnp.where(gblk == 0, NEG_BIG, 0.0))
            k_new.append([qkv[n_pairs + p, rows[u], :].astype(BF16) for p in pairs])
            v_new.append([qkv[2 * n_pairs + p, rows[u], :].astype(BF16) for p in pairs])
            if dilation == 1 and u > 0:
                k_old.append(k_new[u - 1])
                v_old.append(v_new[u - 1])
            else:
                k_old.append([kbuf[1 - slot[u], r * n_pairs + p] for p in pairs])
                v_old.append([vbuf[1 - slot[u], r * n_pairs + p] for p in pairs])
            q2.append([])
            for p in pairs:
                qp = qkv[p, rows[u], :]
                q2[u].append(jnp.concatenate([jnp.where(lo_half, qp, 0.0), jnp.where(lo_half, 0.0, qp)],
                                             axis=0).astype(BF16))
        todo = [(u, p) for u in range(units) for p in pairs]
        s_prev = {up: lax.dot_general(q2[up[0]][up[1]], k_old[up[0]][up[1]], nt, preferred_element_type=F32)
                  for up in todo}
        s_cur = {up: lax.dot_general(q2[up[0]][up[1]], k_new[up[0]][up[1]], nt, preferred_element_type=F32)
                 for up in todo}
        e_prev, e_cur, m, d = {}, {}, {}, {}
        for up in todo:
            u, p = up
            sp = s_prev[up] + (bias_ref[p, :, 0:blk] + pen[u])
            sc = s_cur[up] + bias_ref[p, :, blk:2 * blk]
            m[up] = jnp.max(jnp.maximum(sp, sc), axis=-1, keepdims=True)
            ep = jnp.exp(sp - m[up])
            ec = jnp.exp(sc - m[up])
            d[up] = jnp.sum(ep + ec, axis=-1, keepdims=True)
            e_prev[up] = ep.astype(BF16)
            e_cur[up] = ec.astype(BF16)
        pv_prev = {up: jnp.dot(e_prev[up], v_old[up[0]][up[1]], preferred_element_type=F32) for up in todo}
        pv_cur = {up: jnp.dot(e_cur[up], v_new[up[0]][up[1]], preferred_element_type=F32) for up in todo}
        for u in range(units):
            lse_tile = jnp.zeros((blk, LANES), F32)
            for p in pairs:
                pv = (pv_prev[u, p] + pv_cur[u, p]) / d[u, p]
                o_refs[p][0, rows[u], :] = jnp.where(lo_half, pv[0:blk], pv[blk:2 * blk])
                lse = m[u, p] + jnp.log(d[u, p])
                lse_tile = jnp.where(lane == 2 * p, lse[0:blk], lse_tile)
                lse_tile = jnp.where(lane == 2 * p + 1, lse[blk:2 * blk], lse_tile)
                kbuf[slot[u], res[u] * n_pairs + p] = k_new[u][p]
                vbuf[slot[u], res[u] * n_pairs + p] = v_new[u][p]
            lse_ref[0, rows[u], :] = lse_tile
        return carry

    lax.fori_loop(0, n_sub * dilation // units, block_pair, 0)


def _attn_group(h3, w_qkv, dilation, bias, qw, kw):
    b, s, d = h3.shape
    blk = ATTN_BLK
    n_pairs = HEADS_PER_GROUP // 2
    rt = max(4 * blk, blk * dilation)
    rt = min(rt, s)

    def const(shape):
        return pl.BlockSpec(shape, lambda i, n: (0,) * len(shape))

    token_spec = pl.BlockSpec((1, rt, LANES), lambda i, n: (i, n, 0))
    res = pl.pallas_call(
        functools.partial(_attn_body, dilation=dilation),
        grid=(b, s // rt),
        in_specs=[pl.BlockSpec((1, rt, d), lambda i, n: (i, n, 0)),
                  const((d, 3 * GROUP_WIDTH)),
                  const((n_pairs, 2 * blk, 2 * blk)), const((1, LANES)), const((1, LANES))],
        out_specs=[token_spec] * (n_pairs + 1),
        out_shape=[jax.ShapeDtypeStruct((b, s, LANES), F32)] * (n_pairs + 1),
        scratch_shapes=[pltpu.VMEM((3 * n_pairs, rt, LANES), F32),
                        pltpu.VMEM((2, dilation * n_pairs, blk, LANES), BF16),
                        pltpu.VMEM((2, dilation * n_pairs, blk, LANES), BF16)],
        compiler_params=_cparams(2),
    )(h3, w_qkv, bias, qw, kw)
    outs = [o.reshape(b * s, LANES) for o in res[:n_pairs]]
    return outs, res[n_pairs].reshape(b * s, LANES)


def _mix_body(*refs):
    n_pairs = HEADS_PER_GROUP // 2
    y_ref = refs[0]
    o_refs = refs[1:1 + 3 * n_pairs]
    l_refs = refs[1 + 3 * n_pairs:4 + 3 * n_pairs]
    (h_ref, wgate_ref, x_ref, wssm_ref, wattn_ref, wout_ref, e8_ref, nfw_ref, wr_ref, br_ref,
     x2_ref, h2_ref, lg_ref) = refs[4 + 3 * n_pairs:]
    tm = x_ref.shape[0]
    lses = [l[...] for l in l_refs]
    mx = jnp.maximum(jnp.maximum(lses[0], lses[1]), lses[2])
    es = [jnp.exp(l - mx) for l in lses]
    inv = 1.0 / (es[0] + es[1] + es[2])
    e8 = e8_ref[...]
    att = jnp.zeros((tm, GROUP_WIDTH), F32)
    for g in range(3):
        w = es[g] * inv
        w_hi = w.astype(BF16)
        w_lo = (w - w_hi.astype(F32)).astype(BF16)
        w_full = (jnp.dot(w_hi, e8, preferred_element_type=F32)
                  + jnp.dot(w_lo, e8, preferred_element_type=F32))
        o_g = jnp.concatenate([o_refs[g * n_pairs + p][...] for p in range(n_pairs)], axis=1)
        att = att + w_full * o_g
    y_attn = jnp.dot(att.astype(BF16), wattn_ref[...], preferred_element_type=F32)
    y_ssm = jnp.dot(y_ref[...], wssm_ref[...], preferred_element_type=F32)
    h = h_ref[...]
    g_ssm = jnp.dot(h, wgate_ref[:, 0:D_MODEL], preferred_element_type=F32)
    g_attn = jnp.dot(h, wgate_ref[:, D_MODEL:2 * D_MODEL], preferred_element_type=F32)
    merged = _sigmoid(g_ssm) * y_ssm + _sigmoid(g_attn) * y_attn
    x2 = x_ref[...] + jnp.dot(merged.astype(BF16), wout_ref[...], preferred_element_type=F32)
    x2_ref[...] = x2
    ms = jnp.mean(x2 * x2, axis=-1, keepdims=True)
    h2 = x2 * lax.rsqrt(ms + NORM_EPS) * nfw_ref[...]
    bits = lax.bitcast_convert_type(h2.astype(BF16).astype(F32), jnp.uint32)
    packed = bits[:, 0:D_MODEL // 2] | (bits[:, D_MODEL // 2:] >> 16)
    h2_ref[...] = packed.reshape(tm, 1, D_MODEL // 2)
    lg_ref[...] = jnp.dot(h2.astype(BF16), wr_ref[...], preferred_element_type=F32) + br_ref[...]


def _mix_out(y_ssm, attn_outs, attn_lses, h2d, w_gate, x2d, w_ssm, w_attn, w_out, norm_ffn_w, w_router, b_router):
    t = x2d.shape[0]
    tm = min(512, t)
    e8 = (np.arange(LANES)[:, None] == np.arange(GROUP_WIDTH)[None, :] // ATTN_HEAD_DIM)
    e8 = jnp.asarray(e8.astype(np.float32), dtype=BF16)

    def rows(width, cb=0):
        return pl.BlockSpec((tm, width), lambda i: (i, cb))

    def const(shape):
        return pl.BlockSpec(shape, lambda i: (0,) * len(shape), pipeline_mode=pl.Buffered(1))

    return pl.pallas_call(
        _mix_body,
        grid=(t // tm,),
        in_specs=[rows(D_INNER)] + [rows(LANES)] * (len(attn_outs) + len(attn_lses)) + [
                  rows(D_MODEL), const((D_MODEL, 2 * D_MODEL)), rows(D_MODEL),
                  const((D_INNER, D_MODEL)), const((GROUP_WIDTH, D_MODEL)), const((D_MODEL, D_MODEL)),
                  const((LANES, GROUP_WIDTH)), const((1, D_MODEL)),
                  const((D_MODEL, LANES)), const((1, LANES))],
        out_specs=[rows(D_MODEL),
                   pl.BlockSpec((tm, 1, D_MODEL // 2), lambda i: (i, 0, 0)),
                   rows(LANES)],
        out_shape=[jax.ShapeDtypeStruct((t, D_MODEL), F32),
                   jax.ShapeDtypeStruct((t, 1, D_MODEL // 2), jnp.uint32),
                   jax.ShapeDtypeStruct((t, LANES), F32)],
        compiler_params=_cparams(1),
    )(y_ssm, *attn_outs, *attn_lses, h2d, w_gate, x2d, w_ssm, w_attn, w_out, e8,
      norm_ffn_w[None], w_router, b_router)


def _route_body(lg_ref, sel_ref, gate_ref, cnt_ref, counts):
    tm = lg_ref.shape[0]
    i = pl.program_id(0)

    @pl.when(i == 0)
    def _():
        counts[...] = jnp.zeros_like(counts)

    lg = lg_ref[...]
    lane = lax.broadcasted_iota(jnp.int32, (tm, LANES), 1)
    is_coarse = lane < N_EXPERT_GROUPS
    cmax = jnp.max(jnp.where(is_coarse, lg, NEG_BIG), axis=-1, keepdims=True)
    grp = jnp.min(jnp.where(is_coarse & (lg == cmax), lane, LANES), axis=-1, keepdims=True)
    group_p = 1.0 / jnp.sum(jnp.where(is_coarse, jnp.exp(lg - cmax), 0.0), axis=-1, keepdims=True)
    f_lo = N_EXPERT_GROUPS + EXPERTS_PER_GROUP * grp
    in_grp = (lane >= f_lo) & (lane < f_lo + EXPERTS_PER_GROUP)
    f1 = jnp.max(jnp.where(in_grp, lg, NEG_BIG), axis=-1, keepdims=True)
    i1 = jnp.min(jnp.where(in_grp & (lg == f1), lane, LANES), axis=-1, keepdims=True)
    rest = in_grp & (lane != i1)
    f2 = jnp.max(jnp.where(rest, lg, NEG_BIG), axis=-1, keepdims=True)
    i2 = jnp.min(jnp.where(rest & (lg == f2), lane, LANES), axis=-1, keepdims=True)
    e2 = jnp.exp(f2 - f1)
    g1 = group_p / (1.0 + e2)
    g2 = group_p * e2 / (1.0 + e2)

    e1 = i1 - N_EXPERT_GROUPS
    e2i = i2 - N_EXPERT_GROUPS
    oh1 = lane == e1
    oh2 = lane == e2i
    onehot = jnp.where(oh1 | oh2, 1.0, 0.0)
    r = lax.broadcasted_iota(jnp.int32, (tm, tm), 0)
    c = lax.broadcasted_iota(jnp.int32, (tm, tm), 1)
    before = jnp.dot((r > c).astype(BF16), onehot.astype(BF16), preferred_element_type=F32)
    pos = counts[0:1, :] + before
    r1 = jnp.sum(jnp.where(oh1, pos, 0.0), axis=-1, keepdims=True).astype(jnp.int32)
    r2 = jnp.sum(jnp.where(oh2, pos, 0.0), axis=-1, keepdims=True).astype(jnp.int32)
    counts[...] = counts[...] + jnp.sum(onehot, axis=0, keepdims=True)
    sel_ref[...] = jnp.where(lane == 0, e1, jnp.where(lane == 1, e2i, jnp.where(lane == 2, r1,
                                                                                  jnp.where(lane == 3, r2, 0))))
    gate_ref[...] = jnp.where(lane == 0, g1, jnp.where(lane == 1, g2, 0.0))
    cnt_ref[...] = counts[...]


def _route(logits):
    t = logits.shape[0]
    tm = min(512, t)
    return pl.pallas_call(
        _route_body,
        grid=(t // tm,),
        in_specs=[pl.BlockSpec((tm, LANES), lambda i: (i, 0))],
        out_specs=[pl.BlockSpec((tm, LANES), lambda i: (i, 0)),
                   pl.BlockSpec((tm, LANES), lambda i: (i, 0)),
                   pl.BlockSpec((8, LANES), lambda i: (0, 0))],
        out_shape=[jax.ShapeDtypeStruct((t, LANES), jnp.int32),
                   jax.ShapeDtypeStruct((t, LANES), F32),
                   jax.ShapeDtypeStruct((8, LANES), F32)],
        scratch_shapes=[pltpu.VMEM((8, LANES), F32)],
        compiler_params=_cparams(1),
    )(logits)


def _dispatch_body(meta_ref, dest_ref, h_ref, rows_ref, zbuf, zsem, sem):
    tm = h_ref.shape[0]

    def zero_copy(e):
        start = pl.multiple_of(meta_ref[e] - MOE_BLOCK, MOE_BLOCK)
        return pltpu.make_async_copy(zbuf, rows_ref.at[pl.ds(start, MOE_BLOCK)], zsem)

    def tail_copy(blk):
        start = pl.multiple_of(blk * MOE_BLOCK, MOE_BLOCK)
        return pltpu.make_async_copy(zbuf, rows_ref.at[pl.ds(start, MOE_BLOCK)], zsem)

    @pl.when(pl.program_id(0) == 0)
    def _():
        zbuf[...] = jnp.zeros_like(zbuf)

        def start_zero(e, carry):
            @pl.when(meta_ref[N_EXPERTS + e] > 0)
            def _():
                zero_copy(e).start()
            return carry

        def wait_zero(e, carry):
            @pl.when(meta_ref[N_EXPERTS + e] > 0)
            def _():
                zero_copy(e).wait()
            return carry

        def start_tail(blk, carry):
            tail_copy(blk).start()
            return carry

        def wait_tail(blk, carry):
            tail_copy(blk).wait()
            return carry

        n_used = meta_ref[2 * N_EXPERTS]
        n_blocks = rows_ref.shape[0] // MOE_BLOCK
        lax.fori_loop(0, N_EXPERTS, start_zero, 0)
        lax.fori_loop(n_used, n_blocks, start_tail, 0)
        lax.fori_loop(0, N_EXPERTS, wait_zero, 0)
        lax.fori_loop(n_used, n_blocks, wait_tail, 0)

    def issue(j, carry):
        for k in range(2):
            d = dest_ref[0, 2 * j + k]
            pltpu.make_async_copy(h_ref.at[j], rows_ref.at[d], sem).start(priority=k)
        return carry

    lax.fori_loop(0, tm, issue, 0, unroll=8)
    for _ in range(2):
        pltpu.make_async_copy(h_ref, rows_ref.at[pl.ds(0, tm)], sem).wait()


def _dispatch(meta, dest3, h2, n_rows):
    t = h2.shape[0]
    tm = dest3.shape[2] // 2
    return pl.pallas_call(
        _dispatch_body,
        grid_spec=pltpu.PrefetchScalarGridSpec(
            num_scalar_prefetch=1,
            grid=(t // tm,),
            in_specs=[pl.BlockSpec((None, 1, 2 * tm), lambda i, m: (i, 0, 0), memory_space=pltpu.SMEM),
                      pl.BlockSpec((tm, 1, D_MODEL // 2), lambda i, m: (i, 0, 0))],
            out_specs=pl.BlockSpec(memory_space=pl.ANY),
            scratch_shapes=[pltpu.VMEM((MOE_BLOCK, 1, D_MODEL // 2), jnp.uint32),
                            pltpu.SemaphoreType.DMA(()),
                            pltpu.SemaphoreType.DMA(())]),
        out_shape=jax.ShapeDtypeStruct((n_rows, 1, D_MODEL // 2), jnp.uint32),
        compiler_params=_cparams(1),
    )(meta, dest3, h2)


def _expert_body(sched_ref, x_ref, wg_hbm, wu_hbm, wd_hbm, y_ref, x2d, wg_f, wu_f, wd_f, wg_b, wu_b, wd_b, sems):
    blk = pl.program_id(0)
    n_used = sched_ref[4, 0]

    def weight_copies(expert, slot):
        return [pltpu.make_async_copy(src.at[expert], dst.at[slot], sems.at[slot])
                for src, dst in ((wg_hbm, wg_f), (wu_hbm, wu_f), (wd_hbm, wd_f))]

    @pl.when(blk < n_used)
    def _():
        expert = sched_ref[0, blk]
        slot = sched_ref[2, blk]
        nxt = sched_ref[3, blk]

        @pl.when(sched_ref[1, blk] == 1)
        def _():
            @pl.when(blk == 0)
            def _():
                for c in weight_copies(expert, slot):
                    c.start()

            for c in weight_copies(expert, slot):
                c.wait()

            @pl.when(nxt >= 0)
            def _():
                for c in weight_copies(nxt, 1 - slot):
                    c.start()

            wg_b[...] = wg_f[slot].astype(BF16)
            wu_b[...] = wu_f[slot].astype(BF16)
            wd_b[...] = wd_f[slot].astype(BF16)

        x2d[...] = x_ref[...].reshape(MOE_BLOCK, D_MODEL // 2)
        words = x2d[...]
        x = jnp.concatenate([lax.bitcast_convert_type(words & jnp.uint32(0xFFFF0000), F32),
                             lax.bitcast_convert_type(words << 16, F32)], axis=1).astype(BF16)
        g = jnp.dot(x, wg_b[...], preferred_element_type=F32)
        u = jnp.dot(x, wu_b[...], preferred_element_type=F32)
        hid = (g * _sigmoid(g)) * u
        y = jnp.dot(hid.astype(BF16), wd_b[...], preferred_element_type=F32)
        bits = lax.bitcast_convert_type(y.astype(BF16).astype(F32), jnp.uint32)
        packed = bits[:, 0:D_MODEL // 2] | (bits[:, D_MODEL // 2:] >> 16)
        y_ref[...] = packed.reshape(MOE_BLOCK, 1, D_MODEL // 2)

    @pl.when(blk >= n_used)
    def _():
        y_ref[...] = jnp.zeros_like(y_ref)


def _experts(sched, rows, w_gate, w_up, w_down):
    n_rows = rows.shape[0]
    n_blocks = n_rows // MOE_BLOCK

    def row_map(b, sched):
        return (b, 0, 0)

    any_spec = pl.BlockSpec(memory_space=pl.ANY)
    return pl.pallas_call(
        _expert_body,
        grid_spec=pltpu.PrefetchScalarGridSpec(
            num_scalar_prefetch=1,
            grid=(n_blocks,),
            in_specs=[pl.BlockSpec((MOE_BLOCK, 1, D_MODEL // 2), row_map), any_spec, any_spec, any_spec],
            out_specs=pl.BlockSpec((MOE_BLOCK, 1, D_MODEL // 2), row_map),
            scratch_shapes=[pltpu.VMEM((MOE_BLOCK, D_MODEL // 2), jnp.uint32),
                            pltpu.VMEM((2, D_MODEL, D_EXPERT), F32),
                            pltpu.VMEM((2, D_MODEL, D_EXPERT), F32),
                            pltpu.VMEM((2, D_EXPERT, D_MODEL), F32),
                            pltpu.VMEM((D_MODEL, D_EXPERT), BF16),
                            pltpu.VMEM((D_MODEL, D_EXPERT), BF16),
                            pltpu.VMEM((D_EXPERT, D_MODEL), BF16),
                            pltpu.SemaphoreType.DMA((2,))]),
        out_shape=jax.ShapeDtypeStruct((n_rows, 1, D_MODEL // 2), jnp.uint32),
        compiler_params=_cparams(1),
    )(sched, rows, w_gate, w_up, w_down)


def _combine_body(dcur_ref, dnext_ref, gate_ref, x2_ref, y_ref, o_ref, buf_a, buf_b, y2d, sem_a, sem_b):
    tm = x2_ref.shape[0]
    i = pl.program_id(0)
    n = pl.num_programs(0)

    def issue(dref, buf, sem):
        def body(j, carry):
            for k in range(2):
                d = dref[0, 2 * j + k]
                pltpu.make_async_copy(y_ref.at[d], buf.at[k * tm + j], sem).start(priority=k)
            return carry
        lax.fori_loop(0, tm, body, 0, unroll=8)

    def finish(buf, sem):
        pltpu.make_async_copy(y_ref.at[pl.ds(0, 2 * tm)], buf, sem).wait()
        y2d[...] = buf[...].reshape(2 * tm, D_MODEL // 2)
        words = y2d[...]
        y = jnp.concatenate([lax.bitcast_convert_type(words & jnp.uint32(0xFFFF0000), F32),
                             lax.bitcast_convert_type(words << 16, F32)], axis=1)
        g = gate_ref[...]
        o_ref[...] = x2_ref[...] + g[:, 0:1] * y[0:tm, :] + g[:, 1:2] * y[tm:2 * tm, :]

    @pl.when(i == 0)
    def _():
        issue(dcur_ref, buf_a, sem_a)

    for par, (cur, cur_sem, nxt, nxt_sem) in enumerate(((buf_a, sem_a, buf_b, sem_b),
                                                        (buf_b, sem_b, buf_a, sem_a))):
        @pl.when(i % 2 == par)
        def _(cur=cur, cur_sem=cur_sem, nxt=nxt, nxt_sem=nxt_sem):
            @pl.when(i + 1 < n)
            def _():
                issue(dnext_ref, nxt, nxt_sem)
            finish(cur, cur_sem)


def _combine(dest3, gates, x2, y_rows):
    t = x2.shape[0]
    tm = dest3.shape[2] // 2
    nt = t // tm
    return pl.pallas_call(
        _combine_body,
        grid=(nt,),
        in_specs=[pl.BlockSpec((None, 1, 2 * tm), lambda i: (i, 0, 0), memory_space=pltpu.SMEM),
                  pl.BlockSpec((None, 1, 2 * tm), lambda i: (jnp.minimum(i + 1, nt - 1), 0, 0),
                               memory_space=pltpu.SMEM),
                  pl.BlockSpec((tm, LANES), lambda i: (i, 0)),
                  pl.BlockSpec((tm, D_MODEL), lambda i: (i, 0)),
                  pl.BlockSpec(memory_space=pl.ANY)],
        out_specs=pl.BlockSpec((tm, D_MODEL), lambda i: (i, 0)),
        out_shape=jax.ShapeDtypeStruct((t, D_MODEL), F32),
        scratch_shapes=[pltpu.VMEM((2 * tm, 1, D_MODEL // 2), jnp.uint32),
                        pltpu.VMEM((2 * tm, 1, D_MODEL // 2), jnp.uint32),
                        pltpu.VMEM((2 * tm, D_MODEL // 2), jnp.uint32),
                        pltpu.SemaphoreType.DMA(()),
                        pltpu.SemaphoreType.DMA(())],
        compiler_params=_cparams(1),
    )(dest3, dest3, gates, x2, y_rows)


def _layer(x, norm_mix_w, w_in, conv_w, conv_b, dt_bias, a_log, d_skip, ssm_norm_w, w_ssm_proj,
           q_norm_w, k_norm_w, rel_bias, w_attn_proj, w_out, norm_ffn_w, w_coarse, b_coarse,
           w_fine, b_fine, w_gate_exp, w_up_exp, w_down_exp):
    b, s, d = x.shape
    t = b * s
    x2d = x.reshape(t, d)

    dt_lo = 2 * D_INNER + BC_WIDTH
    qkv_lo = dt_lo + SSM_HEADS
    gate_lo = qkv_lo + 3 * ATTN_WIDTH
    w_ssd = jnp.concatenate([w_in[:, :qkv_lo], jnp.zeros((d, LANES - SSM_HEADS), w_in.dtype)],
                            axis=1).astype(BF16)
    w_gate = w_in[:, gate_lo:].astype(BF16)
    y_ssm, h3 = _ssd(x, norm_mix_w, w_ssd, conv_w, conv_b, dt_bias, a_log, d_skip, ssm_norm_w)
    y_ssm = y_ssm.reshape(t, D_INNER)
    h2d = h3.reshape(t, d)

    qw = jnp.tile(q_norm_w, 2)[None]
    kw = jnp.tile(k_norm_w, 2)[None]
    attn_outs, attn_lses = [], []
    for gi, (window, dilation) in enumerate(DILATED_CONFIGS):
        assert window // dilation == ATTN_BLK and s % window == 0
        hs = slice(gi * GROUP_WIDTH, (gi + 1) * GROUP_WIDTH)
        w_qkv = jnp.concatenate([w_in[:, qkv_lo + j * ATTN_WIDTH:qkv_lo + (j + 1) * ATTN_WIDTH][:, hs]
                                 for j in range(3)], axis=1).astype(BF16)
        bias = _band_bias(rel_bias[:, gi * HEADS_PER_GROUP:(gi + 1) * HEADS_PER_GROUP], dilation)
        o, l = _attn_group(h3, w_qkv, dilation, bias, qw, kw)
        attn_outs.extend(o)
        attn_lses.append(l)

    n_route = N_EXPERT_GROUPS + N_EXPERTS
    w_router = jnp.pad(jnp.concatenate([w_coarse, w_fine], axis=1), ((0, 0), (0, LANES - n_route)))
    b_router = jnp.pad(jnp.concatenate([b_coarse, b_fine]), (0, LANES - n_route))[None]
    x2, h2, logits = _mix_out(y_ssm, attn_outs, attn_lses, h2d, w_gate, x2d, w_ssm_proj.astype(BF16),
                              w_attn_proj.astype(BF16), w_out.astype(BF16), norm_ffn_w,
                              w_router.astype(BF16), b_router)

    sel, gates, counts = _route(logits)

    cnt = counts[0, :N_EXPERTS].astype(jnp.int32)
    padded = (cnt + MOE_BLOCK - 1) // MOE_BLOCK * MOE_BLOCK
    pad_end = jnp.cumsum(padded)
    n_blocks = -(-(2 * t + N_EXPERTS * (MOE_BLOCK - 1)) // MOE_BLOCK)
    block_start = jnp.arange(n_blocks, dtype=jnp.int32) * MOE_BLOCK
    block_expert = jnp.minimum(jnp.sum((pad_end[None, :] <= block_start[:, None]).astype(jnp.int32), axis=1),
                               N_EXPERTS - 1)
    n_used = (pad_end[-1:] // MOE_BLOCK).astype(jnp.int32)
    meta = jnp.concatenate([pad_end, padded, n_used]).astype(jnp.int32)
    chosen = sel[:, 0:2, None] == jnp.arange(N_EXPERTS, dtype=jnp.int32)
    dest = jnp.sum(jnp.where(chosen, pad_end - padded, 0), axis=-1) + sel[:, 2:4]
    blk_ids = jnp.arange(n_blocks, dtype=jnp.int32)
    first = ((blk_ids == 0) | (block_expert != jnp.roll(block_expert, 1))) & (blk_ids < n_used[0])
    slot = (jnp.cumsum(first.astype(jnp.int32)) - 1) % 2
    nxt_blk = blk_ids + padded[block_expert] // MOE_BLOCK
    nxt = jnp.where(nxt_blk < n_used[0], block_expert[jnp.minimum(nxt_blk, n_blocks - 1)], -1)
    sched = jnp.stack([block_expert, first.astype(jnp.int32), slot, nxt,
                       jnp.broadcast_to(n_used, (n_blocks,))]).astype(jnp.int32)

    tm_d = min(512, t)
    dest_d = dest.reshape(t // tm_d, 1, 2 * tm_d)
    rows = _dispatch(meta, dest_d, h2, n_blocks * MOE_BLOCK)
    y_rows = _experts(sched, rows, w_gate_exp, w_up_exp, w_down_exp)
    tm_c = min(256, t)
    dest_c = dest.reshape(t // tm_c, 1, 2 * tm_c)
    out = _combine(dest_c, gates, x2, y_rows)
    return out.reshape(b, s, d)


def kernel(x, norm_mix_w, w_in, conv_w, conv_b, dt_bias, a_log, d_skip, ssm_norm_w, w_ssm_proj,
           q_norm_w, k_norm_w, rel_bias, w_attn_proj, w_out, norm_ffn_w, w_coarse, b_coarse,
           w_fine, b_fine, w_gate_exp, w_up_exp, w_down_exp):
    depth = norm_mix_w.shape[0]
    for layer in range(depth):
        x = _layer(x, norm_mix_w[layer], w_in[layer], conv_w[layer], conv_b[layer], dt_bias[layer],
                   a_log[layer], d_skip[layer], ssm_norm_w[layer], w_ssm_proj[layer],
                   q_norm_w[layer], k_norm_w[layer], rel_bias, w_attn_proj[layer], w_out[layer],
                   norm_ffn_w[layer], w_coarse[layer], b_coarse[layer], w_fine[layer], b_fine[layer],
                   w_gate_exp[layer], w_up_exp[layer], w_down_exp[layer])
    return x
```

```python
import functools
import math

import jax
import jax.numpy as jnp
import numpy as np
from jax import lax
from jax.experimental import pallas as pl
from jax.experimental.pallas import tpu as pltpu

F32 = jnp.float32
BF16 = jnp.bfloat16
HIGHEST = lax.Precision.HIGHEST

LANES = 128
NORM_EPS = 1e-6
NEG_BIG = -1e30
LOG2_E = math.log2(math.e)

D_MODEL = 1024
D_INNER = 2048
SSM_HEAD_DIM = 64
SSM_HEADS = 32
SSM_GROUPS = 2
D_STATE = 128
CONV_K = 4
BC_WIDTH = 2 * SSM_GROUPS * D_STATE
SSD_CHUNK = 128
ATTN_HEAD_DIM = 64
DILATED_CONFIGS = ((128, 1), (512, 4), (2048, 16))
HEADS_PER_GROUP = 8
GROUP_WIDTH = HEADS_PER_GROUP * ATTN_HEAD_DIM
ATTN_WIDTH = 3 * GROUP_WIDTH
ATTN_BLK = 128
NUM_BUCKETS = 32
MAX_DISTANCE = 2048
N_EXPERT_GROUPS = 8
EXPERTS_PER_GROUP = 8
N_EXPERTS = 64
D_EXPERT = 512
MOE_BLOCK = 256

SSD_PROJ_WIDTH = 2 * D_INNER + BC_WIDTH + LANES

VMEM_LIMIT = 56 * 1024 * 1024


def _sigmoid(x):
    return 1.0 / (1.0 + jnp.exp(-x))


def _cparams(n_axes):
    return pltpu.CompilerParams(dimension_semantics=("arbitrary",) * n_axes,
                                vmem_limit_bytes=VMEM_LIMIT)


def _dot3(x, y, x_is_exact):
    v = y if x_is_exact else x
    hi = v.astype(BF16)
    r1 = v - hi.astype(F32)
    mid = r1.astype(BF16)
    lo = (r1 - mid.astype(F32)).astype(BF16)
    if x_is_exact:
        return sum(jnp.dot(x, part, preferred_element_type=F32) for part in (hi, mid, lo))
    return sum(jnp.dot(part, y, preferred_element_type=F32) for part in (hi, mid, lo))


def _ssd_chunk(z, xbuf, r0, dt_raw, cw_ref, cb_ref, dtb_ref, alog_ref, dskip_ref, nw_ref, e_ref, state):
    L = SSD_CHUNK
    half = D_INNER // SSM_GROUPS
    hg = SSM_HEADS // SSM_GROUPS

    conv = cb_ref[...] + cw_ref[CONV_K - 1:CONV_K, :] * xbuf[8 + r0:8 + r0 + L, :]
    for k in range(CONV_K - 1):
        lo = 8 + r0 - (CONV_K - 1 - k)
        conv = conv + cw_ref[k:k + 1, :] * xbuf[lo:lo + L, :]
    xbc = conv * _sigmoid(conv)
    xs = xbc[:, :D_INNER]

    lane = lax.broadcasted_iota(jnp.int32, (L, LANES), 1)
    row = lax.broadcasted_iota(jnp.int32, (L, L), 0)
    col = lax.broadcasted_iota(jnp.int32, (L, L), 1)
    causal = row >= col

    v = dt_raw + dtb_ref[...]
    dt = jnp.maximum(v, 0.0) + jnp.log1p(jnp.exp(-jnp.abs(v)))
    dt = jnp.where(lane < SSM_HEADS, dt, 0.0)
    adt = dt * (-jnp.exp(alog_ref[...]))
    a_cs = _dot3(causal.astype(BF16), adt, True) * LOG2_E
    a_cs_t = a_cs.T
    expand = e_ref[...]
    a_full = _dot3(a_cs, expand, False)
    dt_full = _dot3(dt, expand, False)
    a_tot = a_full[L - 1:L, :]
    decay_from_start = jnp.exp2(a_full)
    decay_to_end = jnp.exp2(a_tot - a_full)
    decay_chunk = jnp.exp2(a_tot)

    xdt = xs * dt_full
    xw_b = (xdt * decay_to_end).astype(BF16)
    head_lo = lax.broadcasted_iota(jnp.int32, (L, D_INNER), 1) % (2 * SSM_HEAD_DIM) < SSM_HEAD_DIM
    xdt_lo = jnp.where(head_lo, xdt, 0.0).astype(BF16)
    xdt_hi = jnp.where(head_lo, 0.0, xdt).astype(BF16)

    b16, c16, cbs, y_offs, s_prevs = [], [], [], [], []
    for g in range(SSM_GROUPS):
        bg = xbc[:, D_INNER + g * D_STATE:D_INNER + (g + 1) * D_STATE]
        cg = xbc[:, D_INNER + (SSM_GROUPS + g) * D_STATE:D_INNER + (SSM_GROUPS + g + 1) * D_STATE]
        b16.append(bg)
        c16.append(cg.astype(BF16))
        cbs.append(lax.dot_general(c16[g], bg.astype(BF16), (((1,), (1,)), ((), ())),
                                   preferred_element_type=F32))
    for g in range(SSM_GROUPS):
        s_prevs.append(state[:, g * half:(g + 1) * half])
        y_offs.append(jnp.dot(c16[g], s_prevs[g].astype(BF16), preferred_element_type=F32))
    y_cols = []
    for g in range(SSM_GROUPS):
        for pr in range(half // LANES):
            h0 = g * hg + 2 * pr
            ps = slice(h0 * SSM_HEAD_DIM, (h0 + 2) * SSM_HEAD_DIM)
            ms = []
            for h in (h0, h0 + 1):
                seg = a_cs[:, h:h + 1] - a_cs_t[h:h + 1, :]
                ms.append((cbs[g] * jnp.exp2(jnp.where(causal, seg, NEG_BIG))).astype(BF16))
            y_cols.append(jnp.dot(jnp.concatenate(ms, axis=1),
                                  jnp.concatenate([xdt_lo[:, ps], xdt_hi[:, ps]], axis=0),
                                  preferred_element_type=F32))
    for g in range(SSM_GROUPS):
        gs = slice(g * half, (g + 1) * half)
        state[:, gs] = decay_chunk[:, gs] * s_prevs[g] + jnp.dot(
            b16[g].T.astype(BF16), xw_b[:, gs], preferred_element_type=F32)

    y = (jnp.concatenate(y_cols, axis=1) + jnp.concatenate(y_offs, axis=1) * decay_from_start
         + dskip_ref[...] * xs)
    y = y * (z * _sigmoid(z))
    normed = []
    for g in range(SSM_GROUPS):
        yg = y[:, g * half:(g + 1) * half]
        ms = jnp.mean(yg * yg, axis=-1, keepdims=True)
        normed.append(yg * lax.rsqrt(ms + NORM_EPS))
    return jnp.concatenate(normed, axis=1) * nw_ref[...]


def _ssd_body(x_ref, nmw_ref, w_ref, cw_ref, cb_ref, dtb_ref, alog_ref, dskip_ref, nw_ref, e_ref,
              y_ref, h_ref, state, xbuf, zbuf, dtbuf):
    L = SSD_CHUNK
    rt = x_ref.shape[1]
    conv_dim = D_INNER + BC_WIDTH

    @pl.when(pl.program_id(1) == 0)
    def _():
        state[...] = jnp.zeros_like(state)
        xbuf[0:8, :] = jnp.zeros((8, conv_dim), F32)

    x = x_ref[0]
    ms = jnp.mean(x * x, axis=-1, keepdims=True)
    h = (x * lax.rsqrt(ms + NORM_EPS) * nmw_ref[...]).astype(BF16)
    h_ref[0] = h
    zbuf[...] = jnp.dot(h, w_ref[:, 0:D_INNER], preferred_element_type=F32)
    xbuf[8:8 + rt, :] = jnp.dot(h, w_ref[:, D_INNER:D_INNER + conv_dim], preferred_element_type=F32)
    dtbuf[...] = jnp.dot(h, w_ref[:, D_INNER + conv_dim:], preferred_element_type=F32)

    for c in range(rt // L):
        r0 = c * L
        y = _ssd_chunk(zbuf[r0:r0 + L, :], xbuf, r0, dtbuf[r0:r0 + L, :], cw_ref, cb_ref,
                       dtb_ref, alog_ref, dskip_ref, nw_ref, e_ref, state)
        y_ref[0, r0:r0 + L, :] = y.astype(BF16)
    xbuf[0:8, :] = xbuf[rt:rt + 8, :]


def _ssd(x, norm_mix_w, w_ssd, conv_w, conv_b, dt_bias, a_log, d_skip, ssm_norm_w):
    b, s, d = x.shape
    rt = min(4 * SSD_CHUNK, s)
    pad = LANES - SSM_HEADS
    conv_dim = D_INNER + BC_WIDTH
    dtb = jnp.pad(dt_bias, (0, pad))[None]
    alog = jnp.pad(a_log, (0, pad))[None]
    dskip = jnp.repeat(d_skip, SSM_HEAD_DIM)[None]
    expand = (np.arange(LANES)[:, None] == np.arange(D_INNER)[None, :] // SSM_HEAD_DIM).astype(np.float32)

    def const(shape):
        return pl.BlockSpec(shape, lambda i, c: (0,) * len(shape))

    return pl.pallas_call(
        _ssd_body,
        grid=(b, s // rt),
        in_specs=[pl.BlockSpec((1, rt, d), lambda i, c: (i, c, 0)),
                  const((1, d)), const((d, SSD_PROJ_WIDTH)),
                  const((CONV_K, conv_dim)), const((1, conv_dim)),
                  const((1, LANES)), const((1, LANES)),
                  const((1, D_INNER)), const((1, D_INNER)),
                  const((LANES, D_INNER))],
        out_specs=[pl.BlockSpec((1, rt, D_INNER), lambda i, c: (i, c, 0)),
                   pl.BlockSpec((1, rt, d), lambda i, c: (i, c, 0))],
        out_shape=[jax.ShapeDtypeStruct((b, s, D_INNER), BF16),
                   jax.ShapeDtypeStruct((b, s, d), BF16)],
        scratch_shapes=[pltpu.VMEM((D_STATE, D_INNER), F32),
                        pltpu.VMEM((rt + 8, conv_dim), F32),
                        pltpu.VMEM((rt, D_INNER), F32),
                        pltpu.VMEM((rt, LANES), F32)],
        compiler_params=_cparams(2),
    )(x, norm_mix_w[None], w_ssd, conv_w, conv_b[None], dtb, alog, dskip, ssm_norm_w[None],
      jnp.asarray(expand, dtype=BF16))


def _t5_causal_bucket(dist):
    max_exact = NUM_BUCKETS // 2
    large = max_exact + (np.log(np.maximum(dist, max_exact) / max_exact)
                         / math.log(MAX_DISTANCE / max_exact) * (NUM_BUCKETS - max_exact)).astype(np.int32)
    return np.where(dist < max_exact, dist, np.minimum(large, NUM_BUCKETS - 1)).astype(np.int32)


def _band_bias(rel_bias_group, dilation):
    blk = ATTN_BLK
    off = np.arange(blk)[:, None] + blk - np.arange(2 * blk)[None, :]
    in_win = (off >= 0) & (off <= blk)
    bucket = _t5_causal_bucket(np.clip(off, 0, None) * dilation)
    onehot = (bucket.reshape(-1, 1) == np.arange(NUM_BUCKETS)[None, :]).astype(np.float32)
    bias = jnp.dot(jnp.asarray(onehot), rel_bias_group.astype(F32), precision=HIGHEST)
    bias = jnp.transpose(bias.reshape(blk, 2 * blk, HEADS_PER_GROUP), (2, 0, 1))
    bias = jnp.where(in_win[None], bias, NEG_BIG)
    return bias.reshape(HEADS_PER_GROUP // 2, 2 * blk, 2 * blk)


def _attn_body(h_ref, w_ref, bias_ref, qw_ref, kw_ref, *rest, dilation):
    n_pairs = HEADS_PER_GROUP // 2
    o_refs = rest[0:n_pairs]
    lse_ref, qkv, kbuf, vbuf = rest[n_pairs:]
    blk = ATTN_BLK
    rt = h_ref.shape[1]
    span = blk * dilation
    n_sub = rt // span
    assert n_sub == 1 or dilation == 1
    step = pl.program_id(1)

    @pl.when(step == 0)
    def _():
        kbuf[...] = jnp.zeros_like(kbuf)
        vbuf[...] = jnp.zeros_like(vbuf)

    first_head = lax.broadcasted_iota(jnp.int32, (rt, LANES), 1) < ATTN_HEAD_DIM

    def head_norm(x, w_ref):
        xx = x * x
        s0 = jnp.sum(jnp.where(first_head, xx, 0.0), axis=-1, keepdims=True)
        s1 = jnp.sum(jnp.where(first_head, 0.0, xx), axis=-1, keepdims=True)
        ss = jnp.where(first_head, s0, s1)
        return x * lax.rsqrt(ss * (1.0 / ATTN_HEAD_DIM) + NORM_EPS) * w_ref[...]

    h = h_ref[0]
    for j in range(3 * n_pairs // 2):
        piece = jnp.dot(h, w_ref[:, 2 * j * LANES:2 * (j + 1) * LANES], preferred_element_type=F32)
        for half in range(2):
            slab = piece[:, half * LANES:(half + 1) * LANES]
            if j < n_pairs // 2:
                slab = head_norm(slab, qw_ref) * (ATTN_HEAD_DIM ** -0.5)
            elif j < n_pairs:
                slab = head_norm(slab, kw_ref)
            qkv[2 * j + half] = slab

    lane = lax.broadcasted_iota(jnp.int32, (blk, LANES), 1)
    lo_half = lane < ATTN_HEAD_DIM
    nt = (((1,), (1,)), ((), ()))

    units = 2

    def block_pair(it, carry):
        pairs = range(n_pairs)
        rows, res, slot, pen, q2, k_new, v_new, k_old, v_old = [], [], [], [], [], [], [], [], []
        for u in range(units):
            blk_id = units * it + u
            sub, r = (blk_id, 0) if dilation == 1 else (0, blk_id)
            gblk = step * n_sub + sub
            rows.append(pl.ds(sub * span + r, blk, stride=dilation))
            res.append(r)
            slot.append(gblk % 2)
            pen.append(jnp.where(gblk == 0, NEG_BIG, 0.0))
            k_new.append([qkv[n_pairs + p, rows[u], :].astype(BF16) for p in pairs])
            v_new.append([qkv[2 * n_pairs + p, rows[u], :].astype(BF16) for p in pairs])
            if dilation == 1 and u > 0:
                k_old.append(k_new[u - 1])
                v_old.append(v_new[u - 1])
            else:
                k_old.append([kbuf[1 - slot[u], r * n_pairs + p] for p in pairs])
                v_old.append([vbuf[1 - slot[u], r * n_pairs + p] for p in pairs])
            q2.append([])
            for p in pairs:
                qp = qkv[p, rows[u], :]
                q2[u].append(jnp.concatenate([jnp.where(lo_half, qp, 0.0), jnp.where(lo_half, 0.0, qp)],
                                             axis=0).astype(BF16))
        todo = [(u, p) for u in range(units) for p in pairs]
        s_prev = {up: lax.dot_general(q2[up[0]][up[1]], k_old[up[0]][up[1]], nt, preferred_element_type=F32)
                  for up in todo}
        s_cur = {up: lax.dot_general(q2[up[0]][up[1]], k_new[up[0]][up[1]], nt, preferred_element_type=F32)
                 for up in todo}
        e_prev, e_cur, m, d = {}, {}, {}, {}
        for up in todo:
            u, p = up
            sp = s_prev[up] + (bias_ref[p, :, 0:blk] + pen[u])
            sc = s_cur[up] + bias_ref[p, :, blk:2 * blk]
            m[up] = jnp.max(jnp.maximum(sp, sc), axis=-1, keepdims=True)
            ep = jnp.exp(sp - m[up])
            ec = jnp.exp(sc - m[up])
            d[up] = jnp.sum(ep + ec, axis=-1, keepdims=True)
            e_prev[up] = ep.astype(BF16)
            e_cur[up] = ec.astype(BF16)
        pv_prev = {up: jnp.dot(e_prev[up], v_old[up[0]][up[1]], preferred_element_type=F32) for up in todo}
        pv_cur = {up: jnp.dot(e_cur[up], v_new[up[0]][up[1]], preferred_element_type=F32) for up in todo}
        for u in range(units):
            lse_tile = jnp.zeros((blk, LANES), F32)
            for p in pairs:
                pv = (pv_prev[u, p] + pv_cur[u, p]) / d[u, p]
                o_refs[p][0, rows[u], :] = jnp.where(lo_half, pv[0:blk], pv[blk:2 * blk])
                lse = m[u, p] + jnp.log(d[u, p])
                lse_tile = jnp.where(lane == 2 * p, lse[0:blk], lse_tile)
                lse_tile = jnp.where(lane == 2 * p + 1, lse[blk:2 * blk], lse_tile)
                kbuf[slot[u], res[u] * n_pairs + p] = k_new[u][p]
                vbuf[slot[u], res[u] * n_pairs + p] = v_new[u][p]
            lse_ref[0, rows[u], :] = lse_tile
        return carry

    lax.fori_loop(0, n_sub * dilation // units, block_pair, 0)


def _attn_group(h3, w_qkv, dilation, bias, qw, kw):
    b, s, d = h3.shape
    blk = ATTN_BLK
    n_pairs = HEADS_PER_GROUP // 2
    rt = max(4 * blk, blk * dilation)
    rt = min(rt, s)

    def const(shape):
        return pl.BlockSpec(shape, lambda i, n: (0,) * len(shape))

    token_spec = pl.BlockSpec((1, rt, LANES), lambda i, n: (i, n, 0))
    res = pl.pallas_call(
        functools.partial(_attn_body, dilation=dilation),
        grid=(b, s // rt),
        in_specs=[pl.BlockSpec((1, rt, d), lambda i, n: (i, n, 0)),
                  const((d, 3 * GROUP_WIDTH)),
                  const((n_pairs, 2 * blk, 2 * blk)), const((1, LANES)), const((1, LANES))],
        out_specs=[token_spec] * (n_pairs + 1),
        out_shape=[jax.ShapeDtypeStruct((b, s, LANES), F32)] * (n_pairs + 1),
        scratch_shapes=[pltpu.VMEM((3 * n_pairs, rt, LANES), F32),
                        pltpu.VMEM((2, dilation * n_pairs, blk, LANES), BF16),
                        pltpu.VMEM((2, dilation * n_pairs, blk, LANES), BF16)],
        compiler_params=_cparams(2),
    )(h3, w_qkv, bias, qw, kw)
    outs = [o.reshape(b * s, LANES) for o in res[:n_pairs]]
    return outs, res[n_pairs].reshape(b * s, LANES)


def _mix_body(*refs):
    n_pairs = HEADS_PER_GROUP // 2
    y_ref = refs[0]
    o_refs = refs[1:1 + 3 * n_pairs]
    l_refs = refs[1 + 3 * n_pairs:4 + 3 * n_pairs]
    (h_ref, wgate_ref, x_ref, wssm_ref, wattn_ref, wout_ref, e8_ref, nfw_ref, wr_ref, br_ref,
     x2_ref, h2_ref, lg_ref) = refs[4 + 3 * n_pairs:]
    tm = x_ref.shape[0]
    lses = [l[...] for l in l_refs]
    mx = jnp.maximum(jnp.maximum(lses[0], lses[1]), lses[2])
    es = [jnp.exp(l - mx) for l in lses]
    inv = 1.0 / (es[0] + es[1] + es[2])
    e8 = e8_ref[...]
    att = jnp.zeros((tm, GROUP_WIDTH), F32)
    for g in range(3):
        w = es[g] * inv
        w_hi = w.astype(BF16)
        w_lo = (w - w_hi.astype(F32)).astype(BF16)
        w_full = (jnp.dot(w_hi, e8, preferred_element_type=F32)
                  + jnp.dot(w_lo, e8, preferred_element_type=F32))
        o_g = jnp.concatenate([o_refs[g * n_pairs + p][...] for p in range(n_pairs)], axis=1)
        att = att + w_full * o_g
    y_attn = jnp.dot(att.astype(BF16), wattn_ref[...], preferred_element_type=F32)
    y_ssm = jnp.dot(y_ref[...], wssm_ref[...], preferred_element_type=F32)
    h = h_ref[...]
    g_ssm = jnp.dot(h, wgate_ref[:, 0:D_MODEL], preferred_element_type=F32)
    g_attn = jnp.dot(h, wgate_ref[:, D_MODEL:2 * D_MODEL], preferred_element_type=F32)
    merged = _sigmoid(g_ssm) * y_ssm + _sigmoid(g_attn) * y_attn
    x2 = x_ref[...] + jnp.dot(merged.astype(BF16), wout_ref[...], preferred_element_type=F32)
    x2_ref[...] = x2
    ms = jnp.mean(x2 * x2, axis=-1, keepdims=True)
    h2 = x2 * lax.rsqrt(ms + NORM_EPS) * nfw_ref[...]
    bits = lax.bitcast_convert_type(h2.astype(BF16).astype(F32), jnp.uint32)
    packed = bits[:, 0:D_MODEL // 2] | (bits[:, D_MODEL // 2:] >> 16)
    h2_ref[...] = packed.reshape(tm, 1, D_MODEL // 2)
    lg_ref[...] = jnp.dot(h2.astype(BF16), wr_ref[...], preferred_element_type=F32) + br_ref[...]


def _mix_out(y_ssm, attn_outs, attn_lses, h2d, w_gate, x2d, w_ssm, w_attn, w_out, norm_ffn_w, w_router, b_router):
    t = x2d.shape[0]
    tm = min(512, t)
    e8 = (np.arange(LANES)[:, None] == np.arange(GROUP_WIDTH)[None, :] // ATTN_HEAD_DIM)
    e8 = jnp.asarray(e8.astype(np.float32), dtype=BF16)

    def rows(width, cb=0):
        return pl.BlockSpec((tm, width), lambda i: (i, cb))

    def const(shape):
        return pl.BlockSpec(shape, lambda i: (0,) * len(shape), pipeline_mode=pl.Buffered(1))

    return pl.pallas_call(
        _mix_body,
        grid=(t // tm,),
        in_specs=[rows(D_INNER)] + [rows(LANES)] * (len(attn_outs) + len(attn_lses)) + [
                  rows(D_MODEL), const((D_MODEL, 2 * D_MODEL)), rows(D_MODEL),
                  const((D_INNER, D_MODEL)), const((GROUP_WIDTH, D_MODEL)), const((D_MODEL, D_MODEL)),
                  const((LANES, GROUP_WIDTH)), const((1, D_MODEL)),
                  const((D_MODEL, LANES)), const((1, LANES))],
        out_specs=[rows(D_MODEL),
                   pl.BlockSpec((tm, 1, D_MODEL // 2), lambda i: (i, 0, 0)),
                   rows(LANES)],
        out_shape=[jax.ShapeDtypeStruct((t, D_MODEL), F32),
                   jax.ShapeDtypeStruct((t, 1, D_MODEL // 2), jnp.uint32),
                   jax.ShapeDtypeStruct((t, LANES), F32)],
        compiler_params=_cparams(1),
    )(y_ssm, *attn_outs, *attn_lses, h2d, w_gate, x2d, w_ssm, w_attn, w_out, e8,
      norm_ffn_w[None], w_router, b_router)


def _route_body(lg_ref, sel_ref, gate_ref, cnt_ref, counts):
    tm = lg_ref.shape[0]
    i = pl.program_id(0)

    @pl.when(i == 0)
    def _():
        counts[...] = jnp.zeros_like(counts)

    lg = lg_ref[...]
    lane = lax.broadcasted_iota(jnp.int32, (tm, LANES), 1)
    is_coarse = lane < N_EXPERT_GROUPS
    cmax = jnp.max(jnp.where(is_coarse, lg, NEG_BIG), axis=-1, keepdims=True)
    grp = jnp.min(jnp.where(is_coarse & (lg == cmax), lane, LANES), axis=-1, keepdims=True)
    group_p = 1.0 / jnp.sum(jnp.where(is_coarse, jnp.exp(lg - cmax), 0.0), axis=-1, keepdims=True)
    f_lo = N_EXPERT_GROUPS + EXPERTS_PER_GROUP * grp
    in_grp = (lane >= f_lo) & (lane < f_lo + EXPERTS_PER_GROUP)
    f1 = jnp.max(jnp.where(in_grp, lg, NEG_BIG), axis=-1, keepdims=True)
    i1 = jnp.min(jnp.where(in_grp & (lg == f1), lane, LANES), axis=-1, keepdims=True)
    rest = in_grp & (lane != i1)
    f2 = jnp.max(jnp.where(rest, lg, NEG_BIG), axis=-1, keepdims=True)
    i2 = jnp.min(jnp.where(rest & (lg == f2), lane, LANES), axis=-1, keepdims=True)
    e2 = jnp.exp(f2 - f1)
    g1 = group_p / (1.0 + e2)
    g2 = group_p * e2 / (1.0 + e2)

    e1 = i1 - N_EXPERT_GROUPS
    e2i = i2 - N_EXPERT_GROUPS
    oh1 = lane == e1
    oh2 = lane == e2i
    onehot = jnp.where(oh1 | oh2, 1.0, 0.0)
    r = lax.broadcasted_iota(jnp.int32, (tm, tm), 0)
    c = lax.broadcasted_iota(jnp.int32, (tm, tm), 1)
    before = jnp.dot((r > c).astype(BF16), onehot.astype(BF16), preferred_element_type=F32)
    pos = counts[0:1, :] + before
    r1 = jnp.sum(jnp.where(oh1, pos, 0.0), axis=-1, keepdims=True).astype(jnp.int32)
    r2 = jnp.sum(jnp.where(oh2, pos, 0.0), axis=-1, keepdims=True).astype(jnp.int32)
    counts[...] = counts[...] + jnp.sum(onehot, axis=0, keepdims=True)
    sel_ref[...] = jnp.where(lane == 0, e1, jnp.where(lane == 1, e2i, jnp.where(lane == 2, r1,
                                                                                  jnp.where(lane == 3, r2, 0))))
    gate_ref[...] = jnp.where(lane == 0, g1, jnp.where(lane == 1, g2, 0.0))
    cnt_ref[...] = counts[...]


def _route(logits):
    t = logits.shape[0]
    tm = min(512, t)
    return pl.pallas_call(
        _route_body,
        grid=(t // tm,),
        in_specs=[pl.BlockSpec((tm, LANES), lambda i: (i, 0))],
        out_specs=[pl.BlockSpec((tm, LANES), lambda i: (i, 0)),
                   pl.BlockSpec((tm, LANES), lambda i: (i, 0)),
                   pl.BlockSpec((8, LANES), lambda i: (0, 0))],
        out_shape=[jax.ShapeDtypeStruct((t, LANES), jnp.int32),
                   jax.ShapeDtypeStruct((t, LANES), F32),
                   jax.ShapeDtypeStruct((8, LANES), F32)],
        scratch_shapes=[pltpu.VMEM((8, LANES), F32)],
        compiler_params=_cparams(1),
    )(logits)


def _dispatch_body(meta_ref, dest_ref, h_ref, rows_ref, zbuf, zsem, sem):
    tm = h_ref.shape[0]

    def zero_copy(e):
        start = pl.multiple_of(meta_ref[e] - MOE_BLOCK, MOE_BLOCK)
        return pltpu.make_async_copy(zbuf, rows_ref.at[pl.ds(start, MOE_BLOCK)], zsem)

    def tail_copy(blk):
        start = pl.multiple_of(blk * MOE_BLOCK, MOE_BLOCK)
        return pltpu.make_async_copy(zbuf, rows_ref.at[pl.ds(start, MOE_BLOCK)], zsem)

    @pl.when(pl.program_id(0) == 0)
    def _():
        zbuf[...] = jnp.zeros_like(zbuf)

        def start_zero(e, carry):
            @pl.when(meta_ref[N_EXPERTS + e] > 0)
            def _():
                zero_copy(e).start()
            return carry

        def wait_zero(e, carry):
            @pl.when(meta_ref[N_EXPERTS + e] > 0)
            def _():
                zero_copy(e).wait()
            return carry

        def start_tail(blk, carry):
            tail_copy(blk).start()
            return carry

        def wait_tail(blk, carry):
            tail_copy(blk).wait()
            return carry

        n_used = meta_ref[2 * N_EXPERTS]
        n_blocks = rows_ref.shape[0] // MOE_BLOCK
        lax.fori_loop(0, N_EXPERTS, start_zero, 0)
        lax.fori_loop(n_used, n_blocks, start_tail, 0)
        lax.fori_loop(0, N_EXPERTS, wait_zero, 0)
        lax.fori_loop(n_used, n_blocks, wait_tail, 0)

    def issue(j, carry):
        for k in range(2):
            d = dest_ref[0, 2 * j + k]
            pltpu.make_async_copy(h_ref.at[j], rows_ref.at[d], sem).start(priority=k)
        return carry

    lax.fori_loop(0, tm, issue, 0, unroll=8)
    for _ in range(2):
        pltpu.make_async_copy(h_ref, rows_ref.at[pl.ds(0, tm)], sem).wait()


def _dispatch(meta, dest3, h2, n_rows):
    t = h2.shape[0]
    tm = dest3.shape[2] // 2
    return pl.pallas_call(
        _dispatch_body,
        grid_spec=pltpu.PrefetchScalarGridSpec(
            num_scalar_prefetch=1,
            grid=(t // tm,),
            in_specs=[pl.BlockSpec((None, 1, 2 * tm), lambda i, m: (i, 0, 0), memory_space=pltpu.SMEM),
                      pl.BlockSpec((tm, 1, D_MODEL // 2), lambda i, m: (i, 0, 0))],
            out_specs=pl.BlockSpec(memory_space=pl.ANY),
            scratch_shapes=[pltpu.VMEM((MOE_BLOCK, 1, D_MODEL // 2), jnp.uint32),
                            pltpu.SemaphoreType.DMA(()),
                            pltpu.SemaphoreType.DMA(())]),
        out_shape=jax.ShapeDtypeStruct((n_rows, 1, D_MODEL // 2), jnp.uint32),
        compiler_params=_cparams(1),
    )(meta, dest3, h2)


def _expert_body(sched_ref, x_ref, wg_hbm, wu_hbm, wd_hbm, y_ref, x2d, wg_f, wu_f, wd_f, wg_b, wu_b, wd_b, sems):
    blk = pl.program_id(0)
    n_used = sched_ref[4, 0]

    def weight_copies(expert, slot):
        return [pltpu.make_async_copy(src.at[expert], dst.at[slot], sems.at[slot])
                for src, dst in ((wg_hbm, wg_f), (wu_hbm, wu_f), (wd_hbm, wd_f))]

    @pl.when(blk < n_used)
    def _():
        expert = sched_ref[0, blk]
        slot = sched_ref[2, blk]
        nxt = sched_ref[3, blk]

        @pl.when(sched_ref[1, blk] == 1)
        def _():
            @pl.when(blk == 0)
            def _():
                for c in weight_copies(expert, slot):
                    c.start()

            for c in weight_copies(expert, slot):
                c.wait()

            @pl.when(nxt >= 0)
            def _():
                for c in weight_copies(nxt, 1 - slot):
                    c.start()

            wg_b[...] = wg_f[slot].astype(BF16)
            wu_b[...] = wu_f[slot].astype(BF16)
            wd_b[...] = wd_f[slot].astype(BF16)

        x2d[...] = x_ref[...].reshape(MOE_BLOCK, D_MODEL // 2)
        words = x2d[...]
        x = jnp.concatenate([lax.bitcast_convert_type(words & jnp.uint32(0xFFFF0000), F32),
                             lax.bitcast_convert_type(words << 16, F32)], axis=1).astype(BF16)
        g = jnp.dot(x, wg_b[...], preferred_element_type=F32)
        u = jnp.dot(x, wu_b[...], preferred_element_type=F32)
        hid = (g * _sigmoid(g)) * u
        y = jnp.dot(hid.astype(BF16), wd_b[...], preferred_element_type=F32)
        bits = lax.bitcast_convert_type(y.astype(BF16).astype(F32), jnp.uint32)
        packed = bits[:, 0:D_MODEL // 2] | (bits[:, D_MODEL // 2:] >> 16)
        y_ref[...] = packed.reshape(MOE_BLOCK, 1, D_MODEL // 2)

    @pl.when(blk >= n_used)
    def _():
        y_ref[...] = jnp.zeros_like(y_ref)


def _experts(sched, rows, w_gate, w_up, w_down):
    n_rows = rows.shape[0]
    n_blocks = n_rows // MOE_BLOCK

    def row_map(b, sched):
        return (b, 0, 0)

    any_spec = pl.BlockSpec(memory_space=pl.ANY)
    return pl.pallas_call(
        _expert_body,
        grid_spec=pltpu.PrefetchScalarGridSpec(
            num_scalar_prefetch=1,
            grid=(n_blocks,),
            in_specs=[pl.BlockSpec((MOE_BLOCK, 1, D_MODEL // 2), row_map), any_spec, any_spec, any_spec],
            out_specs=pl.BlockSpec((MOE_BLOCK, 1, D_MODEL // 2), row_map),
            scratch_shapes=[pltpu.VMEM((MOE_BLOCK, D_MODEL // 2), jnp.uint32),
                            pltpu.VMEM((2, D_MODEL, D_EXPERT), F32),
                            pltpu.VMEM((2, D_MODEL, D_EXPERT), F32),
                            pltpu.VMEM((2, D_EXPERT, D_MODEL), F32),
                            pltpu.VMEM((D_MODEL, D_EXPERT), BF16),
                            pltpu.VMEM((D_MODEL, D_EXPERT), BF16),
                            pltpu.VMEM((D_EXPERT, D_MODEL), BF16),
                            pltpu.SemaphoreType.DMA((2,))]),
        out_shape=jax.ShapeDtypeStruct((n_rows, 1, D_MODEL // 2), jnp.uint32),
        compiler_params=_cparams(1),
    )(sched, rows, w_gate, w_up, w_down)


def _combine_body(dcur_ref, dnext_ref, gate_ref, x2_ref, y_ref, o_ref, buf_a, buf_b, y2d, sem_a, sem_b):
    tm = x2_ref.shape[0]
    i = pl.program_id(0)
    n = pl.num_programs(0)

    def issue(dref, buf, sem):
        def body(j, carry):
            for k in range(2):
                d = dref[0, 2 * j + k]
                pltpu.make_async_copy(y_ref.at[d], buf.at[k * tm + j], sem).start(priority=k)
            return carry
        lax.fori_loop(0, tm, body, 0, unroll=8)

    def finish(buf, sem):
        pltpu.make_async_copy(y_ref.at[pl.ds(0, 2 * tm)], buf, sem).wait()
        y2d[...] = buf[...].reshape(2 * tm, D_MODEL // 2)
        words = y2d[...]
        y = jnp.concatenate([lax.bitcast_convert_type(words & jnp.uint32(0xFFFF0000), F32),
                             lax.bitcast_convert_type(words << 16, F32)], axis=1)
        g = gate_ref[...]
        o_ref[...] = x2_ref[...] + g[:, 0:1] * y[0:tm, :] + g[:, 1:2] * y[tm:2 * tm, :]

    @pl.when(i == 0)
    def _():
        issue(dcur_ref, buf_a, sem_a)

    for par, (cur, cur_sem, nxt, nxt_sem) in enumerate(((buf_a, sem_a, buf_b, sem_b),
                                                        (buf_b, sem_b, buf_a, sem_a))):
        @pl.when(i % 2 == par)
        def _(cur=cur, cur_sem=cur_sem, nxt=nxt, nxt_sem=nxt_sem):
            @pl.when(i + 1 < n)
            def _():
                issue(dnext_ref, nxt, nxt_sem)
            finish(cur, cur_sem)


def _combine(dest3, gates, x2, y_rows):
    t = x2.shape[0]
    tm = dest3.shape[2] // 2
    nt = t // tm
    return pl.pallas_call(
        _combine_body,
        grid=(nt,),
        in_specs=[pl.BlockSpec((None, 1, 2 * tm), lambda i: (i, 0, 0), memory_space=pltpu.SMEM),
                  pl.BlockSpec((None, 1, 2 * tm), lambda i: (jnp.minimum(i + 1, nt - 1), 0, 0),
                               memory_space=pltpu.SMEM),
                  pl.BlockSpec((tm, LANES), lambda i: (i, 0)),
                  pl.BlockSpec((tm, D_MODEL), lambda i: (i, 0)),
                  pl.BlockSpec(memory_space=pl.ANY)],
        out_specs=pl.BlockSpec((tm, D_MODEL), lambda i: (i, 0)),
        out_shape=jax.ShapeDtypeStruct((t, D_MODEL), F32),
        scratch_shapes=[pltpu.VMEM((2 * tm, 1, D_MODEL // 2), jnp.uint32),
                        pltpu.VMEM((2 * tm, 1, D_MODEL // 2), jnp.uint32),
                        pltpu.VMEM((2 * tm, D_MODEL // 2), jnp.uint32),
                        pltpu.SemaphoreType.DMA(()),
                        pltpu.SemaphoreType.DMA(())],
        compiler_params=_cparams(1),
    )(dest3, dest3, gates, x2, y_rows)


def _layer(x, norm_mix_w, w_in, conv_w, conv_b, dt_bias, a_log, d_skip, ssm_norm_w, w_ssm_proj,
           q_norm_w, k_norm_w, rel_bias, w_attn_proj, w_out, norm_ffn_w, w_coarse, b_coarse,
           w_fine, b_fine, w_gate_exp, w_up_exp, w_down_exp):
    b, s, d = x.shape
    t = b * s
    x2d = x.reshape(t, d)

    dt_lo = 2 * D_INNER + BC_WIDTH
    qkv_lo = dt_lo + SSM_HEADS
    gate_lo = qkv_lo + 3 * ATTN_WIDTH
    w_ssd = jnp.concatenate([w_in[:, :qkv_lo], jnp.zeros((d, LANES - SSM_HEADS), w_in.dtype)],
                            axis=1).astype(BF16)
    w_gate = w_in[:, gate_lo:].astype(BF16)
    y_ssm, h3 = _ssd(x, norm_mix_w, w_ssd, conv_w, conv_b, dt_bias, a_log, d_skip, ssm_norm_w)
    y_ssm = y_ssm.reshape(t, D_INNER)
    h2d = h3.reshape(t, d)

    qw = jnp.tile(q_norm_w, 2)[None]
    kw = jnp.tile(k_norm_w, 2)[None]
    attn_outs, attn_lses = [], []
    for gi, (window, dilation) in enumerate(DILATED_CONFIGS):
        assert window // dilation == ATTN_BLK and s % window == 0
        hs = slice(gi * GROUP_WIDTH, (gi + 1) * GROUP_WIDTH)
        w_qkv = jnp.concatenate([w_in[:, qkv_lo + j * ATTN_WIDTH:qkv_lo + (j + 1) * ATTN_WIDTH][:, hs]
                                 for j in range(3)], axis=1).astype(BF16)
        bias = _band_bias(rel_bias[:, gi * HEADS_PER_GROUP:(gi + 1) * HEADS_PER_GROUP], dilation)
        o, l = _attn_group(h3, w_qkv, dilation, bias, qw, kw)
        attn_outs.extend(o)
        attn_lses.append(l)

    n_route = N_EXPERT_GROUPS + N_EXPERTS
    w_router = jnp.pad(jnp.concatenate([w_coarse, w_fine], axis=1), ((0, 0), (0, LANES - n_route)))
    b_router = jnp.pad(jnp.concatenate([b_coarse, b_fine]), (0, LANES - n_route))[None]
    x2, h2, logits = _mix_out(y_ssm, attn_outs, attn_lses, h2d, w_gate, x2d, w_ssm_proj.astype(BF16),
                              w_attn_proj.astype(BF16), w_out.astype(BF16), norm_ffn_w,
                              w_router.astype(BF16), b_router)

    sel, gates, counts = _route(logits)

    cnt = counts[0, :N_EXPERTS].astype(jnp.int32)
    padded = (cnt + MOE_BLOCK - 1) // MOE_BLOCK * MOE_BLOCK
    pad_end = jnp.cumsum(padded)
    n_blocks = -(-(2 * t + N_EXPERTS * (MOE_BLOCK - 1)) // MOE_BLOCK)
    block_start = jnp.arange(n_blocks, dtype=jnp.int32) * MOE_BLOCK
    block_expert = jnp.minimum(jnp.sum((pad_end[None, :] <= block_start[:, None]).astype(jnp.int32), axis=1),
                               N_EXPERTS - 1)
    n_used = (pad_end[-1:] // MOE_BLOCK).astype(jnp.int32)
    meta = jnp.concatenate([pad_end, padded, n_used]).astype(jnp.int32)
    chosen = sel[:, 0:2, None] == jnp.arange(N_EXPERTS, dtype=jnp.int32)
    dest = jnp.sum(jnp.where(chosen, pad_end - padded, 0), axis=-1) + sel[:, 2:4]
    blk_ids = jnp.arange(n_blocks, dtype=jnp.int32)
    first = ((blk_ids == 0) | (block_expert != jnp.roll(block_expert, 1))) & (blk_ids < n_used[0])
    slot = (jnp.cumsum(first.astype(jnp.int32)) - 1) % 2
    nxt_blk = blk_ids + padded[block_expert] // MOE_BLOCK
    nxt = jnp.where(nxt_blk < n_used[0], block_expert[jnp.minimum(nxt_blk, n_blocks - 1)], -1)
    sched = jnp.stack([block_expert, first.astype(jnp.int32), slot, nxt,
                       jnp.broadcast_to(n_used, (n_blocks,))]).astype(jnp.int32)

    tm_d = min(1024, t)
    dest_d = dest.reshape(t // tm_d, 1, 2 * tm_d)
    rows = _dispatch(meta, dest_d, h2, n_blocks * MOE_BLOCK)
    y_rows = _experts(sched, rows, w_gate_exp, w_up_exp, w_down_exp)
    tm_c = min(512, t)
    dest_c = dest.reshape(t // tm_c, 1, 2 * tm_c)
    out = _combine(dest_c, gates, x2, y_rows)
    return out.reshape(b, s, d)


def kernel(x, norm_mix_w, w_in, conv_w, conv_b, dt_bias, a_log, d_skip, ssm_norm_w, w_ssm_proj,
           q_norm_w, k_norm_w, rel_bias, w_attn_proj, w_out, norm_ffn_w, w_coarse, b_coarse,
           w_fine, b_fine, w_gate_exp, w_up_exp, w_down_exp):
    depth = norm_mix_w.shape[0]
    for layer in range(depth):
        x = _layer(x, norm_mix_w[layer], w_in[layer], conv_w[layer], conv_b[layer], dt_bias[layer],
                   a_log[layer], d_skip[layer], ssm_norm_w[layer], w_ssm_proj[layer],
                   q_norm_w[layer], k_norm_w[layer], rel_bias, w_attn_proj[layer], w_out[layer],
                   norm_ffn_w[layer], w_coarse[layer], b_coarse[layer], w_fine[layer], b_fine[layer],
                   w_gate_exp[layer], w_up_exp[layer], w_down_exp[layer])
    return x
```

```python
import functools
import math

import jax
import jax.numpy as jnp
import numpy as np
from jax import lax
from jax.experimental import pallas as pl
from jax.experimental.pallas import tpu as pltpu

F32 = jnp.float32
BF16 = jnp.bfloat16
HIGHEST = lax.Precision.HIGHEST

LANES = 128
NORM_EPS = 1e-6
NEG_BIG = -1e30
LOG2_E = math.log2(math.e)

D_MODEL = 1024
D_INNER = 2048
SSM_HEAD_DIM = 64
SSM_HEADS = 32
SSM_GROUPS = 2
D_STATE = 128
CONV_K = 4
BC_WIDTH = 2 * SSM_GROUPS * D_STATE
SSD_CHUNK = 128
ATTN_HEAD_DIM = 64
DILATED_CONFIGS = ((128, 1), (512, 4), (2048, 16))
HEADS_PER_GROUP = 8
GROUP_WIDTH = HEADS_PER_GROUP * ATTN_HEAD_DIM
ATTN_WIDTH = 3 * GROUP_WIDTH
ATTN_BLK = 128
NUM_BUCKETS = 32
MAX_DISTANCE = 2048
N_EXPERT_GROUPS = 8
EXPERTS_PER_GROUP = 8
N_EXPERTS = 64
D_EXPERT = 512
MOE_BLOCK = 256

SSD_PROJ_WIDTH = 2 * D_INNER + BC_WIDTH + LANES

VMEM_LIMIT = 56 * 1024 * 1024

PROJ_STEP_ROWS = 512
MIX_TILE = 512
ROUTE_TILE = 512
DISPATCH_TILE = 1024
COMBINE_TILE = 512


def _sigmoid(x):
    return 1.0 / (1.0 + jnp.exp(-x))


def _cparams(n_axes):
    return pltpu.CompilerParams(dimension_semantics=("arbitrary",) * n_axes,
                                vmem_limit_bytes=VMEM_LIMIT)


def _dot3(x, y, x_is_exact):
    v = y if x_is_exact else x
    hi = v.astype(BF16)
    r1 = v - hi.astype(F32)
    mid = r1.astype(BF16)
    lo = (r1 - mid.astype(F32)).astype(BF16)
    if x_is_exact:
        return sum(jnp.dot(x, part, preferred_element_type=F32) for part in (hi, mid, lo))
    return sum(jnp.dot(part, y, preferred_element_type=F32) for part in (hi, mid, lo))


def _ssd_chunk(z, xbuf, r0, dt_raw, cw_ref, cb_ref, dtb_ref, alog_ref, dskip_ref, nw_ref, e_ref, state):
    L = SSD_CHUNK
    half = D_INNER // SSM_GROUPS
    hg = SSM_HEADS // SSM_GROUPS

    conv = cb_ref[...] + cw_ref[CONV_K - 1:CONV_K, :] * xbuf[8 + r0:8 + r0 + L, :]
    for k in range(CONV_K - 1):
        lo = 8 + r0 - (CONV_K - 1 - k)
        conv = conv + cw_ref[k:k + 1, :] * xbuf[lo:lo + L, :]
    xbc = conv * _sigmoid(conv)
    xs = xbc[:, :D_INNER]

    lane = lax.broadcasted_iota(jnp.int32, (L, LANES), 1)
    row = lax.broadcasted_iota(jnp.int32, (L, L), 0)
    col = lax.broadcasted_iota(jnp.int32, (L, L), 1)
    causal = row >= col

    v = dt_raw + dtb_ref[...]
    dt = jnp.maximum(v, 0.0) + jnp.log1p(jnp.exp(-jnp.abs(v)))
    dt = jnp.where(lane < SSM_HEADS, dt, 0.0)
    adt = dt * (-jnp.exp(alog_ref[...]))
    a_cs = _dot3(causal.astype(BF16), adt, True) * LOG2_E
    a_cs_t = a_cs.T
    expand = e_ref[...]
    a_full = _dot3(a_cs, expand, False)
    dt_full = _dot3(dt, expand, False)
    a_tot = a_full[L - 1:L, :]
    decay_from_start = jnp.exp2(a_full)
    decay_to_end = jnp.exp2(a_tot - a_full)
    decay_chunk = jnp.exp2(a_tot)

    xdt = xs * dt_full
    xw_b = (xdt * decay_to_end).astype(BF16)
    head_lo = lax.broadcasted_iota(jnp.int32, (L, D_INNER), 1) % (2 * SSM_HEAD_DIM) < SSM_HEAD_DIM
    xdt_lo = jnp.where(head_lo, xdt, 0.0).astype(BF16)
    xdt_hi = jnp.where(head_lo, 0.0, xdt).astype(BF16)

    b16, c16, cbs, y_offs, s_prevs = [], [], [], [], []
    for g in range(SSM_GROUPS):
        bg = xbc[:, D_INNER + g * D_STATE:D_INNER + (g + 1) * D_STATE]
        cg = xbc[:, D_INNER + (SSM_GROUPS + g) * D_STATE:D_INNER + (SSM_GROUPS + g + 1) * D_STATE]
        b16.append(bg)
        c16.append(cg.astype(BF16))
        cbs.append(lax.dot_general(c16[g], bg.astype(BF16), (((1,), (1,)), ((), ())),
                                   preferred_element_type=F32))
    for g in range(SSM_GROUPS):
        s_prevs.append(state[:, g * half:(g + 1) * half])
        y_offs.append(jnp.dot(c16[g], s_prevs[g].astype(BF16), preferred_element_type=F32))
    y_cols = []
    for g in range(SSM_GROUPS):
        for pr in range(half // LANES):
            h0 = g * hg + 2 * pr
            ps = slice(h0 * SSM_HEAD_DIM, (h0 + 2) * SSM_HEAD_DIM)
            ms = []
            for h in (h0, h0 + 1):
                seg = a_cs[:, h:h + 1] - a_cs_t[h:h + 1, :]
                ms.append((cbs[g] * jnp.exp2(jnp.where(causal, seg, NEG_BIG))).astype(BF16))
            y_cols.append(jnp.dot(jnp.concatenate(ms, axis=1),
                                  jnp.concatenate([xdt_lo[:, ps], xdt_hi[:, ps]], axis=0),
                                  preferred_element_type=F32))
    for g in range(SSM_GROUPS):
        gs = slice(g * half, (g + 1) * half)
        state[:, gs] = decay_chunk[:, gs] * s_prevs[g] + jnp.dot(
            b16[g].T.astype(BF16), xw_b[:, gs], preferred_element_type=F32)

    y = (jnp.concatenate(y_cols, axis=1) + jnp.concatenate(y_offs, axis=1) * decay_from_start
         + dskip_ref[...] * xs)
    y = y * (z * _sigmoid(z))
    normed = []
    for g in range(SSM_GROUPS):
        yg = y[:, g * half:(g + 1) * half]
        ms = jnp.mean(yg * yg, axis=-1, keepdims=True)
        normed.append(yg * lax.rsqrt(ms + NORM_EPS))
    return jnp.concatenate(normed, axis=1) * nw_ref[...]


def _ssd_body(x_ref, nmw_ref, w_ref, cw_ref, cb_ref, dtb_ref, alog_ref, dskip_ref, nw_ref, e_ref,
              y_ref, h_ref, state, xbuf, zbuf, dtbuf):
    L = SSD_CHUNK
    rt = x_ref.shape[1]
    conv_dim = D_INNER + BC_WIDTH

    @pl.when(pl.program_id(1) == 0)
    def _():
        state[...] = jnp.zeros_like(state)
        xbuf[0:8, :] = jnp.zeros((8, conv_dim), F32)

    x = x_ref[0]
    ms = jnp.mean(x * x, axis=-1, keepdims=True)
    h = (x * lax.rsqrt(ms + NORM_EPS) * nmw_ref[...]).astype(BF16)
    h_ref[0] = h
    zbuf[...] = jnp.dot(h, w_ref[:, 0:D_INNER], preferred_element_type=F32)
    xbuf[8:8 + rt, :] = jnp.dot(h, w_ref[:, D_INNER:D_INNER + conv_dim], preferred_element_type=F32)
    dtbuf[...] = jnp.dot(h, w_ref[:, D_INNER + conv_dim:], preferred_element_type=F32)

    for c in range(rt // L):
        r0 = c * L
        y = _ssd_chunk(zbuf[r0:r0 + L, :], xbuf, r0, dtbuf[r0:r0 + L, :], cw_ref, cb_ref,
                       dtb_ref, alog_ref, dskip_ref, nw_ref, e_ref, state)
        y_ref[0, r0:r0 + L, :] = y.astype(BF16)
    xbuf[0:8, :] = xbuf[rt:rt + 8, :]


def _ssd(x, norm_mix_w, w_ssd, conv_w, conv_b, dt_bias, a_log, d_skip, ssm_norm_w):
    b, s, d = x.shape
    rt = min(PROJ_STEP_ROWS, s)
    pad = LANES - SSM_HEADS
    conv_dim = D_INNER + BC_WIDTH
    dtb = jnp.pad(dt_bias, (0, pad))[None]
    alog = jnp.pad(a_log, (0, pad))[None]
    dskip = jnp.repeat(d_skip, SSM_HEAD_DIM)[None]
    expand = (np.arange(LANES)[:, None] == np.arange(D_INNER)[None, :] // SSM_HEAD_DIM).astype(np.float32)

    def const(shape):
        return pl.BlockSpec(shape, lambda i, c: (0,) * len(shape))

    return pl.pallas_call(
        _ssd_body,
        grid=(b, s // rt),
        in_specs=[pl.BlockSpec((1, rt, d), lambda i, c: (i, c, 0)),
                  const((1, d)), const((d, SSD_PROJ_WIDTH)),
                  const((CONV_K, conv_dim)), const((1, conv_dim)),
                  const((1, LANES)), const((1, LANES)),
                  const((1, D_INNER)), const((1, D_INNER)),
                  const((LANES, D_INNER))],
        out_specs=[pl.BlockSpec((1, rt, D_INNER), lambda i, c: (i, c, 0)),
                   pl.BlockSpec((1, rt, d), lambda i, c: (i, c, 0))],
        out_shape=[jax.ShapeDtypeStruct((b, s, D_INNER), BF16),
                   jax.ShapeDtypeStruct((b, s, d), BF16)],
        scratch_shapes=[pltpu.VMEM((D_STATE, D_INNER), F32),
                        pltpu.VMEM((rt + 8, conv_dim), F32),
                        pltpu.VMEM((rt, D_INNER), F32),
                        pltpu.VMEM((rt, LANES), F32)],
        compiler_params=_cparams(2),
    )(x, norm_mix_w[None], w_ssd, conv_w, conv_b[None], dtb, alog, dskip, ssm_norm_w[None],
      jnp.asarray(expand, dtype=BF16))


def _t5_causal_bucket(dist):
    max_exact = NUM_BUCKETS // 2
    large = max_exact + (np.log(np.maximum(dist, max_exact) / max_exact)
                         / math.log(MAX_DISTANCE / max_exact) * (NUM_BUCKETS - max_exact)).astype(np.int32)
    return np.where(dist < max_exact, dist, np.minimum(large, NUM_BUCKETS - 1)).astype(np.int32)


def _band_bias(rel_bias_group, dilation):
    blk = ATTN_BLK
    off = np.arange(blk)[:, None] + blk - np.arange(2 * blk)[None, :]
    in_win = (off >= 0) & (off <= blk)
    bucket = _t5_causal_bucket(np.clip(off, 0, None) * dilation)
    onehot = (bucket.reshape(-1, 1) == np.arange(NUM_BUCKETS)[None, :]).astype(np.float32)
    bias = jnp.dot(jnp.asarray(onehot), rel_bias_group.astype(F32), precision=HIGHEST)
    bias = jnp.transpose(bias.reshape(blk, 2 * blk, HEADS_PER_GROUP), (2, 0, 1))
    bias = jnp.where(in_win[None], bias, NEG_BIG)
    return bias.reshape(HEADS_PER_GROUP // 2, 2 * blk, 2 * blk)


def _attn_body(h_ref, w_ref, bias_ref, qw_ref, kw_ref, *rest, dilation):
    n_pairs = HEADS_PER_GROUP // 2
    o_refs = rest[0:n_pairs]
    lse_ref, qkv, kbuf, vbuf = rest[n_pairs:]
    blk = ATTN_BLK
    rt = h_ref.shape[1]
    span = blk * dilation
    n_sub = rt // span
    assert n_sub == 1 or dilation == 1
    step = pl.program_id(1)

    @pl.when(step == 0)
    def _():
        kbuf[...] = jnp.zeros_like(kbuf)
        vbuf[...] = jnp.zeros_like(vbuf)

    first_head = lax.broadcasted_iota(jnp.int32, (rt, LANES), 1) < ATTN_HEAD_DIM

    def head_norm(x, w_ref):
        xx = x * x
        s0 = jnp.sum(jnp.where(first_head, xx, 0.0), axis=-1, keepdims=True)
        s1 = jnp.sum(jnp.where(first_head, 0.0, xx), axis=-1, keepdims=True)
        ss = jnp.where(first_head, s0, s1)
        return x * lax.rsqrt(ss * (1.0 / ATTN_HEAD_DIM) + NORM_EPS) * w_ref[...]

    h = h_ref[0]
    for j in range(3 * n_pairs // 2):
        piece = jnp.dot(h, w_ref[:, 2 * j * LANES:2 * (j + 1) * LANES], preferred_element_type=F32)
        for half in range(2):
            slab = piece[:, half * LANES:(half + 1) * LANES]
            if j < n_pairs // 2:
                slab = head_norm(slab, qw_ref) * (ATTN_HEAD_DIM ** -0.5)
            elif j < n_pairs:
                slab = head_norm(slab, kw_ref)
            qkv[2 * j + half] = slab

    lane = lax.broadcasted_iota(jnp.int32, (blk, LANES), 1)
    lo_half = lane < ATTN_HEAD_DIM
    nt = (((1,), (1,)), ((), ()))

    units = 2

    def block_pair(it, carry):
        pairs = range(n_pairs)
        rows, res, slot, pen, q2, k_new, v_new, k_old, v_old = [], [], [], [], [], [], [], [], []
        for u in range(units):
            blk_id = units * it + u
            sub, r = (blk_id, 0) if dilation == 1 else (0, blk_id)
            gblk = step * n_sub + sub
            rows.append(pl.ds(sub * span + r, blk, stride=dilation))
            res.append(r)
            slot.append(gblk % 2)
            pen.append(jnp.where(gblk == 0, NEG_BIG, 0.0))
            k_new.append([qkv[n_pairs + p, rows[u], :].astype(BF16) for p in pairs])
            v_new.append([qkv[2 * n_pairs + p, rows[u], :].astype(BF16) for p in pairs])
            if dilation == 1 and u > 0:
                k_old.append(k_new[u - 1])
                v_old.append(v_new[u - 1])
            else:
                k_old.append([kbuf[1 - slot[u], r * n_pairs + p] for p in pairs])
                v_old.append([vbuf[1 - slot[u], r * n_pairs + p] for p in pairs])
            q2.append([])
            for p in pairs:
                qp = qkv[p, rows[u], :]
                q2[u].append(jnp.concatenate([jnp.where(lo_half, qp, 0.0), jnp.where(lo_half, 0.0, qp)],
                                             axis=0).astype(BF16))
        todo = [(u, p) for u in range(units) for p in pairs]
        s_prev = {up: lax.dot_general(q2[up[0]][up[1]], k_old[up[0]][up[1]], nt, preferred_element_type=F32)
                  for up in todo}
        s_cur = {up: lax.dot_general(q2[up[0]][up[1]], k_new[up[0]][up[1]], nt, preferred_element_type=F32)
                 for up in todo}
        e_prev, e_cur, m, d = {}, {}, {}, {}
        for up in todo:
            u, p = up
            sp = s_prev[up] + (bias_ref[p, :, 0:blk] + pen[u])
            sc = s_cur[up] + bias_ref[p, :, blk:2 * blk]
            m[up] = jnp.max(jnp.maximum(sp, sc), axis=-1, keepdims=True)
            ep = jnp.exp(sp - m[up])
            ec = jnp.exp(sc - m[up])
            d[up] = jnp.sum(ep + ec, axis=-1, keepdims=True)
            e_prev[up] = ep.astype(BF16)
            e_cur[up] = ec.astype(BF16)
        pv_prev = {up: jnp.dot(e_prev[up], v_old[up[0]][up[1]], preferred_element_type=F32) for up in todo}
        pv_cur = {up: jnp.dot(e_cur[up], v_new[up[0]][up[1]], preferred_element_type=F32) for up in todo}
        for u in range(units):
            lse_tile = jnp.zeros((blk, LANES), F32)
            for p in pairs:
                pv = (pv_prev[u, p] + pv_cur[u, p]) / d[u, p]
                o_refs[p][0, rows[u], :] = jnp.where(lo_half, pv[0:blk], pv[blk:2 * blk])
                lse = m[u, p] + jnp.log(d[u, p])
                lse_tile = jnp.where(lane == 2 * p, lse[0:blk], lse_tile)
                lse_tile = jnp.where(lane == 2 * p + 1, lse[blk:2 * blk], lse_tile)
                kbuf[slot[u], res[u] * n_pairs + p] = k_new[u][p]
                vbuf[slot[u], res[u] * n_pairs + p] = v_new[u][p]
            lse_ref[0, rows[u], :] = lse_tile
        return carry

    lax.fori_loop(0, n_sub * dilation // units, block_pair, 0)


def _attn_group(h3, w_qkv, dilation, bias, qw, kw):
    b, s, d = h3.shape
    blk = ATTN_BLK
    n_pairs = HEADS_PER_GROUP // 2
    rt = max(PROJ_STEP_ROWS, blk * dilation)
    rt = min(rt, s)

    def const(shape):
        return pl.BlockSpec(shape, lambda i, n: (0,) * len(shape))

    token_spec = pl.BlockSpec((1, rt, LANES), lambda i, n: (i, n, 0))
    res = pl.pallas_call(
        functools.partial(_attn_body, dilation=dilation),
        grid=(b, s // rt),
        in_specs=[pl.BlockSpec((1, rt, d), lambda i, n: (i, n, 0)),
                  const((d, 3 * GROUP_WIDTH)),
                  const((n_pairs, 2 * blk, 2 * blk)), const((1, LANES)), const((1, LANES))],
        out_specs=[token_spec] * (n_pairs + 1),
        out_shape=[jax.ShapeDtypeStruct((b, s, LANES), F32)] * (n_pairs + 1),
        scratch_shapes=[pltpu.VMEM((3 * n_pairs, rt, LANES), F32),
                        pltpu.VMEM((2, dilation * n_pairs, blk, LANES), BF16),
                        pltpu.VMEM((2, dilation * n_pairs, blk, LANES), BF16)],
        compiler_params=_cparams(2),
    )(h3, w_qkv, bias, qw, kw)
    outs = [o.reshape(b * s, LANES) for o in res[:n_pairs]]
    return outs, res[n_pairs].reshape(b * s, LANES)


def _mix_body(*refs):
    n_pairs = HEADS_PER_GROUP // 2
    y_ref = refs[0]
    o_refs = refs[1:1 + 3 * n_pairs]
    l_refs = refs[1 + 3 * n_pairs:4 + 3 * n_pairs]
    (h_ref, wgate_ref, x_ref, wssm_ref, wattn_ref, wout_ref, e8_ref, nfw_ref, wr_ref, br_ref,
     x2_ref, h2_ref, lg_ref) = refs[4 + 3 * n_pairs:]
    tm = x_ref.shape[0]
    lses = [l[...] for l in l_refs]
    mx = jnp.maximum(jnp.maximum(lses[0], lses[1]), lses[2])
    es = [jnp.exp(l - mx) for l in lses]
    inv = 1.0 / (es[0] + es[1] + es[2])
    e8 = e8_ref[...]
    att = jnp.zeros((tm, GROUP_WIDTH), F32)
    for g in range(3):
        w = es[g] * inv
        w_hi = w.astype(BF16)
        w_lo = (w - w_hi.astype(F32)).astype(BF16)
        w_full = (jnp.dot(w_hi, e8, preferred_element_type=F32)
                  + jnp.dot(w_lo, e8, preferred_element_type=F32))
        o_g = jnp.concatenate([o_refs[g * n_pairs + p][...] for p in range(n_pairs)], axis=1)
        att = att + w_full * o_g
    y_attn = jnp.dot(att.astype(BF16), wattn_ref[...], preferred_element_type=F32)
    y_ssm = jnp.dot(y_ref[...], wssm_ref[...], preferred_element_type=F32)
    h = h_ref[...]
    g_ssm = jnp.dot(h, wgate_ref[:, 0:D_MODEL], preferred_element_type=F32)
    g_attn = jnp.dot(h, wgate_ref[:, D_MODEL:2 * D_MODEL], preferred_element_type=F32)
    merged = _sigmoid(g_ssm) * y_ssm + _sigmoid(g_attn) * y_attn
    x2 = x_ref[...] + jnp.dot(merged.astype(BF16), wout_ref[...], preferred_element_type=F32)
    x2_ref[...] = x2
    ms = jnp.mean(x2 * x2, axis=-1, keepdims=True)
    h2 = x2 * lax.rsqrt(ms + NORM_EPS) * nfw_ref[...]
    bits = lax.bitcast_convert_type(h2.astype(BF16).astype(F32), jnp.uint32)
    packed = bits[:, 0:D_MODEL // 2] | (bits[:, D_MODEL // 2:] >> 16)
    h2_ref[...] = packed.reshape(tm, 1, D_MODEL // 2)
    lg_ref[...] = jnp.dot(h2.astype(BF16), wr_ref[...], preferred_element_type=F32) + br_ref[...]


def _mix_out(y_ssm, attn_outs, attn_lses, h2d, w_gate, x2d, w_ssm, w_attn, w_out, norm_ffn_w, w_router, b_router):
    t = x2d.shape[0]
    tm = min(MIX_TILE, t)
    e8 = (np.arange(LANES)[:, None] == np.arange(GROUP_WIDTH)[None, :] // ATTN_HEAD_DIM)
    e8 = jnp.asarray(e8.astype(np.float32), dtype=BF16)

    def rows(width, cb=0):
        return pl.BlockSpec((tm, width), lambda i: (i, cb))

    def const(shape):
        return pl.BlockSpec(shape, lambda i: (0,) * len(shape), pipeline_mode=pl.Buffered(1))

    return pl.pallas_call(
        _mix_body,
        grid=(t // tm,),
        in_specs=[rows(D_INNER)] + [rows(LANES)] * (len(attn_outs) + len(attn_lses)) + [
                  rows(D_MODEL), const((D_MODEL, 2 * D_MODEL)), rows(D_MODEL),
                  const((D_INNER, D_MODEL)), const((GROUP_WIDTH, D_MODEL)), const((D_MODEL, D_MODEL)),
                  const((LANES, GROUP_WIDTH)), const((1, D_MODEL)),
                  const((D_MODEL, LANES)), const((1, LANES))],
        out_specs=[rows(D_MODEL),
                   pl.BlockSpec((tm, 1, D_MODEL // 2), lambda i: (i, 0, 0)),
                   rows(LANES)],
        out_shape=[jax.ShapeDtypeStruct((t, D_MODEL), F32),
                   jax.ShapeDtypeStruct((t, 1, D_MODEL // 2), jnp.uint32),
                   jax.ShapeDtypeStruct((t, LANES), F32)],
        compiler_params=_cparams(1),
    )(y_ssm, *attn_outs, *attn_lses, h2d, w_gate, x2d, w_ssm, w_attn, w_out, e8,
      norm_ffn_w[None], w_router, b_router)


def _route_body(lg_ref, sel_ref, gate_ref, cnt_ref, counts):
    tm = lg_ref.shape[0]
    i = pl.program_id(0)

    @pl.when(i == 0)
    def _():
        counts[...] = jnp.zeros_like(counts)

    lg = lg_ref[...]
    lane = lax.broadcasted_iota(jnp.int32, (tm, LANES), 1)
    is_coarse = lane < N_EXPERT_GROUPS
    cmax = jnp.max(jnp.where(is_coarse, lg, NEG_BIG), axis=-1, keepdims=True)
    grp = jnp.min(jnp.where(is_coarse & (lg == cmax), lane, LANES), axis=-1, keepdims=True)
    group_p = 1.0 / jnp.sum(jnp.where(is_coarse, jnp.exp(lg - cmax), 0.0), axis=-1, keepdims=True)
    f_lo = N_EXPERT_GROUPS + EXPERTS_PER_GROUP * grp
    in_grp = (lane >= f_lo) & (lane < f_lo + EXPERTS_PER_GROUP)
    f1 = jnp.max(jnp.where(in_grp, lg, NEG_BIG), axis=-1, keepdims=True)
    i1 = jnp.min(jnp.where(in_grp & (lg == f1), lane, LANES), axis=-1, keepdims=True)
    rest = in_grp & (lane != i1)
    f2 = jnp.max(jnp.where(rest, lg, NEG_BIG), axis=-1, keepdims=True)
    i2 = jnp.min(jnp.where(rest & (lg == f2), lane, LANES), axis=-1, keepdims=True)
    e2 = jnp.exp(f2 - f1)
    g1 = group_p / (1.0 + e2)
    g2 = group_p * e2 / (1.0 + e2)

    e1 = i1 - N_EXPERT_GROUPS
    e2i = i2 - N_EXPERT_GROUPS
    oh1 = lane == e1
    oh2 = lane == e2i
    onehot = jnp.where(oh1 | oh2, 1.0, 0.0)
    r = lax.broadcasted_iota(jnp.int32, (tm, tm), 0)
    c = lax.broadcasted_iota(jnp.int32, (tm, tm), 1)
    before = jnp.dot((r > c).astype(BF16), onehot.astype(BF16), preferred_element_type=F32)
    pos = counts[0:1, :] + before
    r1 = jnp.sum(jnp.where(oh1, pos, 0.0), axis=-1, keepdims=True).astype(jnp.int32)
    r2 = jnp.sum(jnp.where(oh2, pos, 0.0), axis=-1, keepdims=True).astype(jnp.int32)
    counts[...] = counts[...] + jnp.sum(onehot, axis=0, keepdims=True)
    sel_ref[...] = jnp.where(lane == 0, e1, jnp.where(lane == 1, e2i, jnp.where(lane == 2, r1,
                                                                                  jnp.where(lane == 3, r2, 0))))
    gate_ref[...] = jnp.where(lane == 0, g1, jnp.where(lane == 1, g2, 0.0))
    cnt_ref[...] = counts[...]


def _route(logits):
    t = logits.shape[0]
    tm = min(ROUTE_TILE, t)
    return pl.pallas_call(
        _route_body,
        grid=(t // tm,),
        in_specs=[pl.BlockSpec((tm, LANES), lambda i: (i, 0))],
        out_specs=[pl.BlockSpec((tm, LANES), lambda i: (i, 0)),
                   pl.BlockSpec((tm, LANES), lambda i: (i, 0)),
                   pl.BlockSpec((8, LANES), lambda i: (0, 0))],
        out_shape=[jax.ShapeDtypeStruct((t, LANES), jnp.int32),
                   jax.ShapeDtypeStruct((t, LANES), F32),
                   jax.ShapeDtypeStruct((8, LANES), F32)],
        scratch_shapes=[pltpu.VMEM((8, LANES), F32)],
        compiler_params=_cparams(1),
    )(logits)


def _dispatch_body(meta_ref, dest_ref, h_ref, rows_ref, zbuf, zsem, sem):
    tm = h_ref.shape[0]

    def zero_copy(e):
        start = pl.multiple_of(meta_ref[e] - MOE_BLOCK, MOE_BLOCK)
        return pltpu.make_async_copy(zbuf, rows_ref.at[pl.ds(start, MOE_BLOCK)], zsem)

    def tail_copy(blk):
        start = pl.multiple_of(blk * MOE_BLOCK, MOE_BLOCK)
        return pltpu.make_async_copy(zbuf, rows_ref.at[pl.ds(start, MOE_BLOCK)], zsem)

    @pl.when(pl.program_id(0) == 0)
    def _():
        zbuf[...] = jnp.zeros_like(zbuf)

        def start_zero(e, carry):
            @pl.when(meta_ref[N_EXPERTS + e] > 0)
            def _():
                zero_copy(e).start()
            return carry

        def wait_zero(e, carry):
            @pl.when(meta_ref[N_EXPERTS + e] > 0)
            def _():
                zero_copy(e).wait()
            return carry

        def start_tail(blk, carry):
            tail_copy(blk).start()
            return carry

        def wait_tail(blk, carry):
            tail_copy(blk).wait()
            return carry

        n_used = meta_ref[2 * N_EXPERTS]
        n_blocks = rows_ref.shape[0] // MOE_BLOCK
        lax.fori_loop(0, N_EXPERTS, start_zero, 0)
        lax.fori_loop(n_used, n_blocks, start_tail, 0)
        lax.fori_loop(0, N_EXPERTS, wait_zero, 0)
        lax.fori_loop(n_used, n_blocks, wait_tail, 0)

    def issue(j, carry):
        for k in range(2):
            d = dest_ref[0, 2 * j + k]
            pltpu.make_async_copy(h_ref.at[j], rows_ref.at[d], sem).start(priority=k)
        return carry

    lax.fori_loop(0, tm, issue, 0, unroll=8)
    for _ in range(2):
        pltpu.make_async_copy(h_ref, rows_ref.at[pl.ds(0, tm)], sem).wait()


def _dispatch(meta, dest3, h2, n_rows):
    t = h2.shape[0]
    tm = dest3.shape[2] // 2
    return pl.pallas_call(
        _dispatch_body,
        grid_spec=pltpu.PrefetchScalarGridSpec(
            num_scalar_prefetch=1,
            grid=(t // tm,),
            in_specs=[pl.BlockSpec((None, 1, 2 * tm), lambda i, m: (i, 0, 0), memory_space=pltpu.SMEM),
                      pl.BlockSpec((tm, 1, D_MODEL // 2), lambda i, m: (i, 0, 0))],
            out_specs=pl.BlockSpec(memory_space=pl.ANY),
            scratch_shapes=[pltpu.VMEM((MOE_BLOCK, 1, D_MODEL // 2), jnp.uint32),
                            pltpu.SemaphoreType.DMA(()),
                            pltpu.SemaphoreType.DMA(())]),
        out_shape=jax.ShapeDtypeStruct((n_rows, 1, D_MODEL // 2), jnp.uint32),
        compiler_params=_cparams(1),
    )(meta, dest3, h2)


def _expert_body(sched_ref, x_ref, wg_hbm, wu_hbm, wd_hbm, y_ref, x2d, wg_f, wu_f, wd_f, wg_b, wu_b, wd_b, sems):
    blk = pl.program_id(0)
    n_used = sched_ref[4, 0]

    def weight_copies(expert, slot):
        return [pltpu.make_async_copy(src.at[expert], dst.at[slot], sems.at[slot])
                for src, dst in ((wg_hbm, wg_f), (wu_hbm, wu_f), (wd_hbm, wd_f))]

    @pl.when(blk < n_used)
    def _():
        expert = sched_ref[0, blk]
        slot = sched_ref[2, blk]
        nxt = sched_ref[3, blk]

        @pl.when(sched_ref[1, blk] == 1)
        def _():
            @pl.when(blk == 0)
            def _():
                for c in weight_copies(expert, slot):
                    c.start()

            for c in weight_copies(expert, slot):
                c.wait()

            @pl.when(nxt >= 0)
            def _():
                for c in weight_copies(nxt, 1 - slot):
                    c.start()

            wg_b[...] = wg_f[slot].astype(BF16)
            wu_b[...] = wu_f[slot].astype(BF16)
            wd_b[...] = wd_f[slot].astype(BF16)

        x2d[...] = x_ref[...].reshape(MOE_BLOCK, D_MODEL // 2)
        words = x2d[...]
        x = jnp.concatenate([lax.bitcast_convert_type(words & jnp.uint32(0xFFFF0000), F32),
                             lax.bitcast_convert_type(words << 16, F32)], axis=1).astype(BF16)
        g = jnp.dot(x, wg_b[...], preferred_element_type=F32)
        u = jnp.dot(x, wu_b[...], preferred_element_type=F32)
        hid = (g * _sigmoid(g)) * u
        y = jnp.dot(hid.astype(BF16), wd_b[...], preferred_element_type=F32)
        bits = lax.bitcast_convert_type(y.astype(BF16).astype(F32), jnp.uint32)
        packed = bits[:, 0:D_MODEL // 2] | (bits[:, D_MODEL // 2:] >> 16)
        y_ref[...] = packed.reshape(MOE_BLOCK, 1, D_MODEL // 2)

    @pl.when(blk >= n_used)
    def _():
        y_ref[...] = jnp.zeros_like(y_ref)


def _experts(sched, rows, w_gate, w_up, w_down):
    n_rows = rows.shape[0]
    n_blocks = n_rows // MOE_BLOCK

    def row_map(b, sched):
        return (b, 0, 0)

    any_spec = pl.BlockSpec(memory_space=pl.ANY)
    return pl.pallas_call(
        _expert_body,
        grid_spec=pltpu.PrefetchScalarGridSpec(
            num_scalar_prefetch=1,
            grid=(n_blocks,),
            in_specs=[pl.BlockSpec((MOE_BLOCK, 1, D_MODEL // 2), row_map), any_spec, any_spec, any_spec],
            out_specs=pl.BlockSpec((MOE_BLOCK, 1, D_MODEL // 2), row_map),
            scratch_shapes=[pltpu.VMEM((MOE_BLOCK, D_MODEL // 2), jnp.uint32),
                            pltpu.VMEM((2, D_MODEL, D_EXPERT), F32),
                            pltpu.VMEM((2, D_MODEL, D_EXPERT), F32),
                            pltpu.VMEM((2, D_EXPERT, D_MODEL), F32),
                            pltpu.VMEM((D_MODEL, D_EXPERT), BF16),
                            pltpu.VMEM((D_MODEL, D_EXPERT), BF16),
                            pltpu.VMEM((D_EXPERT, D_MODEL), BF16),
                            pltpu.SemaphoreType.DMA((2,))]),
        out_shape=jax.ShapeDtypeStruct((n_rows, 1, D_MODEL // 2), jnp.uint32),
        compiler_params=_cparams(1),
    )(sched, rows, w_gate, w_up, w_down)


def _combine_body(dcur_ref, dnext_ref, gate_ref, x2_ref, y_ref, o_ref, buf_a, buf_b, y2d, sem_a, sem_b):
    tm = x2_ref.shape[0]
    i = pl.program_id(0)
    n = pl.num_programs(0)

    def issue(dref, buf, sem):
        def body(j, carry):
            for k in range(2):
                d = dref[0, 2 * j + k]
                pltpu.make_async_copy(y_ref.at[d], buf.at[k * tm + j], sem).start(priority=k)
            return carry
        lax.fori_loop(0, tm, body, 0, unroll=8)

    def finish(buf, sem):
        pltpu.make_async_copy(y_ref.at[pl.ds(0, 2 * tm)], buf, sem).wait()
        y2d[...] = buf[...].reshape(2 * tm, D_MODEL // 2)
        words = y2d[...]
        y = jnp.concatenate([lax.bitcast_convert_type(words & jnp.uint32(0xFFFF0000), F32),
                             lax.bitcast_convert_type(words << 16, F32)], axis=1)
        g = gate_ref[...]
        o_ref[...] = x2_ref[...] + g[:, 0:1] * y[0:tm, :] + g[:, 1:2] * y[tm:2 * tm, :]

    @pl.when(i == 0)
    def _():
        issue(dcur_ref, buf_a, sem_a)

    for par, (cur, cur_sem, nxt, nxt_sem) in enumerate(((buf_a, sem_a, buf_b, sem_b),
                                                        (buf_b, sem_b, buf_a, sem_a))):
        @pl.when(i % 2 == par)
        def _(cur=cur, cur_sem=cur_sem, nxt=nxt, nxt_sem=nxt_sem):
            @pl.when(i + 1 < n)
            def _():
                issue(dnext_ref, nxt, nxt_sem)
            finish(cur, cur_sem)


def _combine(dest3, gates, x2, y_rows):
    t = x2.shape[0]
    tm = dest3.shape[2] // 2
    nt = t // tm
    return pl.pallas_call(
        _combine_body,
        grid=(nt,),
        in_specs=[pl.BlockSpec((None, 1, 2 * tm), lambda i: (i, 0, 0), memory_space=pltpu.SMEM),
                  pl.BlockSpec((None, 1, 2 * tm), lambda i: (jnp.minimum(i + 1, nt - 1), 0, 0),
                               memory_space=pltpu.SMEM),
                  pl.BlockSpec((tm, LANES), lambda i: (i, 0)),
                  pl.BlockSpec((tm, D_MODEL), lambda i: (i, 0)),
                  pl.BlockSpec(memory_space=pl.ANY)],
        out_specs=pl.BlockSpec((tm, D_MODEL), lambda i: (i, 0)),
        out_shape=jax.ShapeDtypeStruct((t, D_MODEL), F32),
        scratch_shapes=[pltpu.VMEM((2 * tm, 1, D_MODEL // 2), jnp.uint32),
                        pltpu.VMEM((2 * tm, 1, D_MODEL // 2), jnp.uint32),
                        pltpu.VMEM((2 * tm, D_MODEL // 2), jnp.uint32),
                        pltpu.SemaphoreType.DMA(()),
                        pltpu.SemaphoreType.DMA(())],
        compiler_params=_cparams(1),
    )(dest3, dest3, gates, x2, y_rows)


def _layer(x, norm_mix_w, w_in, conv_w, conv_b, dt_bias, a_log, d_skip, ssm_norm_w, w_ssm_proj,
           q_norm_w, k_norm_w, rel_bias, w_attn_proj, w_out, norm_ffn_w, w_coarse, b_coarse,
           w_fine, b_fine, w_gate_exp, w_up_exp, w_down_exp):
    b, s, d = x.shape
    t = b * s
    x2d = x.reshape(t, d)

    dt_lo = 2 * D_INNER + BC_WIDTH
    qkv_lo = dt_lo + SSM_HEADS
    gate_lo = qkv_lo + 3 * ATTN_WIDTH
    w_ssd = jnp.concatenate([w_in[:, :qkv_lo], jnp.zeros((d, LANES - SSM_HEADS), w_in.dtype)],
                            axis=1).astype(BF16)
    w_gate = w_in[:, gate_lo:].astype(BF16)
    y_ssm, h3 = _ssd(x, norm_mix_w, w_ssd, conv_w, conv_b, dt_bias, a_log, d_skip, ssm_norm_w)
    y_ssm = y_ssm.reshape(t, D_INNER)
    h2d = h3.reshape(t, d)

    qw = jnp.tile(q_norm_w, 2)[None]
    kw = jnp.tile(k_norm_w, 2)[None]
    attn_outs, attn_lses = [], []
    for gi, (window, dilation) in enumerate(DILATED_CONFIGS):
        assert window // dilation == ATTN_BLK and s % window == 0
        hs = slice(gi * GROUP_WIDTH, (gi + 1) * GROUP_WIDTH)
        w_qkv = jnp.concatenate([w_in[:, qkv_lo + j * ATTN_WIDTH:qkv_lo + (j + 1) * ATTN_WIDTH][:, hs]
                                 for j in range(3)], axis=1).astype(BF16)
        bias = _band_bias(rel_bias[:, gi * HEADS_PER_GROUP:(gi + 1) * HEADS_PER_GROUP], dilation)
        o, l = _attn_group(h3, w_qkv, dilation, bias, qw, kw)
        attn_outs.extend(o)
        attn_lses.append(l)

    n_route = N_EXPERT_GROUPS + N_EXPERTS
    w_router = jnp.pad(jnp.concatenate([w_coarse, w_fine], axis=1), ((0, 0), (0, LANES - n_route)))
    b_router = jnp.pad(jnp.concatenate([b_coarse, b_fine]), (0, LANES - n_route))[None]
    x2, h2, logits = _mix_out(y_ssm, attn_outs, attn_lses, h2d, w_gate, x2d, w_ssm_proj.astype(BF16),
                              w_attn_proj.astype(BF16), w_out.astype(BF16), norm_ffn_w,
                              w_router.astype(BF16), b_router)

    sel, gates, counts = _route(logits)

    cnt = counts[0, :N_EXPERTS].astype(jnp.int32)
    padded = (cnt + MOE_BLOCK - 1) // MOE_BLOCK * MOE_BLOCK
    pad_end = jnp.cumsum(padded)
    n_blocks = -(-(2 * t + N_EXPERTS * (MOE_BLOCK - 1)) // MOE_BLOCK)
    block_start = jnp.arange(n_blocks, dtype=jnp.int32) * MOE_BLOCK
    block_expert = jnp.minimum(jnp.sum((pad_end[None, :] <= block_start[:, None]).astype(jnp.int32), axis=1),
                               N_EXPERTS - 1)
    n_used = (pad_end[-1:] // MOE_BLOCK).astype(jnp.int32)
    meta = jnp.concatenate([pad_end, padded, n_used]).astype(jnp.int32)
    chosen = sel[:, 0:2, None] == jnp.arange(N_EXPERTS, dtype=jnp.int32)
    dest = jnp.sum(jnp.where(chosen, pad_end - padded, 0), axis=-1) + sel[:, 2:4]
    blk_ids = jnp.arange(n_blocks, dtype=jnp.int32)
    first = ((blk_ids == 0) | (block_expert != jnp.roll(block_expert, 1))) & (blk_ids < n_used[0])
    slot = (jnp.cumsum(first.astype(jnp.int32)) - 1) % 2
    nxt_blk = blk_ids + padded[block_expert] // MOE_BLOCK
    nxt = jnp.where(nxt_blk < n_used[0], block_expert[jnp.minimum(nxt_blk, n_blocks - 1)], -1)
    sched = jnp.stack([block_expert, first.astype(jnp.int32), slot, nxt,
                       jnp.broadcast_to(n_used, (n_blocks,))]).astype(jnp.int32)

    tm_d = min(DISPATCH_TILE, t)
    dest_d = dest.reshape(t // tm_d, 1, 2 * tm_d)
    rows = _dispatch(meta, dest_d, h2, n_blocks * MOE_BLOCK)
    y_rows = _experts(sched, rows, w_gate_exp, w_up_exp, w_down_exp)
    tm_c = min(COMBINE_TILE, t)
    dest_c = dest.reshape(t // tm_c, 1, 2 * tm_c)
    out = _combine(dest_c, gates, x2, y_rows)
    return out.reshape(b, s, d)


def kernel(x, norm_mix_w, w_in, conv_w, conv_b, dt_bias, a_log, d_skip, ssm_norm_w, w_ssm_proj,
           q_norm_w, k_norm_w, rel_bias, w_attn_proj, w_out, norm_ffn_w, w_coarse, b_coarse,
           w_fine, b_fine, w_gate_exp, w_up_exp, w_down_exp):
    depth = norm_mix_w.shape[0]
    for layer in range(depth):
        x = _layer(x, norm_mix_w[layer], w_in[layer], conv_w[layer], conv_b[layer], dt_bias[layer],
                   a_log[layer], d_skip[layer], ssm_norm_w[layer], w_ssm_proj[layer],
                   q_norm_w[layer], k_norm_w[layer], rel_bias, w_attn_proj[layer], w_out[layer],
                   norm_ffn_w[layer], w_coarse[layer], b_coarse[layer], w_fine[layer], b_fine[layer],
                   w_gate_exp[layer], w_up_exp[layer], w_down_exp[layer])
    return x
```

```python
import functools
import math

import jax
import jax.numpy as jnp
import numpy as np
from jax import lax
from jax.experimental import pallas as pl
from jax.experimental.pallas import tpu as pltpu

F32 = jnp.float32
BF16 = jnp.bfloat16
HIGHEST = lax.Precision.HIGHEST

LANES = 128
NORM_EPS = 1e-6
NEG_BIG = -1e30
LOG2_E = math.log2(math.e)

D_MODEL = 1024
D_INNER = 2048
SSM_HEAD_DIM = 64
SSM_HEADS = 32
SSM_GROUPS = 2
D_STATE = 128
CONV_K = 4
BC_WIDTH = 2 * SSM_GROUPS * D_STATE
SSD_CHUNK = 128
ATTN_HEAD_DIM = 64
DILATED_CONFIGS = ((128, 1), (512, 4), (2048, 16))
HEADS_PER_GROUP = 8
GROUP_WIDTH = HEADS_PER_GROUP * ATTN_HEAD_DIM
ATTN_WIDTH = 3 * GROUP_WIDTH
ATTN_BLK = 128
NUM_BUCKETS = 32
MAX_DISTANCE = 2048
N_EXPERT_GROUPS = 8
EXPERTS_PER_GROUP = 8
N_EXPERTS = 64
D_EXPERT = 512
MOE_BLOCK = 256

SSD_PROJ_WIDTH = 2 * D_INNER + BC_WIDTH + LANES

VMEM_LIMIT = 56 * 1024 * 1024

PROJ_STEP_ROWS = 512
MIX_TILE = 512
DISPATCH_TILE = 1024
COMBINE_TILE = 512


def _sigmoid(x):
    return 1.0 / (1.0 + jnp.exp(-x))


def _cparams(n_axes):
    return pltpu.CompilerParams(dimension_semantics=("arbitrary",) * n_axes,
                                vmem_limit_bytes=VMEM_LIMIT)


def _dot3(x, y, x_is_exact):
    v = y if x_is_exact else x
    hi = v.astype(BF16)
    r1 = v - hi.astype(F32)
    mid = r1.astype(BF16)
    lo = (r1 - mid.astype(F32)).astype(BF16)
    if x_is_exact:
        return sum(jnp.dot(x, part, preferred_element_type=F32) for part in (hi, mid, lo))
    return sum(jnp.dot(part, y, preferred_element_type=F32) for part in (hi, mid, lo))


def _ssd_chunk(z, xbuf, r0, dt_raw, cw_ref, cb_ref, dtb_ref, alog_ref, dskip_ref, nw_ref, e_ref, state):
    L = SSD_CHUNK
    half = D_INNER // SSM_GROUPS
    hg = SSM_HEADS // SSM_GROUPS

    conv = cb_ref[...] + cw_ref[CONV_K - 1:CONV_K, :] * xbuf[8 + r0:8 + r0 + L, :]
    for k in range(CONV_K - 1):
        lo = 8 + r0 - (CONV_K - 1 - k)
        conv = conv + cw_ref[k:k + 1, :] * xbuf[lo:lo + L, :]
    xbc = conv * _sigmoid(conv)
    xs = xbc[:, :D_INNER]

    lane = lax.broadcasted_iota(jnp.int32, (L, LANES), 1)
    row = lax.broadcasted_iota(jnp.int32, (L, L), 0)
    col = lax.broadcasted_iota(jnp.int32, (L, L), 1)
    causal = row >= col

    v = dt_raw + dtb_ref[...]
    dt = jnp.maximum(v, 0.0) + jnp.log1p(jnp.exp(-jnp.abs(v)))
    dt = jnp.where(lane < SSM_HEADS, dt, 0.0)
    adt = dt * (-jnp.exp(alog_ref[...]))
    a_cs = _dot3(causal.astype(BF16), adt, True) * LOG2_E
    a_cs_t = a_cs.T
    expand = e_ref[...]
    a_full = _dot3(a_cs, expand, False)
    dt_full = _dot3(dt, expand, False)
    a_tot = a_full[L - 1:L, :]
    decay_from_start = jnp.exp2(a_full)
    decay_to_end = jnp.exp2(a_tot - a_full)
    decay_chunk = jnp.exp2(a_tot)

    xdt = xs * dt_full
    xw_b = (xdt * decay_to_end).astype(BF16)
    head_lo = lax.broadcasted_iota(jnp.int32, (L, D_INNER), 1) % (2 * SSM_HEAD_DIM) < SSM_HEAD_DIM
    xdt_lo = jnp.where(head_lo, xdt, 0.0).astype(BF16)
    xdt_hi = jnp.where(head_lo, 0.0, xdt).astype(BF16)

    b16, c16, cbs, y_offs, s_prevs = [], [], [], [], []
    for g in range(SSM_GROUPS):
        bg = xbc[:, D_INNER + g * D_STATE:D_INNER + (g + 1) * D_STATE]
        cg = xbc[:, D_INNER + (SSM_GROUPS + g) * D_STATE:D_INNER + (SSM_GROUPS + g + 1) * D_STATE]
        b16.append(bg)
        c16.append(cg.astype(BF16))
        cbs.append(lax.dot_general(c16[g], bg.astype(BF16), (((1,), (1,)), ((), ())),
                                   preferred_element_type=F32))
    for g in range(SSM_GROUPS):
        s_prevs.append(state[:, g * half:(g + 1) * half])
        y_offs.append(jnp.dot(c16[g], s_prevs[g].astype(BF16), preferred_element_type=F32))
    y_cols = []
    for g in range(SSM_GROUPS):
        for pr in range(half // LANES):
            h0 = g * hg + 2 * pr
            ps = slice(h0 * SSM_HEAD_DIM, (h0 + 2) * SSM_HEAD_DIM)
            ms = []
            for h in (h0, h0 + 1):
                seg = a_cs[:, h:h + 1] - a_cs_t[h:h + 1, :]
                ms.append((cbs[g] * jnp.exp2(jnp.where(causal, seg, NEG_BIG))).astype(BF16))
            y_cols.append(jnp.dot(jnp.concatenate(ms, axis=1),
                                  jnp.concatenate([xdt_lo[:, ps], xdt_hi[:, ps]], axis=0),
                                  preferred_element_type=F32))
    for g in range(SSM_GROUPS):
        gs = slice(g * half, (g + 1) * half)
        state[:, gs] = decay_chunk[:, gs] * s_prevs[g] + jnp.dot(
            b16[g].T.astype(BF16), xw_b[:, gs], preferred_element_type=F32)

    y = (jnp.concatenate(y_cols, axis=1) + jnp.concatenate(y_offs, axis=1) * decay_from_start
         + dskip_ref[...] * xs)
    y = y * (z * _sigmoid(z))
    normed = []
    for g in range(SSM_GROUPS):
        yg = y[:, g * half:(g + 1) * half]
        ms = jnp.mean(yg * yg, axis=-1, keepdims=True)
        normed.append(yg * lax.rsqrt(ms + NORM_EPS))
    return jnp.concatenate(normed, axis=1) * nw_ref[...]


def _ssd_body(x_ref, nmw_ref, w_ref, cw_ref, cb_ref, dtb_ref, alog_ref, dskip_ref, nw_ref, e_ref,
              y_ref, h_ref, state, xbuf, zbuf, dtbuf):
    L = SSD_CHUNK
    rt = x_ref.shape[1]
    conv_dim = D_INNER + BC_WIDTH

    @pl.when(pl.program_id(1) == 0)
    def _():
        state[...] = jnp.zeros_like(state)
        xbuf[0:8, :] = jnp.zeros((8, conv_dim), F32)

    x = x_ref[0]
    ms = jnp.mean(x * x, axis=-1, keepdims=True)
    h = (x * lax.rsqrt(ms + NORM_EPS) * nmw_ref[...]).astype(BF16)
    h_ref[0] = h
    zbuf[...] = jnp.dot(h, w_ref[:, 0:D_INNER], preferred_element_type=F32)
    xbuf[8:8 + rt, :] = jnp.dot(h, w_ref[:, D_INNER:D_INNER + conv_dim], preferred_element_type=F32)
    dtbuf[...] = jnp.dot(h, w_ref[:, D_INNER + conv_dim:], preferred_element_type=F32)

    for c in range(rt // L):
        r0 = c * L
        y = _ssd_chunk(zbuf[r0:r0 + L, :], xbuf, r0, dtbuf[r0:r0 + L, :], cw_ref, cb_ref,
                       dtb_ref, alog_ref, dskip_ref, nw_ref, e_ref, state)
        y_ref[0, r0:r0 + L, :] = y.astype(BF16)
    xbuf[0:8, :] = xbuf[rt:rt + 8, :]


def _ssd(x, norm_mix_w, w_ssd, conv_w, conv_b, dt_bias, a_log, d_skip, ssm_norm_w):
    b, s, d = x.shape
    rt = min(PROJ_STEP_ROWS, s)
    pad = LANES - SSM_HEADS
    conv_dim = D_INNER + BC_WIDTH
    dtb = jnp.pad(dt_bias, (0, pad))[None]
    alog = jnp.pad(a_log, (0, pad))[None]
    dskip = jnp.repeat(d_skip, SSM_HEAD_DIM)[None]
    expand = (np.arange(LANES)[:, None] == np.arange(D_INNER)[None, :] // SSM_HEAD_DIM).astype(np.float32)

    def const(shape):
        return pl.BlockSpec(shape, lambda i, c: (0,) * len(shape))

    return pl.pallas_call(
        _ssd_body,
        grid=(b, s // rt),
        in_specs=[pl.BlockSpec((1, rt, d), lambda i, c: (i, c, 0)),
                  const((1, d)), const((d, SSD_PROJ_WIDTH)),
                  const((CONV_K, conv_dim)), const((1, conv_dim)),
                  const((1, LANES)), const((1, LANES)),
                  const((1, D_INNER)), const((1, D_INNER)),
                  const((LANES, D_INNER))],
        out_specs=[pl.BlockSpec((1, rt, D_INNER), lambda i, c: (i, c, 0)),
                   pl.BlockSpec((1, rt, d), lambda i, c: (i, c, 0))],
        out_shape=[jax.ShapeDtypeStruct((b, s, D_INNER), BF16),
                   jax.ShapeDtypeStruct((b, s, d), BF16)],
        scratch_shapes=[pltpu.VMEM((D_STATE, D_INNER), F32),
                        pltpu.VMEM((rt + 8, conv_dim), F32),
                        pltpu.VMEM((rt, D_INNER), F32),
                        pltpu.VMEM((rt, LANES), F32)],
        compiler_params=_cparams(2),
    )(x, norm_mix_w[None], w_ssd, conv_w, conv_b[None], dtb, alog, dskip, ssm_norm_w[None],
      jnp.asarray(expand, dtype=BF16))


def _t5_causal_bucket(dist):
    max_exact = NUM_BUCKETS // 2
    large = max_exact + (np.log(np.maximum(dist, max_exact) / max_exact)
                         / math.log(MAX_DISTANCE / max_exact) * (NUM_BUCKETS - max_exact)).astype(np.int32)
    return np.where(dist < max_exact, dist, np.minimum(large, NUM_BUCKETS - 1)).astype(np.int32)


def _band_bias(rel_bias_group, dilation):
    blk = ATTN_BLK
    off = np.arange(blk)[:, None] + blk - np.arange(2 * blk)[None, :]
    in_win = (off >= 0) & (off <= blk)
    bucket = _t5_causal_bucket(np.clip(off, 0, None) * dilation)
    onehot = (bucket.reshape(-1, 1) == np.arange(NUM_BUCKETS)[None, :]).astype(np.float32)
    bias = jnp.dot(jnp.asarray(onehot), rel_bias_group.astype(F32), precision=HIGHEST)
    bias = jnp.transpose(bias.reshape(blk, 2 * blk, HEADS_PER_GROUP), (2, 0, 1))
    bias = jnp.where(in_win[None], bias, NEG_BIG)
    return bias.reshape(HEADS_PER_GROUP // 2, 2 * blk, 2 * blk)


def _attn_body(h_ref, w_ref, bias_ref, qw_ref, kw_ref, *rest, dilation):
    n_pairs = HEADS_PER_GROUP // 2
    o_refs = rest[0:n_pairs]
    lse_ref, qkv, kbuf, vbuf = rest[n_pairs:]
    blk = ATTN_BLK
    rt = h_ref.shape[1]
    span = blk * dilation
    n_sub = rt // span
    assert n_sub == 1 or dilation == 1
    step = pl.program_id(1)

    @pl.when(step == 0)
    def _():
        kbuf[...] = jnp.zeros_like(kbuf)
        vbuf[...] = jnp.zeros_like(vbuf)

    first_head = lax.broadcasted_iota(jnp.int32, (rt, LANES), 1) < ATTN_HEAD_DIM

    def head_norm(x, w_ref):
        xx = x * x
        s0 = jnp.sum(jnp.where(first_head, xx, 0.0), axis=-1, keepdims=True)
        s1 = jnp.sum(jnp.where(first_head, 0.0, xx), axis=-1, keepdims=True)
        ss = jnp.where(first_head, s0, s1)
        return x * lax.rsqrt(ss * (1.0 / ATTN_HEAD_DIM) + NORM_EPS) * w_ref[...]

    h = h_ref[0]
    for j in range(3 * n_pairs // 2):
        piece = jnp.dot(h, w_ref[:, 2 * j * LANES:2 * (j + 1) * LANES], preferred_element_type=F32)
        for half in range(2):
            slab = piece[:, half * LANES:(half + 1) * LANES]
            if j < n_pairs // 2:
                slab = head_norm(slab, qw_ref) * (ATTN_HEAD_DIM ** -0.5)
            elif j < n_pairs:
                slab = head_norm(slab, kw_ref)
            qkv[2 * j + half] = slab

    lane = lax.broadcasted_iota(jnp.int32, (blk, LANES), 1)
    lo_half = lane < ATTN_HEAD_DIM
    nt = (((1,), (1,)), ((), ()))

    units = 2

    def block_pair(it, carry):
        pairs = range(n_pairs)
        rows, res, slot, pen, q2, k_new, v_new, k_old, v_old = [], [], [], [], [], [], [], [], []
        for u in range(units):
            blk_id = units * it + u
            sub, r = (blk_id, 0) if dilation == 1 else (0, blk_id)
            gblk = step * n_sub + sub
            rows.append(pl.ds(sub * span + r, blk, stride=dilation))
            res.append(r)
            slot.append(gblk % 2)
            pen.append(jnp.where(gblk == 0, NEG_BIG, 0.0))
            k_new.append([qkv[n_pairs + p, rows[u], :].astype(BF16) for p in pairs])
            v_new.append([qkv[2 * n_pairs + p, rows[u], :].astype(BF16) for p in pairs])
            if dilation == 1 and u > 0:
                k_old.append(k_new[u - 1])
                v_old.append(v_new[u - 1])
            else:
                k_old.append([kbuf[1 - slot[u], r * n_pairs + p] for p in pairs])
                v_old.append([vbuf[1 - slot[u], r * n_pairs + p] for p in pairs])
            q2.append([])
            for p in pairs:
                qp = qkv[p, rows[u], :]
                q2[u].append(jnp.concatenate([jnp.where(lo_half, qp, 0.0), jnp.where(lo_half, 0.0, qp)],
                                             axis=0).astype(BF16))
        todo = [(u, p) for u in range(units) for p in pairs]
        s_prev = {up: lax.dot_general(q2[up[0]][up[1]], k_old[up[0]][up[1]], nt, preferred_element_type=F32)
                  for up in todo}
        s_cur = {up: lax.dot_general(q2[up[0]][up[1]], k_new[up[0]][up[1]], nt, preferred_element_type=F32)
                 for up in todo}
        e_prev, e_cur, m, d = {}, {}, {}, {}
        for up in todo:
            u, p = up
            sp = s_prev[up] + (bias_ref[p, :, 0:blk] + pen[u])
            sc = s_cur[up] + bias_ref[p, :, blk:2 * blk]
            m[up] = jnp.max(jnp.maximum(sp, sc), axis=-1, keepdims=True)
            ep = jnp.exp(sp - m[up])
            ec = jnp.exp(sc - m[up])
            d[up] = jnp.sum(ep + ec, axis=-1, keepdims=True)
            e_prev[up] = ep.astype(BF16)
            e_cur[up] = ec.astype(BF16)
        pv_prev = {up: jnp.dot(e_prev[up], v_old[up[0]][up[1]], preferred_element_type=F32) for up in todo}
        pv_cur = {up: jnp.dot(e_cur[up], v_new[up[0]][up[1]], preferred_element_type=F32) for up in todo}
        for u in range(units):
            lse_tile = jnp.zeros((blk, LANES), F32)
            for p in pairs:
                pv = (pv_prev[u, p] + pv_cur[u, p]) / d[u, p]
                o_refs[p][0, rows[u], :] = jnp.where(lo_half, pv[0:blk], pv[blk:2 * blk])
                lse = m[u, p] + jnp.log(d[u, p])
                lse_tile = jnp.where(lane == 2 * p, lse[0:blk], lse_tile)
                lse_tile = jnp.where(lane == 2 * p + 1, lse[blk:2 * blk], lse_tile)
                kbuf[slot[u], res[u] * n_pairs + p] = k_new[u][p]
                vbuf[slot[u], res[u] * n_pairs + p] = v_new[u][p]
            lse_ref[0, rows[u], :] = lse_tile
        return carry

    lax.fori_loop(0, n_sub * dilation // units, block_pair, 0)


def _attn_group(h3, w_qkv, dilation, bias, qw, kw):
    b, s, d = h3.shape
    blk = ATTN_BLK
    n_pairs = HEADS_PER_GROUP // 2
    rt = max(PROJ_STEP_ROWS, blk * dilation)
    rt = min(rt, s)

    def const(shape):
        return pl.BlockSpec(shape, lambda i, n: (0,) * len(shape))

    token_spec = pl.BlockSpec((1, rt, LANES), lambda i, n: (i, n, 0))
    res = pl.pallas_call(
        functools.partial(_attn_body, dilation=dilation),
        grid=(b, s // rt),
        in_specs=[pl.BlockSpec((1, rt, d), lambda i, n: (i, n, 0)),
                  const((d, 3 * GROUP_WIDTH)),
                  const((n_pairs, 2 * blk, 2 * blk)), const((1, LANES)), const((1, LANES))],
        out_specs=[token_spec] * (n_pairs + 1),
        out_shape=[jax.ShapeDtypeStruct((b, s, LANES), F32)] * (n_pairs + 1),
        scratch_shapes=[pltpu.VMEM((3 * n_pairs, rt, LANES), F32),
                        pltpu.VMEM((2, dilation * n_pairs, blk, LANES), BF16),
                        pltpu.VMEM((2, dilation * n_pairs, blk, LANES), BF16)],
        compiler_params=_cparams(2),
    )(h3, w_qkv, bias, qw, kw)
    outs = [o.reshape(b * s, LANES) for o in res[:n_pairs]]
    return outs, res[n_pairs].reshape(b * s, LANES)


def _mix_body(*refs):
    n_pairs = HEADS_PER_GROUP // 2
    y_ref = refs[0]
    o_refs = refs[1:1 + 3 * n_pairs]
    l_refs = refs[1 + 3 * n_pairs:4 + 3 * n_pairs]
    (h_ref, wgate_ref, x_ref, wssm_ref, wattn_ref, wout_ref, e8_ref, nfw_ref, wr_ref, br_ref,
     x2_ref, h2_ref, sel_ref, gate_ref, cnt_ref, counts) = refs[4 + 3 * n_pairs:]
    tm = x_ref.shape[0]

    @pl.when(pl.program_id(0) == 0)
    def _():
        counts[...] = jnp.zeros_like(counts)

    lses = [l[...] for l in l_refs]
    mx = jnp.maximum(jnp.maximum(lses[0], lses[1]), lses[2])
    es = [jnp.exp(l - mx) for l in lses]
    inv = 1.0 / (es[0] + es[1] + es[2])
    e8 = e8_ref[...]
    att = jnp.zeros((tm, GROUP_WIDTH), F32)
    for g in range(3):
        w = es[g] * inv
        w_hi = w.astype(BF16)
        w_lo = (w - w_hi.astype(F32)).astype(BF16)
        w_full = (jnp.dot(w_hi, e8, preferred_element_type=F32)
                  + jnp.dot(w_lo, e8, preferred_element_type=F32))
        o_g = jnp.concatenate([o_refs[g * n_pairs + p][...] for p in range(n_pairs)], axis=1)
        att = att + w_full * o_g
    y_attn = jnp.dot(att.astype(BF16), wattn_ref[...], preferred_element_type=F32)
    y_ssm = jnp.dot(y_ref[...], wssm_ref[...], preferred_element_type=F32)
    h = h_ref[...]
    g_ssm = jnp.dot(h, wgate_ref[:, 0:D_MODEL], preferred_element_type=F32)
    g_attn = jnp.dot(h, wgate_ref[:, D_MODEL:2 * D_MODEL], preferred_element_type=F32)
    merged = _sigmoid(g_ssm) * y_ssm + _sigmoid(g_attn) * y_attn
    x2 = x_ref[...] + jnp.dot(merged.astype(BF16), wout_ref[...], preferred_element_type=F32)
    x2_ref[...] = x2
    ms = jnp.mean(x2 * x2, axis=-1, keepdims=True)
    h2 = x2 * lax.rsqrt(ms + NORM_EPS) * nfw_ref[...]
    bits = lax.bitcast_convert_type(h2.astype(BF16).astype(F32), jnp.uint32)
    packed = bits[:, 0:D_MODEL // 2] | (bits[:, D_MODEL // 2:] >> 16)
    h2_ref[...] = packed.reshape(tm, 1, D_MODEL // 2)
    logits = jnp.dot(h2.astype(BF16), wr_ref[...], preferred_element_type=F32) + br_ref[...]
    sel_ref[...], gate_ref[...] = _route_tile(logits, counts)
    cnt_ref[...] = counts[...]


def _mix_out(y_ssm, attn_outs, attn_lses, h2d, w_gate, x2d, w_ssm, w_attn, w_out, norm_ffn_w, w_router, b_router):
    t = x2d.shape[0]
    tm = min(MIX_TILE, t)
    e8 = (np.arange(LANES)[:, None] == np.arange(GROUP_WIDTH)[None, :] // ATTN_HEAD_DIM)
    e8 = jnp.asarray(e8.astype(np.float32), dtype=BF16)

    def rows(width, cb=0):
        return pl.BlockSpec((tm, width), lambda i: (i, cb))

    def const(shape):
        return pl.BlockSpec(shape, lambda i: (0,) * len(shape), pipeline_mode=pl.Buffered(1))

    return pl.pallas_call(
        _mix_body,
        grid=(t // tm,),
        in_specs=[rows(D_INNER)] + [rows(LANES)] * (len(attn_outs) + len(attn_lses)) + [
                  rows(D_MODEL), const((D_MODEL, 2 * D_MODEL)), rows(D_MODEL),
                  const((D_INNER, D_MODEL)), const((GROUP_WIDTH, D_MODEL)), const((D_MODEL, D_MODEL)),
                  const((LANES, GROUP_WIDTH)), const((1, D_MODEL)),
                  const((D_MODEL, LANES)), const((1, LANES))],
        out_specs=[rows(D_MODEL),
                   pl.BlockSpec((tm, 1, D_MODEL // 2), lambda i: (i, 0, 0)),
                   rows(LANES), rows(LANES),
                   pl.BlockSpec((8, LANES), lambda i: (0, 0))],
        out_shape=[jax.ShapeDtypeStruct((t, D_MODEL), F32),
                   jax.ShapeDtypeStruct((t, 1, D_MODEL // 2), jnp.uint32),
                   jax.ShapeDtypeStruct((t, LANES), jnp.int32),
                   jax.ShapeDtypeStruct((t, LANES), F32),
                   jax.ShapeDtypeStruct((8, LANES), F32)],
        scratch_shapes=[pltpu.VMEM((8, LANES), F32)],
        compiler_params=_cparams(1),
    )(y_ssm, *attn_outs, *attn_lses, h2d, w_gate, x2d, w_ssm, w_attn, w_out, e8,
      norm_ffn_w[None], w_router, b_router)


def _route_tile(lg, counts):
    tm = lg.shape[0]
    lane = lax.broadcasted_iota(jnp.int32, (tm, LANES), 1)
    is_coarse = lane < N_EXPERT_GROUPS
    cmax = jnp.max(jnp.where(is_coarse, lg, NEG_BIG), axis=-1, keepdims=True)
    grp = jnp.min(jnp.where(is_coarse & (lg == cmax), lane, LANES), axis=-1, keepdims=True)
    group_p = 1.0 / jnp.sum(jnp.where(is_coarse, jnp.exp(lg - cmax), 0.0), axis=-1, keepdims=True)
    f_lo = N_EXPERT_GROUPS + EXPERTS_PER_GROUP * grp
    in_grp = (lane >= f_lo) & (lane < f_lo + EXPERTS_PER_GROUP)
    f1 = jnp.max(jnp.where(in_grp, lg, NEG_BIG), axis=-1, keepdims=True)
    i1 = jnp.min(jnp.where(in_grp & (lg == f1), lane, LANES), axis=-1, keepdims=True)
    rest = in_grp & (lane != i1)
    f2 = jnp.max(jnp.where(rest, lg, NEG_BIG), axis=-1, keepdims=True)
    i2 = jnp.min(jnp.where(rest & (lg == f2), lane, LANES), axis=-1, keepdims=True)
    e2 = jnp.exp(f2 - f1)
    g1 = group_p / (1.0 + e2)
    g2 = group_p * e2 / (1.0 + e2)

    e1 = i1 - N_EXPERT_GROUPS
    e2i = i2 - N_EXPERT_GROUPS
    oh1 = lane == e1
    oh2 = lane == e2i
    onehot = jnp.where(oh1 | oh2, 1.0, 0.0)
    r = lax.broadcasted_iota(jnp.int32, (tm, tm), 0)
    c = lax.broadcasted_iota(jnp.int32, (tm, tm), 1)
    before = jnp.dot((r > c).astype(BF16), onehot.astype(BF16), preferred_element_type=F32)
    pos = counts[0:1, :] + before
    r1 = jnp.sum(jnp.where(oh1, pos, 0.0), axis=-1, keepdims=True).astype(jnp.int32)
    r2 = jnp.sum(jnp.where(oh2, pos, 0.0), axis=-1, keepdims=True).astype(jnp.int32)
    counts[...] = counts[...] + jnp.sum(onehot, axis=0, keepdims=True)
    sel = jnp.where(lane == 0, e1, jnp.where(lane == 1, e2i, jnp.where(lane == 2, r1, jnp.where(lane == 3, r2, 0))))
    gates = jnp.where(lane == 0, g1, jnp.where(lane == 1, g2, 0.0))
    return sel, gates


def _dispatch_body(meta_ref, dest_ref, h_ref, rows_ref, zbuf, zsem, sem):
    tm = h_ref.shape[0]

    def zero_copy(e):
        start = pl.multiple_of(meta_ref[e] - MOE_BLOCK, MOE_BLOCK)
        return pltpu.make_async_copy(zbuf, rows_ref.at[pl.ds(start, MOE_BLOCK)], zsem)

    def tail_copy(blk):
        start = pl.multiple_of(blk * MOE_BLOCK, MOE_BLOCK)
        return pltpu.make_async_copy(zbuf, rows_ref.at[pl.ds(start, MOE_BLOCK)], zsem)

    @pl.when(pl.program_id(0) == 0)
    def _():
        zbuf[...] = jnp.zeros_like(zbuf)

        def start_zero(e, carry):
            @pl.when(meta_ref[N_EXPERTS + e] > 0)
            def _():
                zero_copy(e).start()
            return carry

        def wait_zero(e, carry):
            @pl.when(meta_ref[N_EXPERTS + e] > 0)
            def _():
                zero_copy(e).wait()
            return carry

        def start_tail(blk, carry):
            tail_copy(blk).start()
            return carry

        def wait_tail(blk, carry):
            tail_copy(blk).wait()
            return carry

        n_used = meta_ref[2 * N_EXPERTS]
        n_blocks = rows_ref.shape[0] // MOE_BLOCK
        lax.fori_loop(0, N_EXPERTS, start_zero, 0)
        lax.fori_loop(n_used, n_blocks, start_tail, 0)
        lax.fori_loop(0, N_EXPERTS, wait_zero, 0)
        lax.fori_loop(n_used, n_blocks, wait_tail, 0)

    def issue(j, carry):
        for k in range(2):
            d = dest_ref[0, 2 * j + k]
            pltpu.make_async_copy(h_ref.at[j], rows_ref.at[d], sem).start(priority=k)
        return carry

    lax.fori_loop(0, tm, issue, 0, unroll=8)
    for _ in range(2):
        pltpu.make_async_copy(h_ref, rows_ref.at[pl.ds(0, tm)], sem).wait()


def _dispatch(meta, dest3, h2, n_rows):
    t = h2.shape[0]
    tm = dest3.shape[2] // 2
    return pl.pallas_call(
        _dispatch_body,
        grid_spec=pltpu.PrefetchScalarGridSpec(
            num_scalar_prefetch=1,
            grid=(t // tm,),
            in_specs=[pl.BlockSpec((None, 1, 2 * tm), lambda i, m: (i, 0, 0), memory_space=pltpu.SMEM),
                      pl.BlockSpec((tm, 1, D_MODEL // 2), lambda i, m: (i, 0, 0))],
            out_specs=pl.BlockSpec(memory_space=pl.ANY),
            scratch_shapes=[pltpu.VMEM((MOE_BLOCK, 1, D_MODEL // 2), jnp.uint32),
                            pltpu.SemaphoreType.DMA(()),
                            pltpu.SemaphoreType.DMA(())]),
        out_shape=jax.ShapeDtypeStruct((n_rows, 1, D_MODEL // 2), jnp.uint32),
        compiler_params=_cparams(1),
    )(meta, dest3, h2)


def _expert_body(sched_ref, x_ref, wg_hbm, wu_hbm, wd_hbm, y_ref, x2d, wg_f, wu_f, wd_f, wg_b, wu_b, wd_b, sems):
    blk = pl.program_id(0)
    n_used = sched_ref[4, 0]

    def weight_copies(expert, slot):
        return [pltpu.make_async_copy(src.at[expert], dst.at[slot], sems.at[slot])
                for src, dst in ((wg_hbm, wg_f), (wu_hbm, wu_f), (wd_hbm, wd_f))]

    @pl.when(blk < n_used)
    def _():
        expert = sched_ref[0, blk]
        slot = sched_ref[2, blk]
        nxt = sched_ref[3, blk]

        @pl.when(sched_ref[1, blk] == 1)
        def _():
            @pl.when(blk == 0)
            def _():
                for c in weight_copies(expert, slot):
                    c.start()

            for c in weight_copies(expert, slot):
                c.wait()

            @pl.when(nxt >= 0)
            def _():
                for c in weight_copies(nxt, 1 - slot):
                    c.start()

            wg_b[...] = wg_f[slot].astype(BF16)
            wu_b[...] = wu_f[slot].astype(BF16)
            wd_b[...] = wd_f[slot].astype(BF16)

        x2d[...] = x_ref[...].reshape(MOE_BLOCK, D_MODEL // 2)
        words = x2d[...]
        x = jnp.concatenate([lax.bitcast_convert_type(words & jnp.uint32(0xFFFF0000), F32),
                             lax.bitcast_convert_type(words << 16, F32)], axis=1).astype(BF16)
        g = jnp.dot(x, wg_b[...], preferred_element_type=F32)
        u = jnp.dot(x, wu_b[...], preferred_element_type=F32)
        hid = (g * _sigmoid(g)) * u
        y = jnp.dot(hid.astype(BF16), wd_b[...], preferred_element_type=F32)
        bits = lax.bitcast_convert_type(y.astype(BF16).astype(F32), jnp.uint32)
        packed = bits[:, 0:D_MODEL // 2] | (bits[:, D_MODEL // 2:] >> 16)
        y_ref[...] = packed.reshape(MOE_BLOCK, 1, D_MODEL // 2)

    @pl.when(blk >= n_used)
    def _():
        y_ref[...] = jnp.zeros_like(y_ref)


def _experts(sched, rows, w_gate, w_up, w_down):
    n_rows = rows.shape[0]
    n_blocks = n_rows // MOE_BLOCK

    def row_map(b, sched):
        return (b, 0, 0)

    any_spec = pl.BlockSpec(memory_space=pl.ANY)
    return pl.pallas_call(
        _expert_body,
        grid_spec=pltpu.PrefetchScalarGridSpec(
            num_scalar_prefetch=1,
            grid=(n_blocks,),
            in_specs=[pl.BlockSpec((MOE_BLOCK, 1, D_MODEL // 2), row_map), any_spec, any_spec, any_spec],
            out_specs=pl.BlockSpec((MOE_BLOCK, 1, D_MODEL // 2), row_map),
            scratch_shapes=[pltpu.VMEM((MOE_BLOCK, D_MODEL // 2), jnp.uint32),
                            pltpu.VMEM((2, D_MODEL, D_EXPERT), F32),
                            pltpu.VMEM((2, D_MODEL, D_EXPERT), F32),
                            pltpu.VMEM((2, D_EXPERT, D_MODEL), F32),
                            pltpu.VMEM((D_MODEL, D_EXPERT), BF16),
                            pltpu.VMEM((D_MODEL, D_EXPERT), BF16),
                            pltpu.VMEM((D_EXPERT, D_MODEL), BF16),
                            pltpu.SemaphoreType.DMA((2,))]),
        out_shape=jax.ShapeDtypeStruct((n_rows, 1, D_MODEL // 2), jnp.uint32),
        compiler_params=_cparams(1),
    )(sched, rows, w_gate, w_up, w_down)


def _combine_body(dcur_ref, dnext_ref, gate_ref, x2_ref, y_ref, o_ref, buf_a, buf_b, y2d, sem_a, sem_b):
    tm = x2_ref.shape[0]
    i = pl.program_id(0)
    n = pl.num_programs(0)

    def issue(dref, buf, sem):
        def body(j, carry):
            for k in range(2):
                d = dref[0, 2 * j + k]
                pltpu.make_async_copy(y_ref.at[d], buf.at[k * tm + j], sem).start(priority=k)
            return carry
        lax.fori_loop(0, tm, body, 0, unroll=8)

    def finish(buf, sem):
        pltpu.make_async_copy(y_ref.at[pl.ds(0, 2 * tm)], buf, sem).wait()
        y2d[...] = buf[...].reshape(2 * tm, D_MODEL // 2)
        words = y2d[...]
        y = jnp.concatenate([lax.bitcast_convert_type(words & jnp.uint32(0xFFFF0000), F32),
                             lax.bitcast_convert_type(words << 16, F32)], axis=1)
        g = gate_ref[...]
        o_ref[...] = x2_ref[...] + g[:, 0:1] * y[0:tm, :] + g[:, 1:2] * y[tm:2 * tm, :]

    @pl.when(i == 0)
    def _():
        issue(dcur_ref, buf_a, sem_a)

    for par, (cur, cur_sem, nxt, nxt_sem) in enumerate(((buf_a, sem_a, buf_b, sem_b),
                                                        (buf_b, sem_b, buf_a, sem_a))):
        @pl.when(i % 2 == par)
        def _(cur=cur, cur_sem=cur_sem, nxt=nxt, nxt_sem=nxt_sem):
            @pl.when(i + 1 < n)
            def _():
                issue(dnext_ref, nxt, nxt_sem)
            finish(cur, cur_sem)


def _combine(dest3, gates, x2, y_rows):
    t = x2.shape[0]
    tm = dest3.shape[2] // 2
    nt = t // tm
    return pl.pallas_call(
        _combine_body,
        grid=(nt,),
        in_specs=[pl.BlockSpec((None, 1, 2 * tm), lambda i: (i, 0, 0), memory_space=pltpu.SMEM),
                  pl.BlockSpec((None, 1, 2 * tm), lambda i: (jnp.minimum(i + 1, nt - 1), 0, 0),
                               memory_space=pltpu.SMEM),
                  pl.BlockSpec((tm, LANES), lambda i: (i, 0)),
                  pl.BlockSpec((tm, D_MODEL), lambda i: (i, 0)),
                  pl.BlockSpec(memory_space=pl.ANY)],
        out_specs=pl.BlockSpec((tm, D_MODEL), lambda i: (i, 0)),
        out_shape=jax.ShapeDtypeStruct((t, D_MODEL), F32),
        scratch_shapes=[pltpu.VMEM((2 * tm, 1, D_MODEL // 2), jnp.uint32),
                        pltpu.VMEM((2 * tm, 1, D_MODEL // 2), jnp.uint32),
                        pltpu.VMEM((2 * tm, D_MODEL // 2), jnp.uint32),
                        pltpu.SemaphoreType.DMA(()),
                        pltpu.SemaphoreType.DMA(())],
        compiler_params=_cparams(1),
    )(dest3, dest3, gates, x2, y_rows)


def _layer(x, norm_mix_w, w_in, conv_w, conv_b, dt_bias, a_log, d_skip, ssm_norm_w, w_ssm_proj,
           q_norm_w, k_norm_w, rel_bias, w_attn_proj, w_out, norm_ffn_w, w_coarse, b_coarse,
           w_fine, b_fine, w_gate_exp, w_up_exp, w_down_exp):
    b, s, d = x.shape
    t = b * s
    x2d = x.reshape(t, d)

    dt_lo = 2 * D_INNER + BC_WIDTH
    qkv_lo = dt_lo + SSM_HEADS
    gate_lo = qkv_lo + 3 * ATTN_WIDTH
    w_ssd = jnp.concatenate([w_in[:, :qkv_lo], jnp.zeros((d, LANES - SSM_HEADS), w_in.dtype)],
                            axis=1).astype(BF16)
    w_gate = w_in[:, gate_lo:].astype(BF16)
    y_ssm, h3 = _ssd(x, norm_mix_w, w_ssd, conv_w, conv_b, dt_bias, a_log, d_skip, ssm_norm_w)
    y_ssm = y_ssm.reshape(t, D_INNER)
    h2d = h3.reshape(t, d)

    qw = jnp.tile(q_norm_w, 2)[None]
    kw = jnp.tile(k_norm_w, 2)[None]
    attn_outs, attn_lses = [], []
    for gi, (window, dilation) in enumerate(DILATED_CONFIGS):
        assert window // dilation == ATTN_BLK and s % window == 0
        hs = slice(gi * GROUP_WIDTH, (gi + 1) * GROUP_WIDTH)
        w_qkv = jnp.concatenate([w_in[:, qkv_lo + j * ATTN_WIDTH:qkv_lo + (j + 1) * ATTN_WIDTH][:, hs]
                                 for j in range(3)], axis=1).astype(BF16)
        bias = _band_bias(rel_bias[:, gi * HEADS_PER_GROUP:(gi + 1) * HEADS_PER_GROUP], dilation)
        o, l = _attn_group(h3, w_qkv, dilation, bias, qw, kw)
        attn_outs.extend(o)
        attn_lses.append(l)

    n_route = N_EXPERT_GROUPS + N_EXPERTS
    w_router = jnp.pad(jnp.concatenate([w_coarse, w_fine], axis=1), ((0, 0), (0, LANES - n_route)))
    b_router = jnp.pad(jnp.concatenate([b_coarse, b_fine]), (0, LANES - n_route))[None]
    x2, h2, sel, gates, counts = _mix_out(y_ssm, attn_outs, attn_lses, h2d, w_gate, x2d, w_ssm_proj.astype(BF16),
                              w_attn_proj.astype(BF16), w_out.astype(BF16), norm_ffn_w,
                              w_router.astype(BF16), b_router)

    cnt = counts[0, :N_EXPERTS].astype(jnp.int32)
    padded = (cnt + MOE_BLOCK - 1) // MOE_BLOCK * MOE_BLOCK
    pad_end = jnp.cumsum(padded)
    n_blocks = -(-(2 * t + N_EXPERTS * (MOE_BLOCK - 1)) // MOE_BLOCK)
    block_start = jnp.arange(n_blocks, dtype=jnp.int32) * MOE_BLOCK
    block_expert = jnp.minimum(jnp.sum((pad_end[None, :] <= block_start[:, None]).astype(jnp.int32), axis=1),
                               N_EXPERTS - 1)
    n_used = (pad_end[-1:] // MOE_BLOCK).astype(jnp.int32)
    meta = jnp.concatenate([pad_end, padded, n_used]).astype(jnp.int32)
    chosen = sel[:, 0:2, None] == jnp.arange(N_EXPERTS, dtype=jnp.int32)
    dest = jnp.sum(jnp.where(chosen, pad_end - padded, 0), axis=-1) + sel[:, 2:4]
    blk_ids = jnp.arange(n_blocks, dtype=jnp.int32)
    first = ((blk_ids == 0) | (block_expert != jnp.roll(block_expert, 1))) & (blk_ids < n_used[0])
    slot = (jnp.cumsum(first.astype(jnp.int32)) - 1) % 2
    nxt_blk = blk_ids + padded[block_expert] // MOE_BLOCK
    nxt = jnp.where(nxt_blk < n_used[0], block_expert[jnp.minimum(nxt_blk, n_blocks - 1)], -1)
    sched = jnp.stack([block_expert, first.astype(jnp.int32), slot, nxt,
                       jnp.broadcast_to(n_used, (n_blocks,))]).astype(jnp.int32)

    tm_d = min(DISPATCH_TILE, t)
    dest_d = dest.reshape(t // tm_d, 1, 2 * tm_d)
    rows = _dispatch(meta, dest_d, h2, n_blocks * MOE_BLOCK)
    y_rows = _experts(sched, rows, w_gate_exp, w_up_exp, w_down_exp)
    tm_c = min(COMBINE_TILE, t)
    dest_c = dest.reshape(t // tm_c, 1, 2 * tm_c)
    out = _combine(dest_c, gates, x2, y_rows)
    return out.reshape(b, s, d)


def kernel(x, norm_mix_w, w_in, conv_w, conv_b, dt_bias, a_log, d_skip, ssm_norm_w, w_ssm_proj,
           q_norm_w, k_norm_w, rel_bias, w_attn_proj, w_out, norm_ffn_w, w_coarse, b_coarse,
           w_fine, b_fine, w_gate_exp, w_up_exp, w_down_exp):
    depth = norm_mix_w.shape[0]
    for layer in range(depth):
        x = _layer(x, norm_mix_w[layer], w_in[layer], conv_w[layer], conv_b[layer], dt_bias[layer],
                   a_log[layer], d_skip[layer], ssm_norm_w[layer], w_ssm_proj[layer],
                   q_norm_w[layer], k_norm_w[layer], rel_bias, w_attn_proj[layer], w_out[layer],
                   norm_ffn_w[layer], w_coarse[layer], b_coarse[layer], w_fine[layer], b_fine[layer],
                   w_gate_exp[layer], w_up_exp[layer], w_down_exp[layer])
    return x
```
